```python
import jax, jax.numpy as jnp
from jax import lax
import numpy as np

D_MODEL = 1024
BATCH = 8
SEQ = 2048
DEPTH = 1

CHUNK = 64
D_MIX = D_MODEL
D_GMLP = D_MIX // 2
GMLP_GROUPS = 4
GMLP_GROUP_DIM = D_GMLP // GMLP_GROUPS
GMLP_BLOCK = 128
D_SB = D_MIX - D_GMLP
SB_HEADS = 8
SB_HEAD_DIM = D_SB // SB_HEADS
SB_QBLOCK = 128
N_EXPERTS = 64
TOP_K = 8
N_GROUPS = 8
TOPK_GROUPS = 4
D_EXPERT = D_MODEL // 4
D_SHARED = D_EXPERT
ROUTED_SCALE = 2.5
EPS = 1e-6
D_IN = 2 * D_GMLP + 3 * D_SB

kernel_name = 'hybrid_gmlp_stickbreak_moe_block'


def rmsnorm(x, g):
    xf = x.astype(jnp.float32)
    y = xf * lax.rsqrt(jnp.mean(xf * xf, axis=-1, keepdims=True) + EPS)
    return (y * g.astype(jnp.float32)).astype(x.dtype)


def layernorm(x, g, b):
    xf = x.astype(jnp.float32)
    mu = jnp.mean(xf, axis=-1, keepdims=True)
    var = jnp.mean(jnp.square(xf - mu), axis=-1, keepdims=True)
    y = (xf - mu) * lax.rsqrt(var + EPS)
    return (y * g.astype(jnp.float32) + b.astype(jnp.float32)).astype(x.dtype)


def gmlp_spatial_gating(u, v, ln_g, ln_b, w_s, b_s):
    B, S, _ = v.shape
    v = layernorm(v, ln_g, ln_b)
    i = jnp.arange(GMLP_BLOCK)
    mask = (i[None, :] // CHUNK) <= (i[:, None] // CHUNK)
    w = jnp.where(mask, w_s, 0.0)
    vb = v.reshape(B, S // GMLP_BLOCK, GMLP_BLOCK, GMLP_GROUPS, GMLP_GROUP_DIM)
    mixed = jnp.einsum('gij,bnjgc->bnigc', w, vb) + b_s.T[None, None, :, :, None]
    return u * mixed.reshape(B, S, D_GMLP)


def stick_breaking_attention(q, k, v):
    S = q.shape[2]
    scale = SB_HEAD_DIM ** -0.5
    outs = []
    for blk in range(S // SB_QBLOCK):
        q0 = blk * SB_QBLOCK
        kv_len = q0 + SB_QBLOCK
        qb = q[:, :, q0:kv_len].astype(jnp.float32)
        kb = k[:, :, :kv_len].astype(jnp.float32)
        vb = v[:, :, :kv_len]
        z = jnp.einsum('bhqd,bhkd->bhqk', qb, kb) * scale
        t = q0 + jnp.arange(SB_QBLOCK)[:, None]
        s = jnp.arange(kv_len)[None, :]
        mask = s < t
        log_1mb = jnp.where(mask, jax.nn.log_sigmoid(-z), 0.0)
        between = lax.cumsum(log_1mb, axis=3, reverse=True) - log_1mb
        a = jnp.where(mask, jnp.exp(jax.nn.log_sigmoid(z) + between), 0.0)
        outs.append(jnp.einsum('bhqk,bhkd->bhqd', a.astype(vb.dtype), vb))
    return jnp.concatenate(outs, axis=2)


def route(h, w_router, router_bias):
    T = h.shape[0]
    scores = jax.nn.sigmoid(h.astype(jnp.float32) @ w_router.astype(jnp.float32))
    biased = scores + router_bias.astype(jnp.float32)
    grouped = biased.reshape(T, N_GROUPS, N_EXPERTS // N_GROUPS)
    group_score = lax.top_k(grouped, 2)[0].sum(-1)
    _, top_groups = lax.top_k(group_score, TOPK_GROUPS)
    group_mask = (top_groups[:, :, None] == jnp.arange(N_GROUPS)[None, None, :]).any(axis=1)
    expert_mask = jnp.repeat(group_mask, N_EXPERTS // N_GROUPS, axis=1)
    masked = jnp.where(expert_mask, biased, -jnp.inf)
    _, idx = lax.top_k(masked, TOP_K)
    w = jnp.take_along_axis(scores, idx, axis=1)
    w = w / jnp.sum(w, axis=-1, keepdims=True) * ROUTED_SCALE
    return jnp.einsum('tk,tke->te', w, jax.nn.one_hot(idx, N_EXPERTS, dtype=jnp.float32))


def moe_ffn(h, w_router, router_bias, w_gate, w_up, w_down, ws_gate, ws_up, ws_down):
    B, S, D = h.shape
    gates = route(h.reshape(B * S, D), w_router, router_bias).reshape(B, S, N_EXPERTS).astype(h.dtype)

    def per_seq(args):
        hs, gs = args
        g = jnp.einsum('sd,edf->sef', hs, w_gate)
        u = jnp.einsum('sd,edf->sef', hs, w_up)
        act = jax.nn.silu(g) * u * gs[:, :, None]
        return jnp.einsum('sef,efd->sd', act, w_down)

    routed = lax.map(per_seq, (h, gates))
    shared = (jax.nn.silu(h @ ws_gate) * (h @ ws_up)) @ ws_down
    return routed + shared


def setup_inputs(seed: int = 0) -> dict:
    key = jax.random.key(seed)
    ks = jax.random.split(key, 24)
    n = lambda k, shape, s: jax.random.normal(k, shape, jnp.float32) * s
    L, D, E, F = DEPTH, D_MODEL, N_EXPERTS, D_EXPERT
    return {
        'x': n(ks[0], (BATCH, SEQ, D), 1.0),
        'c': n(ks[1], (BATCH, D), 1.0),
        'w_ada': n(ks[2], (L, D, 6 * D), 0.5 * D ** -0.5),
        'b_ada': n(ks[3], (L, 6 * D), 0.02),
        'g_pre_mix': 1.0 + n(ks[4], (L, D), 0.02),
        'w_in': n(ks[5], (L, D, D_IN), D ** -0.5),
        'ln_sgu_g': 1.0 + n(ks[6], (L, D_GMLP), 0.02),
        'ln_sgu_b': n(ks[7], (L, D_GMLP), 0.02),
        'w_spatial': n(ks[8], (L, GMLP_GROUPS, GMLP_BLOCK, GMLP_BLOCK), GMLP_BLOCK ** -0.5),
        'b_spatial': 1.0 + n(ks[9], (L, GMLP_GROUPS, GMLP_BLOCK), 0.1),
        'g_branch': 1.0 + n(ks[10], (L, D_MIX), 0.02),
        'w_out': n(ks[11], (L, D_MIX, D), D_MIX ** -0.5),
        'g_post_mix': 1.0 + n(ks[12], (L, D), 0.02),
        'g_pre_ffn': 1.0 + n(ks[13], (L, D), 0.02),
        'w_router': n(ks[14], (L, D, E), D ** -0.5),
        'router_bias': n(ks[15], (L, E), 0.01),
        'w_gate': n(ks[16], (L, E, D, F), D ** -0.5),
        'w_up': n(ks[17], (L, E, D, F), D ** -0.5),
        'w_down': n(ks[18], (L, E, F, D), F ** -0.5),
        'ws_gate': n(ks[19], (L, D, D_SHARED), D ** -0.5),
        'ws_up': n(ks[20], (L, D, D_SHARED), D ** -0.5),
        'ws_down': n(ks[21], (L, D_SHARED, D), D_SHARED ** -0.5),
        'g_post_ffn': 1.0 + n(ks[22], (L, D), 0.02),
    }


def reference(x, c, w_ada, b_ada, g_pre_mix, w_in, ln_sgu_g, ln_sgu_b, w_spatial, b_spatial,
              g_branch, w_out, g_post_mix, g_pre_ffn, w_router, router_bias, w_gate, w_up,
              w_down, ws_gate, ws_up, ws_down, g_post_ffn):
    B, S, D = x.shape
    sc = jax.nn.silu(c)
    for l in range(DEPTH):
        mod = (sc @ w_ada[l] + b_ada[l])[:, None, :]
        sh1, sc1, ga1, sh2, sc2, ga2 = jnp.split(mod, 6, axis=-1)

        h = rmsnorm(x, g_pre_mix[l]) * (1.0 + sc1) + sh1
        proj = h @ w_in[l]
        uv = jax.nn.gelu(proj[..., :2 * D_GMLP])
        u, v = uv[..., :D_GMLP], uv[..., D_GMLP:]
        q, k, vv = jnp.split(proj[..., 2 * D_GMLP:], 3, axis=-1)
        to_heads = lambda t: t.reshape(B, S, SB_HEADS, SB_HEAD_DIM).transpose(0, 2, 1, 3)
        o_sb = stick_breaking_attention(to_heads(q), to_heads(k), to_heads(vv))
        o_sb = o_sb.transpose(0, 2, 1, 3).reshape(B, S, D_SB)
        o_g = gmlp_spatial_gating(u, v, ln_sgu_g[l], ln_sgu_b[l], w_spatial[l], b_spatial[l])
        mixed = jnp.concatenate([rmsnorm(o_g, g_branch[l, :D_GMLP]),
                                 rmsnorm(o_sb, g_branch[l, D_GMLP:])], axis=-1)
        x = x + ga1 * rmsnorm(mixed @ w_out[l], g_post_mix[l])

        h = rmsnorm(x, g_pre_ffn[l]) * (1.0 + sc2) + sh2
        y = moe_ffn(h, w_router[l], router_bias[l], w_gate[l], w_up[l], w_down[l],
                    ws_gate[l], ws_up[l], ws_down[l])
        x = x + ga2 * rmsnorm(y, g_post_ffn[l])
    return x
```

```python
import functools

import jax
import jax.numpy as jnp
from jax import lax
from jax.experimental import pallas as pl
from jax.experimental.pallas import tpu as pltpu

F32 = jnp.float32
BF16 = jnp.bfloat16

D_MODEL = 1024
D_GMLP = 512
GMLP_GROUPS = 4
GMLP_BLOCK = 128
CHUNK = 64
D_SB = 512
SB_HEAD_DIM = 64
N_PAIRS = D_SB // 128
N_EXPERTS = 64
N_GROUPS = 8
GROUP_SIZE = N_EXPERTS // N_GROUPS
TOPK_GROUPS = 4
TOP_K = 8
D_EXPERT = 256
ROUTED_SCALE = 2.5
EPS = 1e-6
D_IN = 2 * D_GMLP + 3 * D_SB

TM_MIX = 256
TQ = 128
TR = 512
TM_MOE = 1024
VMEM_LIMIT = 56 * 1024 * 1024
ATTN_DEAD_LOG = -110.0


def _rms(x, g):
    return x * lax.rsqrt(jnp.mean(x * x, axis=-1, keepdims=True) + EPS) * g


def _silu(x):
    return x * jax.nn.sigmoid(x)


def _ada_kernel(c_ref, w_ref, b_ref, o_ref):
    o_ref[...] = jnp.dot(_silu(c_ref[...]), w_ref[...], preferred_element_type=F32,
                         precision=lax.Precision.HIGHEST) + b_ref[...]


def _ada(c, w, b):
    bsz, d = c.shape
    n = w.shape[1]
    tn = 512
    return pl.pallas_call(
        _ada_kernel,
        grid=(n // tn,),
        in_specs=[pl.BlockSpec((bsz, d), lambda j: (0, 0)),
                  pl.BlockSpec((d, tn), lambda j: (0, j)),
                  pl.BlockSpec((1, tn), lambda j: (0, j))],
        out_specs=pl.BlockSpec((bsz, tn), lambda j: (0, j)),
        out_shape=jax.ShapeDtypeStruct((bsz, n), F32),
        name="ada_mod",
    )(c, w, b.reshape(1, n))


def _premix_kernel(x_ref, mod_ref, gpre_ref, win_ref, lng_ref, lnb_ref, wsp_ref, bsp_ref,
                   gbr_ref, og_ref, q_ref, k_ref, v_ref):
    x = x_ref[...]
    sh1 = mod_ref[0:1, :]
    sc1 = mod_ref[1:2, :]
    h = _rms(x, gpre_ref[...]) * (1.0 + sc1) + sh1
    proj = jnp.dot(h.astype(BF16), win_ref[...], preferred_element_type=F32)

    u = jax.nn.gelu(proj[:, :D_GMLP])
    v = jax.nn.gelu(proj[:, D_GMLP:2 * D_GMLP])
    mu = jnp.mean(v, axis=-1, keepdims=True)
    var = jnp.mean(jnp.square(v - mu), axis=-1, keepdims=True)
    v = ((v - mu) * lax.rsqrt(var + EPS) * lng_ref[...] + lnb_ref[...]).astype(BF16)

    i = lax.broadcasted_iota(jnp.int32, (GMLP_BLOCK, GMLP_BLOCK), 0)
    j = lax.broadcasted_iota(jnp.int32, (GMLP_BLOCK, GMLP_BLOCK), 1)
    causal = (j // CHUNK) <= (i // CHUNK)
    gd = D_GMLP // GMLP_GROUPS
    blocks = []
    for nb in range(x.shape[0] // GMLP_BLOCK):
        rows = slice(nb * GMLP_BLOCK, (nb + 1) * GMLP_BLOCK)
        cols = []
        for g in range(GMLP_GROUPS):
            w = jnp.where(causal, wsp_ref[g], 0.0).astype(BF16)
            mixed = jnp.dot(w, v[rows, g * gd:(g + 1) * gd], preferred_element_type=F32)
            cols.append(mixed + bsp_ref[:, g:g + 1])
        blocks.append(u[rows, :] * jnp.concatenate(cols, axis=1))
    og = jnp.concatenate(blocks, axis=0)
    og_ref[...] = _rms(og, gbr_ref[...]).astype(BF16)

    base = 2 * D_GMLP
    scale = SB_HEAD_DIM ** -0.5
    for p in range(N_PAIRS):
        q_ref[p] = (proj[:, base + 128 * p:base + 128 * (p + 1)] * scale).astype(BF16)
        k_ref[p] = proj[:, base + D_SB + 128 * p:base + D_SB + 128 * (p + 1)].astype(BF16)
        v_ref[p] = proj[:, base + 2 * D_SB + 128 * p:base + 2 * D_SB + 128 * (p + 1)].astype(BF16)


def _premix(x, mod3, g_pre, w_in, ln_g, ln_b, w_sp, b_sp_t, g_br):
    bsz, s, d = x.shape
    tm = TM_MIX
    full = lambda shape: pl.BlockSpec(shape, lambda b, i: (0,) * len(shape))
    qkv_spec = pl.BlockSpec((None, N_PAIRS, tm, 128), lambda b, i: (b, 0, i, 0))
    qkv_shape = jax.ShapeDtypeStruct((bsz, N_PAIRS, s, 128), BF16)
    return pl.pallas_call(
        _premix_kernel,
        grid=(bsz, s // tm),
        in_specs=[pl.BlockSpec((None, tm, d), lambda b, i: (b, i, 0)),
                  pl.BlockSpec((None, 6, d), lambda b, i: (b, 0, 0)),
                  full((1, d)), full((d, D_IN)), full((1, D_GMLP)), full((1, D_GMLP)),
                  full((GMLP_GROUPS, GMLP_BLOCK, GMLP_BLOCK)), full((GMLP_BLOCK, GMLP_GROUPS)),
                  full((1, D_GMLP))],
        out_specs=[pl.BlockSpec((None, tm, D_GMLP), lambda b, i: (b, i, 0)),
                   qkv_spec, qkv_spec, qkv_spec],
        out_shape=[jax.ShapeDtypeStruct((bsz, s, D_GMLP), BF16), qkv_shape, qkv_shape, qkv_shape],
        compiler_params=pltpu.CompilerParams(
            dimension_semantics=("parallel", "parallel"), vmem_limit_bytes=VMEM_LIMIT),
        name="premix",
    )(x, mod3, g_pre, w_in, ln_g, ln_b, w_sp, b_sp_t, g_br)


def _attn_kernel(q_ref, k_ref, v_ref, o_ref, acc_ref, carry_ref):
    qi = pl.program_id(2)
    lane = lax.broadcasted_iota(jnp.int32, (TQ, 128), 1)
    first_head = lane < SB_HEAD_DIM
    q2 = q_ref[...]
    zero = jnp.zeros_like(q2)
    q_heads = (jnp.where(first_head, q2, zero), jnp.where(first_head, zero, q2))

    r = lax.broadcasted_iota(jnp.int32, (TQ, TQ), 0)
    c = lax.broadcasted_iota(jnp.int32, (TQ, TQ), 1)
    strict_causal = c < r
    suffix = jnp.concatenate([(r > c).astype(BF16), jnp.ones((TQ, TQ), BF16)], axis=1)

    acc_ref[...] = jnp.zeros_like(acc_ref)
    carry_ref[...] = jnp.zeros_like(carry_ref)

    def key_block(j, diagonal):
        start = pl.multiple_of(j * TQ, TQ)
        kj = k_ref[pl.ds(start, TQ), :]
        vj = v_ref[pl.ds(start, TQ), :]
        for hd in range(2):
            z = lax.dot_general(q_heads[hd], kj, (((1,), (1,)), ((), ())),
                                preferred_element_type=F32)
            log_beta = jnp.minimum(z, 0.0) - jnp.log(1.0 + jnp.exp(-jnp.abs(z)))
            log_1mb = log_beta - z
            if diagonal:
                log_1mb = jnp.where(strict_causal, log_1mb, 0.0)
            hi = log_1mb.astype(BF16)
            lo = (log_1mb - hi.astype(F32)).astype(BF16)
            sums = (jnp.dot(hi, suffix, preferred_element_type=F32)
                    + jnp.dot(lo, suffix, preferred_element_type=F32))
            carry = carry_ref[hd]
            a = jnp.exp(log_beta + carry + sums[:, :TQ])
            if diagonal:
                a = jnp.where(strict_causal, a, 0.0)
            acc_ref[hd] += jnp.dot(a.astype(BF16), vj, preferred_element_type=F32)
            carry_ref[hd] = carry + sums[:, TQ:]
        return jnp.max(jnp.maximum(carry_ref[0], carry_ref[1]))

    live = key_block(qi, True)

    def cond(state):
        j, live = state
        return jnp.logical_and(j >= 0, live > ATTN_DEAD_LOG)

    def body(state):
        j, _ = state
        return j - 1, key_block(j, False)

    lax.while_loop(cond, body, (qi - 1, live))
    o_ref[...] = jnp.where(first_head, acc_ref[0], acc_ref[1])


def _attention(q, k, v):
    bsz, npair, s, _ = q.shape
    kv_spec = pl.BlockSpec((None, None, s, 128), lambda b, p, i: (b, p, 0, 0))
    return pl.pallas_call(
        _attn_kernel,
        grid=(bsz, npair, s // TQ),
        in_specs=[pl.BlockSpec((None, None, TQ, 128), lambda b, p, i: (b, p, i, 0)),
                  kv_spec, kv_spec],
        out_specs=pl.BlockSpec((None, TQ, 128), lambda b, p, i: (b, i, p)),
        out_shape=jax.ShapeDtypeStruct((bsz, s, npair * 128), F32),
        scratch_shapes=[pltpu.VMEM((2, TQ, 128), F32), pltpu.VMEM((2, TQ, TQ), F32)],
        compiler_params=pltpu.CompilerParams(
            dimension_semantics=("parallel", "parallel", "parallel")),
        name="stickbreak_attn",
    )(q, k, v)


def _postmix_kernel(og_ref, osb_ref, x_ref, mod_ref, gbr_ref, wout_ref, gpost_ref, gpre_ref,
                    wrt_ref, x1_ref, h2_ref, logit_ref):
    ga1 = mod_ref[2:3, :]
    sh2 = mod_ref[3:4, :]
    sc2 = mod_ref[4:5, :]
    osb = _rms(osb_ref[...], gbr_ref[...]).astype(BF16)
    m = (jnp.dot(og_ref[...], wout_ref[:D_GMLP, :], preferred_element_type=F32)
         + jnp.dot(osb, wout_ref[D_GMLP:, :], preferred_element_type=F32))
    x1 = x_ref[...] + ga1 * _rms(m, gpost_ref[...])
    x1_ref[...] = x1
    h2 = _rms(x1, gpre_ref[...]) * (1.0 + sc2) + sh2
    h2_ref[...] = h2.astype(BF16)
    logit_ref[...] = lax.dot_general(wrt_ref[...], h2, (((1,), (1,)), ((), ())),
                                     preferred_element_type=F32,
                                     precision=lax.Precision.HIGHEST)


def _postmix(og, osb, x, mod3, g_br, w_out, g_post, g_pre, w_router_t):
    bsz, s, d = x.shape
    tm = TM_MIX
    nt = s // tm
    full = lambda shape: pl.BlockSpec(shape, lambda b, i: (0,) * len(shape))
    return pl.pallas_call(
        _postmix_kernel,
        grid=(bsz, nt),
        in_specs=[pl.BlockSpec((None, tm, D_GMLP), lambda b, i: (b, i, 0)),
                  pl.BlockSpec((None, tm, D_SB), lambda b, i: (b, i, 0)),
                  pl.BlockSpec((None, tm, d), lambda b, i: (b, i, 0)),
                  pl.BlockSpec((None, 6, d), lambda b, i: (b, 0, 0)),
                  full((1, D_SB)), full((d, d)), full((1, d)), full((1, d)),
                  full((N_EXPERTS, d))],
        out_specs=[pl.BlockSpec((None, tm, d), lambda b, i: (b, i, 0)),
                   pl.BlockSpec((None, tm, d), lambda b, i: (b, i, 0)),
                   pl.BlockSpec((N_EXPERTS, tm), lambda b, i: (0, b * nt + i))],
        out_shape=[jax.ShapeDtypeStruct((bsz, s, d), F32),
                   jax.ShapeDtypeStruct((bsz, s, d), BF16),
                   jax.ShapeDtypeStruct((N_EXPERTS, bsz * s), F32)],
        compiler_params=pltpu.CompilerParams(
            dimension_semantics=("parallel", "parallel"), vmem_limit_bytes=VMEM_LIMIT),
        name="postmix",
    )(og, osb, x, mod3, g_br, w_out, g_post, g_pre, w_router_t)


def _first_index_of_max(x, idx, axis, size):
    m = jnp.max(x, axis=axis, keepdims=True)
    return jnp.min(jnp.where(x == m, idx, size), axis=axis, keepdims=True)


def _route_kernel(logit_ref, bias_ref, gate_ref):
    tr = logit_ref.shape[1]
    scores = jax.nn.sigmoid(logit_ref[...])
    biased = scores + bias_ref[...]
    neg = jnp.float32(-jnp.inf)

    grouped = biased.reshape(N_GROUPS, GROUP_SIZE, tr)
    within = lax.broadcasted_iota(jnp.int32, grouped.shape, 1)
    top1 = jnp.max(grouped, axis=1, keepdims=True)
    first = _first_index_of_max(grouped, within, 1, GROUP_SIZE)
    top2 = jnp.max(jnp.where(within == first, neg, grouped), axis=1, keepdims=True)
    group_score = (top1 + top2).reshape(N_GROUPS, tr)

    gidx = lax.broadcasted_iota(jnp.int32, group_score.shape, 0)
    group_on = jnp.zeros(group_score.shape, jnp.bool_)
    for _ in range(TOPK_GROUPS):
        pick = gidx == _first_index_of_max(group_score, gidx, 0, N_GROUPS)
        group_on = jnp.logical_or(group_on, pick)
        group_score = jnp.where(pick, neg, group_score)

    masked = jnp.where(group_on.reshape(N_GROUPS, 1, tr), grouped, neg).reshape(N_EXPERTS, tr)
    eidx = lax.broadcasted_iota(jnp.int32, masked.shape, 0)
    chosen = jnp.zeros(masked.shape, jnp.bool_)
    for _ in range(TOP_K):
        pick = eidx == _first_index_of_max(masked, eidx, 0, N_EXPERTS)
        chosen = jnp.logical_or(chosen, pick)
        masked = jnp.where(pick, neg, masked)

    w = jnp.where(chosen, scores, 0.0)
    gate_ref[...] = w / jnp.sum(w, axis=0, keepdims=True) * ROUTED_SCALE


def _route(logits_t, bias):
    e, t = logits_t.shape
    return pl.pallas_call(
        _route_kernel,
        grid=(t // TR,),
        in_specs=[pl.BlockSpec((e, TR), lambda i: (0, i)),
                  pl.BlockSpec((e, 1), lambda i: (0, 0))],
        out_specs=pl.BlockSpec((e, TR), lambda i: (0, i)),
        out_shape=jax.ShapeDtypeStruct((e, t), F32),
        compiler_params=pltpu.CompilerParams(dimension_semantics=("parallel",)),
        name="route",
    )(logits_t, bias.reshape(e, 1))


def _moe_kernel(h_ref, gate_ref, wg_ref, wu_ref, wd_ref, wsg_ref, wsu_ref, wsd_ref, x1_ref,
                mod_ref, gpost_ref, o_ref, acc_ref):
    e = pl.program_id(2)
    h = h_ref[...]

    @pl.when(e == 0)
    def _():
        act = _silu(jnp.dot(h, wsg_ref[...], preferred_element_type=F32)) * jnp.dot(
            h, wsu_ref[...], preferred_element_type=F32)
        acc_ref[...] = jnp.dot(act.astype(BF16), wsd_ref[...], preferred_element_type=F32)

    gates = gate_ref[...]
    lane = lax.broadcasted_iota(jnp.int32, gates.shape, 1)
    gcol = jnp.sum(jnp.where(lane == e, gates, 0.0), axis=1, keepdims=True)
    g = jnp.dot(h, wg_ref[...], preferred_element_type=F32)
    u = jnp.dot(h, wu_ref[...], preferred_element_type=F32)
    act = _silu(g) * u * gcol
    acc_ref[...] += jnp.dot(act.astype(BF16), wd_ref[...], preferred_element_type=F32)

    @pl.when(e == pl.num_programs(2) - 1)
    def _():
        ga2 = mod_ref[5:6, :]
        o_ref[...] = x1_ref[...] + ga2 * _rms(acc_ref[...], gpost_ref[...])


def _moe_dense(h2, gates, w_gate, w_up, w_down, ws_gate, ws_up, ws_down, x1, mod3, g_post):
    bsz, s, d = x1.shape
    tm = TM_MOE
    nt = s // tm
    f = D_EXPERT
    full = lambda shape: pl.BlockSpec(shape, lambda b, i, e: (0,) * len(shape))
    return pl.pallas_call(
        _moe_kernel,
        grid=(bsz, nt, N_EXPERTS),
        in_specs=[pl.BlockSpec((None, tm, d), lambda b, i, e: (b, i, 0)),
                  pl.BlockSpec((tm, N_EXPERTS), lambda b, i, e: (b * nt + i, 0)),
                  pl.BlockSpec((None, d, f), lambda b, i, e: (e, 0, 0)),
                  pl.BlockSpec((None, d, f), lambda b, i, e: (e, 0, 0)),
                  pl.BlockSpec((None, f, d), lambda b, i, e: (e, 0, 0)),
                  full((d, f)), full((d, f)), full((f, d)),
                  pl.BlockSpec((None, tm, d), lambda b, i, e: (b, i, 0)),
                  pl.BlockSpec((None, 6, d), lambda b, i, e: (b, 0, 0)),
                  full((1, d))],
        out_specs=pl.BlockSpec((None, tm, d), lambda b, i, e: (b, i, 0)),
        out_shape=jax.ShapeDtypeStruct((bsz, s, d), F32),
        scratch_shapes=[pltpu.VMEM((tm, d), F32)],
        compiler_params=pltpu.CompilerParams(
            dimension_semantics=("parallel", "parallel", "arbitrary"),
            vmem_limit_bytes=VMEM_LIMIT),
        name="moe_dense",
    )(h2, gates, w_gate, w_up, w_down, ws_gate, ws_up, ws_down, x1, mod3, g_post)


def kernel(x, c, w_ada, b_ada, g_pre_mix, w_in, ln_sgu_g, ln_sgu_b, w_spatial, b_spatial,
           g_branch, w_out, g_post_mix, g_pre_ffn, w_router, router_bias, w_gate, w_up, w_down,
           ws_gate, ws_up, ws_down, g_post_ffn):
    bsz, s, d = x.shape
    row = lambda a: a.reshape(1, -1)
    for l in range(w_ada.shape[0]):
        mod3 = _ada(c, w_ada[l], b_ada[l]).reshape(bsz, 6, d)
        og, q, k, v = _premix(x, mod3, row(g_pre_mix[l]), w_in[l].astype(BF16),
                              row(ln_sgu_g[l]), row(ln_sgu_b[l]), w_spatial[l],
                              b_spatial[l].T, row(g_branch[l, :D_GMLP]))
        osb = _attention(q, k, v)
        x1, h2, logits_t = _postmix(og, osb, x, mod3, row(g_branch[l, D_GMLP:]),
                                    w_out[l].astype(BF16), row(g_post_mix[l]),
                                    row(g_pre_ffn[l]), w_router[l].T)
        gates_t = _route(logits_t, router_bias[l])
        x = _moe_dense(h2, gates_t.T, w_gate[l].astype(BF16), w_up[l].astype(BF16),
                       w_down[l].astype(BF16), ws_gate[l].astype(BF16), ws_up[l].astype(BF16),
                       ws_down[l].astype(BF16), x1, mod3, row(g_post_ffn[l]))
    return x
```

```python
import jax
import jax.numpy as jnp
from jax import lax
from jax.experimental import pallas as pl
from jax.experimental.pallas import tpu as pltpu

F32 = jnp.float32
BF16 = jnp.bfloat16
I32 = jnp.int32

D_MODEL = 1024
D_GMLP = 512
GMLP_GROUPS = 4
GMLP_BLOCK = 128
CHUNK = 64
D_SB = 512
SB_HEAD_DIM = 64
N_PAIRS = D_SB // 128
N_EXPERTS = 64
N_GROUPS = 8
GROUP_SIZE = N_EXPERTS // N_GROUPS
TOPK_GROUPS = 4
TOP_K = 8
D_EXPERT = 256
ROUTED_SCALE = 2.5
EPS = 1e-6
D_IN = 2 * D_GMLP + 3 * D_SB

TM_MIX = 256
TQ = 128
VMEM_LIMIT = 56 * 1024 * 1024
ATTN_DEAD_LOG = -110.0

TB = 256
ROW_CHUNK = 16
KC = 256
BLOCK_ROWS = -(-(TB * TOP_K + N_EXPERTS * (ROW_CHUNK - 1)) // KC) * KC
BLOCK_CHUNKS = BLOCK_ROWS // ROW_CHUNK
TM_FFN = 256


def _rms(x, g):
    return x * lax.rsqrt(jnp.mean(x * x, axis=-1, keepdims=True) + EPS) * g


def _silu(x):
    return x * jax.nn.sigmoid(x)


def _ada_kernel(c_ref, w_ref, b_ref, o_ref):
    o_ref[...] = jnp.dot(_silu(c_ref[...]), w_ref[...], preferred_element_type=F32,
                         precision=lax.Precision.HIGHEST) + b_ref[...]


def _ada(c, w, b):
    bsz, d = c.shape
    n = w.shape[1]
    tn = 512
    return pl.pallas_call(
        _ada_kernel,
        grid=(n // tn,),
        in_specs=[pl.BlockSpec((bsz, d), lambda j: (0, 0)),
                  pl.BlockSpec((d, tn), lambda j: (0, j)),
                  pl.BlockSpec((1, tn), lambda j: (0, j))],
        out_specs=pl.BlockSpec((bsz, tn), lambda j: (0, j)),
        out_shape=jax.ShapeDtypeStruct((bsz, n), F32),
        name="ada_mod",
    )(c, w, b.reshape(1, n))


def _premix_kernel(x_ref, mod_ref, gpre_ref, win_ref, lng_ref, lnb_ref, wsp_ref, bsp_ref,
                   gbr_ref, og_ref, q_ref, k_ref, v_ref):
    x = x_ref[...]
    sh1 = mod_ref[0:1, :]
    sc1 = mod_ref[1:2, :]
    h = _rms(x, gpre_ref[...]) * (1.0 + sc1) + sh1
    proj = jnp.dot(h.astype(BF16), win_ref[...], preferred_element_type=F32)

    u = jax.nn.gelu(proj[:, :D_GMLP])
    v = jax.nn.gelu(proj[:, D_GMLP:2 * D_GMLP])
    mu = jnp.mean(v, axis=-1, keepdims=True)
    var = jnp.mean(jnp.square(v - mu), axis=-1, keepdims=True)
    v = ((v - mu) * lax.rsqrt(var + EPS) * lng_ref[...] + lnb_ref[...]).astype(BF16)

    i = lax.broadcasted_iota(I32, (GMLP_BLOCK, GMLP_BLOCK), 0)
    j = lax.broadcasted_iota(I32, (GMLP_BLOCK, GMLP_BLOCK), 1)
    causal = (j // CHUNK) <= (i // CHUNK)
    gd = D_GMLP // GMLP_GROUPS
    blocks = []
    for nb in range(x.shape[0] // GMLP_BLOCK):
        rows = slice(nb * GMLP_BLOCK, (nb + 1) * GMLP_BLOCK)
        cols = []
        for g in range(GMLP_GROUPS):
            w = jnp.where(causal, wsp_ref[g], 0.0).astype(BF16)
            mixed = jnp.dot(w, v[rows, g * gd:(g + 1) * gd], preferred_element_type=F32)
            cols.append(mixed + bsp_ref[:, g:g + 1])
        blocks.append(u[rows, :] * jnp.concatenate(cols, axis=1))
    og = jnp.concatenate(blocks, axis=0)
    og_ref[...] = _rms(og, gbr_ref[...]).astype(BF16)

    base = 2 * D_GMLP
    scale = SB_HEAD_DIM ** -0.5
    for p in range(N_PAIRS):
        q_ref[p] = (proj[:, base + 128 * p:base + 128 * (p + 1)] * scale).astype(BF16)
        k_ref[p] = proj[:, base + D_SB + 128 * p:base + D_SB + 128 * (p + 1)].astype(BF16)
        v_ref[p] = proj[:, base + 2 * D_SB + 128 * p:base + 2 * D_SB + 128 * (p + 1)].astype(BF16)


def _premix(x, mod3, g_pre, w_in, ln_g, ln_b, w_sp, b_sp_t, g_br):
    bsz, s, d = x.shape
    tm = TM_MIX
    full = lambda shape: pl.BlockSpec(shape, lambda b, i: (0,) * len(shape))
    qkv_spec = pl.BlockSpec((None, N_PAIRS, tm, 128), lambda b, i: (b, 0, i, 0))
    qkv_shape = jax.ShapeDtypeStruct((bsz, N_PAIRS, s, 128), BF16)
    return pl.pallas_call(
        _premix_kernel,
        grid=(bsz, s // tm),
        in_specs=[pl.BlockSpec((None, tm, d), lambda b, i: (b, i, 0)),
                  pl.BlockSpec((None, 6, d), lambda b, i: (b, 0, 0)),
                  full((1, d)), full((d, D_IN)), full((1, D_GMLP)), full((1, D_GMLP)),
                  full((GMLP_GROUPS, GMLP_BLOCK, GMLP_BLOCK)), full((GMLP_BLOCK, GMLP_GROUPS)),
                  full((1, D_GMLP))],
        out_specs=[pl.BlockSpec((None, tm, D_GMLP), lambda b, i: (b, i, 0)),
                   qkv_spec, qkv_spec, qkv_spec],
        out_shape=[jax.ShapeDtypeStruct((bsz, s, D_GMLP), BF16), qkv_shape, qkv_shape, qkv_shape],
        compiler_params=pltpu.CompilerParams(
            dimension_semantics=("parallel", "parallel"), vmem_limit_bytes=VMEM_LIMIT),
        name="premix",
    )(x, mod3, g_pre, w_in, ln_g, ln_b, w_sp, b_sp_t, g_br)


def _attn_kernel(q_ref, k_ref, v_ref, o_ref, qs_ref, acc_ref, carry_ref):
    qi = pl.program_id(1)
    first_head = lax.broadcasted_iota(I32, (TQ, 128), 1) < SB_HEAD_DIM
    for p in range(N_PAIRS):
        q2 = q_ref[p]
        zero = jnp.zeros_like(q2)
        qs_ref[p, :TQ, :] = jnp.where(first_head, q2, zero)
        qs_ref[p, TQ:, :] = jnp.where(first_head, zero, q2)

    r = jnp.bitwise_and(lax.broadcasted_iota(I32, (2 * TQ, TQ), 0), TQ - 1)
    c = lax.broadcasted_iota(I32, (2 * TQ, TQ), 1)
    strict_causal = c < r
    kr = lax.broadcasted_iota(I32, (TQ, TQ), 0)
    kc = lax.broadcasted_iota(I32, (TQ, TQ), 1)
    suffix = jnp.concatenate([(kr > kc).astype(BF16), jnp.ones((TQ, TQ), BF16)], axis=1)

    acc_ref[...] = jnp.zeros_like(acc_ref)
    carry_ref[...] = jnp.zeros_like(carry_ref)

    def key_block(j, diagonal):
        start = pl.multiple_of(j * TQ, TQ)
        live = None
        for p in range(N_PAIRS):
            kj = k_ref[p, pl.ds(start, TQ), :]
            vj = v_ref[p, pl.ds(start, TQ), :]
            z = lax.dot_general(qs_ref[p], kj, (((1,), (1,)), ((), ())),
                                preferred_element_type=F32)
            log_beta = jnp.minimum(z, 0.0) - jnp.log(1.0 + jnp.exp(-jnp.abs(z)))
            log_1mb = log_beta - z
            if diagonal:
                log_1mb = jnp.where(strict_causal, log_1mb, 0.0)
            hi = log_1mb.astype(BF16)
            lo = (log_1mb - hi.astype(F32)).astype(BF16)
            sums = jnp.dot(jnp.concatenate([hi, lo], axis=0), suffix,
                           preferred_element_type=F32)
            sums = sums[:2 * TQ] + sums[2 * TQ:]
            carry = carry_ref[p]
            a = jnp.exp(log_beta + carry + sums[:, :TQ])
            if diagonal:
                a = jnp.where(strict_causal, a, 0.0)
            acc_ref[p] += jnp.dot(a.astype(BF16), vj, preferred_element_type=F32)
            carry = carry + sums[:, TQ:]
            carry_ref[p] = carry
            live = carry if live is None else jnp.maximum(live, carry)
        return jnp.max(live)

    live = key_block(qi, True)

    def cond(state):
        j, live = state
        return jnp.logical_and(j >= 0, live > ATTN_DEAD_LOG)

    def body(state):
        j, _ = state
        return j - 1, key_block(j, False)

    lax.while_loop(cond, body, (qi - 1, live))
    for p in range(N_PAIRS):
        o_ref[:, 128 * p:128 * (p + 1)] = jnp.where(first_head, acc_ref[p, :TQ, :],
                                                    acc_ref[p, TQ:, :])


def _attention(q, k, v):
    bsz, npair, s, _ = q.shape
    kv_spec = pl.BlockSpec((None, npair, s, 128), lambda b, i: (b, 0, 0, 0))
    return pl.pallas_call(
        _attn_kernel,
        grid=(bsz, s // TQ),
        in_specs=[pl.BlockSpec((None, npair, TQ, 128), lambda b, i: (b, 0, i, 0)),
                  kv_spec, kv_spec],
        out_specs=pl.BlockSpec((None, TQ, npair * 128), lambda b, i: (b, i, 0)),
        out_shape=jax.ShapeDtypeStruct((bsz, s, npair * 128), F32),
        scratch_shapes=[pltpu.VMEM((npair, 2 * TQ, 128), BF16),
                        pltpu.VMEM((npair, 2 * TQ, 128), F32),
                        pltpu.VMEM((npair, 2 * TQ, TQ), F32)],
        compiler_params=pltpu.CompilerParams(
            dimension_semantics=("parallel", "parallel"), vmem_limit_bytes=VMEM_LIMIT),
        name="stickbreak_attn",
    )(q, k, v)


def _postmix_kernel(og_ref, osb_ref, x_ref, mod_ref, gbr_ref, wout_ref, gpost_ref, gpre_ref,
                    wrt_ref, x1_ref, h2_ref, logit_ref):
    ga1 = mod_ref[2:3, :]
    sh2 = mod_ref[3:4, :]
    sc2 = mod_ref[4:5, :]
    osb = _rms(osb_ref[...], gbr_ref[...]).astype(BF16)
    m = (jnp.dot(og_ref[...], wout_ref[:D_GMLP, :], preferred_element_type=F32)
         + jnp.dot(osb, wout_ref[D_GMLP:, :], preferred_element_type=F32))
    x1 = x_ref[...] + ga1 * _rms(m, gpost_ref[...])
    x1_ref[...] = x1
    h2 = _rms(x1, gpre_ref[...]) * (1.0 + sc2) + sh2
    h2_ref[...] = h2.astype(BF16)
    logit_ref[...] = lax.dot_general(wrt_ref[...], h2, (((1,), (1,)), ((), ())),
                                     preferred_element_type=F32,
                                     precision=lax.Precision.HIGHEST)


def _postmix(og, osb, x, mod3, g_br, w_out, g_post, g_pre, w_router_t):
    bsz, s, d = x.shape
    tm = TM_MIX
    nt = s // tm
    full = lambda shape: pl.BlockSpec(shape, lambda b, i: (0,) * len(shape))
    return pl.pallas_call(
        _postmix_kernel,
        grid=(bsz, nt),
        in_specs=[pl.BlockSpec((None, tm, D_GMLP), lambda b, i: (b, i, 0)),
                  pl.BlockSpec((None, tm, D_SB), lambda b, i: (b, i, 0)),
                  pl.BlockSpec((None, tm, d), lambda b, i: (b, i, 0)),
                  pl.BlockSpec((None, 6, d), lambda b, i: (b, 0, 0)),
                  full((1, D_SB)), full((d, d)), full((1, d)), full((1, d)),
                  full((N_EXPERTS, d))],
        out_specs=[pl.BlockSpec((None, tm, d), lambda b, i: (b, i, 0)),
                   pl.BlockSpec((None, tm, d), lambda b, i: (b, i, 0)),
                   pl.BlockSpec((N_EXPERTS, tm), lambda b, i: (0, b * nt + i))],
        out_shape=[jax.ShapeDtypeStruct((bsz, s, d), F32),
                   jax.ShapeDtypeStruct((bsz, s, d), BF16),
                   jax.ShapeDtypeStruct((N_EXPERTS, bsz * s), F32)],
        compiler_params=pltpu.CompilerParams(
            dimension_semantics=("parallel", "parallel"), vmem_limit_bytes=VMEM_LIMIT),
        name="postmix",
    )(og, osb, x, mod3, g_br, w_out, g_post, g_pre, w_router_t)


def _first_index_of_max(x, idx, axis, size):
    m = jnp.max(x, axis=axis, keepdims=True)
    return jnp.min(jnp.where(x == m, idx, size), axis=axis, keepdims=True)


def _route_kernel(logit_ref, bias_ref, info_ref, infot_ref, cnt_ref):
    scores = jax.nn.sigmoid(logit_ref[...])
    biased = scores + bias_ref[...]
    neg = jnp.float32(-jnp.inf)

    grouped = biased.reshape(N_GROUPS, GROUP_SIZE, TB)
    within = lax.broadcasted_iota(I32, grouped.shape, 1)
    top1 = jnp.max(grouped, axis=1, keepdims=True)
    first = _first_index_of_max(grouped, within, 1, GROUP_SIZE)
    top2 = jnp.max(jnp.where(within == first, neg, grouped), axis=1, keepdims=True)
    group_score = (top1 + top2).reshape(N_GROUPS, TB)

    gidx = lax.broadcasted_iota(I32, group_score.shape, 0)
    group_on = jnp.zeros(group_score.shape, jnp.bool_)
    for _ in range(TOPK_GROUPS):
        pick = gidx == _first_index_of_max(group_score, gidx, 0, N_GROUPS)
        group_on = jnp.logical_or(group_on, pick)
        group_score = jnp.where(pick, neg, group_score)

    masked = jnp.where(group_on.reshape(N_GROUPS, 1, TB), grouped, neg).reshape(N_EXPERTS, TB)
    eidx = lax.broadcasted_iota(I32, masked.shape, 0)
    chosen = jnp.zeros(masked.shape, jnp.bool_)
    picks = []
    for _ in range(TOP_K):
        pick = eidx == _first_index_of_max(masked, eidx, 0, N_EXPERTS)
        picks.append(pick)
        chosen = jnp.logical_or(chosen, pick)
        masked = jnp.where(pick, neg, masked)

    w = jnp.where(chosen, scores, 0.0)
    gates = w / jnp.sum(w, axis=0, keepdims=True) * ROUTED_SCALE

    chosen_f = chosen.astype(F32)
    cnt = jnp.sum(chosen_f, axis=1, keepdims=True)
    padded = jnp.ceil(cnt * (1.0 / ROW_CHUNK)) * ROW_CHUNK
    er = lax.broadcasted_iota(I32, (N_EXPERTS, N_EXPERTS), 0)
    ec = lax.broadcasted_iota(I32, (N_EXPERTS, N_EXPERTS), 1)
    start = jnp.dot((ec < er).astype(BF16), jnp.broadcast_to(padded, (N_EXPERTS, TB)).astype(BF16),
                    preferred_element_type=F32)
    tr = lax.broadcasted_iota(I32, (TB, TB), 0)
    tc = lax.broadcasted_iota(I32, (TB, TB), 1)
    rank = jnp.dot(chosen_f.astype(BF16), (tr < tc).astype(BF16), preferred_element_type=F32)
    row = start + rank

    dest = [jnp.sum(jnp.where(pk, row, 0.0), axis=0, keepdims=True) for pk in picks]
    wk = [jnp.sum(jnp.where(pk, gates, 0.0), axis=0, keepdims=True) for pk in picks]
    info = jnp.concatenate(dest + wk, axis=0)
    info_ref[...] = info
    infot_ref[...] = jnp.concatenate(
        [info, jnp.zeros((128 - 2 * TOP_K, TB), F32)], axis=0).T
    cnt_ref[...] = jnp.broadcast_to(cnt, (N_EXPERTS, 128))


def _route(logits_t, bias):
    e, t = logits_t.shape
    nblk = t // TB
    return pl.pallas_call(
        _route_kernel,
        grid=(nblk,),
        in_specs=[pl.BlockSpec((e, TB), lambda i: (0, i)),
                  pl.BlockSpec((e, 1), lambda i: (0, 0))],
        out_specs=[pl.BlockSpec((2 * TOP_K, TB), lambda i: (0, i)),
                   pl.BlockSpec((TB, 128), lambda i: (i, 0)),
                   pl.BlockSpec((None, e, 128), lambda i: (i, 0, 0))],
        out_shape=[jax.ShapeDtypeStruct((2 * TOP_K, t), F32),
                   jax.ShapeDtypeStruct((t, 128), F32),
                   jax.ShapeDtypeStruct((nblk, e, 128), F32)],
        compiler_params=pltpu.CompilerParams(dimension_semantics=("parallel",)),
        name="route",
    )(logits_t, bias.reshape(e, 1))


def _dispatch_plan(cnt, n_tiles):
    nblk, e = cnt.shape
    pc = (cnt + ROW_CHUNK - 1) // ROW_CHUNK * ROW_CHUNK
    start = jnp.cumsum(pc, axis=1) - pc
    nchunk = jnp.sum(pc, axis=1) // ROW_CHUNK
    off = jnp.cumsum(pc, axis=0) - pc
    ecount = jnp.sum(pc, axis=0)
    epad = (ecount + TM_FFN - 1) // TM_FFN * TM_FFN
    gend = jnp.cumsum(epad)
    gbase = gend - epad
    nact = gend[-1] // TM_FFN
    tiles = jnp.minimum(jnp.arange(n_tiles, dtype=I32), nact - 1)
    tile_expert = jnp.minimum(
        jnp.sum((gend[None, :] <= tiles[:, None] * TM_FFN).astype(I32), axis=1), e - 1)
    cidx = jnp.arange(BLOCK_CHUNKS, dtype=I32)
    start16 = start // ROW_CHUNK
    e_of_chunk = jnp.sum((start16[:, None, :] <= cidx[None, :, None]).astype(I32), axis=2) - 1
    shift = (gbase[None, :] + off) // ROW_CHUNK - start16
    gdst = jnp.take_along_axis(shift, e_of_chunk, axis=1) + cidx[None, :]
    zbase = (gbase + ecount) // ROW_CHUNK
    zn = (epad - ecount) // ROW_CHUNK
    as_i32 = lambda a: a.astype(I32)
    return (as_i32(nchunk), as_i32(gdst.reshape(-1)), as_i32(zn), as_i32(zbase),
            as_i32(tile_expert), as_i32(nact.reshape(1)))


def _dispatch_kernel(nchunk_ref, gdst_ref, zn_ref, zbase_ref, h_ref, info_ref, xs_ref,
                     buf_ref, zero_ref, sem, zsem):
    j = pl.program_id(0)
    last = pl.num_programs(0) - 1
    slot = j % 2

    def chunk_copy(s, c, g):
        return pltpu.make_async_copy(
            buf_ref.at[s, pl.ds(pl.multiple_of(c * ROW_CHUNK, ROW_CHUNK), ROW_CHUNK), :],
            xs_ref.at[pl.ds(pl.multiple_of(g * ROW_CHUNK, ROW_CHUNK), ROW_CHUNK), :],
            sem.at[s])

    def wait_block(jj, s):
        def body(c, carry):
            chunk_copy(s, 0, 0).wait()
            return carry
        lax.fori_loop(0, nchunk_ref[jj], body, 0)

    @pl.when(j >= 2)
    def _():
        wait_block(j - 2, slot)

    h = h_ref[...]
    rows_used = nchunk_ref[j] * ROW_CHUNK
    for kc in range(BLOCK_ROWS // KC):
        @pl.when(kc * KC < rows_used)
        def _():
            rows = (lax.broadcasted_iota(I32, (KC, TB), 0) + kc * KC).astype(F32)
            onehot = jnp.zeros((KC, TB), F32)
            for k in range(TOP_K):
                onehot = jnp.where(info_ref[k:k + 1, :] == rows, 1.0, onehot)
            xs = jnp.dot(onehot.astype(BF16), h, preferred_element_type=F32)
            buf_ref[slot, kc * KC:(kc + 1) * KC, :] = xs.astype(BF16)

    def issue(c, carry):
        chunk_copy(slot, c, gdst_ref[j * BLOCK_CHUNKS + c]).start()
        return carry
    lax.fori_loop(0, nchunk_ref[j], issue, 0)

    @pl.when(j == last)
    def _():
        wait_block(j, slot)

        @pl.when(j >= 1)
        def _():
            wait_block(j - 1, 1 - slot)

        zero_ref[...] = jnp.zeros_like(zero_ref)

        def zero_copy(g):
            return pltpu.make_async_copy(
                zero_ref, xs_ref.at[pl.ds(pl.multiple_of(g * ROW_CHUNK, ROW_CHUNK), ROW_CHUNK), :],
                zsem.at[0])

        def per_expert(e, carry):
            def start_one(i, c2):
                zero_copy(zbase_ref[e] + i).start()
                return c2
            lax.fori_loop(0, zn_ref[e], start_one, 0)
            return carry
        lax.fori_loop(0, N_EXPERTS, per_expert, 0)

        def per_expert_wait(e, carry):
            def wait_one(i, c2):
                zero_copy(0).wait()
                return c2
            lax.fori_loop(0, zn_ref[e], wait_one, 0)
            return carry
        lax.fori_loop(0, N_EXPERTS, per_expert_wait, 0)


def _dispatch(plan, h2_flat, info, n_rows):
    nchunk, gdst, zn, zbase = plan
    t, d = h2_flat.shape
    grid_spec = pltpu.PrefetchScalarGridSpec(
        num_scalar_prefetch=4,
        grid=(t // TB,),
        in_specs=[pl.BlockSpec((TB, d), lambda j, *_: (j, 0)),
                  pl.BlockSpec((2 * TOP_K, TB), lambda j, *_: (0, j))],
        out_specs=pl.BlockSpec(memory_space=pl.ANY),
        scratch_shapes=[pltpu.VMEM((2, BLOCK_ROWS, d), BF16),
                        pltpu.VMEM((ROW_CHUNK, d), BF16),
                        pltpu.SemaphoreType.DMA((2,)),
                        pltpu.SemaphoreType.DMA((1,))],
    )
    return pl.pallas_call(
        _dispatch_kernel,
        grid_spec=grid_spec,
        out_shape=jax.ShapeDtypeStruct((n_rows, d), BF16),
        compiler_params=pltpu.CompilerParams(
            dimension_semantics=("arbitrary",), vmem_limit_bytes=VMEM_LIMIT),
        name="moe_dispatch",
    )(nchunk, gdst, zn, zbase, h2_flat, info)


def _ffn_kernel(te_ref, nact_ref, x_ref, wg_ref, wu_ref, wd_ref, y_ref, wgu_s, wd_s):
    i = pl.program_id(0)

    @pl.when(i < nact_ref[0])
    def _():
        @pl.when(jnp.logical_or(i == 0, te_ref[i] != te_ref[jnp.maximum(i - 1, 0)]))
        def _():
            wgu_s[:, :D_EXPERT] = wg_ref[...].astype(BF16)
            wgu_s[:, D_EXPERT:] = wu_ref[...].astype(BF16)
            wd_s[...] = wd_ref[...].astype(BF16)

        gu = jnp.dot(x_ref[...], wgu_s[...], preferred_element_type=F32)
        act = _silu(gu[:, :D_EXPERT]) * gu[:, D_EXPERT:]
        y_ref[...] = jnp.dot(act.astype(BF16), wd_s[...], preferred_element_type=F32).astype(BF16)


def _expert_ffn(tile_expert, nact, xs, w_gate, w_up, w_down):
    n_rows, d = xs.shape
    f = D_EXPERT
    row_tile = lambda i, te, na: (jnp.minimum(i, na[0] - 1), 0)
    grid_spec = pltpu.PrefetchScalarGridSpec(
        num_scalar_prefetch=2,
        grid=(n_rows // TM_FFN,),
        in_specs=[pl.BlockSpec((TM_FFN, d), row_tile),
                  pl.BlockSpec((None, d, f), lambda i, te, na: (te[i], 0, 0)),
                  pl.BlockSpec((None, d, f), lambda i, te, na: (te[i], 0, 0)),
                  pl.BlockSpec((None, f, d), lambda i, te, na: (te[i], 0, 0))],
        out_specs=pl.BlockSpec((TM_FFN, d), row_tile),
        scratch_shapes=[pltpu.VMEM((d, 2 * f), BF16), pltpu.VMEM((f, d), BF16)],
    )
    return pl.pallas_call(
        _ffn_kernel,
        grid_spec=grid_spec,
        out_shape=jax.ShapeDtypeStruct((n_rows, d), BF16),
        compiler_params=pltpu.CompilerParams(
            dimension_semantics=("arbitrary",), vmem_limit_bytes=VMEM_LIMIT),
        name="moe_experts",
    )(tile_expert, nact, xs, w_gate, w_up, w_down)


def _combine_kernel(nchunk_ref, gdst_ref, ys_ref, infot_ref, h_ref, wsg_ref, wsu_ref, wsd_ref,
                    x1_ref, mod_ref, gpost_ref, o_ref, buf_ref, sem):
    j = pl.program_id(0)
    last = pl.num_programs(0) - 1
    slot = j % 2

    def chunk_copy(s, c, g):
        return pltpu.make_async_copy(
            ys_ref.at[pl.ds(pl.multiple_of(g * ROW_CHUNK, ROW_CHUNK), ROW_CHUNK), :],
            buf_ref.at[s, pl.ds(pl.multiple_of(c * ROW_CHUNK, ROW_CHUNK), ROW_CHUNK), :],
            sem.at[s])

    def fetch_block(jj, s):
        n = nchunk_ref[jj]

        def issue(c, carry):
            chunk_copy(s, c, gdst_ref[jj * BLOCK_CHUNKS + c]).start()
            return carry
        lax.fori_loop(0, n, issue, 0)

        def clear(c, carry):
            buf_ref[s, pl.ds(pl.multiple_of(c * ROW_CHUNK, ROW_CHUNK), ROW_CHUNK), :] = (
                jnp.zeros((ROW_CHUNK, buf_ref.shape[2]), BF16))
            return carry
        per_step = KC // ROW_CHUNK
        lax.fori_loop(n, (n + per_step - 1) // per_step * per_step, clear, 0)

    @pl.when(j == 0)
    def _():
        fetch_block(0, 0)

    @pl.when(j < last)
    def _():
        fetch_block(j + 1, 1 - slot)

    h = h_ref[...]
    act = _silu(jnp.dot(h, wsg_ref[...], preferred_element_type=F32)) * jnp.dot(
        h, wsu_ref[...], preferred_element_type=F32)
    y = jnp.dot(act.astype(BF16), wsd_ref[...], preferred_element_type=F32)

    def wait_one(c, carry):
        chunk_copy(slot, 0, 0).wait()
        return carry
    lax.fori_loop(0, nchunk_ref[j], wait_one, 0)

    dest = [jnp.broadcast_to(infot_ref[:, k:k + 1], (TB, KC)) for k in range(TOP_K)]
    gate = [jnp.broadcast_to(infot_ref[:, TOP_K + k:TOP_K + k + 1], (TB, KC))
            for k in range(TOP_K)]
    lane = lax.broadcasted_iota(I32, (TB, KC), 1)

    def step(kc, y):
        base = pl.multiple_of(kc * KC, KC)
        rows = (lane + base).astype(F32)
        weights = jnp.zeros((TB, KC), F32)
        for k in range(TOP_K):
            weights = jnp.where(dest[k] == rows, gate[k], weights)
        return y + jnp.dot(weights.astype(BF16), buf_ref[slot, pl.ds(base, KC), :],
                           preferred_element_type=F32)
    n_steps = (nchunk_ref[j] * ROW_CHUNK + KC - 1) // KC
    y = lax.fori_loop(0, n_steps, step, y)

    ga2 = mod_ref[5:6, :]
    o_ref[...] = x1_ref[...] + ga2 * _rms(y, gpost_ref[...])


def _combine(plan, ys, infot, h2_flat, ws_gate, ws_up, ws_down, x1_flat, mod3, g_post, s):
    nchunk, gdst = plan
    t, d = h2_flat.shape
    f = ws_gate.shape[1]
    blocks_per_seq = s // TB
    full = lambda shape: pl.BlockSpec(shape, lambda j, *_: (0,) * len(shape))
    grid_spec = pltpu.PrefetchScalarGridSpec(
        num_scalar_prefetch=2,
        grid=(t // TB,),
        in_specs=[pl.BlockSpec(memory_space=pl.ANY),
                  pl.BlockSpec((TB, 128), lambda j, *_: (j, 0)),
                  pl.BlockSpec((TB, d), lambda j, *_: (j, 0)),
                  full((d, f)), full((d, f)), full((f, d)),
                  pl.BlockSpec((TB, d), lambda j, *_: (j, 0)),
                  pl.BlockSpec((None, 6, d), lambda j, *_: (j // blocks_per_seq, 0, 0)),
                  full((1, d))],
        out_specs=pl.BlockSpec((TB, d), lambda j, *_: (j, 0)),
        scratch_shapes=[pltpu.VMEM((2, BLOCK_ROWS, d), BF16),
                        pltpu.SemaphoreType.DMA((2,))],
    )
    return pl.pallas_call(
        _combine_kernel,
        grid_spec=grid_spec,
        out_shape=jax.ShapeDtypeStruct((t, d), F32),
        compiler_params=pltpu.CompilerParams(
            dimension_semantics=("arbitrary",), vmem_limit_bytes=VMEM_LIMIT),
        name="moe_combine",
    )(nchunk, gdst, ys, infot, h2_flat, ws_gate, ws_up, ws_down, x1_flat, mod3, g_post)


def kernel(x, c, w_ada, b_ada, g_pre_mix, w_in, ln_sgu_g, ln_sgu_b, w_spatial, b_spatial,
           g_branch, w_out, g_post_mix, g_pre_ffn, w_router, router_bias, w_gate, w_up, w_down,
           ws_gate, ws_up, ws_down, g_post_ffn):
    bsz, s, d = x.shape
    t = bsz * s
    nblk = t // TB
    max_rows = t * TOP_K + nblk * N_EXPERTS * (ROW_CHUNK - 1) + N_EXPERTS * (TM_FFN - 1)
    n_tiles = -(-max_rows // TM_FFN)
    row = lambda a: a.reshape(1, -1)
    for l in range(w_ada.shape[0]):
        mod3 = _ada(c, w_ada[l], b_ada[l]).reshape(bsz, 6, d)
        og, q, k, v = _premix(x, mod3, row(g_pre_mix[l]), w_in[l].astype(BF16),
                              row(ln_sgu_g[l]), row(ln_sgu_b[l]), w_spatial[l],
                              b_spatial[l].T, row(g_branch[l, :D_GMLP]))
        osb = _attention(q, k, v)
        x1, h2, logits_t = _postmix(og, osb, x, mod3, row(g_branch[l, D_GMLP:]),
                                    w_out[l].astype(BF16), row(g_post_mix[l]),
                                    row(g_pre_ffn[l]), w_router[l].T)
        info, infot, cnt = _route(logits_t, router_bias[l])
        nchunk, gdst, zn, zbase, tile_expert, nact = _dispatch_plan(
            cnt[:, :, 0].astype(I32), n_tiles)
        h2_flat = h2.reshape(t, d)
        xs = _dispatch((nchunk, gdst, zn, zbase), h2_flat, info, n_tiles * TM_FFN)
        ys = _expert_ffn(tile_expert, nact, xs, w_gate[l], w_up[l], w_down[l])
        out = _combine((nchunk, gdst), ys, infot, h2_flat, ws_gate[l].astype(BF16),
                       ws_up[l].astype(BF16), ws_down[l].astype(BF16), x1.reshape(t, d), mod3,
                       row(g_post_ffn[l]), s)
        x = out.reshape(bsz, s, d)
    return x
```

```python
import jax
import jax.numpy as jnp
from jax import lax
from jax.experimental import pallas as pl
from jax.experimental.pallas import tpu as pltpu

F32 = jnp.float32
BF16 = jnp.bfloat16
I32 = jnp.int32

D_MODEL = 1024
D_GMLP = 512
GMLP_GROUPS = 4
GMLP_BLOCK = 128
CHUNK = 64
D_SB = 512
SB_HEAD_DIM = 64
N_PAIRS = D_SB // 128
N_EXPERTS = 64
N_GROUPS = 8
GROUP_SIZE = N_EXPERTS // N_GROUPS
TOPK_GROUPS = 4
TOP_K = 8
D_EXPERT = 256
ROUTED_SCALE = 2.5
EPS = 1e-6
D_IN = 2 * D_GMLP + 3 * D_SB

TM_MIX = 256
TQ = 128
VMEM_LIMIT = 56 * 1024 * 1024
ATTN_DEAD_LOG = -110.0

TB = 256
ROW_CHUNK = 16
KC = 256
MM_ROWS = 1024
GROUP_CHUNKS = 16
GROUP_ROWS = GROUP_CHUNKS * ROW_CHUNK
BLOCK_ROWS = -(-(TB * TOP_K + N_EXPERTS * (ROW_CHUNK - 1)) // MM_ROWS) * MM_ROWS
BLOCK_CHUNKS = BLOCK_ROWS // ROW_CHUNK
TM_FFN = 1024
ZERO_ROWS = 128


def _rms(x, g):
    return x * lax.rsqrt(jnp.mean(x * x, axis=-1, keepdims=True) + EPS) * g


def _silu(x):
    return x * jax.nn.sigmoid(x)


def _ada_kernel(c_ref, w_ref, b_ref, o_ref):
    o_ref[...] = jnp.dot(_silu(c_ref[...]), w_ref[...], preferred_element_type=F32,
                         precision=lax.Precision.HIGHEST) + b_ref[...]


def _ada(c, w, b):
    bsz, d = c.shape
    n = w.shape[1]
    tn = 512
    return pl.pallas_call(
        _ada_kernel,
        grid=(n // tn,),
        in_specs=[pl.BlockSpec((bsz, d), lambda j: (0, 0)),
                  pl.BlockSpec((d, tn), lambda j: (0, j)),
                  pl.BlockSpec((1, tn), lambda j: (0, j))],
        out_specs=pl.BlockSpec((bsz, tn), lambda j: (0, j)),
        out_shape=jax.ShapeDtypeStruct((bsz, n), F32),
        name="ada_mod",
    )(c, w, b.reshape(1, n))


def _premix_kernel(x_ref, mod_ref, gpre_ref, win_ref, lng_ref, lnb_ref, wsp_ref, bsp_ref,
                   gbr_ref, og_ref, q_ref, k_ref, v_ref):
    x = x_ref[...]
    sh1 = mod_ref[0:1, :]
    sc1 = mod_ref[1:2, :]
    h = _rms(x, gpre_ref[...]) * (1.0 + sc1) + sh1
    proj = jnp.dot(h.astype(BF16), win_ref[...], preferred_element_type=F32)

    u = jax.nn.gelu(proj[:, :D_GMLP])
    v = jax.nn.gelu(proj[:, D_GMLP:2 * D_GMLP])
    mu = jnp.mean(v, axis=-1, keepdims=True)
    var = jnp.mean(jnp.square(v - mu), axis=-1, keepdims=True)
    v = ((v - mu) * lax.rsqrt(var + EPS) * lng_ref[...] + lnb_ref[...]).astype(BF16)

    i = lax.broadcasted_iota(I32, (GMLP_BLOCK, GMLP_BLOCK), 0)
    j = lax.broadcasted_iota(I32, (GMLP_BLOCK, GMLP_BLOCK), 1)
    causal = (j // CHUNK) <= (i // CHUNK)
    gd = D_GMLP // GMLP_GROUPS
    blocks = []
    for nb in range(x.shape[0] // GMLP_BLOCK):
        rows = slice(nb * GMLP_BLOCK, (nb + 1) * GMLP_BLOCK)
        cols = []
        for g in range(GMLP_GROUPS):
            w = jnp.where(causal, wsp_ref[g], 0.0).astype(BF16)
            mixed = jnp.dot(w, v[rows, g * gd:(g + 1) * gd], preferred_element_type=F32)
            cols.append(mixed + bsp_ref[:, g:g + 1])
        blocks.append(u[rows, :] * jnp.concatenate(cols, axis=1))
    og = jnp.concatenate(blocks, axis=0)
    og_ref[...] = _rms(og, gbr_ref[...]).astype(BF16)

    base = 2 * D_GMLP
    scale = SB_HEAD_DIM ** -0.5
    for p in range(N_PAIRS):
        q_ref[p] = (proj[:, base + 128 * p:base + 128 * (p + 1)] * scale).astype(BF16)
        k_ref[p] = proj[:, base + D_SB + 128 * p:base + D_SB + 128 * (p + 1)].astype(BF16)
        v_ref[p] = proj[:, base + 2 * D_SB + 128 * p:base + 2 * D_SB + 128 * (p + 1)].astype(BF16)


def _premix(x, mod3, g_pre, w_in, ln_g, ln_b, w_sp, b_sp_t, g_br):
    bsz, s, d = x.shape
    tm = TM_MIX
    full = lambda shape: pl.BlockSpec(shape, lambda b, i: (0,) * len(shape))
    qkv_spec = pl.BlockSpec((None, N_PAIRS, tm, 128), lambda b, i: (b, 0, i, 0))
    qkv_shape = jax.ShapeDtypeStruct((bsz, N_PAIRS, s, 128), BF16)
    return pl.pallas_call(
        _premix_kernel,
        grid=(bsz, s // tm),
        in_specs=[pl.BlockSpec((None, tm, d), lambda b, i: (b, i, 0)),
                  pl.BlockSpec((None, 6, d), lambda b, i: (b, 0, 0)),
                  full((1, d)), full((d, D_IN)), full((1, D_GMLP)), full((1, D_GMLP)),
                  full((GMLP_GROUPS, GMLP_BLOCK, GMLP_BLOCK)), full((GMLP_BLOCK, GMLP_GROUPS)),
                  full((1, D_GMLP))],
        out_specs=[pl.BlockSpec((None, tm, D_GMLP), lambda b, i: (b, i, 0)),
                   qkv_spec, qkv_spec, qkv_spec],
        out_shape=[jax.ShapeDtypeStruct((bsz, s, D_GMLP), BF16), qkv_shape, qkv_shape, qkv_shape],
        compiler_params=pltpu.CompilerParams(
            dimension_semantics=("parallel", "parallel"), vmem_limit_bytes=VMEM_LIMIT),
        name="premix",
    )(x, mod3, g_pre, w_in, ln_g, ln_b, w_sp, b_sp_t, g_br)


def _attn_kernel(q_ref, k_ref, v_ref, o_ref, qs_ref, acc_ref, carry_ref):
    qi = pl.program_id(1)
    first_head = lax.broadcasted_iota(I32, (TQ, 128), 1) < SB_HEAD_DIM
    for p in range(N_PAIRS):
        q2 = q_ref[p]
        zero = jnp.zeros_like(q2)
        qs_ref[p, :TQ, :] = jnp.where(first_head, q2, zero)
        qs_ref[p, TQ:, :] = jnp.where(first_head, zero, q2)

    r = jnp.bitwise_and(lax.broadcasted_iota(I32, (2 * TQ, TQ), 0), TQ - 1)
    c = lax.broadcasted_iota(I32, (2 * TQ, TQ), 1)
    strict_causal = c < r
    kr = lax.broadcasted_iota(I32, (TQ, TQ), 0)
    kc = lax.broadcasted_iota(I32, (TQ, TQ), 1)
    suffix = jnp.concatenate([(kr > kc).astype(BF16), jnp.ones((TQ, TQ), BF16)], axis=1)

    acc_ref[...] = jnp.zeros_like(acc_ref)
    carry_ref[...] = jnp.zeros_like(carry_ref)

    def key_block(j, diagonal):
        start = pl.multiple_of(j * TQ, TQ)
        live = None
        for p in range(N_PAIRS):
            kj = k_ref[p, pl.ds(start, TQ), :]
            vj = v_ref[p, pl.ds(start, TQ), :]
            z = lax.dot_general(qs_ref[p], kj, (((1,), (1,)), ((), ())),
                                preferred_element_type=F32)
            log_beta = jnp.minimum(z, 0.0) - jnp.log(1.0 + jnp.exp(-jnp.abs(z)))
            log_1mb = log_beta - z
            if diagonal:
                log_1mb = jnp.where(strict_causal, log_1mb, 0.0)
            hi = log_1mb.astype(BF16)
            lo = (log_1mb - hi.astype(F32)).astype(BF16)
            sums = jnp.dot(jnp.concatenate([hi, lo], axis=0), suffix,
                           preferred_element_type=F32)
            sums = sums[:2 * TQ] + sums[2 * TQ:]
            carry = carry_ref[p]
            a = jnp.exp(log_beta + carry + sums[:, :TQ])
            if diagonal:
                a = jnp.where(strict_causal, a, 0.0)
            acc_ref[p] += jnp.dot(a.astype(BF16), vj, preferred_element_type=F32)
            carry = carry + sums[:, TQ:]
            carry_ref[p] = carry
            live = carry if live is None else jnp.maximum(live, carry)
        return jnp.max(live)

    live = key_block(qi, True)

    def cond(state):
        j, live = state
        return jnp.logical_and(j >= 0, live > ATTN_DEAD_LOG)

    def body(state):
        j, _ = state
        return j - 1, key_block(j, False)

    lax.while_loop(cond, body, (qi - 1, live))
    for p in range(N_PAIRS):
        o_ref[:, 128 * p:128 * (p + 1)] = jnp.where(first_head, acc_ref[p, :TQ, :],
                                                    acc_ref[p, TQ:, :])


def _attention(q, k, v):
    bsz, npair, s, _ = q.shape
    kv_spec = pl.BlockSpec((None, npair, s, 128), lambda b, i: (b, 0, 0, 0))
    return pl.pallas_call(
        _attn_kernel,
        grid=(bsz, s // TQ),
        in_specs=[pl.BlockSpec((None, npair, TQ, 128), lambda b, i: (b, 0, i, 0)),
                  kv_spec, kv_spec],
        out_specs=pl.BlockSpec((None, TQ, npair * 128), lambda b, i: (b, i, 0)),
        out_shape=jax.ShapeDtypeStruct((bsz, s, npair * 128), F32),
        scratch_shapes=[pltpu.VMEM((npair, 2 * TQ, 128), BF16),
                        pltpu.VMEM((npair, 2 * TQ, 128), F32),
                        pltpu.VMEM((npair, 2 * TQ, TQ), F32)],
        compiler_params=pltpu.CompilerParams(
            dimension_semantics=("parallel", "parallel"), vmem_limit_bytes=VMEM_LIMIT),
        name="stickbreak_attn",
    )(q, k, v)


def _postmix_kernel(og_ref, osb_ref, x_ref, mod_ref, gbr_ref, wout_ref, gpost_ref, gpre_ref,
                    wrt_ref, x1_ref, h2_ref, logit_ref):
    ga1 = mod_ref[2:3, :]
    sh2 = mod_ref[3:4, :]
    sc2 = mod_ref[4:5, :]
    osb = _rms(osb_ref[...], gbr_ref[...]).astype(BF16)
    m = (jnp.dot(og_ref[...], wout_ref[:D_GMLP, :], preferred_element_type=F32)
         + jnp.dot(osb, wout_ref[D_GMLP:, :], preferred_element_type=F32))
    x1 = x_ref[...] + ga1 * _rms(m, gpost_ref[...])
    x1_ref[...] = x1
    h2 = _rms(x1, gpre_ref[...]) * (1.0 + sc2) + sh2
    h2_ref[...] = h2.astype(BF16)
    logit_ref[...] = lax.dot_general(wrt_ref[...], h2, (((1,), (1,)), ((), ())),
                                     preferred_element_type=F32,
                                     precision=lax.Precision.HIGHEST)


def _postmix(og, osb, x, mod3, g_br, w_out, g_post, g_pre, w_router_t):
    bsz, s, d = x.shape
    tm = TM_MIX
    nt = s // tm
    full = lambda shape: pl.BlockSpec(shape, lambda b, i: (0,) * len(shape))
    return pl.pallas_call(
        _postmix_kernel,
        grid=(bsz, nt),
        in_specs=[pl.BlockSpec((None, tm, D_GMLP), lambda b, i: (b, i, 0)),
                  pl.BlockSpec((None, tm, D_SB), lambda b, i: (b, i, 0)),
                  pl.BlockSpec((None, tm, d), lambda b, i: (b, i, 0)),
                  pl.BlockSpec((None, 6, d), lambda b, i: (b, 0, 0)),
                  full((1, D_SB)), full((d, d)), full((1, d)), full((1, d)),
                  full((N_EXPERTS, d))],
        out_specs=[pl.BlockSpec((None, tm, d), lambda b, i: (b, i, 0)),
                   pl.BlockSpec((None, tm, d), lambda b, i: (b, i, 0)),
                   pl.BlockSpec((N_EXPERTS, tm), lambda b, i: (0, b * nt + i))],
        out_shape=[jax.ShapeDtypeStruct((bsz, s, d), F32),
                   jax.ShapeDtypeStruct((bsz, s, d), BF16),
                   jax.ShapeDtypeStruct((N_EXPERTS, bsz * s), F32)],
        compiler_params=pltpu.CompilerParams(
            dimension_semantics=("parallel", "parallel"), vmem_limit_bytes=VMEM_LIMIT),
        name="postmix",
    )(og, osb, x, mod3, g_br, w_out, g_post, g_pre, w_router_t)


def _first_index_of_max(x, idx, axis, size):
    m = jnp.max(x, axis=axis, keepdims=True)
    return jnp.min(jnp.where(x == m, idx, size), axis=axis, keepdims=True)


def _route_kernel(logit_ref, bias_ref, em_ref, tm_ref, cnt_ref):
    scores = jax.nn.sigmoid(logit_ref[...])
    biased = scores + bias_ref[...]
    neg = jnp.float32(-jnp.inf)

    grouped = biased.reshape(N_GROUPS, GROUP_SIZE, TB)
    within = lax.broadcasted_iota(I32, grouped.shape, 1)
    top1 = jnp.max(grouped, axis=1, keepdims=True)
    first = _first_index_of_max(grouped, within, 1, GROUP_SIZE)
    top2 = jnp.max(jnp.where(within == first, neg, grouped), axis=1, keepdims=True)
    group_score = (top1 + top2).reshape(N_GROUPS, TB)

    gidx = lax.broadcasted_iota(I32, group_score.shape, 0)
    group_on = jnp.zeros(group_score.shape, jnp.bool_)
    for _ in range(TOPK_GROUPS):
        pick = gidx == _first_index_of_max(group_score, gidx, 0, N_GROUPS)
        group_on = jnp.logical_or(group_on, pick)
        group_score = jnp.where(pick, neg, group_score)

    masked = jnp.where(group_on.reshape(N_GROUPS, 1, TB), grouped, neg).reshape(N_EXPERTS, TB)
    eidx = lax.broadcasted_iota(I32, masked.shape, 0)
    chosen = jnp.zeros(masked.shape, jnp.bool_)
    for _ in range(TOP_K):
        pick = eidx == _first_index_of_max(masked, eidx, 0, N_EXPERTS)
        chosen = jnp.logical_or(chosen, pick)
        masked = jnp.where(pick, neg, masked)

    w = jnp.where(chosen, scores, 0.0)
    gates = w / jnp.sum(w, axis=0, keepdims=True) * ROUTED_SCALE

    chosen_f = chosen.astype(F32)
    tr = lax.broadcasted_iota(I32, (TB, TB), 0)
    tc = lax.broadcasted_iota(I32, (TB, TB), 1)
    rank = jnp.dot(chosen_f.astype(BF16), (tr < tc).astype(BF16), preferred_element_type=F32)
    both = jnp.concatenate([jnp.where(chosen, rank, -1.0), gates], axis=0)
    em_ref[...] = both
    tm_ref[...] = both.T
    cnt_ref[...] = jnp.broadcast_to(jnp.sum(chosen_f, axis=1, keepdims=True), (N_EXPERTS, 128))


def _route(logits_t, bias):
    e, t = logits_t.shape
    nblk = t // TB
    return pl.pallas_call(
        _route_kernel,
        grid=(nblk,),
        in_specs=[pl.BlockSpec((e, TB), lambda i: (0, i)),
                  pl.BlockSpec((e, 1), lambda i: (0, 0))],
        out_specs=[pl.BlockSpec((2 * e, TB), lambda i: (0, i)),
                   pl.BlockSpec((TB, 2 * e), lambda i: (i, 0)),
                   pl.BlockSpec((None, e, 128), lambda i: (i, 0, 0))],
        out_shape=[jax.ShapeDtypeStruct((2 * e, t), F32),
                   jax.ShapeDtypeStruct((t, 2 * e), F32),
                   jax.ShapeDtypeStruct((nblk, e, 128), F32)],
        compiler_params=pltpu.CompilerParams(dimension_semantics=("parallel",)),
        name="route",
    )(logits_t, bias.reshape(e, 1))


def _dispatch_plan(cnt, n_tiles):
    nblk, e = cnt.shape
    pc = (cnt + ROW_CHUNK - 1) // ROW_CHUNK * ROW_CHUNK
    start = jnp.cumsum(pc, axis=1) - pc
    nchunk = jnp.sum(pc, axis=1) // ROW_CHUNK
    off = jnp.cumsum(pc, axis=0) - pc
    ecount = jnp.sum(pc, axis=0)
    epad = (ecount + TM_FFN - 1) // TM_FFN * TM_FFN
    gend = jnp.cumsum(epad)
    gbase = gend - epad
    nact = gend[-1] // TM_FFN
    tiles = jnp.minimum(jnp.arange(n_tiles, dtype=I32), nact - 1)
    tile_expert = jnp.minimum(
        jnp.sum((gend[None, :] <= tiles[:, None] * TM_FFN).astype(I32), axis=1), e - 1)
    cidx = jnp.arange(BLOCK_CHUNKS, dtype=I32)
    start16 = start // ROW_CHUNK
    shift = (gbase[None, :] + off) // ROW_CHUNK - start16
    dshift = shift - jnp.pad(shift, ((0, 0), (1, 0)))[:, :-1]
    in_or_after = (start16[:, None, :] <= cidx[None, :, None]).astype(I32)
    where = cidx[None, :] + jnp.sum(in_or_after * dshift[:, None, :], axis=2)
    used = cidx[None, :] < nchunk[:, None]
    spare = (n_tiles * TM_FFN // ROW_CHUNK
             + jnp.arange(nblk, dtype=I32)[:, None] * GROUP_CHUNKS + cidx[None, :] % GROUP_CHUNKS)
    gdst = jnp.where(used, where, spare)
    gsrc = jnp.where(used, where, 0)
    ngrp = (nchunk + GROUP_CHUNKS - 1) // GROUP_CHUNKS
    zbase = (gbase + ecount) // ROW_CHUNK
    zn = (epad - ecount) // ROW_CHUNK

    lo = jnp.tile(start.astype(F32), (1, 2))
    hi = jnp.tile((start + pc).astype(F32), (1, 2))
    bounds_row = jnp.stack([jnp.broadcast_to(lo[:, None, :], (nblk, 8, 2 * e)),
                            jnp.broadcast_to(hi[:, None, :], (nblk, 8, 2 * e))], axis=1)
    bounds_col = jnp.stack([jnp.broadcast_to(lo[:, :, None], (nblk, 2 * e, 128)),
                            jnp.broadcast_to(hi[:, :, None], (nblk, 2 * e, 128))], axis=1)
    as_i32 = lambda a: a.astype(I32)
    return dict(ngrp=as_i32(ngrp), gdst=as_i32(gdst.reshape(-1)), gsrc=as_i32(gsrc.reshape(-1)),
                zn=as_i32(zn), zbase=as_i32(zbase), tile_expert=as_i32(tile_expert),
                nact=as_i32(nact.reshape(1)), bounds_row=bounds_row, bounds_col=bounds_col)


def _dispatch_kernel(ngrp_ref, gdst_ref, zn_ref, zbase_ref, h_ref, em_ref, brow_ref, bcol_ref,
                     xs_ref, buf_ref, p_ref, zero_ref, sem, zsem):
    j = pl.program_id(0)
    last = pl.num_programs(0) - 1
    slot = j % 2

    def chunk_copy(s, c, g):
        return pltpu.make_async_copy(
            buf_ref.at[s, pl.ds(pl.multiple_of(c * ROW_CHUNK, ROW_CHUNK), ROW_CHUNK), :],
            xs_ref.at[pl.ds(pl.multiple_of(g * ROW_CHUNK, ROW_CHUNK), ROW_CHUNK), :],
            sem.at[s])

    def wait_block(jj, s):
        def body(i, carry):
            pltpu.make_async_copy(buf_ref.at[s, pl.ds(0, GROUP_ROWS), :],
                                  xs_ref.at[pl.ds(0, GROUP_ROWS), :], sem.at[s]).wait()
            return carry
        lax.fori_loop(0, ngrp_ref[jj], body, 0)

    def zero_copy(g, rows):
        return pltpu.make_async_copy(
            zero_ref.at[pl.ds(0, rows), :],
            xs_ref.at[pl.ds(pl.multiple_of(g * ROW_CHUNK, ROW_CHUNK), rows), :], zsem.at[0])

    def zero_fill(start):
        big = ZERO_ROWS // ROW_CHUNK

        def per_expert(e, carry):
            n = zn_ref[e]
            base = zbase_ref[e]

            def big_copy(i, c2):
                cp = zero_copy(base + i * big, ZERO_ROWS)
                cp.start() if start else cp.wait()
                return c2
            lax.fori_loop(0, n // big, big_copy, 0)

            def small_copy(i, c2):
                cp = zero_copy(base + n // big * big + i, ROW_CHUNK)
                cp.start() if start else cp.wait()
                return c2
            lax.fori_loop(0, n % big, small_copy, 0)
            return carry
        lax.fori_loop(0, N_EXPERTS, per_expert, 0)

    @pl.when(j == 0)
    def _():
        zero_ref[...] = jnp.zeros_like(zero_ref)
        zero_fill(True)

    @pl.when(j >= 2)
    def _():
        wait_block(j - 2, slot)

    lo = brow_ref[0, 0:1, :]
    hi = brow_ref[1, 0:1, :]
    first_copy = lax.broadcasted_iota(I32, (KC, 2 * N_EXPERTS), 1) < N_EXPERTS
    rank_and_lo = jnp.concatenate([em_ref[...], bcol_ref[0]], axis=1).astype(BF16)
    for kc in range(BLOCK_ROWS // KC):
        rows_e = (lax.broadcasted_iota(I32, (KC, 2 * N_EXPERTS), 0) + kc * KC).astype(F32)
        in_run = jnp.logical_and(jnp.logical_and(rows_e >= lo, rows_e < hi), first_copy)
        sel = jnp.dot(jnp.where(in_run, 1.0, 0.0).astype(BF16), rank_and_lo,
                      preferred_element_type=F32)
        rows_t = (lax.broadcasted_iota(I32, (KC, TB), 0) + kc * KC).astype(F32)
        local = rows_t - jnp.concatenate([sel[:, TB:]] * (TB // 128), axis=1)
        p_ref[kc * KC:(kc + 1) * KC, :] = jnp.where(sel[:, :TB] == local, 1.0, 0.0).astype(BF16)

    h = h_ref[...]
    rows_used = ngrp_ref[j] * GROUP_ROWS
    for m in range(BLOCK_ROWS // MM_ROWS):
        @pl.when(m * MM_ROWS < rows_used)
        def _():
            xs = jnp.dot(p_ref[m * MM_ROWS:(m + 1) * MM_ROWS, :], h, preferred_element_type=F32)
            buf_ref[slot, m * MM_ROWS:(m + 1) * MM_ROWS, :] = xs.astype(BF16)

    def issue_group(gi, carry):
        for k in range(GROUP_CHUNKS):
            c = gi * GROUP_CHUNKS + k
            chunk_copy(slot, c, gdst_ref[j * BLOCK_CHUNKS + c]).start()
        return carry
    lax.fori_loop(0, ngrp_ref[j], issue_group, 0)

    @pl.when(j == last)
    def _():
        wait_block(j, slot)

        @pl.when(j >= 1)
        def _():
            wait_block(j - 1, 1 - slot)

        zero_fill(False)


def _bounds_specs():
    return [pl.BlockSpec((None, 2, 8, 2 * N_EXPERTS), lambda j, *_: (j, 0, 0, 0)),
            pl.BlockSpec((None, 2, 2 * N_EXPERTS, 128), lambda j, *_: (j, 0, 0, 0))]


def _dispatch(plan, h2_flat, em, n_rows):
    t, d = h2_flat.shape
    grid_spec = pltpu.PrefetchScalarGridSpec(
        num_scalar_prefetch=4,
        grid=(t // TB,),
        in_specs=[pl.BlockSpec((TB, d), lambda j, *_: (j, 0)),
                  pl.BlockSpec((2 * N_EXPERTS, TB), lambda j, *_: (0, j))] + _bounds_specs(),
        out_specs=pl.BlockSpec(memory_space=pl.ANY),
        scratch_shapes=[pltpu.VMEM((2, BLOCK_ROWS, d), BF16),
                        pltpu.VMEM((BLOCK_ROWS, TB), BF16),
                        pltpu.VMEM((ZERO_ROWS, d), BF16),
                        pltpu.SemaphoreType.DMA((2,)),
                        pltpu.SemaphoreType.DMA((1,))],
    )
    return pl.pallas_call(
        _dispatch_kernel,
        grid_spec=grid_spec,
        out_shape=jax.ShapeDtypeStruct((n_rows, d), BF16),
        compiler_params=pltpu.CompilerParams(
            dimension_semantics=("arbitrary",), vmem_limit_bytes=VMEM_LIMIT),
        name="moe_dispatch",
    )(plan["ngrp"], plan["gdst"], plan["zn"], plan["zbase"], h2_flat, em,
      plan["bounds_row"], plan["bounds_col"])


def _ffn_kernel(te_ref, nact_ref, x_ref, wg_ref, wu_ref, wd_ref, y_ref, wgu_s, wd_s):
    i = pl.program_id(0)

    @pl.when(i < nact_ref[0])
    def _():
        @pl.when(jnp.logical_or(i == 0, te_ref[i] != te_ref[jnp.maximum(i - 1, 0)]))
        def _():
            wgu_s[:, :D_EXPERT] = wg_ref[...].astype(BF16)
            wgu_s[:, D_EXPERT:] = wu_ref[...].astype(BF16)
            wd_s[...] = wd_ref[...].astype(BF16)

        gu = jnp.dot(x_ref[...], wgu_s[...], preferred_element_type=F32)
        act = _silu(gu[:, :D_EXPERT]) * gu[:, D_EXPERT:]
        y_ref[...] = jnp.dot(act.astype(BF16), wd_s[...], preferred_element_type=F32).astype(BF16)


def _expert_ffn(tile_expert, nact, xs, n_tiles, w_gate, w_up, w_down):
    n_rows, d = n_tiles * TM_FFN, xs.shape[1]
    f = D_EXPERT
    row_tile = lambda i, te, na: (jnp.minimum(i, na[0] - 1), 0)
    grid_spec = pltpu.PrefetchScalarGridSpec(
        num_scalar_prefetch=2,
        grid=(n_tiles,),
        in_specs=[pl.BlockSpec((TM_FFN, d), row_tile),
                  pl.BlockSpec((None, d, f), lambda i, te, na: (te[i], 0, 0)),
                  pl.BlockSpec((None, d, f), lambda i, te, na: (te[i], 0, 0)),
                  pl.BlockSpec((None, f, d), lambda i, te, na: (te[i], 0, 0))],
        out_specs=pl.BlockSpec((TM_FFN, d), row_tile),
        scratch_shapes=[pltpu.VMEM((d, 2 * f), BF16), pltpu.VMEM((f, d), BF16)],
    )
    return pl.pallas_call(
        _ffn_kernel,
        grid_spec=grid_spec,
        out_shape=jax.ShapeDtypeStruct((n_rows, d), BF16),
        compiler_params=pltpu.CompilerParams(
            dimension_semantics=("arbitrary",), vmem_limit_bytes=VMEM_LIMIT),
        name="moe_experts",
    )(tile_expert, nact, xs, w_gate, w_up, w_down)


def _combine_kernel(ngrp_ref, gsrc_ref, ys_ref, tm_ref, brow_ref, bcol_ref, h_ref, wsg_ref,
                    wsu_ref, wsd_ref, x1_ref, mod_ref, gpost_ref, o_ref, buf_ref, p_ref, sem):
    j = pl.program_id(0)
    last = pl.num_programs(0) - 1
    slot = j % 2

    def chunk_copy(s, c, g):
        return pltpu.make_async_copy(
            ys_ref.at[pl.ds(pl.multiple_of(g * ROW_CHUNK, ROW_CHUNK), ROW_CHUNK), :],
            buf_ref.at[s, pl.ds(pl.multiple_of(c * ROW_CHUNK, ROW_CHUNK), ROW_CHUNK), :],
            sem.at[s])

    def fetch_block(jj, s):
        def issue_group(gi, carry):
            for k in range(GROUP_CHUNKS):
                c = gi * GROUP_CHUNKS + k
                chunk_copy(s, c, gsrc_ref[jj * BLOCK_CHUNKS + c]).start()
            return carry
        lax.fori_loop(0, ngrp_ref[jj], issue_group, 0)

    @pl.when(j == 0)
    def _():
        buf_ref[...] = jnp.zeros_like(buf_ref)
        fetch_block(0, 0)

    @pl.when(j < last)
    def _():
        fetch_block(j + 1, 1 - slot)

    h = h_ref[...]
    act = _silu(jnp.dot(h, wsg_ref[...], preferred_element_type=F32)) * jnp.dot(
        h, wsu_ref[...], preferred_element_type=F32)
    y = jnp.dot(act.astype(BF16), wsd_ref[...], preferred_element_type=F32)

    tm = tm_ref[...]
    lane = lax.broadcasted_iota(I32, tm.shape, 1)
    rank_t = jnp.where(lane < N_EXPERTS, tm, 0.0).astype(BF16)
    gate_t = jnp.where(lane < N_EXPERTS, 0.0, tm).astype(BF16)
    lane8 = lax.broadcasted_iota(I32, (8, 2 * N_EXPERTS), 1)
    lo_row = jnp.where(lane8 < N_EXPERTS, brow_ref[0], 0.0).astype(BF16)
    lo_col = bcol_ref[0]
    hi_col = bcol_ref[1]
    for cb in range(BLOCK_ROWS // 128):
        rows = (lax.broadcasted_iota(I32, (2 * N_EXPERTS, 128), 1) + cb * 128).astype(F32)
        in_run = jnp.where(jnp.logical_and(rows >= lo_col, rows < hi_col), 1.0, 0.0).astype(BF16)
        sel_rank = jnp.dot(rank_t, in_run, preferred_element_type=F32)
        sel_gate = jnp.dot(gate_t, in_run, preferred_element_type=F32)
        sel_lo = jnp.dot(lo_row, in_run, preferred_element_type=F32)
        local = rows[0:1, :] - sel_lo[0:1, :]
        p_ref[:, cb * 128:(cb + 1) * 128] = jnp.where(sel_rank == local, sel_gate,
                                                      0.0).astype(BF16)

    def wait_group(gi, carry):
        pltpu.make_async_copy(ys_ref.at[pl.ds(0, GROUP_ROWS), :],
                              buf_ref.at[slot, pl.ds(0, GROUP_ROWS), :], sem.at[slot]).wait()
        return carry
    lax.fori_loop(0, ngrp_ref[j], wait_group, 0)

    y = y + jnp.dot(p_ref[...], buf_ref[slot], preferred_element_type=F32)

    ga2 = mod_ref[5:6, :]
    o_ref[...] = x1_ref[...] + ga2 * _rms(y, gpost_ref[...])


def _combine(plan, ys, tm, h2_flat, ws_gate, ws_up, ws_down, x1_flat, mod3, g_post, s):
    t, d = h2_flat.shape
    f = ws_gate.shape[1]
    blocks_per_seq = s // TB
    full = lambda shape: pl.BlockSpec(shape, lambda j, *_: (0,) * len(shape))
    grid_spec = pltpu.PrefetchScalarGridSpec(
        num_scalar_prefetch=2,
        grid=(t // TB,),
        in_specs=[pl.BlockSpec(memory_space=pl.ANY),
                  pl.BlockSpec((TB, 2 * N_EXPERTS), lambda j, *_: (j, 0))] + _bounds_specs() + [
                  pl.BlockSpec((TB, d), lambda j, *_: (j, 0)),
                  full((d, f)), full((d, f)), full((f, d)),
                  pl.BlockSpec((TB, d), lambda j, *_: (j, 0)),
                  pl.BlockSpec((None, 6, d), lambda j, *_: (j // blocks_per_seq, 0, 0)),
                  full((1, d))],
        out_specs=pl.BlockSpec((TB, d), lambda j, *_: (j, 0)),
        scratch_shapes=[pltpu.VMEM((2, BLOCK_ROWS, d), BF16),
                        pltpu.VMEM((TB, BLOCK_ROWS), BF16),
                        pltpu.SemaphoreType.DMA((2,))],
    )
    return pl.pallas_call(
        _combine_kernel,
        grid_spec=grid_spec,
        out_shape=jax.ShapeDtypeStruct((t, d), F32),
        compiler_params=pltpu.CompilerParams(
            dimension_semantics=("arbitrary",), vmem_limit_bytes=VMEM_LIMIT),
        name="moe_combine",
    )(plan["ngrp"], plan["gsrc"], ys, tm, plan["bounds_row"], plan["bounds_col"], h2_flat,
      ws_gate, ws_up, ws_down, x1_flat, mod3, g_post)


def kernel(x, c, w_ada, b_ada, g_pre_mix, w_in, ln_sgu_g, ln_sgu_b, w_spatial, b_spatial,
           g_branch, w_out, g_post_mix, g_pre_ffn, w_router, router_bias, w_gate, w_up, w_down,
           ws_gate, ws_up, ws_down, g_post_ffn):
    bsz, s, d = x.shape
    t = bsz * s
    nblk = t // TB
    max_rows = t * TOP_K + nblk * N_EXPERTS * (ROW_CHUNK - 1) + N_EXPERTS * (TM_FFN - 1)
    n_tiles = -(-max_rows // TM_FFN)
    row = lambda a: a.reshape(1, -1)
    for l in range(w_ada.shape[0]):
        mod3 = _ada(c, w_ada[l], b_ada[l]).reshape(bsz, 6, d)
        og, q, k, v = _premix(x, mod3, row(g_pre_mix[l]), w_in[l].astype(BF16),
                              row(ln_sgu_g[l]), row(ln_sgu_b[l]), w_spatial[l],
                              b_spatial[l].T, row(g_branch[l, :D_GMLP]))
        osb = _attention(q, k, v)
        x1, h2, logits_t = _postmix(og, osb, x, mod3, row(g_branch[l, D_GMLP:]),
                                    w_out[l].astype(BF16), row(g_post_mix[l]),
                                    row(g_pre_ffn[l]), w_router[l].T)
        em, tm, cnt = _route(logits_t, router_bias[l])
        plan = _dispatch_plan(cnt[:, :, 0].astype(I32), n_tiles)
        h2_flat = h2.reshape(t, d)
        xs = _dispatch(plan, h2_flat, em, n_tiles * TM_FFN + nblk * GROUP_ROWS)
        ys = _expert_ffn(plan["tile_expert"], plan["nact"], xs, n_tiles,
                         w_gate[l], w_up[l], w_down[l])
        out = _combine(plan, ys, tm, h2_flat, ws_gate[l].astype(BF16), ws_up[l].astype(BF16),
                       ws_down[l].astype(BF16), x1.reshape(t, d), mod3, row(g_post_ffn[l]), s)
        x = out.reshape(bsz, s, d)
    return x
```

```python
import jax
import jax.numpy as jnp
from jax import lax
from jax.experimental import pallas as pl
from jax.experimental.pallas import tpu as pltpu

F32 = jnp.float32
BF16 = jnp.bfloat16
I32 = jnp.int32

D_MODEL = 1024
D_GMLP = 512
GMLP_GROUPS = 4
GMLP_BLOCK = 128
CHUNK = 64
D_SB = 512
SB_HEAD_DIM = 64
N_PAIRS = D_SB // 128
N_EXPERTS = 64
N_GROUPS = 8
GROUP_SIZE = N_EXPERTS // N_GROUPS
TOPK_GROUPS = 4
TOP_K = 8
D_EXPERT = 256
ROUTED_SCALE = 2.5
EPS = 1e-6
D_IN = 2 * D_GMLP + 3 * D_SB

TM_MIX = 256
TQ = 128
VMEM_LIMIT = 56 * 1024 * 1024
ATTN_DEAD_LOG = -110.0

TB = 256
ROW_CHUNK = 16
KC = 256
MM_ROWS = 1024
GROUP_CHUNKS = 16
GROUP_ROWS = GROUP_CHUNKS * ROW_CHUNK
BLOCK_ROWS = -(-(TB * TOP_K + N_EXPERTS * (ROW_CHUNK - 1)) // MM_ROWS) * MM_ROWS
BLOCK_CHUNKS = BLOCK_ROWS // ROW_CHUNK
TM_FFN = 1024
ZERO_ROWS = 128


def _rms(x, g):
    return x * lax.rsqrt(jnp.mean(x * x, axis=-1, keepdims=True) + EPS) * g


def _silu(x):
    return x * jax.nn.sigmoid(x)


def _ada_kernel(c_ref, w_ref, b_ref, o_ref):
    o_ref[...] = jnp.dot(_silu(c_ref[...]), w_ref[...], preferred_element_type=F32,
                         precision=lax.Precision.HIGHEST) + b_ref[...]


def _ada(c, w, b):
    bsz, d = c.shape
    n = w.shape[1]
    tn = 512
    return pl.pallas_call(
        _ada_kernel,
        grid=(n // tn,),
        in_specs=[pl.BlockSpec((bsz, d), lambda j: (0, 0)),
                  pl.BlockSpec((d, tn), lambda j: (0, j)),
                  pl.BlockSpec((1, tn), lambda j: (0, j))],
        out_specs=pl.BlockSpec((bsz, tn), lambda j: (0, j)),
        out_shape=jax.ShapeDtypeStruct((bsz, n), F32),
        name="ada_mod",
    )(c, w, b.reshape(1, n))


def _premix_kernel(x_ref, mod_ref, gpre_ref, win_ref, lng_ref, lnb_ref, wsp_ref, bsp_ref,
                   gbr_ref, og_ref, q_ref, k_ref, v_ref):
    x = x_ref[...]
    sh1 = mod_ref[0:1, :]
    sc1 = mod_ref[1:2, :]
    h = _rms(x, gpre_ref[...]) * (1.0 + sc1) + sh1
    proj = jnp.dot(h.astype(BF16), win_ref[...], preferred_element_type=F32)

    u = jax.nn.gelu(proj[:, :D_GMLP])
    v = jax.nn.gelu(proj[:, D_GMLP:2 * D_GMLP])
    mu = jnp.mean(v, axis=-1, keepdims=True)
    var = jnp.mean(jnp.square(v - mu), axis=-1, keepdims=True)
    v = ((v - mu) * lax.rsqrt(var + EPS) * lng_ref[...] + lnb_ref[...]).astype(BF16)

    i = lax.broadcasted_iota(I32, (GMLP_BLOCK, GMLP_BLOCK), 0)
    j = lax.broadcasted_iota(I32, (GMLP_BLOCK, GMLP_BLOCK), 1)
    causal = (j // CHUNK) <= (i // CHUNK)
    gd = D_GMLP // GMLP_GROUPS
    blocks = []
    for nb in range(x.shape[0] // GMLP_BLOCK):
        rows = slice(nb * GMLP_BLOCK, (nb + 1) * GMLP_BLOCK)
        cols = []
        for g in range(GMLP_GROUPS):
            w = jnp.where(causal, wsp_ref[g], 0.0).astype(BF16)
            mixed = jnp.dot(w, v[rows, g * gd:(g + 1) * gd], preferred_element_type=F32)
            cols.append(mixed + bsp_ref[:, g:g + 1])
        blocks.append(u[rows, :] * jnp.concatenate(cols, axis=1))
    og = jnp.concatenate(blocks, axis=0)
    og_ref[...] = _rms(og, gbr_ref[...]).astype(BF16)

    base = 2 * D_GMLP
    scale = SB_HEAD_DIM ** -0.5
    for p in range(N_PAIRS):
        q_ref[p] = (proj[:, base + 128 * p:base + 128 * (p + 1)] * scale).astype(BF16)
        k_ref[p] = proj[:, base + D_SB + 128 * p:base + D_SB + 128 * (p + 1)].astype(BF16)
        v_ref[p] = proj[:, base + 2 * D_SB + 128 * p:base + 2 * D_SB + 128 * (p + 1)].astype(BF16)


def _premix(x, mod3, g_pre, w_in, ln_g, ln_b, w_sp, b_sp_t, g_br):
    bsz, s, d = x.shape
    tm = TM_MIX
    full = lambda shape: pl.BlockSpec(shape, lambda b, i: (0,) * len(shape))
    qkv_spec = pl.BlockSpec((None, N_PAIRS, tm, 128), lambda b, i: (b, 0, i, 0))
    qkv_shape = jax.ShapeDtypeStruct((bsz, N_PAIRS, s, 128), BF16)
    return pl.pallas_call(
        _premix_kernel,
        grid=(bsz, s // tm),
        in_specs=[pl.BlockSpec((None, tm, d), lambda b, i: (b, i, 0)),
                  pl.BlockSpec((None, 6, d), lambda b, i: (b, 0, 0)),
                  full((1, d)), full((d, D_IN)), full((1, D_GMLP)), full((1, D_GMLP)),
                  full((GMLP_GROUPS, GMLP_BLOCK, GMLP_BLOCK)), full((GMLP_BLOCK, GMLP_GROUPS)),
                  full((1, D_GMLP))],
        out_specs=[pl.BlockSpec((None, tm, D_GMLP), lambda b, i: (b, i, 0)),
                   qkv_spec, qkv_spec, qkv_spec],
        out_shape=[jax.ShapeDtypeStruct((bsz, s, D_GMLP), BF16), qkv_shape, qkv_shape, qkv_shape],
        compiler_params=pltpu.CompilerParams(
            dimension_semantics=("parallel", "parallel"), vmem_limit_bytes=VMEM_LIMIT),
        name="premix",
    )(x, mod3, g_pre, w_in, ln_g, ln_b, w_sp, b_sp_t, g_br)


def _attn_kernel(q_ref, k_ref, v_ref, o_ref, *scratch):
    qs_refs = scratch[:N_PAIRS]
    acc_refs = scratch[N_PAIRS:2 * N_PAIRS]
    carry_refs = scratch[2 * N_PAIRS:]
    qi = pl.program_id(1)
    first_head = lax.broadcasted_iota(I32, (TQ, 128), 1) < SB_HEAD_DIM
    for p in range(N_PAIRS):
        q2 = q_ref[p]
        zero = jnp.zeros_like(q2)
        qs_refs[p][:TQ, :] = jnp.where(first_head, q2, zero)
        qs_refs[p][TQ:, :] = jnp.where(first_head, zero, q2)
        acc_refs[p][...] = jnp.zeros_like(acc_refs[p])
        carry_refs[p][...] = jnp.zeros_like(carry_refs[p])

    r = jnp.bitwise_and(lax.broadcasted_iota(I32, (2 * TQ, TQ), 0), TQ - 1)
    c = lax.broadcasted_iota(I32, (2 * TQ, TQ), 1)
    strict_causal = c < r
    kr = lax.broadcasted_iota(I32, (TQ, TQ), 0)
    kc = lax.broadcasted_iota(I32, (TQ, TQ), 1)
    suffix = jnp.concatenate([(kr > kc).astype(BF16), jnp.ones((TQ, TQ), BF16)], axis=1)
    suffix2 = jnp.concatenate([suffix, suffix], axis=0)

    def key_block(j, diagonal):
        start = pl.multiple_of(j * TQ, TQ)
        pairs = range(N_PAIRS)
        zs = [lax.dot_general(qs_refs[p][...], k_ref[p, pl.ds(start, TQ), :],
                              (((1,), (1,)), ((), ())), preferred_element_type=F32)
              for p in pairs]
        log_betas, splits = [], []
        for p in pairs:
            z = zs[p]
            log_beta = jnp.minimum(z, 0.0) - jnp.log(1.0 + jnp.exp(-jnp.abs(z)))
            log_1mb = log_beta - z
            if diagonal:
                log_1mb = jnp.where(strict_causal, log_1mb, 0.0)
            hi = log_1mb.astype(BF16)
            lo = (log_1mb - hi.astype(F32)).astype(BF16)
            log_betas.append(log_beta)
            splits.append(jnp.concatenate([hi, lo], axis=1))
        sums = [jnp.dot(splits[p], suffix2, preferred_element_type=F32) for p in pairs]
        weights = []
        live = None
        for p in pairs:
            s = sums[p]
            carry = carry_refs[p][...]
            a = jnp.exp(log_betas[p] + carry + s[:, :TQ])
            if diagonal:
                a = jnp.where(strict_causal, a, 0.0)
            weights.append(a.astype(BF16))
            carry = carry + s[:, TQ:]
            carry_refs[p][...] = carry
            live = carry if live is None else jnp.maximum(live, carry)
        for p in pairs:
            acc_refs[p][...] += jnp.dot(weights[p], v_ref[p, pl.ds(start, TQ), :],
                                        preferred_element_type=F32)
        return jnp.max(live)

    live = key_block(qi, True)

    def cond(state):
        j, live = state
        return jnp.logical_and(j >= 0, live > ATTN_DEAD_LOG)

    def body(state):
        j, _ = state
        return j - 1, key_block(j, False)

    lax.while_loop(cond, body, (qi - 1, live))
    for p in range(N_PAIRS):
        o_ref[:, 128 * p:128 * (p + 1)] = jnp.where(first_head, acc_refs[p][:TQ, :],
                                                    acc_refs[p][TQ:, :])


def _attention(q, k, v):
    bsz, npair, s, _ = q.shape
    kv_spec = pl.BlockSpec((None, npair, s, 128), lambda b, i: (b, 0, 0, 0))
    return pl.pallas_call(
        _attn_kernel,
        grid=(bsz, s // TQ),
        in_specs=[pl.BlockSpec((None, npair, TQ, 128), lambda b, i: (b, 0, i, 0)),
                  kv_spec, kv_spec],
        out_specs=pl.BlockSpec((None, TQ, npair * 128), lambda b, i: (b, i, 0)),
        out_shape=jax.ShapeDtypeStruct((bsz, s, npair * 128), F32),
        scratch_shapes=([pltpu.VMEM((2 * TQ, 128), BF16)] * npair
                        + [pltpu.VMEM((2 * TQ, 128), F32)] * npair
                        + [pltpu.VMEM((2 * TQ, TQ), F32)] * npair),
        compiler_params=pltpu.CompilerParams(
            dimension_semantics=("parallel", "parallel"), vmem_limit_bytes=VMEM_LIMIT),
        name="stickbreak_attn",
    )(q, k, v)


def _postmix_kernel(og_ref, osb_ref, x_ref, mod_ref, gbr_ref, wout_ref, gpost_ref, gpre_ref,
                    wrt_ref, x1_ref, h2_ref, logit_ref):
    ga1 = mod_ref[2:3, :]
    sh2 = mod_ref[3:4, :]
    sc2 = mod_ref[4:5, :]
    osb = _rms(osb_ref[...], gbr_ref[...]).astype(BF16)
    m = (jnp.dot(og_ref[...], wout_ref[:D_GMLP, :], preferred_element_type=F32)
         + jnp.dot(osb, wout_ref[D_GMLP:, :], preferred_element_type=F32))
    x1 = x_ref[...] + ga1 * _rms(m, gpost_ref[...])
    x1_ref[...] = x1
    h2 = _rms(x1, gpre_ref[...]) * (1.0 + sc2) + sh2
    h2_ref[...] = h2.astype(BF16)
    logit_ref[...] = lax.dot_general(wrt_ref[...], h2, (((1,), (1,)), ((), ())),
                                     preferred_element_type=F32,
                                     precision=lax.Precision.HIGHEST)


def _postmix(og, osb, x, mod3, g_br, w_out, g_post, g_pre, w_router_t):
    bsz, s, d = x.shape
    tm = TM_MIX
    nt = s // tm
    full = lambda shape: pl.BlockSpec(shape, lambda b, i: (0,) * len(shape))
    return pl.pallas_call(
        _postmix_kernel,
        grid=(bsz, nt),
        in_specs=[pl.BlockSpec((None, tm, D_GMLP), lambda b, i: (b, i, 0)),
                  pl.BlockSpec((None, tm, D_SB), lambda b, i: (b, i, 0)),
                  pl.BlockSpec((None, tm, d), lambda b, i: (b, i, 0)),
                  pl.BlockSpec((None, 6, d), lambda b, i: (b, 0, 0)),
                  full((1, D_SB)), full((d, d)), full((1, d)), full((1, d)),
                  full((N_EXPERTS, d))],
        out_specs=[pl.BlockSpec((None, tm, d), lambda b, i: (b, i, 0)),
                   pl.BlockSpec((None, tm, d), lambda b, i: (b, i, 0)),
                   pl.BlockSpec((N_EXPERTS, tm), lambda b, i: (0, b * nt + i))],
        out_shape=[jax.ShapeDtypeStruct((bsz, s, d), F32),
                   jax.ShapeDtypeStruct((bsz, s, d), BF16),
                   jax.ShapeDtypeStruct((N_EXPERTS, bsz * s), F32)],
        compiler_params=pltpu.CompilerParams(
            dimension_semantics=("parallel", "parallel"), vmem_limit_bytes=VMEM_LIMIT),
        name="postmix",
    )(og, osb, x, mod3, g_br, w_out, g_post, g_pre, w_router_t)


def _first_index_of_max(x, idx, axis, size):
    m = jnp.max(x, axis=axis, keepdims=True)
    return jnp.min(jnp.where(x == m, idx, size), axis=axis, keepdims=True)


def _route_kernel(logit_ref, bias_ref, em_ref, tm_ref, cnt_ref):
    scores = jax.nn.sigmoid(logit_ref[...])
    biased = scores + bias_ref[...]
    neg = jnp.float32(-jnp.inf)

    grouped = biased.reshape(N_GROUPS, GROUP_SIZE, TB)
    within = lax.broadcasted_iota(I32, grouped.shape, 1)
    top1 = jnp.max(grouped, axis=1, keepdims=True)
    first = _first_index_of_max(grouped, within, 1, GROUP_SIZE)
    top2 = jnp.max(jnp.where(within == first, neg, grouped), axis=1, keepdims=True)
    group_score = (top1 + top2).reshape(N_GROUPS, TB)

    gidx = lax.broadcasted_iota(I32, group_score.shape, 0)
    group_on = jnp.zeros(group_score.shape, jnp.bool_)
    for _ in range(TOPK_GROUPS):
        pick = gidx == _first_index_of_max(group_score, gidx, 0, N_GROUPS)
        group_on = jnp.logical_or(group_on, pick)
        group_score = jnp.where(pick, neg, group_score)

    masked = jnp.where(group_on.reshape(N_GROUPS, 1, TB), grouped, neg).reshape(N_EXPERTS, TB)
    eidx = lax.broadcasted_iota(I32, masked.shape, 0)
    chosen = jnp.zeros(masked.shape, jnp.bool_)
    for _ in range(TOP_K):
        pick = eidx == _first_index_of_max(masked, eidx, 0, N_EXPERTS)
        chosen = jnp.logical_or(chosen, pick)
        masked = jnp.where(pick, neg, masked)

    w = jnp.where(chosen, scores, 0.0)
    gates = w / jnp.sum(w, axis=0, keepdims=True) * ROUTED_SCALE

    chosen_f = chosen.astype(F32)
    tr = lax.broadcasted_iota(I32, (TB, TB), 0)
    tc = lax.broadcasted_iota(I32, (TB, TB), 1)
    rank = jnp.dot(chosen_f.astype(BF16), (tr < tc).astype(BF16), preferred_element_type=F32)
    both = jnp.concatenate([jnp.where(chosen, rank, -1.0), gates], axis=0)
    em_ref[...] = both
    tm_ref[...] = both.T
    cnt_ref[...] = jnp.broadcast_to(jnp.sum(chosen_f, axis=1, keepdims=True), (N_EXPERTS, 128))


def _route(logits_t, bias):
    e, t = logits_t.shape
    nblk = t // TB
    return pl.pallas_call(
        _route_kernel,
        grid=(nblk,),
        in_specs=[pl.BlockSpec((e, TB), lambda i: (0, i)),
                  pl.BlockSpec((e, 1), lambda i: (0, 0))],
        out_specs=[pl.BlockSpec((2 * e, TB), lambda i: (0, i)),
                   pl.BlockSpec((TB, 2 * e), lambda i: (i, 0)),
                   pl.BlockSpec((None, e, 128), lambda i: (i, 0, 0))],
        out_shape=[jax.ShapeDtypeStruct((2 * e, t), F32),
                   jax.ShapeDtypeStruct((t, 2 * e), F32),
                   jax.ShapeDtypeStruct((nblk, e, 128), F32)],
        compiler_params=pltpu.CompilerParams(dimension_semantics=("parallel",)),
        name="route",
    )(logits_t, bias.reshape(e, 1))


def _dispatch_plan(cnt, n_tiles):
    nblk, e = cnt.shape
    pc = (cnt + ROW_CHUNK - 1) // ROW_CHUNK * ROW_CHUNK
    start = jnp.cumsum(pc, axis=1) - pc
    nchunk = jnp.sum(pc, axis=1) // ROW_CHUNK
    off = jnp.cumsum(pc, axis=0) - pc
    ecount = jnp.sum(pc, axis=0)
    epad = (ecount + TM_FFN - 1) // TM_FFN * TM_FFN
    gend = jnp.cumsum(epad)
    gbase = gend - epad
    nact = gend[-1] // TM_FFN
    tiles = jnp.minimum(jnp.arange(n_tiles, dtype=I32), nact - 1)
    tile_expert = jnp.minimum(
        jnp.sum((gend[None, :] <= tiles[:, None] * TM_FFN).astype(I32), axis=1), e - 1)
    cidx = jnp.arange(BLOCK_CHUNKS, dtype=I32)
    start16 = start // ROW_CHUNK
    shift = (gbase[None, :] + off) // ROW_CHUNK - start16
    dshift = shift - jnp.pad(shift, ((0, 0), (1, 0)))[:, :-1]
    in_or_after = (start16[:, None, :] <= cidx[None, :, None]).astype(I32)
    where = cidx[None, :] + jnp.sum(in_or_after * dshift[:, None, :], axis=2)
    used = cidx[None, :] < nchunk[:, None]
    spare = (n_tiles * TM_FFN // ROW_CHUNK
             + jnp.arange(nblk, dtype=I32)[:, None] * GROUP_CHUNKS + cidx[None, :] % GROUP_CHUNKS)
    gdst = jnp.where(used, where, spare)
    gsrc = jnp.where(used, where, 0)
    ngrp = (nchunk + GROUP_CHUNKS - 1) // GROUP_CHUNKS
    zbase = (gbase + ecount) // ROW_CHUNK
    zn = (epad - ecount) // ROW_CHUNK

    lo = jnp.tile(start.astype(F32), (1, 2))
    hi = jnp.tile((start + pc).astype(F32), (1, 2))
    bounds_row = jnp.stack([jnp.broadcast_to(lo[:, None, :], (nblk, 8, 2 * e)),
                            jnp.broadcast_to(hi[:, None, :], (nblk, 8, 2 * e))], axis=1)
    bounds_col = jnp.stack([jnp.broadcast_to(lo[:, :, None], (nblk, 2 * e, 128)),
                            jnp.broadcast_to(hi[:, :, None], (nblk, 2 * e, 128))], axis=1)
    as_i32 = lambda a: a.astype(I32)
    return dict(ngrp=as_i32(ngrp), gdst=as_i32(gdst.reshape(-1)), gsrc=as_i32(gsrc.reshape(-1)),
                zn=as_i32(zn), zbase=as_i32(zbase), tile_expert=as_i32(tile_expert),
                nact=as_i32(nact.reshape(1)), bounds_row=bounds_row, bounds_col=bounds_col)


def _dispatch_kernel(ngrp_ref, gdst_ref, zn_ref, zbase_ref, h_ref, em_ref, brow_ref, bcol_ref,
                     xs_ref, buf_ref, p_ref, zero_ref, sem, zsem):
    j = pl.program_id(0)
    last = pl.num_programs(0) - 1
    slot = j % 2

    def chunk_copy(s, c, g):
        return pltpu.make_async_copy(
            buf_ref.at[s, pl.ds(pl.multiple_of(c * ROW_CHUNK, ROW_CHUNK), ROW_CHUNK), :],
            xs_ref.at[pl.ds(pl.multiple_of(g * ROW_CHUNK, ROW_CHUNK), ROW_CHUNK), :],
            sem.at[s])

    def wait_block(jj, s):
        def body(i, carry):
            pltpu.make_async_copy(buf_ref.at[s, pl.ds(0, GROUP_ROWS), :],
                                  xs_ref.at[pl.ds(0, GROUP_ROWS), :], sem.at[s]).wait()
            return carry
        lax.fori_loop(0, ngrp_ref[jj], body, 0)

    def zero_copy(g, rows):
        return pltpu.make_async_copy(
            zero_ref.at[pl.ds(0, rows), :],
            xs_ref.at[pl.ds(pl.multiple_of(g * ROW_CHUNK, ROW_CHUNK), rows), :], zsem.at[0])

    def zero_fill(start):
        big = ZERO_ROWS // ROW_CHUNK

        def per_expert(e, carry):
            n = zn_ref[e]
            base = zbase_ref[e]

            def big_copy(i, c2):
                cp = zero_copy(base + i * big, ZERO_ROWS)
                cp.start() if start else cp.wait()
                return c2
            lax.fori_loop(0, n // big, big_copy, 0)

            def small_copy(i, c2):
                cp = zero_copy(base + n // big * big + i, ROW_CHUNK)
                cp.start() if start else cp.wait()
                return c2
            lax.fori_loop(0, n % big, small_copy, 0)
            return carry
        lax.fori_loop(0, N_EXPERTS, per_expert, 0)

    @pl.when(j == 0)
    def _():
        zero_ref[...] = jnp.zeros_like(zero_ref)
        zero_fill(True)

    @pl.when(j >= 2)
    def _():
        wait_block(j - 2, slot)

    lo = brow_ref[0, 0:1, :]
    hi = brow_ref[1, 0:1, :]
    first_copy = lax.broadcasted_iota(I32, (KC, 2 * N_EXPERTS), 1) < N_EXPERTS
    rank_and_lo = jnp.concatenate([em_ref[...], bcol_ref[0]], axis=1).astype(BF16)
    for kc in range(BLOCK_ROWS // KC):
        rows_e = (lax.broadcasted_iota(I32, (KC, 2 * N_EXPERTS), 0) + kc * KC).astype(F32)
        in_run = jnp.logical_and(jnp.logical_and(rows_e >= lo, rows_e < hi), first_copy)
        sel = jnp.dot(jnp.where(in_run, 1.0, 0.0).astype(BF16), rank_and_lo,
                      preferred_element_type=F32)
        rows_t = (lax.broadcasted_iota(I32, (KC, TB), 0) + kc * KC).astype(F32)
        local = rows_t - jnp.concatenate([sel[:, TB:]] * (TB // 128), axis=1)
        p_ref[kc * KC:(kc + 1) * KC, :] = jnp.where(sel[:, :TB] == local, 1.0, 0.0).astype(BF16)

    h = h_ref[...]
    rows_used = ngrp_ref[j] * GROUP_ROWS
    for m in range(BLOCK_ROWS // MM_ROWS):
        @pl.when(m * MM_ROWS < rows_used)
        def _():
            xs = jnp.dot(p_ref[m * MM_ROWS:(m + 1) * MM_ROWS, :], h, preferred_element_type=F32)
            buf_ref[slot, m * MM_ROWS:(m + 1) * MM_ROWS, :] = xs.astype(BF16)

    def issue_group(gi, carry):
        for k in range(GROUP_CHUNKS):
            c = gi * GROUP_CHUNKS + k
            chunk_copy(slot, c, gdst_ref[j * BLOCK_CHUNKS + c]).start()
        return carry
    lax.fori_loop(0, ngrp_ref[j], issue_group, 0)

    @pl.when(j == last)
    def _():
        wait_block(j, slot)

        @pl.when(j >= 1)
        def _():
            wait_block(j - 1, 1 - slot)

        zero_fill(False)


def _bounds_specs():
    return [pl.BlockSpec((None, 2, 8, 2 * N_EXPERTS), lambda j, *_: (j, 0, 0, 0)),
            pl.BlockSpec((None, 2, 2 * N_EXPERTS, 128), lambda j, *_: (j, 0, 0, 0))]


def _dispatch(plan, h2_flat, em, n_rows):
    t, d = h2_flat.shape
    grid_spec = pltpu.PrefetchScalarGridSpec(
        num_scalar_prefetch=4,
        grid=(t // TB,),
        in_specs=[pl.BlockSpec((TB, d), lambda j, *_: (j, 0)),
                  pl.BlockSpec((2 * N_EXPERTS, TB), lambda j, *_: (0, j))] + _bounds_specs(),
        out_specs=pl.BlockSpec(memory_space=pl.ANY),
        scratch_shapes=[pltpu.VMEM((2, BLOCK_ROWS, d), BF16),
                        pltpu.VMEM((BLOCK_ROWS, TB), BF16),
                        pltpu.VMEM((ZERO_ROWS, d), BF16),
                        pltpu.SemaphoreType.DMA((2,)),
                        pltpu.SemaphoreType.DMA((1,))],
    )
    return pl.pallas_call(
        _dispatch_kernel,
        grid_spec=grid_spec,
        out_shape=jax.ShapeDtypeStruct((n_rows, d), BF16),
        compiler_params=pltpu.CompilerParams(
            dimension_semantics=("arbitrary",), vmem_limit_bytes=VMEM_LIMIT),
        name="moe_dispatch",
    )(plan["ngrp"], plan["gdst"], plan["zn"], plan["zbase"], h2_flat, em,
      plan["bounds_row"], plan["bounds_col"])


def _ffn_kernel(te_ref, nact_ref, x_ref, wg_ref, wu_ref, wd_ref, y_ref, wgu_s, wd_s):
    i = pl.program_id(0)

    @pl.when(i < nact_ref[0])
    def _():
        @pl.when(jnp.logical_or(i == 0, te_ref[i] != te_ref[jnp.maximum(i - 1, 0)]))
        def _():
            wgu_s[:, :D_EXPERT] = wg_ref[...].astype(BF16)
            wgu_s[:, D_EXPERT:] = wu_ref[...].astype(BF16)
            wd_s[...] = wd_ref[...].astype(BF16)

        gu = jnp.dot(x_ref[...], wgu_s[...], preferred_element_type=F32)
        act = _silu(gu[:, :D_EXPERT]) * gu[:, D_EXPERT:]
        y_ref[...] = jnp.dot(act.astype(BF16), wd_s[...], preferred_element_type=F32).astype(BF16)


def _expert_ffn(tile_expert, nact, xs, n_tiles, w_gate, w_up, w_down):
    n_rows, d = n_tiles * TM_FFN, xs.shape[1]
    f = D_EXPERT
    row_tile = lambda i, te, na: (jnp.minimum(i, na[0] - 1), 0)
    grid_spec = pltpu.PrefetchScalarGridSpec(
        num_scalar_prefetch=2,
        grid=(n_tiles,),
        in_specs=[pl.BlockSpec((TM_FFN, d), row_tile),
                  pl.BlockSpec((None, d, f), lambda i, te, na: (te[i], 0, 0)),
                  pl.BlockSpec((None, d, f), lambda i, te, na: (te[i], 0, 0)),
                  pl.BlockSpec((None, f, d), lambda i, te, na: (te[i], 0, 0))],
        out_specs=pl.BlockSpec((TM_FFN, d), row_tile),
        scratch_shapes=[pltpu.VMEM((d, 2 * f), BF16), pltpu.VMEM((f, d), BF16)],
    )
    return pl.pallas_call(
        _ffn_kernel,
        grid_spec=grid_spec,
        out_shape=jax.ShapeDtypeStruct((n_rows, d), BF16),
        compiler_params=pltpu.CompilerParams(
            dimension_semantics=("arbitrary",), vmem_limit_bytes=VMEM_LIMIT),
        name="moe_experts",
    )(tile_expert, nact, xs, w_gate, w_up, w_down)


def _combine_kernel(ngrp_ref, gsrc_ref, ys_ref, tm_ref, brow_ref, bcol_ref, h_ref, wsg_ref,
                    wsu_ref, wsd_ref, x1_ref, mod_ref, gpost_ref, o_ref, buf_ref, p_ref, sem):
    j = pl.program_id(0)
    last = pl.num_programs(0) - 1
    slot = j % 2

    def chunk_copy(s, c, g):
        return pltpu.make_async_copy(
            ys_ref.at[pl.ds(pl.multiple_of(g * ROW_CHUNK, ROW_CHUNK), ROW_CHUNK), :],
            buf_ref.at[s, pl.ds(pl.multiple_of(c * ROW_CHUNK, ROW_CHUNK), ROW_CHUNK), :],
            sem.at[s])

    def fetch_block(jj, s):
        def issue_group(gi, carry):
            for k in range(GROUP_CHUNKS):
                c = gi * GROUP_CHUNKS + k
                chunk_copy(s, c, gsrc_ref[jj * BLOCK_CHUNKS + c]).start()
            return carry
        lax.fori_loop(0, ngrp_ref[jj], issue_group, 0)

    @pl.when(j == 0)
    def _():
        buf_ref[...] = jnp.zeros_like(buf_ref)
        fetch_block(0, 0)

    @pl.when(j < last)
    def _():
        fetch_block(j + 1, 1 - slot)

    h = h_ref[...]
    act = _silu(jnp.dot(h, wsg_ref[...], preferred_element_type=F32)) * jnp.dot(
        h, wsu_ref[...], preferred_element_type=F32)
    y = jnp.dot(act.astype(BF16), wsd_ref[...], preferred_element_type=F32)

    tm = tm_ref[...]
    lane = lax.broadcasted_iota(I32, tm.shape, 1)
    rank_t = jnp.where(lane < N_EXPERTS, tm, 0.0).astype(BF16)
    gate_t = jnp.where(lane < N_EXPERTS, 0.0, tm).astype(BF16)
    lane8 = lax.broadcasted_iota(I32, (8, 2 * N_EXPERTS), 1)
    lo_row = jnp.where(lane8 < N_EXPERTS, brow_ref[0], 0.0).astype(BF16)
    lo_col = bcol_ref[0]
    hi_col = bcol_ref[1]
    for cb in range(BLOCK_ROWS // 128):
        rows = (lax.broadcasted_iota(I32, (2 * N_EXPERTS, 128), 1) + cb * 128).astype(F32)
        in_run = jnp.where(jnp.logical_and(rows >= lo_col, rows < hi_col), 1.0, 0.0).astype(BF16)
        sel_rank = jnp.dot(rank_t, in_run, preferred_element_type=F32)
        sel_gate = jnp.dot(gate_t, in_run, preferred_element_type=F32)
        sel_lo = jnp.dot(lo_row, in_run, preferred_element_type=F32)
        local = rows[0:1, :] - sel_lo[0:1, :]
        p_ref[:, cb * 128:(cb + 1) * 128] = jnp.where(sel_rank == local, sel_gate,
                                                      0.0).astype(BF16)

    def wait_group(gi, carry):
        pltpu.make_async_copy(ys_ref.at[pl.ds(0, GROUP_ROWS), :],
                              buf_ref.at[slot, pl.ds(0, GROUP_ROWS), :], sem.at[slot]).wait()
        return carry
    lax.fori_loop(0, ngrp_ref[j], wait_group, 0)

    y = y + jnp.dot(p_ref[...], buf_ref[slot], preferred_element_type=F32)

    ga2 = mod_ref[5:6, :]
    o_ref[...] = x1_ref[...] + ga2 * _rms(y, gpost_ref[...])


def _combine(plan, ys, tm, h2_flat, ws_gate, ws_up, ws_down, x1_flat, mod3, g_post, s):
    t, d = h2_flat.shape
    f = ws_gate.shape[1]
    blocks_per_seq = s // TB
    full = lambda shape: pl.BlockSpec(shape, lambda j, *_: (0,) * len(shape))
    grid_spec = pltpu.PrefetchScalarGridSpec(
        num_scalar_prefetch=2,
        grid=(t // TB,),
        in_specs=[pl.BlockSpec(memory_space=pl.ANY),
                  pl.BlockSpec((TB, 2 * N_EXPERTS), lambda j, *_: (j, 0))] + _bounds_specs() + [
                  pl.BlockSpec((TB, d), lambda j, *_: (j, 0)),
                  full((d, f)), full((d, f)), full((f, d)),
                  pl.BlockSpec((TB, d), lambda j, *_: (j, 0)),
                  pl.BlockSpec((None, 6, d), lambda j, *_: (j // blocks_per_seq, 0, 0)),
                  full((1, d))],
        out_specs=pl.BlockSpec((TB, d), lambda j, *_: (j, 0)),
        scratch_shapes=[pltpu.VMEM((2, BLOCK_ROWS, d), BF16),
                        pltpu.VMEM((TB, BLOCK_ROWS), BF16),
                        pltpu.SemaphoreType.DMA((2,))],
    )
    return pl.pallas_call(
        _combine_kernel,
        grid_spec=grid_spec,
        out_shape=jax.ShapeDtypeStruct((t, d), F32),
        compiler_params=pltpu.CompilerParams(
            dimension_semantics=("arbitrary",), vmem_limit_bytes=VMEM_LIMIT),
        name="moe_combine",
    )(plan["ngrp"], plan["gsrc"], ys, tm, plan["bounds_row"], plan["bounds_col"], h2_flat,
      ws_gate, ws_up, ws_down, x1_flat, mod3, g_post)


def kernel(x, c, w_ada, b_ada, g_pre_mix, w_in, ln_sgu_g, ln_sgu_b, w_spatial, b_spatial,
           g_branch, w_out, g_post_mix, g_pre_ffn, w_router, router_bias, w_gate, w_up, w_down,
           ws_gate, ws_up, ws_down, g_post_ffn):
    bsz, s, d = x.shape
    t = bsz * s
    nblk = t // TB
    max_rows = t * TOP_K + nblk * N_EXPERTS * (ROW_CHUNK - 1) + N_EXPERTS * (TM_FFN - 1)
    n_tiles = -(-max_rows // TM_FFN)
    row = lambda a: a.reshape(1, -1)
    for l in range(w_ada.shape[0]):
        mod3 = _ada(c, w_ada[l], b_ada[l]).reshape(bsz, 6, d)
        og, q, k, v = _premix(x, mod3, row(g_pre_mix[l]), w_in[l].astype(BF16),
                              row(ln_sgu_g[l]), row(ln_sgu_b[l]), w_spatial[l],
                              b_spatial[l].T, row(g_branch[l, :D_GMLP]))
        osb = _attention(q, k, v)
        x1, h2, logits_t = _postmix(og, osb, x, mod3, row(g_branch[l, D_GMLP:]),
                                    w_out[l].astype(BF16), row(g_post_mix[l]),
                                    row(g_pre_ffn[l]), w_router[l].T)
        em, tm, cnt = _route(logits_t, router_bias[l])
        plan = _dispatch_plan(cnt[:, :, 0].astype(I32), n_tiles)
        h2_flat = h2.reshape(t, d)
        xs = _dispatch(plan, h2_flat, em, n_tiles * TM_FFN + nblk * GROUP_ROWS)
        ys = _expert_ffn(plan["tile_expert"], plan["nact"], xs, n_tiles,
                         w_gate[l], w_up[l], w_down[l])
        out = _combine(plan, ys, tm, h2_flat, ws_gate[l].astype(BF16), ws_up[l].astype(BF16),
                       ws_down[l].astype(BF16), x1.reshape(t, d), mod3, row(g_post_ffn[l]), s)
        x = out.reshape(bsz, s, d)
    return x
```

```python
import jax
import jax.numpy as jnp
from jax import lax
from jax.experimental import pallas as pl
from jax.experimental.pallas import tpu as pltpu

F32 = jnp.float32
BF16 = jnp.bfloat16
I32 = jnp.int32

D_MODEL = 1024
D_GMLP = 512
GMLP_GROUPS = 4
GMLP_BLOCK = 128
CHUNK = 64
D_SB = 512
SB_HEAD_DIM = 64
N_PAIRS = D_SB // 128
N_EXPERTS = 64
N_GROUPS = 8
GROUP_SIZE = N_EXPERTS // N_GROUPS
TOPK_GROUPS = 4
TOP_K = 8
D_EXPERT = 256
ROUTED_SCALE = 2.5
EPS = 1e-6
D_IN = 2 * D_GMLP + 3 * D_SB

TM_MIX = 512
TQ = 128
VMEM_LIMIT = 56 * 1024 * 1024
ATTN_DEAD_LOG = -110.0

TB = 256
ROW_CHUNK = 16
KC = 256
MM_ROWS = 1024
GROUP_CHUNKS = 16
GROUP_ROWS = GROUP_CHUNKS * ROW_CHUNK
BLOCK_ROWS = -(-(TB * TOP_K + N_EXPERTS * (ROW_CHUNK - 1)) // MM_ROWS) * MM_ROWS
BLOCK_CHUNKS = BLOCK_ROWS // ROW_CHUNK
TM_FFN = 1024
ZERO_ROWS = 128


def _rms(x, g):
    return x * lax.rsqrt(jnp.mean(x * x, axis=-1, keepdims=True) + EPS) * g


def _silu(x):
    return x * jax.nn.sigmoid(x)


def _ada_kernel(c_ref, w_ref, b_ref, o_ref):
    o_ref[...] = jnp.dot(_silu(c_ref[...]), w_ref[...], preferred_element_type=F32,
                         precision=lax.Precision.HIGHEST) + b_ref[...]


def _ada(c, w, b):
    bsz, d = c.shape
    n = w.shape[1]
    tn = 512
    return pl.pallas_call(
        _ada_kernel,
        grid=(n // tn,),
        in_specs=[pl.BlockSpec((bsz, d), lambda j: (0, 0)),
                  pl.BlockSpec((d, tn), lambda j: (0, j)),
                  pl.BlockSpec((1, tn), lambda j: (0, j))],
        out_specs=pl.BlockSpec((bsz, tn), lambda j: (0, j)),
        out_shape=jax.ShapeDtypeStruct((bsz, n), F32),
        name="ada_mod",
    )(c, w, b.reshape(1, n))


def _premix_kernel(x_ref, mod_ref, gpre_ref, win_ref, lng_ref, lnb_ref, wsp_ref, bsp_ref,
                   gbr_ref, og_ref, q_ref, k_ref, v_ref):
    x = x_ref[...]
    sh1 = mod_ref[0:1, :]
    sc1 = mod_ref[1:2, :]
    h = _rms(x, gpre_ref[...]) * (1.0 + sc1) + sh1
    proj = jnp.dot(h.astype(BF16), win_ref[...], preferred_element_type=F32)

    u = jax.nn.gelu(proj[:, :D_GMLP])
    v = jax.nn.gelu(proj[:, D_GMLP:2 * D_GMLP])
    mu = jnp.mean(v, axis=-1, keepdims=True)
    var = jnp.mean(jnp.square(v - mu), axis=-1, keepdims=True)
    v = ((v - mu) * lax.rsqrt(var + EPS) * lng_ref[...] + lnb_ref[...]).astype(BF16)

    i = lax.broadcasted_iota(I32, (GMLP_BLOCK, GMLP_BLOCK), 0)
    j = lax.broadcasted_iota(I32, (GMLP_BLOCK, GMLP_BLOCK), 1)
    causal = (j // CHUNK) <= (i // CHUNK)
    gd = D_GMLP // GMLP_GROUPS
    blocks = []
    for nb in range(x.shape[0] // GMLP_BLOCK):
        rows = slice(nb * GMLP_BLOCK, (nb + 1) * GMLP_BLOCK)
        cols = []
        for g in range(GMLP_GROUPS):
            w = jnp.where(causal, wsp_ref[g], 0.0).astype(BF16)
            mixed = jnp.dot(w, v[rows, g * gd:(g + 1) * gd], preferred_element_type=F32)
            cols.append(mixed + bsp_ref[:, g:g + 1])
        blocks.append(u[rows, :] * jnp.concatenate(cols, axis=1))
    og = jnp.concatenate(blocks, axis=0)
    og_ref[...] = _rms(og, gbr_ref[...]).astype(BF16)

    base = 2 * D_GMLP
    scale = SB_HEAD_DIM ** -0.5
    for p in range(N_PAIRS):
        q_ref[p] = (proj[:, base + 128 * p:base + 128 * (p + 1)] * scale).astype(BF16)
        k_ref[p] = proj[:, base + D_SB + 128 * p:base + D_SB + 128 * (p + 1)].astype(BF16)
        v_ref[p] = proj[:, base + 2 * D_SB + 128 * p:base + 2 * D_SB + 128 * (p + 1)].astype(BF16)


def _premix(x, mod3, g_pre, w_in, ln_g, ln_b, w_sp, b_sp_t, g_br):
    bsz, s, d = x.shape
    tm = TM_MIX
    full = lambda shape: pl.BlockSpec(shape, lambda b, i: (0,) * len(shape))
    qkv_spec = pl.BlockSpec((None, N_PAIRS, tm, 128), lambda b, i: (b, 0, i, 0))
    qkv_shape = jax.ShapeDtypeStruct((bsz, N_PAIRS, s, 128), BF16)
    return pl.pallas_call(
        _premix_kernel,
        grid=(bsz, s // tm),
        in_specs=[pl.BlockSpec((None, tm, d), lambda b, i: (b, i, 0)),
                  pl.BlockSpec((None, 6, d), lambda b, i: (b, 0, 0)),
                  full((1, d)), full((d, D_IN)), full((1, D_GMLP)), full((1, D_GMLP)),
                  full((GMLP_GROUPS, GMLP_BLOCK, GMLP_BLOCK)), full((GMLP_BLOCK, GMLP_GROUPS)),
                  full((1, D_GMLP))],
        out_specs=[pl.BlockSpec((None, tm, D_GMLP), lambda b, i: (b, i, 0)),
                   qkv_spec, qkv_spec, qkv_spec],
        out_shape=[jax.ShapeDtypeStruct((bsz, s, D_GMLP), BF16), qkv_shape, qkv_shape, qkv_shape],
        compiler_params=pltpu.CompilerParams(
            dimension_semantics=("parallel", "parallel"), vmem_limit_bytes=VMEM_LIMIT),
        name="premix",
    )(x, mod3, g_pre, w_in, ln_g, ln_b, w_sp, b_sp_t, g_br)


def _attn_kernel(q_ref, k_ref, v_ref, o_ref, *scratch):
    qs_refs = scratch[:N_PAIRS]
    acc_refs = scratch[N_PAIRS:2 * N_PAIRS]
    carry_refs = scratch[2 * N_PAIRS:]
    qi = pl.program_id(1)
    first_head = lax.broadcasted_iota(I32, (TQ, 128), 1) < SB_HEAD_DIM
    for p in range(N_PAIRS):
        q2 = q_ref[p]
        zero = jnp.zeros_like(q2)
        qs_refs[p][:TQ, :] = jnp.where(first_head, q2, zero)
        qs_refs[p][TQ:, :] = jnp.where(first_head, zero, q2)
        acc_refs[p][...] = jnp.zeros_like(acc_refs[p])
        carry_refs[p][...] = jnp.zeros_like(carry_refs[p])

    r = jnp.bitwise_and(lax.broadcasted_iota(I32, (2 * TQ, TQ), 0), TQ - 1)
    c = lax.broadcasted_iota(I32, (2 * TQ, TQ), 1)
    strict_causal = c < r
    kr = lax.broadcasted_iota(I32, (TQ, TQ), 0)
    kc = lax.broadcasted_iota(I32, (TQ, TQ), 1)
    suffix = jnp.concatenate([(kr > kc).astype(BF16), jnp.ones((TQ, TQ), BF16)], axis=1)
    suffix2 = jnp.concatenate([suffix, suffix], axis=0)

    def key_block(j, diagonal):
        start = pl.multiple_of(j * TQ, TQ)
        pairs = range(N_PAIRS)
        zs = [lax.dot_general(qs_refs[p][...], k_ref[p, pl.ds(start, TQ), :],
                              (((1,), (1,)), ((), ())), preferred_element_type=F32)
              for p in pairs]
        log_betas, splits = [], []
        for p in pairs:
            z = zs[p]
            log_beta = jnp.minimum(z, 0.0) - jnp.log(1.0 + jnp.exp(-jnp.abs(z)))
            log_1mb = log_beta - z
            if diagonal:
                log_1mb = jnp.where(strict_causal, log_1mb, 0.0)
            hi = log_1mb.astype(BF16)
            lo = (log_1mb - hi.astype(F32)).astype(BF16)
            log_betas.append(log_beta)
            splits.append(jnp.concatenate([hi, lo], axis=1))
        sums = [jnp.dot(splits[p], suffix2, preferred_element_type=F32) for p in pairs]
        weights = []
        live = None
        for p in pairs:
            s = sums[p]
            carry = carry_refs[p][...]
            a = jnp.exp(log_betas[p] + carry + s[:, :TQ])
            if diagonal:
                a = jnp.where(strict_causal, a, 0.0)
            weights.append(a.astype(BF16))
            carry = carry + s[:, TQ:]
            carry_refs[p][...] = carry
            live = carry if live is None else jnp.maximum(live, carry)
        for p in pairs:
            acc_refs[p][...] += jnp.dot(weights[p], v_ref[p, pl.ds(start, TQ), :],
                                        preferred_element_type=F32)
        return jnp.max(live)

    live = key_block(qi, True)

    def cond(state):
        j, live = state
        return jnp.logical_and(j >= 0, live > ATTN_DEAD_LOG)

    def body(state):
        j, _ = state
        return j - 1, key_block(j, False)

    lax.while_loop(cond, body, (qi - 1, live))
    for p in range(N_PAIRS):
        o_ref[:, 128 * p:128 * (p + 1)] = jnp.where(first_head, acc_refs[p][:TQ, :],
                                                    acc_refs[p][TQ:, :])


def _attention(q, k, v):
    bsz, npair, s, _ = q.shape
    kv_spec = pl.BlockSpec((None, npair, s, 128), lambda b, i: (b, 0, 0, 0))
    return pl.pallas_call(
        _attn_kernel,
        grid=(bsz, s // TQ),
        in_specs=[pl.BlockSpec((None, npair, TQ, 128), lambda b, i: (b, 0, i, 0)),
                  kv_spec, kv_spec],
        out_specs=pl.BlockSpec((None, TQ, npair * 128), lambda b, i: (b, i, 0)),
        out_shape=jax.ShapeDtypeStruct((bsz, s, npair * 128), F32),
        scratch_shapes=([pltpu.VMEM((2 * TQ, 128), BF16)] * npair
                        + [pltpu.VMEM((2 * TQ, 128), F32)] * npair
                        + [pltpu.VMEM((2 * TQ, TQ), F32)] * npair),
        compiler_params=pltpu.CompilerParams(
            dimension_semantics=("parallel", "parallel"), vmem_limit_bytes=VMEM_LIMIT),
        name="stickbreak_attn",
    )(q, k, v)


def _postmix_kernel(og_ref, osb_ref, x_ref, mod_ref, gbr_ref, wout_ref, gpost_ref, gpre_ref,
                    wrt_ref, x1_ref, h2_ref, logit_ref):
    ga1 = mod_ref[2:3, :]
    sh2 = mod_ref[3:4, :]
    sc2 = mod_ref[4:5, :]
    osb = _rms(osb_ref[...], gbr_ref[...]).astype(BF16)
    m = (jnp.dot(og_ref[...], wout_ref[:D_GMLP, :], preferred_element_type=F32)
         + jnp.dot(osb, wout_ref[D_GMLP:, :], preferred_element_type=F32))
    x1 = x_ref[...] + ga1 * _rms(m, gpost_ref[...])
    x1_ref[...] = x1
    h2 = _rms(x1, gpre_ref[...]) * (1.0 + sc2) + sh2
    h_hi = h2.astype(BF16)
    h2_ref[...] = h_hi
    h_lo = (h2 - h_hi.astype(F32)).astype(BF16)
    w = wrt_ref[...]
    w_hi = w.astype(BF16)
    w_lo = (w - w_hi.astype(F32)).astype(BF16)
    nt = (((1,), (1,)), ((), ()))
    by_hi = lax.dot_general(jnp.concatenate([w_hi, w_lo], axis=0), h_hi, nt,
                            preferred_element_type=F32)
    by_lo = lax.dot_general(w_hi, h_lo, nt, preferred_element_type=F32)
    logit_ref[...] = by_hi[:N_EXPERTS] + by_hi[N_EXPERTS:] + by_lo


def _postmix(og, osb, x, mod3, g_br, w_out, g_post, g_pre, w_router_t):
    bsz, s, d = x.shape
    tm = TM_MIX
    nt = s // tm
    full = lambda shape: pl.BlockSpec(shape, lambda b, i: (0,) * len(shape))
    return pl.pallas_call(
        _postmix_kernel,
        grid=(bsz, nt),
        in_specs=[pl.BlockSpec((None, tm, D_GMLP), lambda b, i: (b, i, 0)),
                  pl.BlockSpec((None, tm, D_SB), lambda b, i: (b, i, 0)),
                  pl.BlockSpec((None, tm, d), lambda b, i: (b, i, 0)),
                  pl.BlockSpec((None, 6, d), lambda b, i: (b, 0, 0)),
                  full((1, D_SB)), full((d, d)), full((1, d)), full((1, d)),
                  full((N_EXPERTS, d))],
        out_specs=[pl.BlockSpec((None, tm, d), lambda b, i: (b, i, 0)),
                   pl.BlockSpec((None, tm, d), lambda b, i: (b, i, 0)),
                   pl.BlockSpec((N_EXPERTS, tm), lambda b, i: (0, b * nt + i))],
        out_shape=[jax.ShapeDtypeStruct((bsz, s, d), F32),
                   jax.ShapeDtypeStruct((bsz, s, d), BF16),
                   jax.ShapeDtypeStruct((N_EXPERTS, bsz * s), F32)],
        compiler_params=pltpu.CompilerParams(
            dimension_semantics=("parallel", "parallel"), vmem_limit_bytes=VMEM_LIMIT),
        name="postmix",
    )(og, osb, x, mod3, g_br, w_out, g_post, g_pre, w_router_t)


def _first_index_of_max(x, idx, axis, size):
    m = jnp.max(x, axis=axis, keepdims=True)
    return jnp.min(jnp.where(x == m, idx, size), axis=axis, keepdims=True)


def _route_kernel(logit_ref, bias_ref, em_ref, tm_ref, cnt_ref):
    scores = jax.nn.sigmoid(logit_ref[...])
    biased = scores + bias_ref[...]
    neg = jnp.float32(-jnp.inf)

    grouped = biased.reshape(N_GROUPS, GROUP_SIZE, TB)
    within = lax.broadcasted_iota(I32, grouped.shape, 1)
    top1 = jnp.max(grouped, axis=1, keepdims=True)
    first = _first_index_of_max(grouped, within, 1, GROUP_SIZE)
    top2 = jnp.max(jnp.where(within == first, neg, grouped), axis=1, keepdims=True)
    group_score = (top1 + top2).reshape(N_GROUPS, TB)

    gidx = lax.broadcasted_iota(I32, group_score.shape, 0)
    group_on = jnp.zeros(group_score.shape, jnp.bool_)
    for _ in range(TOPK_GROUPS):
        pick = gidx == _first_index_of_max(group_score, gidx, 0, N_GROUPS)
        group_on = jnp.logical_or(group_on, pick)
        group_score = jnp.where(pick, neg, group_score)

    masked = jnp.where(group_on.reshape(N_GROUPS, 1, TB), grouped, neg).reshape(N_EXPERTS, TB)
    eidx = lax.broadcasted_iota(I32, masked.shape, 0)
    chosen = jnp.zeros(masked.shape, jnp.bool_)
    for _ in range(TOP_K):
        pick = eidx == _first_index_of_max(masked, eidx, 0, N_EXPERTS)
        chosen = jnp.logical_or(chosen, pick)
        masked = jnp.where(pick, neg, masked)

    w = jnp.where(chosen, scores, 0.0)
    gates = w / jnp.sum(w, axis=0, keepdims=True) * ROUTED_SCALE

    chosen_f = chosen.astype(F32)
    tr = lax.broadcasted_iota(I32, (TB, TB), 0)
    tc = lax.broadcasted_iota(I32, (TB, TB), 1)
    rank = jnp.dot(chosen_f.astype(BF16), (tr < tc).astype(BF16), preferred_element_type=F32)
    both = jnp.concatenate([jnp.where(chosen, rank, -1.0), gates], axis=0)
    em_ref[...] = both
    tm_ref[...] = both.T
    cnt_ref[...] = jnp.broadcast_to(jnp.sum(chosen_f, axis=1, keepdims=True), (N_EXPERTS, 128))


def _route(logits_t, bias):
    e, t = logits_t.shape
    nblk = t // TB
    return pl.pallas_call(
        _route_kernel,
        grid=(nblk,),
        in_specs=[pl.BlockSpec((e, TB), lambda i: (0, i)),
                  pl.BlockSpec((e, 1), lambda i: (0, 0))],
        out_specs=[pl.BlockSpec((2 * e, TB), lambda i: (0, i)),
                   pl.BlockSpec((TB, 2 * e), lambda i: (i, 0)),
                   pl.BlockSpec((None, e, 128), lambda i: (i, 0, 0))],
        out_shape=[jax.ShapeDtypeStruct((2 * e, t), F32),
                   jax.ShapeDtypeStruct((t, 2 * e), F32),
                   jax.ShapeDtypeStruct((nblk, e, 128), F32)],
        compiler_params=pltpu.CompilerParams(dimension_semantics=("parallel",)),
        name="route",
    )(logits_t, bias.reshape(e, 1))


def _dispatch_plan(cnt, n_tiles):
    nblk, e = cnt.shape
    pc = (cnt + ROW_CHUNK - 1) // ROW_CHUNK * ROW_CHUNK
    start = jnp.cumsum(pc, axis=1) - pc
    nchunk = jnp.sum(pc, axis=1) // ROW_CHUNK
    off = jnp.cumsum(pc, axis=0) - pc
    ecount = jnp.sum(pc, axis=0)
    epad = (ecount + TM_FFN - 1) // TM_FFN * TM_FFN
    gend = jnp.cumsum(epad)
    gbase = gend - epad
    nact = gend[-1] // TM_FFN
    tiles = jnp.minimum(jnp.arange(n_tiles, dtype=I32), nact - 1)
    tile_expert = jnp.minimum(
        jnp.sum((gend[None, :] <= tiles[:, None] * TM_FFN).astype(I32), axis=1), e - 1)
    cidx = jnp.arange(BLOCK_CHUNKS, dtype=I32)
    start16 = start // ROW_CHUNK
    shift = (gbase[None, :] + off) // ROW_CHUNK - start16
    dshift = shift - jnp.pad(shift, ((0, 0), (1, 0)))[:, :-1]
    in_or_after = (start16[:, None, :] <= cidx[None, :, None]).astype(I32)
    where = cidx[None, :] + jnp.sum(in_or_after * dshift[:, None, :], axis=2)
    used = cidx[None, :] < nchunk[:, None]
    spare = (n_tiles * TM_FFN // ROW_CHUNK
             + jnp.arange(nblk, dtype=I32)[:, None] * GROUP_CHUNKS + cidx[None, :] % GROUP_CHUNKS)
    gdst = jnp.where(used, where, spare)
    gsrc = jnp.where(used, where, 0)
    ngrp = (nchunk + GROUP_CHUNKS - 1) // GROUP_CHUNKS
    zbase = (gbase + ecount) // ROW_CHUNK
    zn = (epad - ecount) // ROW_CHUNK

    lo = jnp.tile(start.astype(F32), (1, 2))
    hi = jnp.tile((start + pc).astype(F32), (1, 2))
    bounds_row = jnp.stack([jnp.broadcast_to(lo[:, None, :], (nblk, 8, 2 * e)),
                            jnp.broadcast_to(hi[:, None, :], (nblk, 8, 2 * e))], axis=1)
    bounds_col = jnp.stack([jnp.broadcast_to(lo[:, :, None], (nblk, 2 * e, 128)),
                            jnp.broadcast_to(hi[:, :, None], (nblk, 2 * e, 128))], axis=1)
    as_i32 = lambda a: a.astype(I32)
    return dict(ngrp=as_i32(ngrp), gdst=as_i32(gdst.reshape(-1)), gsrc=as_i32(gsrc.reshape(-1)),
                zn=as_i32(zn), zbase=as_i32(zbase), tile_expert=as_i32(tile_expert),
                nact=as_i32(nact.reshape(1)), bounds_row=bounds_row, bounds_col=bounds_col)


def _dispatch_kernel(ngrp_ref, gdst_ref, zn_ref, zbase_ref, h_ref, em_ref, brow_ref,
                     xs_ref, buf_ref, p_ref, zero_ref, sem, zsem):
    j = pl.program_id(0)
    last = pl.num_programs(0) - 1
    slot = j % 2

    def chunk_copy(s, c, g):
        return pltpu.make_async_copy(
            buf_ref.at[s, pl.ds(pl.multiple_of(c * ROW_CHUNK, ROW_CHUNK), ROW_CHUNK), :],
            xs_ref.at[pl.ds(pl.multiple_of(g * ROW_CHUNK, ROW_CHUNK), ROW_CHUNK), :],
            sem.at[s])

    def wait_block(jj, s):
        def body(i, carry):
            pltpu.make_async_copy(buf_ref.at[s, pl.ds(0, GROUP_ROWS), :],
                                  xs_ref.at[pl.ds(0, GROUP_ROWS), :], sem.at[s]).wait()
            return carry
        lax.fori_loop(0, ngrp_ref[jj], body, 0)

    def zero_copy(g, rows):
        return pltpu.make_async_copy(
            zero_ref.at[pl.ds(0, rows), :],
            xs_ref.at[pl.ds(pl.multiple_of(g * ROW_CHUNK, ROW_CHUNK), rows), :], zsem.at[0])

    def zero_fill(start):
        big = ZERO_ROWS // ROW_CHUNK

        def per_expert(e, carry):
            n = zn_ref[e]
            base = zbase_ref[e]

            def big_copy(i, c2):
                cp = zero_copy(base + i * big, ZERO_ROWS)
                cp.start() if start else cp.wait()
                return c2
            lax.fori_loop(0, n // big, big_copy, 0)

            def small_copy(i, c2):
                cp = zero_copy(base + n // big * big + i, ROW_CHUNK)
                cp.start() if start else cp.wait()
                return c2
            lax.fori_loop(0, n % big, small_copy, 0)
            return carry
        lax.fori_loop(0, N_EXPERTS, per_expert, 0)

    @pl.when(j == 0)
    def _():
        zero_ref[...] = jnp.zeros_like(zero_ref)
        zero_fill(True)

    @pl.when(j >= 2)
    def _():
        wait_block(j - 2, slot)

    lo = brow_ref[0, 0:1, :]
    hi = brow_ref[1, 0:1, :]
    first_copy = lax.broadcasted_iota(I32, (KC, 2 * N_EXPERTS), 1) < N_EXPERTS
    ranks = em_ref[...].astype(BF16)
    for kc in range(BLOCK_ROWS // KC):
        rows_e = (lax.broadcasted_iota(I32, (KC, 2 * N_EXPERTS), 0) + kc * KC).astype(F32)
        in_run = jnp.logical_and(jnp.logical_and(rows_e >= lo, rows_e < hi), first_copy)
        sel = jnp.dot(jnp.where(in_run, 1.0, 0.0).astype(BF16), ranks,
                      preferred_element_type=F32)
        local = jnp.sum(jnp.where(in_run, rows_e - lo + 3.0, 0.0), axis=1, keepdims=True) - 3.0
        p_ref[kc * KC:(kc + 1) * KC, :] = jnp.where(sel == local, 1.0, 0.0).astype(BF16)

    h = h_ref[...]
    rows_used = ngrp_ref[j] * GROUP_ROWS
    for m in range(BLOCK_ROWS // MM_ROWS):
        @pl.when(m * MM_ROWS < rows_used)
        def _():
            xs = jnp.dot(p_ref[m * MM_ROWS:(m + 1) * MM_ROWS, :], h, preferred_element_type=F32)
            buf_ref[slot, m * MM_ROWS:(m + 1) * MM_ROWS, :] = xs.astype(BF16)

    def issue_group(gi, carry):
        for k in range(GROUP_CHUNKS):
            c = gi * GROUP_CHUNKS + k
            chunk_copy(slot, c, gdst_ref[j * BLOCK_CHUNKS + c]).start()
        return carry
    lax.fori_loop(0, ngrp_ref[j], issue_group, 0)

    @pl.when(j == last)
    def _():
        wait_block(j, slot)

        @pl.when(j >= 1)
        def _():
            wait_block(j - 1, 1 - slot)

        zero_fill(False)


def _bounds_specs():
    return [pl.BlockSpec((None, 2, 8, 2 * N_EXPERTS), lambda j, *_: (j, 0, 0, 0)),
            pl.BlockSpec((None, 2, 2 * N_EXPERTS, 128), lambda j, *_: (j, 0, 0, 0))]


def _dispatch(plan, h2_flat, em, n_rows):
    t, d = h2_flat.shape
    grid_spec = pltpu.PrefetchScalarGridSpec(
        num_scalar_prefetch=4,
        grid=(t // TB,),
        in_specs=[pl.BlockSpec((TB, d), lambda j, *_: (j, 0)),
                  pl.BlockSpec((2 * N_EXPERTS, TB), lambda j, *_: (0, j))] + _bounds_specs()[:1],
        out_specs=pl.BlockSpec(memory_space=pl.ANY),
        scratch_shapes=[pltpu.VMEM((2, BLOCK_ROWS, d), BF16),
                        pltpu.VMEM((BLOCK_ROWS, TB), BF16),
                        pltpu.VMEM((ZERO_ROWS, d), BF16),
                        pltpu.SemaphoreType.DMA((2,)),
                        pltpu.SemaphoreType.DMA((1,))],
    )
    return pl.pallas_call(
        _dispatch_kernel,
        grid_spec=grid_spec,
        out_shape=jax.ShapeDtypeStruct((n_rows, d), BF16),
        compiler_params=pltpu.CompilerParams(
            dimension_semantics=("arbitrary",), vmem_limit_bytes=VMEM_LIMIT),
        name="moe_dispatch",
    )(plan["ngrp"], plan["gdst"], plan["zn"], plan["zbase"], h2_flat, em, plan["bounds_row"])


def _ffn_kernel(te_ref, nact_ref, x_ref, wg_ref, wu_ref, wd_ref, y_ref, wgu_s, wd_s):
    i = pl.program_id(0)

    @pl.when(i < nact_ref[0])
    def _():
        @pl.when(jnp.logical_or(i == 0, te_ref[i] != te_ref[jnp.maximum(i - 1, 0)]))
        def _():
            wgu_s[:, :D_EXPERT] = wg_ref[...].astype(BF16)
            wgu_s[:, D_EXPERT:] = wu_ref[...].astype(BF16)
            wd_s[...] = wd_ref[...].astype(BF16)

        gu = jnp.dot(x_ref[...], wgu_s[...], preferred_element_type=F32)
        act = _silu(gu[:, :D_EXPERT]) * gu[:, D_EXPERT:]
        y_ref[...] = jnp.dot(act.astype(BF16), wd_s[...], preferred_element_type=F32).astype(BF16)


def _expert_ffn(tile_expert, nact, xs, n_tiles, w_gate, w_up, w_down):
    n_rows, d = n_tiles * TM_FFN, xs.shape[1]
    f = D_EXPERT
    row_tile = lambda i, te, na: (jnp.minimum(i, na[0] - 1), 0)
    grid_spec = pltpu.PrefetchScalarGridSpec(
        num_scalar_prefetch=2,
        grid=(n_tiles,),
        in_specs=[pl.BlockSpec((TM_FFN, d), row_tile),
                  pl.BlockSpec((None, d, f), lambda i, te, na: (te[i], 0, 0)),
                  pl.BlockSpec((None, d, f), lambda i, te, na: (te[i], 0, 0)),
                  pl.BlockSpec((None, f, d), lambda i, te, na: (te[i], 0, 0))],
        out_specs=pl.BlockSpec((TM_FFN, d), row_tile),
        scratch_shapes=[pltpu.VMEM((d, 2 * f), BF16), pltpu.VMEM((f, d), BF16)],
    )
    return pl.pallas_call(
        _ffn_kernel,
        grid_spec=grid_spec,
        out_shape=jax.ShapeDtypeStruct((n_rows, d), BF16),
        compiler_params=pltpu.CompilerParams(
            dimension_semantics=("arbitrary",), vmem_limit_bytes=VMEM_LIMIT),
        name="moe_experts",
    )(tile_expert, nact, xs, w_gate, w_up, w_down)


def _combine_kernel(ngrp_ref, gsrc_ref, ys_ref, tm_ref, brow_ref, bcol_ref, h_ref, wsg_ref,
                    wsu_ref, wsd_ref, x1_ref, mod_ref, gpost_ref, o_ref, buf_ref, p_ref, sem):
    j = pl.program_id(0)
    last = pl.num_programs(0) - 1
    slot = j % 2

    def chunk_copy(s, c, g):
        return pltpu.make_async_copy(
            ys_ref.at[pl.ds(pl.multiple_of(g * ROW_CHUNK, ROW_CHUNK), ROW_CHUNK), :],
            buf_ref.at[s, pl.ds(pl.multiple_of(c * ROW_CHUNK, ROW_CHUNK), ROW_CHUNK), :],
            sem.at[s])

    def fetch_block(jj, s):
        def issue_group(gi, carry):
            for k in range(GROUP_CHUNKS):
                c = gi * GROUP_CHUNKS + k
                chunk_copy(s, c, gsrc_ref[jj * BLOCK_CHUNKS + c]).start()
            return carry
        lax.fori_loop(0, ngrp_ref[jj], issue_group, 0)

    @pl.when(j == 0)
    def _():
        buf_ref[...] = jnp.zeros_like(buf_ref)
        fetch_block(0, 0)

    @pl.when(j < last)
    def _():
        fetch_block(j + 1, 1 - slot)

    h = h_ref[...]
    act = _silu(jnp.dot(h, wsg_ref[...], preferred_element_type=F32)) * jnp.dot(
        h, wsu_ref[...], preferred_element_type=F32)
    y = jnp.dot(act.astype(BF16), wsd_ref[...], preferred_element_type=F32)

    tm = tm_ref[...]
    lane = lax.broadcasted_iota(I32, tm.shape, 1)
    rank_t = jnp.where(lane < N_EXPERTS, tm, 0.0).astype(BF16)
    gate_t = jnp.where(lane < N_EXPERTS, 0.0, tm).astype(BF16)
    lane8 = lax.broadcasted_iota(I32, (8, 2 * N_EXPERTS), 1)
    lo_row = jnp.where(lane8 < N_EXPERTS, brow_ref[0], 0.0).astype(BF16)
    lo_col = jnp.concatenate([bcol_ref[0]] * (KC // 128), axis=1)
    hi_col = jnp.concatenate([bcol_ref[1]] * (KC // 128), axis=1)
    for cb in range(BLOCK_ROWS // KC):
        rows = (lax.broadcasted_iota(I32, (2 * N_EXPERTS, KC), 1) + cb * KC).astype(F32)
        in_run = jnp.where(jnp.logical_and(rows >= lo_col, rows < hi_col), 1.0, 0.0).astype(BF16)
        sel_rank = jnp.dot(rank_t, in_run, preferred_element_type=F32)
        sel_gate = jnp.dot(gate_t, in_run, preferred_element_type=F32)
        sel_lo = jnp.dot(lo_row, in_run, preferred_element_type=F32)
        local = rows[0:1, :] - sel_lo[0:1, :]
        p_ref[:, cb * KC:(cb + 1) * KC] = jnp.where(sel_rank == local, sel_gate,
                                                    0.0).astype(BF16)

    def wait_group(gi, carry):
        pltpu.make_async_copy(ys_ref.at[pl.ds(0, GROUP_ROWS), :],
                              buf_ref.at[slot, pl.ds(0, GROUP_ROWS), :], sem.at[slot]).wait()
        return carry
    lax.fori_loop(0, ngrp_ref[j], wait_group, 0)

    y = y + jnp.dot(p_ref[...], buf_ref[slot], preferred_element_type=F32)

    ga2 = mod_ref[5:6, :]
    o_ref[...] = x1_ref[...] + ga2 * _rms(y, gpost_ref[...])


def _combine(plan, ys, tm, h2_flat, ws_gate, ws_up, ws_down, x1_flat, mod3, g_post, s):
    t, d = h2_flat.shape
    f = ws_gate.shape[1]
    blocks_per_seq = s // TB
    full = lambda shape: pl.BlockSpec(shape, lambda j, *_: (0,) * len(shape))
    grid_spec = pltpu.PrefetchScalarGridSpec(
        num_scalar_prefetch=2,
        grid=(t // TB,),
        in_specs=[pl.BlockSpec(memory_space=pl.ANY),
                  pl.BlockSpec((TB, 2 * N_EXPERTS), lambda j, *_: (j, 0))] + _bounds_specs() + [
                  pl.BlockSpec((TB, d), lambda j, *_: (j, 0)),
                  full((d, f)), full((d, f)), full((f, d)),
                  pl.BlockSpec((TB, d), lambda j, *_: (j, 0)),
                  pl.BlockSpec((None, 6, d), lambda j, *_: (j // blocks_per_seq, 0, 0)),
                  full((1, d))],
        out_specs=pl.BlockSpec((TB, d), lambda j, *_: (j, 0)),
        scratch_shapes=[pltpu.VMEM((2, BLOCK_ROWS, d), BF16),
                        pltpu.VMEM((TB, BLOCK_ROWS), BF16),
                        pltpu.SemaphoreType.DMA((2,))],
    )
    return pl.pallas_call(
        _combine_kernel,
        grid_spec=grid_spec,
        out_shape=jax.ShapeDtypeStruct((t, d), F32),
        compiler_params=pltpu.CompilerParams(
            dimension_semantics=("arbitrary",), vmem_limit_bytes=VMEM_LIMIT),
        name="moe_combine",
    )(plan["ngrp"], plan["gsrc"], ys, tm, plan["bounds_row"], plan["bounds_col"], h2_flat,
      ws_gate, ws_up, ws_down, x1_flat, mod3, g_post)


def kernel(x, c, w_ada, b_ada, g_pre_mix, w_in, ln_sgu_g, ln_sgu_b, w_spatial, b_spatial,
           g_branch, w_out, g_post_mix, g_pre_ffn, w_router, router_bias, w_gate, w_up, w_down,
           ws_gate, ws_up, ws_down, g_post_ffn):
    bsz, s, d = x.shape
    t = bsz * s
    nblk = t // TB
    max_rows = t * TOP_K + nblk * N_EXPERTS * (ROW_CHUNK - 1) + N_EXPERTS * (TM_FFN - 1)
    n_tiles = -(-max_rows // TM_FFN)
    row = lambda a: a.reshape(1, -1)
    for l in range(w_ada.shape[0]):
        mod3 = _ada(c, w_ada[l], b_ada[l]).reshape(bsz, 6, d)
        og, q, k, v = _premix(x, mod3, row(g_pre_mix[l]), w_in[l].astype(BF16),
                              row(ln_sgu_g[l]), row(ln_sgu_b[l]), w_spatial[l],
                              b_spatial[l].T, row(g_branch[l, :D_GMLP]))
        osb = _attention(q, k, v)
        x1, h2, logits_t = _postmix(og, osb, x, mod3, row(g_branch[l, D_GMLP:]),
                                    w_out[l].astype(BF16), row(g_post_mix[l]),
                                    row(g_pre_ffn[l]), w_router[l].T)
        em, tm, cnt = _route(logits_t, router_bias[l])
        plan = _dispatch_plan(cnt[:, :, 0].astype(I32), n_tiles)
        h2_flat = h2.reshape(t, d)
        xs = _dispatch(plan, h2_flat, em, n_tiles * TM_FFN + nblk * GROUP_ROWS)
        ys = _expert_ffn(plan["tile_expert"], plan["nact"], xs, n_tiles,
                         w_gate[l], w_up[l], w_down[l])
        out = _combine(plan, ys, tm, h2_flat, ws_gate[l].astype(BF16), ws_up[l].astype(BF16),
                       ws_down[l].astype(BF16), x1.reshape(t, d), mod3, row(g_post_ffn[l]), s)
        x = out.reshape(bsz, s, d)
    return x
```

```python
import jax
import jax.numpy as jnp
from jax import lax
from jax.experimental import pallas as pl
from jax.experimental.pallas import tpu as pltpu

F32 = jnp.float32
BF16 = jnp.bfloat16
I32 = jnp.int32

D_MODEL = 1024
D_GMLP = 512
GMLP_GROUPS = 4
GMLP_BLOCK = 128
CHUNK = 64
D_SB = 512
SB_HEAD_DIM = 64
N_PAIRS = D_SB // 128
N_EXPERTS = 64
N_GROUPS = 8
GROUP_SIZE = N_EXPERTS // N_GROUPS
TOPK_GROUPS = 4
TOP_K = 8
D_EXPERT = 256
ROUTED_SCALE = 2.5
EPS = 1e-6
D_IN = 2 * D_GMLP + 3 * D_SB

TM_MIX = 512
TQ = 128
VMEM_LIMIT = 56 * 1024 * 1024
ATTN_DEAD_LOG = -110.0

TB = 256
ROW_CHUNK = 16
KC = 256
GROUP_CHUNKS = 32
GROUP_ROWS = GROUP_CHUNKS * ROW_CHUNK
BLOCK_ROWS = -(-(TB * TOP_K + N_EXPERTS * (ROW_CHUNK - 1)) // GROUP_ROWS) * GROUP_ROWS
BLOCK_CHUNKS = BLOCK_ROWS // ROW_CHUNK
TM_FFN = 1024
ZERO_ROWS = 128


def _rms(x, g):
    return x * lax.rsqrt(jnp.mean(x * x, axis=-1, keepdims=True) + EPS) * g


def _silu(x):
    return x * jax.nn.sigmoid(x)


def _ada_kernel(c_ref, w_ref, b_ref, o_ref):
    o_ref[...] = jnp.dot(_silu(c_ref[...]), w_ref[...], preferred_element_type=F32,
                         precision=lax.Precision.HIGHEST) + b_ref[...]


def _ada(c, w, b):
    bsz, d = c.shape
    n = w.shape[1]
    tn = 512
    return pl.pallas_call(
        _ada_kernel,
        grid=(n // tn,),
        in_specs=[pl.BlockSpec((bsz, d), lambda j: (0, 0)),
                  pl.BlockSpec((d, tn), lambda j: (0, j)),
                  pl.BlockSpec((1, tn), lambda j: (0, j))],
        out_specs=pl.BlockSpec((bsz, tn), lambda j: (0, j)),
        out_shape=jax.ShapeDtypeStruct((bsz, n), F32),
        name="ada_mod",
    )(c, w, b.reshape(1, n))


def _premix_kernel(x_ref, mod_ref, gpre_ref, win_ref, lng_ref, lnb_ref, wsp_ref, bsp_ref,
                   gbr_ref, og_ref, q_ref, k_ref, v_ref):
    x = x_ref[...]
    sh1 = mod_ref[0:1, :]
    sc1 = mod_ref[1:2, :]
    h = _rms(x, gpre_ref[...]) * (1.0 + sc1) + sh1
    proj = jnp.dot(h.astype(BF16), win_ref[...], preferred_element_type=F32)

    u = jax.nn.gelu(proj[:, :D_GMLP])
    v = jax.nn.gelu(proj[:, D_GMLP:2 * D_GMLP])
    mu = jnp.mean(v, axis=-1, keepdims=True)
    var = jnp.mean(jnp.square(v - mu), axis=-1, keepdims=True)
    v = ((v - mu) * lax.rsqrt(var + EPS) * lng_ref[...] + lnb_ref[...]).astype(BF16)

    i = lax.broadcasted_iota(I32, (GMLP_BLOCK, GMLP_BLOCK), 0)
    j = lax.broadcasted_iota(I32, (GMLP_BLOCK, GMLP_BLOCK), 1)
    causal = (j // CHUNK) <= (i // CHUNK)
    gd = D_GMLP // GMLP_GROUPS
    blocks = []
    for nb in range(x.shape[0] // GMLP_BLOCK):
        rows = slice(nb * GMLP_BLOCK, (nb + 1) * GMLP_BLOCK)
        cols = []
        for g in range(GMLP_GROUPS):
            w = jnp.where(causal, wsp_ref[g], 0.0).astype(BF16)
            mixed = jnp.dot(w, v[rows, g * gd:(g + 1) * gd], preferred_element_type=F32)
            cols.append(mixed + bsp_ref[:, g:g + 1])
        blocks.append(u[rows, :] * jnp.concatenate(cols, axis=1))
    og = jnp.concatenate(blocks, axis=0)
    og_ref[...] = _rms(og, gbr_ref[...]).astype(BF16)

    base = 2 * D_GMLP
    scale = SB_HEAD_DIM ** -0.5
    for p in range(N_PAIRS):
        q_ref[p] = (proj[:, base + 128 * p:base + 128 * (p + 1)] * scale).astype(BF16)
        k_ref[p] = proj[:, base + D_SB + 128 * p:base + D_SB + 128 * (p + 1)].astype(BF16)
        v_ref[p] = proj[:, base + 2 * D_SB + 128 * p:base + 2 * D_SB + 128 * (p + 1)].astype(BF16)


def _premix(x, mod3, g_pre, w_in, ln_g, ln_b, w_sp, b_sp_t, g_br):
    bsz, s, d = x.shape
    tm = TM_MIX
    full = lambda shape: pl.BlockSpec(shape, lambda b, i: (0,) * len(shape))
    qkv_spec = pl.BlockSpec((None, N_PAIRS, tm, 128), lambda b, i: (b, 0, i, 0))
    qkv_shape = jax.ShapeDtypeStruct((bsz, N_PAIRS, s, 128), BF16)
    return pl.pallas_call(
        _premix_kernel,
        grid=(bsz, s // tm),
        in_specs=[pl.BlockSpec((None, tm, d), lambda b, i: (b, i, 0)),
                  pl.BlockSpec((None, 6, d), lambda b, i: (b, 0, 0)),
                  full((1, d)), full((d, D_IN)), full((1, D_GMLP)), full((1, D_GMLP)),
                  full((GMLP_GROUPS, GMLP_BLOCK, GMLP_BLOCK)), full((GMLP_BLOCK, GMLP_GROUPS)),
                  full((1, D_GMLP))],
        out_specs=[pl.BlockSpec((None, tm, D_GMLP), lambda b, i: (b, i, 0)),
                   qkv_spec, qkv_spec, qkv_spec],
        out_shape=[jax.ShapeDtypeStruct((bsz, s, D_GMLP), BF16), qkv_shape, qkv_shape, qkv_shape],
        compiler_params=pltpu.CompilerParams(
            dimension_semantics=("parallel", "parallel"), vmem_limit_bytes=VMEM_LIMIT),
        name="premix",
    )(x, mod3, g_pre, w_in, ln_g, ln_b, w_sp, b_sp_t, g_br)


def _attn_kernel(q_ref, k_ref, v_ref, o_ref, *scratch):
    qs_refs = scratch[:N_PAIRS]
    acc_refs = scratch[N_PAIRS:2 * N_PAIRS]
    carry_refs = scratch[2 * N_PAIRS:]
    qi = pl.program_id(1)
    first_head = lax.broadcasted_iota(I32, (TQ, 128), 1) < SB_HEAD_DIM
    for p in range(N_PAIRS):
        q2 = q_ref[p]
        zero = jnp.zeros_like(q2)
        qs_refs[p][:TQ, :] = jnp.where(first_head, q2, zero)
        qs_refs[p][TQ:, :] = jnp.where(first_head, zero, q2)
        acc_refs[p][...] = jnp.zeros_like(acc_refs[p])
        carry_refs[p][...] = jnp.zeros_like(carry_refs[p])

    r = jnp.bitwise_and(lax.broadcasted_iota(I32, (2 * TQ, TQ), 0), TQ - 1)
    c = lax.broadcasted_iota(I32, (2 * TQ, TQ), 1)
    strict_causal = c < r
    kr = lax.broadcasted_iota(I32, (TQ, TQ), 0)
    kc = lax.broadcasted_iota(I32, (TQ, TQ), 1)
    suffix = jnp.concatenate([(kr > kc).astype(BF16), jnp.ones((TQ, TQ), BF16)], axis=1)
    suffix2 = jnp.concatenate([suffix, suffix], axis=0)

    def key_block(j, diagonal):
        start = pl.multiple_of(j * TQ, TQ)
        pairs = range(N_PAIRS)
        zs = [lax.dot_general(qs_refs[p][...], k_ref[p, pl.ds(start, TQ), :],
                              (((1,), (1,)), ((), ())), preferred_element_type=F32)
              for p in pairs]
        log_betas, splits = [], []
        for p in pairs:
            z = zs[p]
            log_beta = jnp.minimum(z, 0.0) - jnp.log(1.0 + jnp.exp(-jnp.abs(z)))
            log_1mb = log_beta - z
            if diagonal:
                log_1mb = jnp.where(strict_causal, log_1mb, 0.0)
            hi = log_1mb.astype(BF16)
            lo = (log_1mb - hi.astype(F32)).astype(BF16)
            log_betas.append(log_beta)
            splits.append(jnp.concatenate([hi, lo], axis=1))
        sums = [jnp.dot(splits[p], suffix2, preferred_element_type=F32) for p in pairs]
        weights = []
        live = None
        for p in pairs:
            s = sums[p]
            carry = carry_refs[p][...]
            a = jnp.exp(log_betas[p] + carry + s[:, :TQ])
            if diagonal:
                a = jnp.where(strict_causal, a, 0.0)
            weights.append(a.astype(BF16))
            carry = carry + s[:, TQ:]
            carry_refs[p][...] = carry
            live = carry if live is None else jnp.maximum(live, carry)
        for p in pairs:
            acc_refs[p][...] += jnp.dot(weights[p], v_ref[p, pl.ds(start, TQ), :],
                                        preferred_element_type=F32)
        return jnp.max(live)

    live = key_block(qi, True)

    def cond(state):
        j, live = state
        return jnp.logical_and(j >= 0, live > ATTN_DEAD_LOG)

    def body(state):
        j, _ = state
        return j - 1, key_block(j, False)

    lax.while_loop(cond, body, (qi - 1, live))
    for p in range(N_PAIRS):
        o_ref[:, 128 * p:128 * (p + 1)] = jnp.where(first_head, acc_refs[p][:TQ, :],
                                                    acc_refs[p][TQ:, :])


def _attention(q, k, v):
    bsz, npair, s, _ = q.shape
    kv_spec = pl.BlockSpec((None, npair, s, 128), lambda b, i: (b, 0, 0, 0))
    return pl.pallas_call(
        _attn_kernel,
        grid=(bsz, s // TQ),
        in_specs=[pl.BlockSpec((None, npair, TQ, 128), lambda b, i: (b, 0, i, 0)),
                  kv_spec, kv_spec],
        out_specs=pl.BlockSpec((None, TQ, npair * 128), lambda b, i: (b, i, 0)),
        out_shape=jax.ShapeDtypeStruct((bsz, s, npair * 128), F32),
        scratch_shapes=([pltpu.VMEM((2 * TQ, 128), BF16)] * npair
                        + [pltpu.VMEM((2 * TQ, 128), F32)] * npair
                        + [pltpu.VMEM((2 * TQ, TQ), F32)] * npair),
        compiler_params=pltpu.CompilerParams(
            dimension_semantics=("parallel", "parallel"), vmem_limit_bytes=VMEM_LIMIT),
        name="stickbreak_attn",
    )(q, k, v)


def _postmix_kernel(og_ref, osb_ref, x_ref, mod_ref, gbr_ref, wout_ref, gpost_ref, gpre_ref,
                    wrt_ref, x1_ref, h2_ref, logit_ref):
    ga1 = mod_ref[2:3, :]
    sh2 = mod_ref[3:4, :]
    sc2 = mod_ref[4:5, :]
    osb = _rms(osb_ref[...], gbr_ref[...]).astype(BF16)
    m = (jnp.dot(og_ref[...], wout_ref[:D_GMLP, :], preferred_element_type=F32)
         + jnp.dot(osb, wout_ref[D_GMLP:, :], preferred_element_type=F32))
    x1 = x_ref[...] + ga1 * _rms(m, gpost_ref[...])
    x1_ref[...] = x1
    h2 = _rms(x1, gpre_ref[...]) * (1.0 + sc2) + sh2
    h_hi = h2.astype(BF16)
    h2_ref[...] = h_hi
    h_lo = (h2 - h_hi.astype(F32)).astype(BF16)
    w = wrt_ref[...]
    w_hi = w.astype(BF16)
    w_lo = (w - w_hi.astype(F32)).astype(BF16)
    nt = (((1,), (1,)), ((), ()))
    by_hi = lax.dot_general(jnp.concatenate([w_hi, w_lo], axis=0), h_hi, nt,
                            preferred_element_type=F32)
    by_lo = lax.dot_general(w_hi, h_lo, nt, preferred_element_type=F32)
    logit_ref[...] = by_hi[:N_EXPERTS] + by_hi[N_EXPERTS:] + by_lo


def _postmix(og, osb, x, mod3, g_br, w_out, g_post, g_pre, w_router_t):
    bsz, s, d = x.shape
    tm = TM_MIX
    nt = s // tm
    full = lambda shape: pl.BlockSpec(shape, lambda b, i: (0,) * len(shape))
    return pl.pallas_call(
        _postmix_kernel,
        grid=(bsz, nt),
        in_specs=[pl.BlockSpec((None, tm, D_GMLP), lambda b, i: (b, i, 0)),
                  pl.BlockSpec((None, tm, D_SB), lambda b, i: (b, i, 0)),
                  pl.BlockSpec((None, tm, d), lambda b, i: (b, i, 0)),
                  pl.BlockSpec((None, 6, d), lambda b, i: (b, 0, 0)),
                  full((1, D_SB)), full((d, d)), full((1, d)), full((1, d)),
                  full((N_EXPERTS, d))],
        out_specs=[pl.BlockSpec((None, tm, d), lambda b, i: (b, i, 0)),
                   pl.BlockSpec((None, tm, d), lambda b, i: (b, i, 0)),
                   pl.BlockSpec((N_EXPERTS, tm), lambda b, i: (0, b * nt + i))],
        out_shape=[jax.ShapeDtypeStruct((bsz, s, d), F32),
                   jax.ShapeDtypeStruct((bsz, s, d), BF16),
                   jax.ShapeDtypeStruct((N_EXPERTS, bsz * s), F32)],
        compiler_params=pltpu.CompilerParams(
            dimension_semantics=("parallel", "parallel"), vmem_limit_bytes=VMEM_LIMIT),
        name="postmix",
    )(og, osb, x, mod3, g_br, w_out, g_post, g_pre, w_router_t)


def _first_index_of_max(x, idx, axis, size):
    m = jnp.max(x, axis=axis, keepdims=True)
    return jnp.min(jnp.where(x == m, idx, size), axis=axis, keepdims=True)


def _route_kernel(logit_ref, bias_ref, em_ref, tm_ref, cnt_ref):
    scores = jax.nn.sigmoid(logit_ref[...])
    biased = scores + bias_ref[...]
    neg = jnp.float32(-jnp.inf)

    grouped = biased.reshape(N_GROUPS, GROUP_SIZE, TB)
    within = lax.broadcasted_iota(I32, grouped.shape, 1)
    top1 = jnp.max(grouped, axis=1, keepdims=True)
    first = _first_index_of_max(grouped, within, 1, GROUP_SIZE)
    top2 = jnp.max(jnp.where(within == first, neg, grouped), axis=1, keepdims=True)
    group_score = (top1 + top2).reshape(N_GROUPS, TB)

    gidx = lax.broadcasted_iota(I32, group_score.shape, 0)
    group_on = jnp.zeros(group_score.shape, jnp.bool_)
    for _ in range(TOPK_GROUPS):
        pick = gidx == _first_index_of_max(group_score, gidx, 0, N_GROUPS)
        group_on = jnp.logical_or(group_on, pick)
        group_score = jnp.where(pick, neg, group_score)

    masked = jnp.where(group_on.reshape(N_GROUPS, 1, TB), grouped, neg).reshape(N_EXPERTS, TB)
    eidx = lax.broadcasted_iota(I32, masked.shape, 0)
    chosen = jnp.zeros(masked.shape, jnp.bool_)
    for _ in range(TOP_K):
        pick = eidx == _first_index_of_max(masked, eidx, 0, N_EXPERTS)
        chosen = jnp.logical_or(chosen, pick)
        masked = jnp.where(pick, neg, masked)

    w = jnp.where(chosen, scores, 0.0)
    gates = w / jnp.sum(w, axis=0, keepdims=True) * ROUTED_SCALE

    chosen_f = chosen.astype(F32)
    tr = lax.broadcasted_iota(I32, (TB, TB), 0)
    tc = lax.broadcasted_iota(I32, (TB, TB), 1)
    rank = jnp.dot(chosen_f.astype(BF16), (tr < tc).astype(BF16), preferred_element_type=F32)
    both = jnp.concatenate([jnp.where(chosen, rank, -1.0), gates], axis=0)
    em_ref[...] = both
    tm_ref[...] = both.T
    cnt_ref[...] = jnp.broadcast_to(jnp.sum(chosen_f, axis=1, keepdims=True), (N_EXPERTS, 128))


def _route(logits_t, bias):
    e, t = logits_t.shape
    nblk = t // TB
    return pl.pallas_call(
        _route_kernel,
        grid=(nblk,),
        in_specs=[pl.BlockSpec((e, TB), lambda i: (0, i)),
                  pl.BlockSpec((e, 1), lambda i: (0, 0))],
        out_specs=[pl.BlockSpec((2 * e, TB), lambda i: (0, i)),
                   pl.BlockSpec((TB, 2 * e), lambda i: (i, 0)),
                   pl.BlockSpec((None, e, 128), lambda i: (i, 0, 0))],
        out_shape=[jax.ShapeDtypeStruct((2 * e, t), F32),
                   jax.ShapeDtypeStruct((t, 2 * e), F32),
                   jax.ShapeDtypeStruct((nblk, e, 128), F32)],
        compiler_params=pltpu.CompilerParams(dimension_semantics=("parallel",)),
        name="route",
    )(logits_t, bias.reshape(e, 1))


def _dispatch_plan(cnt, n_tiles):
    nblk, e = cnt.shape
    pc = (cnt + ROW_CHUNK - 1) // ROW_CHUNK * ROW_CHUNK
    start = jnp.cumsum(pc, axis=1) - pc
    nchunk = jnp.sum(pc, axis=1) // ROW_CHUNK
    off = jnp.cumsum(pc, axis=0) - pc
    ecount = jnp.sum(pc, axis=0)
    epad = (ecount + TM_FFN - 1) // TM_FFN * TM_FFN
    gend = jnp.cumsum(epad)
    gbase = gend - epad
    nact = gend[-1] // TM_FFN
    tiles = jnp.minimum(jnp.arange(n_tiles, dtype=I32), nact - 1)
    tile_expert = jnp.minimum(
        jnp.sum((gend[None, :] <= tiles[:, None] * TM_FFN).astype(I32), axis=1), e - 1)
    cidx = jnp.arange(BLOCK_CHUNKS, dtype=I32)
    start16 = start // ROW_CHUNK
    shift = (gbase[None, :] + off) // ROW_CHUNK - start16
    dshift = shift - jnp.pad(shift, ((0, 0), (1, 0)))[:, :-1]
    in_or_after = (start16[:, None, :] <= cidx[None, :, None]).astype(I32)
    where = cidx[None, :] + jnp.sum(in_or_after * dshift[:, None, :], axis=2)
    used = cidx[None, :] < nchunk[:, None]
    spare = (n_tiles * TM_FFN // ROW_CHUNK
             + jnp.arange(nblk, dtype=I32)[:, None] * GROUP_CHUNKS + cidx[None, :] % GROUP_CHUNKS)
    gdst = jnp.where(used, where, spare)
    gsrc = jnp.where(used, where, 0)
    ngrp = (nchunk + GROUP_CHUNKS - 1) // GROUP_CHUNKS
    zbase = (gbase + ecount) // ROW_CHUNK
    zn = (epad - ecount) // ROW_CHUNK

    lo = jnp.tile(start.astype(F32), (1, 2))
    hi = jnp.tile((start + pc).astype(F32), (1, 2))
    bounds_row = jnp.stack([jnp.broadcast_to(lo[:, None, :], (nblk, 8, 2 * e)),
                            jnp.broadcast_to(hi[:, None, :], (nblk, 8, 2 * e))], axis=1)
    bounds_col = jnp.stack([jnp.broadcast_to(lo[:, :, None], (nblk, 2 * e, 128)),
                            jnp.broadcast_to(hi[:, :, None], (nblk, 2 * e, 128))], axis=1)
    as_i32 = lambda a: a.astype(I32)
    return dict(ngrp=as_i32(ngrp), gdst=as_i32(gdst.reshape(-1)), gsrc=as_i32(gsrc.reshape(-1)),
                zn=as_i32(zn), zbase=as_i32(zbase), tile_expert=as_i32(tile_expert),
                nact=as_i32(nact.reshape(1)), bounds_row=bounds_row, bounds_col=bounds_col)


def _dispatch_kernel(ngrp_ref, gdst_ref, zn_ref, zbase_ref, h_ref, em_ref, brow_ref,
                     xs_ref, buf_ref, zero_ref, sem, zsem):
    j = pl.program_id(0)
    last = pl.num_programs(0) - 1
    slot = j % 2

    def chunk_copy(s, c, g):
        return pltpu.make_async_copy(buf_ref.at[s, c], xs_ref.at[g], sem.at[s])

    def wait_block(jj, s):
        def body(i, carry):
            pltpu.make_async_copy(buf_ref.at[s, pl.ds(0, GROUP_CHUNKS)],
                                  xs_ref.at[pl.ds(0, GROUP_CHUNKS)], sem.at[s]).wait()
            return carry
        lax.fori_loop(0, ngrp_ref[jj], body, 0)

    def zero_copy(g, chunks):
        return pltpu.make_async_copy(zero_ref.at[pl.ds(0, chunks)], xs_ref.at[pl.ds(g, chunks)],
                                     zsem.at[0])

    def zero_fill(start):
        big = ZERO_ROWS // ROW_CHUNK

        def per_expert(e, carry):
            n = zn_ref[e]
            base = zbase_ref[e]

            def big_copy(i, c2):
                cp = zero_copy(base + i * big, big)
                cp.start() if start else cp.wait()
                return c2
            lax.fori_loop(0, n // big, big_copy, 0)

            def small_copy(i, c2):
                cp = zero_copy(base + n // big * big + i, 1)
                cp.start() if start else cp.wait()
                return c2
            lax.fori_loop(0, n % big, small_copy, 0)
            return carry
        lax.fori_loop(0, N_EXPERTS, per_expert, 0)

    @pl.when(j == 0)
    def _():
        zero_ref[...] = jnp.zeros_like(zero_ref)
        zero_fill(True)

    @pl.when(j >= 2)
    def _():
        wait_block(j - 2, slot)

    lo = brow_ref[0, 0:1, :]
    hi = brow_ref[1, 0:1, :]
    first_copy = lax.broadcasted_iota(I32, (KC, 2 * N_EXPERTS), 1) < N_EXPERTS
    row_iota = lax.broadcasted_iota(I32, (KC, 2 * N_EXPERTS), 0)
    ranks = em_ref[...].astype(BF16)
    h = h_ref[...]

    def trip(ti, carry):
        onehots = []
        for u in range(GROUP_ROWS // KC):
            rows_e = (row_iota + ti * GROUP_ROWS + u * KC).astype(F32)
            in_run = jnp.logical_and(jnp.logical_and(rows_e >= lo, rows_e < hi), first_copy)
            sel = jnp.dot(jnp.where(in_run, 1.0, 0.0).astype(BF16), ranks,
                          preferred_element_type=F32)
            local = jnp.sum(jnp.where(in_run, rows_e - lo + 3.0, 0.0), axis=1,
                            keepdims=True) - 3.0
            onehots.append(jnp.where(sel == local, 1.0, 0.0).astype(BF16))
        xs = jnp.dot(jnp.concatenate(onehots, axis=0), h, preferred_element_type=F32)
        first = pl.multiple_of(ti * GROUP_CHUNKS, GROUP_CHUNKS)
        buf_ref[slot, pl.ds(first, GROUP_CHUNKS)] = xs.astype(BF16).reshape(
            GROUP_CHUNKS, ROW_CHUNK, xs.shape[1])
        for k in range(GROUP_CHUNKS):
            chunk_copy(slot, first + k, gdst_ref[j * BLOCK_CHUNKS + first + k]).start()
        return carry
    lax.fori_loop(0, ngrp_ref[j], trip, 0)

    @pl.when(j == last)
    def _():
        wait_block(j, slot)

        @pl.when(j >= 1)
        def _():
            wait_block(j - 1, 1 - slot)

        zero_fill(False)


def _bounds_specs():
    return [pl.BlockSpec((None, 2, 8, 2 * N_EXPERTS), lambda j, *_: (j, 0, 0, 0)),
            pl.BlockSpec((None, 2, 2 * N_EXPERTS, 128), lambda j, *_: (j, 0, 0, 0))]


def _dispatch(plan, h2_flat, em, n_rows):
    t, d = h2_flat.shape
    grid_spec = pltpu.PrefetchScalarGridSpec(
        num_scalar_prefetch=4,
        grid=(t // TB,),
        in_specs=[pl.BlockSpec((TB, d), lambda j, *_: (j, 0)),
                  pl.BlockSpec((2 * N_EXPERTS, TB), lambda j, *_: (0, j))] + _bounds_specs()[:1],
        out_specs=pl.BlockSpec(memory_space=pl.ANY),
        scratch_shapes=[pltpu.VMEM((2, BLOCK_CHUNKS, ROW_CHUNK, d), BF16),
                        pltpu.VMEM((ZERO_ROWS // ROW_CHUNK, ROW_CHUNK, d), BF16),
                        pltpu.SemaphoreType.DMA((2,)),
                        pltpu.SemaphoreType.DMA((1,))],
    )
    xs = pl.pallas_call(
        _dispatch_kernel,
        grid_spec=grid_spec,
        out_shape=jax.ShapeDtypeStruct((n_rows // ROW_CHUNK, ROW_CHUNK, d), BF16),
        compiler_params=pltpu.CompilerParams(
            dimension_semantics=("arbitrary",), vmem_limit_bytes=VMEM_LIMIT),
        name="moe_dispatch",
    )(plan["ngrp"], plan["gdst"], plan["zn"], plan["zbase"], h2_flat, em, plan["bounds_row"])
    return xs.reshape(n_rows, d)


def _ffn_kernel(te_ref, nact_ref, x_ref, wg_ref, wu_ref, wd_ref, y_ref, wgu_s, wd_s):
    i = pl.program_id(0)

    @pl.when(i < nact_ref[0])
    def _():
        @pl.when(jnp.logical_or(i == 0, te_ref[i] != te_ref[jnp.maximum(i - 1, 0)]))
        def _():
            wgu_s[:, :D_EXPERT] = wg_ref[...].astype(BF16)
            wgu_s[:, D_EXPERT:] = wu_ref[...].astype(BF16)
            wd_s[...] = wd_ref[...].astype(BF16)

        gu = jnp.dot(x_ref[...], wgu_s[...], preferred_element_type=F32)
        act = _silu(gu[:, :D_EXPERT]) * gu[:, D_EXPERT:]
        y_ref[...] = jnp.dot(act.astype(BF16), wd_s[...], preferred_element_type=F32).astype(BF16)


def _expert_ffn(tile_expert, nact, xs, n_tiles, w_gate, w_up, w_down):
    n_rows, d = n_tiles * TM_FFN, xs.shape[1]
    f = D_EXPERT
    row_tile = lambda i, te, na: (jnp.minimum(i, na[0] - 1), 0)
    grid_spec = pltpu.PrefetchScalarGridSpec(
        num_scalar_prefetch=2,
        grid=(n_tiles,),
        in_specs=[pl.BlockSpec((TM_FFN, d), row_tile),
                  pl.BlockSpec((None, d, f), lambda i, te, na: (te[i], 0, 0)),
                  pl.BlockSpec((None, d, f), lambda i, te, na: (te[i], 0, 0)),
                  pl.BlockSpec((None, f, d), lambda i, te, na: (te[i], 0, 0))],
        out_specs=pl.BlockSpec((TM_FFN, d), row_tile),
        scratch_shapes=[pltpu.VMEM((d, 2 * f), BF16), pltpu.VMEM((f, d), BF16)],
    )
    return pl.pallas_call(
        _ffn_kernel,
        grid_spec=grid_spec,
        out_shape=jax.ShapeDtypeStruct((n_rows, d), BF16),
        compiler_params=pltpu.CompilerParams(
            dimension_semantics=("arbitrary",), vmem_limit_bytes=VMEM_LIMIT),
        name="moe_experts",
    )(tile_expert, nact, xs, w_gate, w_up, w_down)


def _combine_kernel(ngrp_ref, gsrc_ref, ys_ref, tm_ref, brow_ref, bcol_ref, h_ref, wsg_ref,
                    wsu_ref, wsd_ref, x1_ref, mod_ref, gpost_ref, o_ref, buf_ref, sem):
    j = pl.program_id(0)
    last = pl.num_programs(0) - 1
    slot = j % 2

    def chunk_copy(s, c, g):
        return pltpu.make_async_copy(ys_ref.at[g], buf_ref.at[s, c], sem.at[s])

    def fetch_group(jj, s, gi):
        for k in range(GROUP_CHUNKS):
            c = gi * GROUP_CHUNKS + k
            chunk_copy(s, c, gsrc_ref[jj * BLOCK_CHUNKS + c]).start()

    @pl.when(j == 0)
    def _():
        lax.fori_loop(0, ngrp_ref[0], lambda gi, c: (fetch_group(0, 0, gi), c)[1], 0)

    h = h_ref[...]
    act = _silu(jnp.dot(h, wsg_ref[...], preferred_element_type=F32)) * jnp.dot(
        h, wsu_ref[...], preferred_element_type=F32)
    y = jnp.dot(act.astype(BF16), wsd_ref[...], preferred_element_type=F32)

    tm = tm_ref[...]
    lane = lax.broadcasted_iota(I32, tm.shape, 1)
    rank_t = jnp.where(lane < N_EXPERTS, tm, 0.0).astype(BF16)
    gate_t = jnp.where(lane < N_EXPERTS, 0.0, tm).astype(BF16)
    lane8 = lax.broadcasted_iota(I32, (8, 2 * N_EXPERTS), 1)
    lo_row = jnp.where(lane8 < N_EXPERTS, brow_ref[0], 0.0).astype(BF16)
    lo_col = jnp.concatenate([bcol_ref[0]] * (KC // 128), axis=1)
    hi_col = jnp.concatenate([bcol_ref[1]] * (KC // 128), axis=1)
    lane_iota = lax.broadcasted_iota(I32, (2 * N_EXPERTS, KC), 1)

    n_here = ngrp_ref[j]

    def wait_group(gi, carry):
        pltpu.make_async_copy(ys_ref.at[pl.ds(0, GROUP_CHUNKS)],
                              buf_ref.at[slot, pl.ds(0, GROUP_CHUNKS)], sem.at[slot]).wait()
        return carry
    lax.fori_loop(0, n_here, wait_group, 0)

    def trip(gi, y, fetch_next):
        pieces = []
        for u in range(GROUP_ROWS // KC):
            rows = (lane_iota + gi * GROUP_ROWS + u * KC).astype(F32)
            in_run = jnp.where(jnp.logical_and(rows >= lo_col, rows < hi_col),
                               1.0, 0.0).astype(BF16)
            sel_rank = jnp.dot(rank_t, in_run, preferred_element_type=F32)
            sel_gate = jnp.dot(gate_t, in_run, preferred_element_type=F32)
            sel_lo = jnp.dot(lo_row, in_run, preferred_element_type=F32)
            local = rows[0:1, :] - sel_lo[0:1, :]
            pieces.append(jnp.where(sel_rank == local, sel_gate, 0.0).astype(BF16))
        first = pl.multiple_of(gi * GROUP_CHUNKS, GROUP_CHUNKS)
        rows = buf_ref[slot, pl.ds(first, GROUP_CHUNKS)].reshape(GROUP_ROWS, buf_ref.shape[3])
        y = y + jnp.dot(jnp.concatenate(pieces, axis=1), rows, preferred_element_type=F32)
        if fetch_next:
            fetch_group(j + 1, 1 - slot, gi)
        return y

    n_next = jnp.where(j < last, ngrp_ref[jnp.minimum(j + 1, last)], 0)
    n_both = jnp.minimum(n_here, n_next)
    y = lax.fori_loop(0, n_both, lambda gi, y: trip(gi, y, True), y)
    y = lax.fori_loop(n_both, n_here, lambda gi, y: trip(gi, y, False), y)
    lax.fori_loop(n_both, n_next, lambda gi, c: (fetch_group(j + 1, 1 - slot, gi), c)[1], 0)

    ga2 = mod_ref[5:6, :]
    o_ref[...] = x1_ref[...] + ga2 * _rms(y, gpost_ref[...])


def _combine(plan, ys, tm, h2_flat, ws_gate, ws_up, ws_down, x1_flat, mod3, g_post, s):
    t, d = h2_flat.shape
    f = ws_gate.shape[1]
    blocks_per_seq = s // TB
    full = lambda shape: pl.BlockSpec(shape, lambda j, *_: (0,) * len(shape))
    grid_spec = pltpu.PrefetchScalarGridSpec(
        num_scalar_prefetch=2,
        grid=(t // TB,),
        in_specs=[pl.BlockSpec(memory_space=pl.ANY),
                  pl.BlockSpec((TB, 2 * N_EXPERTS), lambda j, *_: (j, 0))] + _bounds_specs() + [
                  pl.BlockSpec((TB, d), lambda j, *_: (j, 0)),
                  full((d, f)), full((d, f)), full((f, d)),
                  pl.BlockSpec((TB, d), lambda j, *_: (j, 0)),
                  pl.BlockSpec((None, 6, d), lambda j, *_: (j // blocks_per_seq, 0, 0)),
                  full((1, d))],
        out_specs=pl.BlockSpec((TB, d), lambda j, *_: (j, 0)),
        scratch_shapes=[pltpu.VMEM((2, BLOCK_CHUNKS, ROW_CHUNK, d), BF16),
                        pltpu.SemaphoreType.DMA((2,))],
    )
    return pl.pallas_call(
        _combine_kernel,
        grid_spec=grid_spec,
        out_shape=jax.ShapeDtypeStruct((t, d), F32),
        compiler_params=pltpu.CompilerParams(
            dimension_semantics=("arbitrary",), vmem_limit_bytes=VMEM_LIMIT),
        name="moe_combine",
    )(plan["ngrp"], plan["gsrc"], ys.reshape(-1, ROW_CHUNK, d), tm, plan["bounds_row"],
      plan["bounds_col"], h2_flat, ws_gate, ws_up, ws_down, x1_flat, mod3, g_post)


def kernel(x, c, w_ada, b_ada, g_pre_mix, w_in, ln_sgu_g, ln_sgu_b, w_spatial, b_spatial,
           g_branch, w_out, g_post_mix, g_pre_ffn, w_router, router_bias, w_gate, w_up, w_down,
           ws_gate, ws_up, ws_down, g_post_ffn):
    bsz, s, d = x.shape
    t = bsz * s
    nblk = t // TB
    max_rows = t * TOP_K + nblk * N_EXPERTS * (ROW_CHUNK - 1) + N_EXPERTS * (TM_FFN - 1)
    n_tiles = -(-max_rows // TM_FFN)
    row = lambda a: a.reshape(1, -1)
    for l in range(w_ada.shape[0]):
        mod3 = _ada(c, w_ada[l], b_ada[l]).reshape(bsz, 6, d)
        og, q, k, v = _premix(x, mod3, row(g_pre_mix[l]), w_in[l].astype(BF16),
                              row(ln_sgu_g[l]), row(ln_sgu_b[l]), w_spatial[l],
                              b_spatial[l].T, row(g_branch[l, :D_GMLP]))
        osb = _attention(q, k, v)
        x1, h2, logits_t = _postmix(og, osb, x, mod3, row(g_branch[l, D_GMLP:]),
                                    w_out[l].astype(BF16), row(g_post_mix[l]),
                                    row(g_pre_ffn[l]), w_router[l].T)
        em, tm, cnt = _route(logits_t, router_bias[l])
        plan = _dispatch_plan(cnt[:, :, 0].astype(I32), n_tiles)
        h2_flat = h2.reshape(t, d)
        xs = _dispatch(plan, h2_flat, em, n_tiles * TM_FFN + nblk * GROUP_ROWS)
        ys = _expert_ffn(plan["tile_expert"], plan["nact"], xs, n_tiles,
                         w_gate[l], w_up[l], w_down[l])
        out = _combine(plan, ys, tm, h2_flat, ws_gate[l].astype(BF16), ws_up[l].astype(BF16),
                       ws_down[l].astype(BF16), x1.reshape(t, d), mod3, row(g_post_ffn[l]), s)
        x = out.reshape(bsz, s, d)
    return x
```

```python
import jax
import jax.numpy as jnp
from jax import lax
from jax.experimental import pallas as pl
from jax.experimental.pallas import tpu as pltpu

F32 = jnp.float32
BF16 = jnp.bfloat16
I32 = jnp.int32

D_MODEL = 1024
D_GMLP = 512
GMLP_GROUPS = 4
GMLP_BLOCK = 128
CHUNK = 64
D_SB = 512
SB_HEAD_DIM = 64
N_PAIRS = D_SB // 128
N_EXPERTS = 64
N_GROUPS = 8
GROUP_SIZE = N_EXPERTS // N_GROUPS
TOPK_GROUPS = 4
TOP_K = 8
D_EXPERT = 256
ROUTED_SCALE = 2.5
EPS = 1e-6
D_IN = 2 * D_GMLP + 3 * D_SB

TM_MIX = 512
TQ = 128
VMEM_LIMIT = 56 * 1024 * 1024
ATTN_DEAD_LOG = -110.0

TB = 256
ROW_CHUNK = 16
KC = 256
GROUP_CHUNKS = 32
GROUP_ROWS = GROUP_CHUNKS * ROW_CHUNK
BLOCK_ROWS = -(-(TB * TOP_K + N_EXPERTS * (ROW_CHUNK - 1)) // GROUP_ROWS) * GROUP_ROWS
BLOCK_CHUNKS = BLOCK_ROWS // ROW_CHUNK
TM_FFN = 512
ZERO_ROWS = 128


def _rms(x, g):
    return x * lax.rsqrt(jnp.mean(x * x, axis=-1, keepdims=True) + EPS) * g


def _silu(x):
    return x * jax.nn.sigmoid(x)


def _ada_kernel(c_ref, w_ref, b_ref, o_ref):
    o_ref[...] = jnp.dot(_silu(c_ref[...]), w_ref[...], preferred_element_type=F32,
                         precision=lax.Precision.HIGHEST) + b_ref[...]


def _ada(c, w, b):
    bsz, d = c.shape
    n = w.shape[1]
    tn = 512
    return pl.pallas_call(
        _ada_kernel,
        grid=(n // tn,),
        in_specs=[pl.BlockSpec((bsz, d), lambda j: (0, 0)),
                  pl.BlockSpec((d, tn), lambda j: (0, j)),
                  pl.BlockSpec((1, tn), lambda j: (0, j))],
        out_specs=pl.BlockSpec((bsz, tn), lambda j: (0, j)),
        out_shape=jax.ShapeDtypeStruct((bsz, n), F32),
        name="ada_mod",
    )(c, w, b.reshape(1, n))


def _premix_kernel(x_ref, mod_ref, gpre_ref, win_ref, lng_ref, lnb_ref, wsp_ref, bsp_ref,
                   gbr_ref, og_ref, q_ref, k_ref, v_ref):
    x = x_ref[...]
    sh1 = mod_ref[0:1, :]
    sc1 = mod_ref[1:2, :]
    h = _rms(x, gpre_ref[...]) * (1.0 + sc1) + sh1
    proj = jnp.dot(h.astype(BF16), win_ref[...], preferred_element_type=F32)

    u = jax.nn.gelu(proj[:, :D_GMLP])
    v = jax.nn.gelu(proj[:, D_GMLP:2 * D_GMLP])
    mu = jnp.mean(v, axis=-1, keepdims=True)
    var = jnp.mean(jnp.square(v - mu), axis=-1, keepdims=True)
    v = ((v - mu) * lax.rsqrt(var + EPS) * lng_ref[...] + lnb_ref[...]).astype(BF16)

    i = lax.broadcasted_iota(I32, (GMLP_BLOCK, GMLP_BLOCK), 0)
    j = lax.broadcasted_iota(I32, (GMLP_BLOCK, GMLP_BLOCK), 1)
    causal = (j // CHUNK) <= (i // CHUNK)
    gd = D_GMLP // GMLP_GROUPS
    blocks = []
    for nb in range(x.shape[0] // GMLP_BLOCK):
        rows = slice(nb * GMLP_BLOCK, (nb + 1) * GMLP_BLOCK)
        cols = []
        for g in range(GMLP_GROUPS):
            w = jnp.where(causal, wsp_ref[g], 0.0).astype(BF16)
            mixed = jnp.dot(w, v[rows, g * gd:(g + 1) * gd], preferred_element_type=F32)
            cols.append(mixed + bsp_ref[:, g:g + 1])
        blocks.append(u[rows, :] * jnp.concatenate(cols, axis=1))
    og = jnp.concatenate(blocks, axis=0)
    og_ref[...] = _rms(og, gbr_ref[...]).astype(BF16)

    base = 2 * D_GMLP
    scale = SB_HEAD_DIM ** -0.5
    for p in range(N_PAIRS):
        q_ref[p] = (proj[:, base + 128 * p:base + 128 * (p + 1)] * scale).astype(BF16)
        k_ref[p] = proj[:, base + D_SB + 128 * p:base + D_SB + 128 * (p + 1)].astype(BF16)
        v_ref[p] = proj[:, base + 2 * D_SB + 128 * p:base + 2 * D_SB + 128 * (p + 1)].astype(BF16)


def _premix(x, mod3, g_pre, w_in, ln_g, ln_b, w_sp, b_sp_t, g_br):
    bsz, s, d = x.shape
    tm = TM_MIX
    full = lambda shape: pl.BlockSpec(shape, lambda b, i: (0,) * len(shape))
    qkv_spec = pl.BlockSpec((None, N_PAIRS, tm, 128), lambda b, i: (b, 0, i, 0))
    qkv_shape = jax.ShapeDtypeStruct((bsz, N_PAIRS, s, 128), BF16)
    return pl.pallas_call(
        _premix_kernel,
        grid=(bsz, s // tm),
        in_specs=[pl.BlockSpec((None, tm, d), lambda b, i: (b, i, 0)),
                  pl.BlockSpec((None, 6, d), lambda b, i: (b, 0, 0)),
                  full((1, d)), full((d, D_IN)), full((1, D_GMLP)), full((1, D_GMLP)),
                  full((GMLP_GROUPS, GMLP_BLOCK, GMLP_BLOCK)), full((GMLP_BLOCK, GMLP_GROUPS)),
                  full((1, D_GMLP))],
        out_specs=[pl.BlockSpec((None, tm, D_GMLP), lambda b, i: (b, i, 0)),
                   qkv_spec, qkv_spec, qkv_spec],
        out_shape=[jax.ShapeDtypeStruct((bsz, s, D_GMLP), BF16), qkv_shape, qkv_shape, qkv_shape],
        compiler_params=pltpu.CompilerParams(
            dimension_semantics=("parallel", "parallel"), vmem_limit_bytes=VMEM_LIMIT),
        name="premix",
    )(x, mod3, g_pre, w_in, ln_g, ln_b, w_sp, b_sp_t, g_br)


def _attn_kernel(q_ref, k_ref, v_ref, o_ref, *scratch):
    qs_refs = scratch[:N_PAIRS]
    acc_refs = scratch[N_PAIRS:2 * N_PAIRS]
    carry_refs = scratch[2 * N_PAIRS:]
    qi = pl.program_id(1)
    first_head = lax.broadcasted_iota(I32, (TQ, 128), 1) < SB_HEAD_DIM
    for p in range(N_PAIRS):
        q2 = q_ref[p]
        zero = jnp.zeros_like(q2)
        qs_refs[p][:TQ, :] = jnp.where(first_head, q2, zero)
        qs_refs[p][TQ:, :] = jnp.where(first_head, zero, q2)
        acc_refs[p][...] = jnp.zeros_like(acc_refs[p])
        carry_refs[p][...] = jnp.zeros_like(carry_refs[p])

    r = jnp.bitwise_and(lax.broadcasted_iota(I32, (2 * TQ, TQ), 0), TQ - 1)
    c = lax.broadcasted_iota(I32, (2 * TQ, TQ), 1)
    strict_causal = c < r
    kr = lax.broadcasted_iota(I32, (TQ, TQ), 0)
    kc = lax.broadcasted_iota(I32, (TQ, TQ), 1)
    suffix = jnp.concatenate([(kr > kc).astype(BF16), jnp.ones((TQ, TQ), BF16)], axis=1)
    suffix2 = jnp.concatenate([suffix, suffix], axis=0)

    def key_block(j, diagonal):
        start = pl.multiple_of(j * TQ, TQ)
        pairs = range(N_PAIRS)
        zs = [lax.dot_general(qs_refs[p][...], k_ref[p, pl.ds(start, TQ), :],
                              (((1,), (1,)), ((), ())), preferred_element_type=F32)
              for p in pairs]
        log_betas, splits = [], []
        for p in pairs:
            z = zs[p]
            log_beta = jnp.minimum(z, 0.0) - jnp.log(1.0 + jnp.exp(-jnp.abs(z)))
            log_1mb = log_beta - z
            if diagonal:
                log_1mb = jnp.where(strict_causal, log_1mb, 0.0)
            hi = log_1mb.astype(BF16)
            lo = (log_1mb - hi.astype(F32)).astype(BF16)
            log_betas.append(log_beta)
            splits.append(jnp.concatenate([hi, lo], axis=1))
        sums = [jnp.dot(splits[p], suffix2, preferred_element_type=F32) for p in pairs]
        weights = []
        live = None
        for p in pairs:
            s = sums[p]
            carry = carry_refs[p][...]
            a = jnp.exp(log_betas[p] + carry + s[:, :TQ])
            if diagonal:
                a = jnp.where(strict_causal, a, 0.0)
            weights.append(a.astype(BF16))
            carry = carry + s[:, TQ:]
            carry_refs[p][...] = carry
            live = carry if live is None else jnp.maximum(live, carry)
        for p in pairs:
            acc_refs[p][...] += jnp.dot(weights[p], v_ref[p, pl.ds(start, TQ), :],
                                        preferred_element_type=F32)
        return jnp.max(live)

    live = key_block(qi, True)

    def cond(state):
        j, live = state
        return jnp.logical_and(j >= 0, live > ATTN_DEAD_LOG)

    def body(state):
        j, _ = state
        return j - 1, key_block(j, False)

    lax.while_loop(cond, body, (qi - 1, live))
    for p in range(N_PAIRS):
        o_ref[:, 128 * p:128 * (p + 1)] = jnp.where(first_head, acc_refs[p][:TQ, :],
                                                    acc_refs[p][TQ:, :])


def _attention(q, k, v):
    bsz, npair, s, _ = q.shape
    kv_spec = pl.BlockSpec((None, npair, s, 128), lambda b, i: (b, 0, 0, 0))
    return pl.pallas_call(
        _attn_kernel,
        grid=(bsz, s // TQ),
        in_specs=[pl.BlockSpec((None, npair, TQ, 128), lambda b, i: (b, 0, i, 0)),
                  kv_spec, kv_spec],
        out_specs=pl.BlockSpec((None, TQ, npair * 128), lambda b, i: (b, i, 0)),
        out_shape=jax.ShapeDtypeStruct((bsz, s, npair * 128), F32),
        scratch_shapes=([pltpu.VMEM((2 * TQ, 128), BF16)] * npair
                        + [pltpu.VMEM((2 * TQ, 128), F32)] * npair
                        + [pltpu.VMEM((2 * TQ, TQ), F32)] * npair),
        compiler_params=pltpu.CompilerParams(
            dimension_semantics=("parallel", "parallel"), vmem_limit_bytes=VMEM_LIMIT),
        name="stickbreak_attn",
    )(q, k, v)


def _postmix_kernel(og_ref, osb_ref, x_ref, mod_ref, gbr_ref, wout_ref, gpost_ref, gpre_ref,
                    wrt_ref, x1_ref, h2_ref, logit_ref):
    ga1 = mod_ref[2:3, :]
    sh2 = mod_ref[3:4, :]
    sc2 = mod_ref[4:5, :]
    osb = _rms(osb_ref[...], gbr_ref[...]).astype(BF16)
    m = (jnp.dot(og_ref[...], wout_ref[:D_GMLP, :], preferred_element_type=F32)
         + jnp.dot(osb, wout_ref[D_GMLP:, :], preferred_element_type=F32))
    x1 = x_ref[...] + ga1 * _rms(m, gpost_ref[...])
    x1_ref[...] = x1
    h2 = _rms(x1, gpre_ref[...]) * (1.0 + sc2) + sh2
    h_hi = h2.astype(BF16)
    h2_ref[...] = h_hi
    h_lo = (h2 - h_hi.astype(F32)).astype(BF16)
    w = wrt_ref[...]
    w_hi = w.astype(BF16)
    w_lo = (w - w_hi.astype(F32)).astype(BF16)
    nt = (((1,), (1,)), ((), ()))
    by_hi = lax.dot_general(jnp.concatenate([w_hi, w_lo], axis=0), h_hi, nt,
                            preferred_element_type=F32)
    by_lo = lax.dot_general(w_hi, h_lo, nt, preferred_element_type=F32)
    logit_ref[...] = by_hi[:N_EXPERTS] + by_hi[N_EXPERTS:] + by_lo


def _postmix(og, osb, x, mod3, g_br, w_out, g_post, g_pre, w_router_t):
    bsz, s, d = x.shape
    tm = TM_MIX
    nt = s // tm
    full = lambda shape: pl.BlockSpec(shape, lambda b, i: (0,) * len(shape))
    return pl.pallas_call(
        _postmix_kernel,
        grid=(bsz, nt),
        in_specs=[pl.BlockSpec((None, tm, D_GMLP), lambda b, i: (b, i, 0)),
                  pl.BlockSpec((None, tm, D_SB), lambda b, i: (b, i, 0)),
                  pl.BlockSpec((None, tm, d), lambda b, i: (b, i, 0)),
                  pl.BlockSpec((None, 6, d), lambda b, i: (b, 0, 0)),
                  full((1, D_SB)), full((d, d)), full((1, d)), full((1, d)),
                  full((N_EXPERTS, d))],
        out_specs=[pl.BlockSpec((None, tm, d), lambda b, i: (b, i, 0)),
                   pl.BlockSpec((None, tm, d), lambda b, i: (b, i, 0)),
                   pl.BlockSpec((N_EXPERTS, tm), lambda b, i: (0, b * nt + i))],
        out_shape=[jax.ShapeDtypeStruct((bsz, s, d), F32),
                   jax.ShapeDtypeStruct((bsz, s, d), BF16),
                   jax.ShapeDtypeStruct((N_EXPERTS, bsz * s), F32)],
        compiler_params=pltpu.CompilerParams(
            dimension_semantics=("parallel", "parallel"), vmem_limit_bytes=VMEM_LIMIT),
        name="postmix",
    )(og, osb, x, mod3, g_br, w_out, g_post, g_pre, w_router_t)


def _first_index_of_max(x, idx, axis, size):
    m = jnp.max(x, axis=axis, keepdims=True)
    return jnp.min(jnp.where(x == m, idx, size), axis=axis, keepdims=True)


def _route_kernel(logit_ref, bias_ref, em_ref, tm_ref, cnt_ref):
    scores = jax.nn.sigmoid(logit_ref[...])
    biased = scores + bias_ref[...]
    neg = jnp.float32(-jnp.inf)

    grouped = biased.reshape(N_GROUPS, GROUP_SIZE, TB)
    within = lax.broadcasted_iota(I32, grouped.shape, 1)
    top1 = jnp.max(grouped, axis=1, keepdims=True)
    first = _first_index_of_max(grouped, within, 1, GROUP_SIZE)
    top2 = jnp.max(jnp.where(within == first, neg, grouped), axis=1, keepdims=True)
    group_score = (top1 + top2).reshape(N_GROUPS, TB)

    gidx = lax.broadcasted_iota(I32, group_score.shape, 0)
    group_on = jnp.zeros(group_score.shape, jnp.bool_)
    for _ in range(TOPK_GROUPS):
        pick = gidx == _first_index_of_max(group_score, gidx, 0, N_GROUPS)
        group_on = jnp.logical_or(group_on, pick)
        group_score = jnp.where(pick, neg, group_score)

    masked = jnp.where(group_on.reshape(N_GROUPS, 1, TB), grouped, neg).reshape(N_EXPERTS, TB)
    eidx = lax.broadcasted_iota(I32, masked.shape, 0)
    chosen = jnp.zeros(masked.shape, jnp.bool_)
    for _ in range(TOP_K):
        pick = eidx == _first_index_of_max(masked, eidx, 0, N_EXPERTS)
        chosen = jnp.logical_or(chosen, pick)
        masked = jnp.where(pick, neg, masked)

    w = jnp.where(chosen, scores, 0.0)
    gates = w / jnp.sum(w, axis=0, keepdims=True) * ROUTED_SCALE

    chosen_f = chosen.astype(F32)
    tr = lax.broadcasted_iota(I32, (TB, TB), 0)
    tc = lax.broadcasted_iota(I32, (TB, TB), 1)
    rank = jnp.dot(chosen_f.astype(BF16), (tr < tc).astype(BF16), preferred_element_type=F32)
    both = jnp.concatenate([jnp.where(chosen, rank, -1.0), gates], axis=0)
    em_ref[...] = both
    tm_ref[...] = both.T
    cnt_ref[...] = jnp.broadcast_to(jnp.sum(chosen_f, axis=1, keepdims=True), (N_EXPERTS, 128))


def _route(logits_t, bias):
    e, t = logits_t.shape
    nblk = t // TB
    return pl.pallas_call(
        _route_kernel,
        grid=(nblk,),
        in_specs=[pl.BlockSpec((e, TB), lambda i: (0, i)),
                  pl.BlockSpec((e, 1), lambda i: (0, 0))],
        out_specs=[pl.BlockSpec((2 * e, TB), lambda i: (0, i)),
                   pl.BlockSpec((TB, 2 * e), lambda i: (i, 0)),
                   pl.BlockSpec((None, e, 128), lambda i: (i, 0, 0))],
        out_shape=[jax.ShapeDtypeStruct((2 * e, t), F32),
                   jax.ShapeDtypeStruct((t, 2 * e), F32),
                   jax.ShapeDtypeStruct((nblk, e, 128), F32)],
        compiler_params=pltpu.CompilerParams(dimension_semantics=("parallel",)),
        name="route",
    )(logits_t, bias.reshape(e, 1))


def _dispatch_plan(cnt, n_tiles):
    nblk, e = cnt.shape
    pc = (cnt + ROW_CHUNK - 1) // ROW_CHUNK * ROW_CHUNK
    start = jnp.cumsum(pc, axis=1) - pc
    nchunk = jnp.sum(pc, axis=1) // ROW_CHUNK
    off = jnp.cumsum(pc, axis=0) - pc
    ecount = jnp.sum(pc, axis=0)
    epad = (ecount + TM_FFN - 1) // TM_FFN * TM_FFN
    gend = jnp.cumsum(epad)
    gbase = gend - epad
    nact = gend[-1] // TM_FFN
    tile_start = jnp.concatenate([gbase, gend[-1:]]) // TM_FFN
    cidx = jnp.arange(BLOCK_CHUNKS, dtype=I32)
    start16 = start // ROW_CHUNK
    shift = (gbase[None, :] + off) // ROW_CHUNK - start16
    dshift = shift - jnp.pad(shift, ((0, 0), (1, 0)))[:, :-1]
    in_or_after = (start16[:, None, :] <= cidx[None, :, None]).astype(I32)
    where = cidx[None, :] + jnp.sum(in_or_after * dshift[:, None, :], axis=2)
    used = cidx[None, :] < nchunk[:, None]
    spare = (n_tiles * TM_FFN // ROW_CHUNK
             + jnp.arange(nblk, dtype=I32)[:, None] * GROUP_CHUNKS + cidx[None, :] % GROUP_CHUNKS)
    gdst = jnp.where(used, where, spare)
    gsrc = jnp.where(used, where, 0)
    ngrp = (nchunk + GROUP_CHUNKS - 1) // GROUP_CHUNKS
    zbase = (gbase + ecount) // ROW_CHUNK
    zn = (epad - ecount) // ROW_CHUNK

    lo = jnp.tile(start.astype(F32), (1, 2))
    hi = jnp.tile((start + pc).astype(F32), (1, 2))
    bounds_row = jnp.stack([jnp.broadcast_to(lo[:, None, :], (nblk, 8, 2 * e)),
                            jnp.broadcast_to(hi[:, None, :], (nblk, 8, 2 * e))], axis=1)
    bounds_col = jnp.stack([jnp.broadcast_to(lo[:, :, None], (nblk, 2 * e, 128)),
                            jnp.broadcast_to(hi[:, :, None], (nblk, 2 * e, 128))], axis=1)
    as_i32 = lambda a: a.astype(I32)
    return dict(ngrp=as_i32(ngrp), gdst=as_i32(gdst.reshape(-1)), gsrc=as_i32(gsrc.reshape(-1)),
                zn=as_i32(zn), zbase=as_i32(zbase), tile_start=as_i32(tile_start),
                nact=as_i32(nact.reshape(1)), bounds_row=bounds_row, bounds_col=bounds_col)


def _dispatch_kernel(ngrp_ref, gdst_ref, zn_ref, zbase_ref, h_ref, em_ref, brow_ref,
                     xs_ref, buf_ref, zero_ref, sem, zsem):
    j = pl.program_id(0)
    last = pl.num_programs(0) - 1
    slot = j % 2

    def chunk_copy(s, c, g):
        return pltpu.make_async_copy(buf_ref.at[s, c], xs_ref.at[g], sem.at[s])

    def wait_block(jj, s):
        def body(i, carry):
            pltpu.make_async_copy(buf_ref.at[s, pl.ds(0, GROUP_CHUNKS)],
                                  xs_ref.at[pl.ds(0, GROUP_CHUNKS)], sem.at[s]).wait()
            return carry
        lax.fori_loop(0, ngrp_ref[jj], body, 0)

    def zero_copy(g, chunks):
        return pltpu.make_async_copy(zero_ref.at[pl.ds(0, chunks)], xs_ref.at[pl.ds(g, chunks)],
                                     zsem.at[0])

    def zero_fill(start):
        big = ZERO_ROWS // ROW_CHUNK

        def per_expert(e, carry):
            n = zn_ref[e]
            base = zbase_ref[e]

            def big_copy(i, c2):
                cp = zero_copy(base + i * big, big)
                cp.start() if start else cp.wait()
                return c2
            lax.fori_loop(0, n // big, big_copy, 0)

            def small_copy(i, c2):
                cp = zero_copy(base + n // big * big + i, 1)
                cp.start() if start else cp.wait()
                return c2
            lax.fori_loop(0, n % big, small_copy, 0)
            return carry
        lax.fori_loop(0, N_EXPERTS, per_expert, 0)

    @pl.when(j == 0)
    def _():
        zero_ref[...] = jnp.zeros_like(zero_ref)
        zero_fill(True)

    @pl.when(j >= 2)
    def _():
        wait_block(j - 2, slot)

    lo = brow_ref[0, 0:1, :]
    hi = brow_ref[1, 0:1, :]
    first_copy = lax.broadcasted_iota(I32, (KC, 2 * N_EXPERTS), 1) < N_EXPERTS
    row_iota = lax.broadcasted_iota(I32, (KC, 2 * N_EXPERTS), 0)
    ranks = em_ref[...].astype(BF16)
    h = h_ref[...]

    def trip(ti, carry):
        onehots = []
        for u in range(GROUP_ROWS // KC):
            rows_e = (row_iota + ti * GROUP_ROWS + u * KC).astype(F32)
            in_run = jnp.logical_and(jnp.logical_and(rows_e >= lo, rows_e < hi), first_copy)
            sel = jnp.dot(jnp.where(in_run, 1.0, 0.0).astype(BF16), ranks,
                          preferred_element_type=F32)
            local = jnp.sum(jnp.where(in_run, rows_e - lo + 3.0, 0.0), axis=1,
                            keepdims=True) - 3.0
            onehots.append(jnp.where(sel == local, 1.0, 0.0).astype(BF16))
        xs = jnp.dot(jnp.concatenate(onehots, axis=0), h, preferred_element_type=F32)
        first = pl.multiple_of(ti * GROUP_CHUNKS, GROUP_CHUNKS)
        buf_ref[slot, pl.ds(first, GROUP_CHUNKS)] = xs.astype(BF16).reshape(
            GROUP_CHUNKS, ROW_CHUNK, xs.shape[1])
        for k in range(GROUP_CHUNKS):
            chunk_copy(slot, first + k, gdst_ref[j * BLOCK_CHUNKS + first + k]).start()
        return carry
    lax.fori_loop(0, ngrp_ref[j], trip, 0)

    @pl.when(j == last)
    def _():
        wait_block(j, slot)

        @pl.when(j >= 1)
        def _():
            wait_block(j - 1, 1 - slot)

        zero_fill(False)


def _bounds_specs():
    return [pl.BlockSpec((None, 2, 8, 2 * N_EXPERTS), lambda j, *_: (j, 0, 0, 0)),
            pl.BlockSpec((None, 2, 2 * N_EXPERTS, 128), lambda j, *_: (j, 0, 0, 0))]


def _dispatch(plan, h2_flat, em, n_rows):
    t, d = h2_flat.shape
    grid_spec = pltpu.PrefetchScalarGridSpec(
        num_scalar_prefetch=4,
        grid=(t // TB,),
        in_specs=[pl.BlockSpec((TB, d), lambda j, *_: (j, 0)),
                  pl.BlockSpec((2 * N_EXPERTS, TB), lambda j, *_: (0, j))] + _bounds_specs()[:1],
        out_specs=pl.BlockSpec(memory_space=pl.ANY),
        scratch_shapes=[pltpu.VMEM((2, BLOCK_CHUNKS, ROW_CHUNK, d), BF16),
                        pltpu.VMEM((ZERO_ROWS // ROW_CHUNK, ROW_CHUNK, d), BF16),
                        pltpu.SemaphoreType.DMA((2,)),
                        pltpu.SemaphoreType.DMA((1,))],
    )
    xs = pl.pallas_call(
        _dispatch_kernel,
        grid_spec=grid_spec,
        out_shape=jax.ShapeDtypeStruct((n_rows // ROW_CHUNK, ROW_CHUNK, d), BF16),
        compiler_params=pltpu.CompilerParams(
            dimension_semantics=("arbitrary",), vmem_limit_bytes=VMEM_LIMIT),
        name="moe_dispatch",
    )(plan["ngrp"], plan["gdst"], plan["zn"], plan["zbase"], h2_flat, em, plan["bounds_row"])
    return xs.reshape(n_rows, d)


def _ffn_kernel(tstart_ref, nact_ref, wg_ref, wu_ref, wd_ref, xs_ref, ys_ref,
                xbuf, ybuf, wgu_s, wd_s, xsem, ysem):
    e = pl.program_id(0)
    nact = nact_ref[0]
    t0 = tstart_ref[e]
    t1 = tstart_ref[e + 1]

    def x_copy(t, s):
        return pltpu.make_async_copy(
            xs_ref.at[pl.ds(pl.multiple_of(t * TM_FFN, TM_FFN), TM_FFN), :], xbuf.at[s],
            xsem.at[s])

    def y_copy(t, s):
        return pltpu.make_async_copy(
            ybuf.at[s], ys_ref.at[pl.ds(pl.multiple_of(t * TM_FFN, TM_FFN), TM_FFN), :],
            ysem.at[s])

    @pl.when(e == 0)
    def _():
        x_copy(0, 0).start()

    @pl.when(t1 > t0)
    def _():
        wgu_s[:, :D_EXPERT] = wg_ref[...].astype(BF16)
        wgu_s[:, D_EXPERT:] = wu_ref[...].astype(BF16)
        wd_s[...] = wd_ref[...].astype(BF16)

        def tile(t, carry):
            s = t % 2
            x_copy(t, s).wait()

            @pl.when(t + 1 < nact)
            def _():
                x_copy(t + 1, 1 - s).start()

            gu = jnp.dot(xbuf[s], wgu_s[...], preferred_element_type=F32)
            act = _silu(gu[:, :D_EXPERT]) * gu[:, D_EXPERT:]
            y = jnp.dot(act.astype(BF16), wd_s[...], preferred_element_type=F32)

            @pl.when(t >= 2)
            def _():
                y_copy(t - 2, s).wait()

            ybuf[s] = y.astype(BF16)
            y_copy(t, s).start()
            return carry
        lax.fori_loop(t0, t1, tile, 0)

    @pl.when(e == pl.num_programs(0) - 1)
    def _():
        @pl.when(nact >= 2)
        def _():
            y_copy(nact - 2, nact % 2).wait()
        y_copy(nact - 1, (nact - 1) % 2).wait()


def _expert_ffn(tile_start, nact, xs, n_tiles, w_gate, w_up, w_down):
    n_rows, d = n_tiles * TM_FFN, xs.shape[1]
    f = D_EXPERT
    grid_spec = pltpu.PrefetchScalarGridSpec(
        num_scalar_prefetch=2,
        grid=(N_EXPERTS,),
        in_specs=[pl.BlockSpec((None, d, f), lambda e, ts, na: (e, 0, 0)),
                  pl.BlockSpec((None, d, f), lambda e, ts, na: (e, 0, 0)),
                  pl.BlockSpec((None, f, d), lambda e, ts, na: (e, 0, 0)),
                  pl.BlockSpec(memory_space=pl.ANY)],
        out_specs=pl.BlockSpec(memory_space=pl.ANY),
        scratch_shapes=[pltpu.VMEM((2, TM_FFN, d), BF16), pltpu.VMEM((2, TM_FFN, d), BF16),
                        pltpu.VMEM((d, 2 * f), BF16), pltpu.VMEM((f, d), BF16),
                        pltpu.SemaphoreType.DMA((2,)), pltpu.SemaphoreType.DMA((2,))],
    )
    return pl.pallas_call(
        _ffn_kernel,
        grid_spec=grid_spec,
        out_shape=jax.ShapeDtypeStruct((n_rows, d), BF16),
        compiler_params=pltpu.CompilerParams(
            dimension_semantics=("arbitrary",), vmem_limit_bytes=VMEM_LIMIT),
        name="moe_experts",
    )(tile_start, nact, w_gate, w_up, w_down, xs)


def _combine_kernel(ngrp_ref, gsrc_ref, ys_ref, tm_ref, brow_ref, bcol_ref, h_ref, wsg_ref,
                    wsu_ref, wsd_ref, x1_ref, mod_ref, gpost_ref, o_ref, buf_ref, p_ref, sem):
    j = pl.program_id(0)
    last = pl.num_programs(0) - 1
    slot = j % 2

    def chunk_copy(s, c, g):
        return pltpu.make_async_copy(ys_ref.at[g], buf_ref.at[s, c], sem.at[s])

    def fetch_block(jj, s):
        def issue_group(gi, carry):
            for k in range(GROUP_CHUNKS):
                c = gi * GROUP_CHUNKS + k
                chunk_copy(s, c, gsrc_ref[jj * BLOCK_CHUNKS + c]).start()
            return carry
        lax.fori_loop(0, ngrp_ref[jj], issue_group, 0)

    @pl.when(j == 0)
    def _():
        buf_ref[...] = jnp.zeros_like(buf_ref)
        fetch_block(0, 0)

    @pl.when(j < last)
    def _():
        fetch_block(j + 1, 1 - slot)

    h = h_ref[...]
    act = _silu(jnp.dot(h, wsg_ref[...], preferred_element_type=F32)) * jnp.dot(
        h, wsu_ref[...], preferred_element_type=F32)
    y = jnp.dot(act.astype(BF16), wsd_ref[...], preferred_element_type=F32)

    tm = tm_ref[...]
    lane = lax.broadcasted_iota(I32, tm.shape, 1)
    rank_t = jnp.where(lane < N_EXPERTS, tm, 0.0).astype(BF16)
    gate_t = jnp.where(lane < N_EXPERTS, 0.0, tm).astype(BF16)
    lane8 = lax.broadcasted_iota(I32, (8, 2 * N_EXPERTS), 1)
    lo_row = jnp.where(lane8 < N_EXPERTS, brow_ref[0], 0.0).astype(BF16)
    lo_col = jnp.concatenate([bcol_ref[0]] * (KC // 128), axis=1)
    hi_col = jnp.concatenate([bcol_ref[1]] * (KC // 128), axis=1)
    for cb in range(BLOCK_ROWS // KC):
        rows = (lax.broadcasted_iota(I32, (2 * N_EXPERTS, KC), 1) + cb * KC).astype(F32)
        in_run = jnp.where(jnp.logical_and(rows >= lo_col, rows < hi_col), 1.0, 0.0).astype(BF16)
        sel_rank = jnp.dot(rank_t, in_run, preferred_element_type=F32)
        sel_gate = jnp.dot(gate_t, in_run, preferred_element_type=F32)
        sel_lo = jnp.dot(lo_row, in_run, preferred_element_type=F32)
        local = rows[0:1, :] - sel_lo[0:1, :]
        p_ref[:, cb * KC:(cb + 1) * KC] = jnp.where(sel_rank == local, sel_gate,
                                                    0.0).astype(BF16)

    def wait_group(gi, carry):
        pltpu.make_async_copy(ys_ref.at[pl.ds(0, GROUP_CHUNKS)],
                              buf_ref.at[slot, pl.ds(0, GROUP_CHUNKS)], sem.at[slot]).wait()
        return carry
    lax.fori_loop(0, ngrp_ref[j], wait_group, 0)

    rows = buf_ref[slot].reshape(BLOCK_ROWS, buf_ref.shape[3])
    y = y + jnp.dot(p_ref[...], rows, preferred_element_type=F32)

    ga2 = mod_ref[5:6, :]
    o_ref[...] = x1_ref[...] + ga2 * _rms(y, gpost_ref[...])


def _combine(plan, ys, tm, h2_flat, ws_gate, ws_up, ws_down, x1_flat, mod3, g_post, s):
    t, d = h2_flat.shape
    f = ws_gate.shape[1]
    blocks_per_seq = s // TB
    full = lambda shape: pl.BlockSpec(shape, lambda j, *_: (0,) * len(shape))
    grid_spec = pltpu.PrefetchScalarGridSpec(
        num_scalar_prefetch=2,
        grid=(t // TB,),
        in_specs=[pl.BlockSpec(memory_space=pl.ANY),
                  pl.BlockSpec((TB, 2 * N_EXPERTS), lambda j, *_: (j, 0))] + _bounds_specs() + [
                  pl.BlockSpec((TB, d), lambda j, *_: (j, 0)),
                  full((d, f)), full((d, f)), full((f, d)),
                  pl.BlockSpec((TB, d), lambda j, *_: (j, 0)),
                  pl.BlockSpec((None, 6, d), lambda j, *_: (j // blocks_per_seq, 0, 0)),
                  full((1, d))],
        out_specs=pl.BlockSpec((TB, d), lambda j, *_: (j, 0)),
        scratch_shapes=[pltpu.VMEM((2, BLOCK_CHUNKS, ROW_CHUNK, d), BF16),
                        pltpu.VMEM((TB, BLOCK_ROWS), BF16),
                        pltpu.SemaphoreType.DMA((2,))],
    )
    return pl.pallas_call(
        _combine_kernel,
        grid_spec=grid_spec,
        out_shape=jax.ShapeDtypeStruct((t, d), F32),
        compiler_params=pltpu.CompilerParams(
            dimension_semantics=("arbitrary",), vmem_limit_bytes=VMEM_LIMIT),
        name="moe_combine",
    )(plan["ngrp"], plan["gsrc"], ys.reshape(-1, ROW_CHUNK, d), tm, plan["bounds_row"],
      plan["bounds_col"], h2_flat, ws_gate, ws_up, ws_down, x1_flat, mod3, g_post)


def kernel(x, c, w_ada, b_ada, g_pre_mix, w_in, ln_sgu_g, ln_sgu_b, w_spatial, b_spatial,
           g_branch, w_out, g_post_mix, g_pre_ffn, w_router, router_bias, w_gate, w_up, w_down,
           ws_gate, ws_up, ws_down, g_post_ffn):
    bsz, s, d = x.shape
    t = bsz * s
    nblk = t // TB
    max_rows = t * TOP_K + nblk * N_EXPERTS * (ROW_CHUNK - 1) + N_EXPERTS * (TM_FFN - 1)
    n_tiles = -(-max_rows // TM_FFN)
    row = lambda a: a.reshape(1, -1)
    for l in range(w_ada.shape[0]):
        mod3 = _ada(c, w_ada[l], b_ada[l]).reshape(bsz, 6, d)
        og, q, k, v = _premix(x, mod3, row(g_pre_mix[l]), w_in[l].astype(BF16),
                              row(ln_sgu_g[l]), row(ln_sgu_b[l]), w_spatial[l],
                              b_spatial[l].T, row(g_branch[l, :D_GMLP]))
        osb = _attention(q, k, v)
        x1, h2, logits_t = _postmix(og, osb, x, mod3, row(g_branch[l, D_GMLP:]),
                                    w_out[l].astype(BF16), row(g_post_mix[l]),
                                    row(g_pre_ffn[l]), w_router[l].T)
        em, tm, cnt = _route(logits_t, router_bias[l])
        plan = _dispatch_plan(cnt[:, :, 0].astype(I32), n_tiles)
        h2_flat = h2.reshape(t, d)
        xs = _dispatch(plan, h2_flat, em, n_tiles * TM_FFN + nblk * GROUP_ROWS)
        ys = _expert_ffn(plan["tile_start"], plan["nact"], xs, n_tiles,
                         w_gate[l], w_up[l], w_down[l])
        out = _combine(plan, ys, tm, h2_flat, ws_gate[l].astype(BF16), ws_up[l].astype(BF16),
                       ws_down[l].astype(BF16), x1.reshape(t, d), mod3, row(g_post_ffn[l]), s)
        x = out.reshape(bsz, s, d)
    return x
```

```python
import jax
import jax.numpy as jnp
from jax import lax
from jax.experimental import pallas as pl
from jax.experimental.pallas import tpu as pltpu

F32 = jnp.float32
BF16 = jnp.bfloat16
I32 = jnp.int32

D_MODEL = 1024
D_GMLP = 512
GMLP_GROUPS = 4
GMLP_BLOCK = 128
CHUNK = 64
D_SB = 512
SB_HEAD_DIM = 64
N_PAIRS = D_SB // 128
N_EXPERTS = 64
N_GROUPS = 8
GROUP_SIZE = N_EXPERTS // N_GROUPS
TOPK_GROUPS = 4
TOP_K = 8
D_EXPERT = 256
ROUTED_SCALE = 2.5
EPS = 1e-6
D_IN = 2 * D_GMLP + 3 * D_SB

TM_MIX = 512
TQ = 128
VMEM_LIMIT = 56 * 1024 * 1024
ATTN_DEAD_LOG = -110.0

TB = 256
ROW_CHUNK = 16
KC = 256
GROUP_CHUNKS = 32
GROUP_ROWS = GROUP_CHUNKS * ROW_CHUNK
BLOCK_ROWS = -(-(TB * TOP_K + N_EXPERTS * (ROW_CHUNK - 1)) // GROUP_ROWS) * GROUP_ROWS
BLOCK_CHUNKS = BLOCK_ROWS // ROW_CHUNK
TM_FFN = 512
ZERO_ROWS = 128


def _rms(x, g):
    return x * lax.rsqrt(jnp.mean(x * x, axis=-1, keepdims=True) + EPS) * g


def _silu(x):
    return x * jax.nn.sigmoid(x)


def _ada_kernel(c_ref, w_ref, b_ref, o_ref):
    o_ref[...] = jnp.dot(_silu(c_ref[...]), w_ref[...], preferred_element_type=F32,
                         precision=lax.Precision.HIGHEST) + b_ref[...]


def _ada(c, w, b):
    bsz, d = c.shape
    n = w.shape[1]
    tn = 512
    return pl.pallas_call(
        _ada_kernel,
        grid=(n // tn,),
        in_specs=[pl.BlockSpec((bsz, d), lambda j: (0, 0)),
                  pl.BlockSpec((d, tn), lambda j: (0, j)),
                  pl.BlockSpec((1, tn), lambda j: (0, j))],
        out_specs=pl.BlockSpec((bsz, tn), lambda j: (0, j)),
        out_shape=jax.ShapeDtypeStruct((bsz, n), F32),
        name="ada_mod",
    )(c, w, b.reshape(1, n))


def _premix_kernel(x_ref, mod_ref, gpre_ref, win_ref, lng_ref, lnb_ref, wsp_ref, bsp_ref,
                   gbr_ref, og_ref, q_ref, k_ref, v_ref):
    x = x_ref[...]
    sh1 = mod_ref[0:1, :]
    sc1 = mod_ref[1:2, :]
    h = _rms(x, gpre_ref[...]) * (1.0 + sc1) + sh1
    proj = jnp.dot(h.astype(BF16), win_ref[...], preferred_element_type=F32)

    u = jax.nn.gelu(proj[:, :D_GMLP])
    v = jax.nn.gelu(proj[:, D_GMLP:2 * D_GMLP])
    mu = jnp.mean(v, axis=-1, keepdims=True)
    var = jnp.mean(jnp.square(v - mu), axis=-1, keepdims=True)
    v = ((v - mu) * lax.rsqrt(var + EPS) * lng_ref[...] + lnb_ref[...]).astype(BF16)

    i = lax.broadcasted_iota(I32, (GMLP_BLOCK, GMLP_BLOCK), 0)
    j = lax.broadcasted_iota(I32, (GMLP_BLOCK, GMLP_BLOCK), 1)
    causal = (j // CHUNK) <= (i // CHUNK)
    gd = D_GMLP // GMLP_GROUPS
    blocks = []
    for nb in range(x.shape[0] // GMLP_BLOCK):
        rows = slice(nb * GMLP_BLOCK, (nb + 1) * GMLP_BLOCK)
        cols = []
        for g in range(GMLP_GROUPS):
            w = jnp.where(causal, wsp_ref[g], 0.0).astype(BF16)
            mixed = jnp.dot(w, v[rows, g * gd:(g + 1) * gd], preferred_element_type=F32)
            cols.append(mixed + bsp_ref[:, g:g + 1])
        blocks.append(u[rows, :] * jnp.concatenate(cols, axis=1))
    og = jnp.concatenate(blocks, axis=0)
    og_ref[...] = _rms(og, gbr_ref[...]).astype(BF16)

    base = 2 * D_GMLP
    scale = SB_HEAD_DIM ** -0.5
    for p in range(N_PAIRS):
        q_ref[p] = (proj[:, base + 128 * p:base + 128 * (p + 1)] * scale).astype(BF16)
        k_ref[p] = proj[:, base + D_SB + 128 * p:base + D_SB + 128 * (p + 1)].astype(BF16)
        v_ref[p] = proj[:, base + 2 * D_SB + 128 * p:base + 2 * D_SB + 128 * (p + 1)].astype(BF16)


def _premix(x, mod3, g_pre, w_in, ln_g, ln_b, w_sp, b_sp_t, g_br):
    bsz, s, d = x.shape
    tm = TM_MIX
    full = lambda shape: pl.BlockSpec(shape, lambda b, i: (0,) * len(shape))
    qkv_spec = pl.BlockSpec((None, N_PAIRS, tm, 128), lambda b, i: (b, 0, i, 0))
    qkv_shape = jax.ShapeDtypeStruct((bsz, N_PAIRS, s, 128), BF16)
    return pl.pallas_call(
        _premix_kernel,
        grid=(bsz, s // tm),
        in_specs=[pl.BlockSpec((None, tm, d), lambda b, i: (b, i, 0)),
                  pl.BlockSpec((None, 6, d), lambda b, i: (b, 0, 0)),
                  full((1, d)), full((d, D_IN)), full((1, D_GMLP)), full((1, D_GMLP)),
                  full((GMLP_GROUPS, GMLP_BLOCK, GMLP_BLOCK)), full((GMLP_BLOCK, GMLP_GROUPS)),
                  full((1, D_GMLP))],
        out_specs=[pl.BlockSpec((None, tm, D_GMLP), lambda b, i: (b, i, 0)),
                   qkv_spec, qkv_spec, qkv_spec],
        out_shape=[jax.ShapeDtypeStruct((bsz, s, D_GMLP), BF16), qkv_shape, qkv_shape, qkv_shape],
        compiler_params=pltpu.CompilerParams(
            dimension_semantics=("parallel", "parallel"), vmem_limit_bytes=VMEM_LIMIT),
        name="premix",
    )(x, mod3, g_pre, w_in, ln_g, ln_b, w_sp, b_sp_t, g_br)


def _attn_kernel(q_ref, k_ref, v_ref, o_ref, *scratch):
    qs_refs = scratch[:N_PAIRS]
    acc_refs = scratch[N_PAIRS:2 * N_PAIRS]
    carry_refs = scratch[2 * N_PAIRS:]
    qi = pl.program_id(1)
    first_head = lax.broadcasted_iota(I32, (TQ, 128), 1) < SB_HEAD_DIM
    for p in range(N_PAIRS):
        q2 = q_ref[p]
        zero = jnp.zeros_like(q2)
        qs_refs[p][:TQ, :] = jnp.where(first_head, q2, zero)
        qs_refs[p][TQ:, :] = jnp.where(first_head, zero, q2)
        acc_refs[p][...] = jnp.zeros_like(acc_refs[p])
        carry_refs[p][...] = jnp.zeros_like(carry_refs[p])

    r = jnp.bitwise_and(lax.broadcasted_iota(I32, (2 * TQ, TQ), 0), TQ - 1)
    c = lax.broadcasted_iota(I32, (2 * TQ, TQ), 1)
    strict_causal = c < r
    kr = lax.broadcasted_iota(I32, (TQ, TQ), 0)
    kc = lax.broadcasted_iota(I32, (TQ, TQ), 1)
    suffix = jnp.concatenate([(kr > kc).astype(BF16), jnp.ones((TQ, TQ), BF16)], axis=1)
    suffix2 = jnp.concatenate([suffix, suffix], axis=0)

    def key_block(j, diagonal):
        start = pl.multiple_of(j * TQ, TQ)
        pairs = range(N_PAIRS)
        zs = [lax.dot_general(qs_refs[p][...], k_ref[p, pl.ds(start, TQ), :],
                              (((1,), (1,)), ((), ())), preferred_element_type=F32)
              for p in pairs]
        log_betas, splits = [], []
        for p in pairs:
            z = zs[p]
            log_beta = jnp.minimum(z, 0.0) - jnp.log(1.0 + jnp.exp(-jnp.abs(z)))
            log_1mb = log_beta - z
            if diagonal:
                log_1mb = jnp.where(strict_causal, log_1mb, 0.0)
            hi = log_1mb.astype(BF16)
            lo = (log_1mb - hi.astype(F32)).astype(BF16)
            log_betas.append(log_beta)
            splits.append(jnp.concatenate([hi, lo], axis=1))
        sums = [jnp.dot(splits[p], suffix2, preferred_element_type=F32) for p in pairs]
        weights = []
        live = None
        for p in pairs:
            s = sums[p]
            carry = carry_refs[p][...]
            a = jnp.exp(log_betas[p] + carry + s[:, :TQ])
            if diagonal:
                a = jnp.where(strict_causal, a, 0.0)
            weights.append(a.astype(BF16))
            carry = carry + s[:, TQ:]
            carry_refs[p][...] = carry
            live = carry if live is None else jnp.maximum(live, carry)
        for p in pairs:
            acc_refs[p][...] += jnp.dot(weights[p], v_ref[p, pl.ds(start, TQ), :],
                                        preferred_element_type=F32)
        return jnp.max(live)

    live = key_block(qi, True)

    def cond(state):
        j, live = state
        return jnp.logical_and(j >= 0, live > ATTN_DEAD_LOG)

    def body(state):
        j, _ = state
        return j - 1, key_block(j, False)

    lax.while_loop(cond, body, (qi - 1, live))
    for p in range(N_PAIRS):
        o_ref[:, 128 * p:128 * (p + 1)] = jnp.where(first_head, acc_refs[p][:TQ, :],
                                                    acc_refs[p][TQ:, :])


def _attention(q, k, v):
    bsz, npair, s, _ = q.shape
    kv_spec = pl.BlockSpec((None, npair, s, 128), lambda b, i: (b, 0, 0, 0))
    return pl.pallas_call(
        _attn_kernel,
        grid=(bsz, s // TQ),
        in_specs=[pl.BlockSpec((None, npair, TQ, 128), lambda b, i: (b, 0, i, 0)),
                  kv_spec, kv_spec],
        out_specs=pl.BlockSpec((None, TQ, npair * 128), lambda b, i: (b, i, 0)),
        out_shape=jax.ShapeDtypeStruct((bsz, s, npair * 128), F32),
        scratch_shapes=([pltpu.VMEM((2 * TQ, 128), BF16)] * npair
                        + [pltpu.VMEM((2 * TQ, 128), F32)] * npair
                        + [pltpu.VMEM((2 * TQ, TQ), F32)] * npair),
        compiler_params=pltpu.CompilerParams(
            dimension_semantics=("parallel", "parallel"), vmem_limit_bytes=VMEM_LIMIT),
        name="stickbreak_attn",
    )(q, k, v)


def _postmix_kernel(og_ref, osb_ref, x_ref, mod_ref, gbr_ref, wout_ref, gpost_ref, gpre_ref,
                    wrt_ref, x1_ref, h2_ref, logit_ref):
    ga1 = mod_ref[2:3, :]
    sh2 = mod_ref[3:4, :]
    sc2 = mod_ref[4:5, :]
    osb = _rms(osb_ref[...], gbr_ref[...]).astype(BF16)
    m = (jnp.dot(og_ref[...], wout_ref[:D_GMLP, :], preferred_element_type=F32)
         + jnp.dot(osb, wout_ref[D_GMLP:, :], preferred_element_type=F32))
    x1 = x_ref[...] + ga1 * _rms(m, gpost_ref[...])
    x1_ref[...] = x1
    h2 = _rms(x1, gpre_ref[...]) * (1.0 + sc2) + sh2
    h_hi = h2.astype(BF16)
    h2_ref[...] = h_hi
    h_lo = (h2 - h_hi.astype(F32)).astype(BF16)
    w = wrt_ref[...]
    w_hi = w.astype(BF16)
    w_lo = (w - w_hi.astype(F32)).astype(BF16)
    nt = (((1,), (1,)), ((), ()))
    by_hi = lax.dot_general(jnp.concatenate([w_hi, w_lo], axis=0), h_hi, nt,
                            preferred_element_type=F32)
    by_lo = lax.dot_general(w_hi, h_lo, nt, preferred_element_type=F32)
    logit_ref[...] = by_hi[:N_EXPERTS] + by_hi[N_EXPERTS:] + by_lo


def _postmix(og, osb, x, mod3, g_br, w_out, g_post, g_pre, w_router_t):
    bsz, s, d = x.shape
    tm = TM_MIX
    nt = s // tm
    full = lambda shape: pl.BlockSpec(shape, lambda b, i: (0,) * len(shape))
    return pl.pallas_call(
        _postmix_kernel,
        grid=(bsz, nt),
        in_specs=[pl.BlockSpec((None, tm, D_GMLP), lambda b, i: (b, i, 0)),
                  pl.BlockSpec((None, tm, D_SB), lambda b, i: (b, i, 0)),
                  pl.BlockSpec((None, tm, d), lambda b, i: (b, i, 0)),
                  pl.BlockSpec((None, 6, d), lambda b, i: (b, 0, 0)),
                  full((1, D_SB)), full((d, d)), full((1, d)), full((1, d)),
                  full((N_EXPERTS, d))],
        out_specs=[pl.BlockSpec((None, tm, d), lambda b, i: (b, i, 0)),
                   pl.BlockSpec((None, tm, d), lambda b, i: (b, i, 0)),
                   pl.BlockSpec((N_EXPERTS, tm), lambda b, i: (0, b * nt + i))],
        out_shape=[jax.ShapeDtypeStruct((bsz, s, d), F32),
                   jax.ShapeDtypeStruct((bsz, s, d), BF16),
                   jax.ShapeDtypeStruct((N_EXPERTS, bsz * s), F32)],
        compiler_params=pltpu.CompilerParams(
            dimension_semantics=("parallel", "parallel"), vmem_limit_bytes=VMEM_LIMIT),
        name="postmix",
    )(og, osb, x, mod3, g_br, w_out, g_post, g_pre, w_router_t)


def _first_index_of_max(x, idx, axis, size):
    m = jnp.max(x, axis=axis, keepdims=True)
    return jnp.min(jnp.where(x == m, idx, size), axis=axis, keepdims=True)


def _route_kernel(logit_ref, bias_ref, em_ref, tm_ref, cnt_ref):
    scores = jax.nn.sigmoid(logit_ref[...])
    biased = scores + bias_ref[...]
    neg = jnp.float32(-jnp.inf)

    grouped = biased.reshape(N_GROUPS, GROUP_SIZE, TB)
    within = lax.broadcasted_iota(I32, grouped.shape, 1)
    top1 = jnp.max(grouped, axis=1, keepdims=True)
    first = _first_index_of_max(grouped, within, 1, GROUP_SIZE)
    top2 = jnp.max(jnp.where(within == first, neg, grouped), axis=1, keepdims=True)
    group_score = (top1 + top2).reshape(N_GROUPS, TB)

    gidx = lax.broadcasted_iota(I32, group_score.shape, 0)
    group_on = jnp.zeros(group_score.shape, jnp.bool_)
    for _ in range(TOPK_GROUPS):
        pick = gidx == _first_index_of_max(group_score, gidx, 0, N_GROUPS)
        group_on = jnp.logical_or(group_on, pick)
        group_score = jnp.where(pick, neg, group_score)

    masked = jnp.where(group_on.reshape(N_GROUPS, 1, TB), grouped, neg).reshape(N_EXPERTS, TB)
    eidx = lax.broadcasted_iota(I32, masked.shape, 0)
    chosen = jnp.zeros(masked.shape, jnp.bool_)
    for _ in range(TOP_K):
        pick = eidx == _first_index_of_max(masked, eidx, 0, N_EXPERTS)
        chosen = jnp.logical_or(chosen, pick)
        masked = jnp.where(pick, neg, masked)

    w = jnp.where(chosen, scores, 0.0)
    gates = w / jnp.sum(w, axis=0, keepdims=True) * ROUTED_SCALE

    chosen_f = chosen.astype(F32)
    tr = lax.broadcasted_iota(I32, (TB, TB), 0)
    tc = lax.broadcasted_iota(I32, (TB, TB), 1)
    rank = jnp.dot(chosen_f.astype(BF16), (tr < tc).astype(BF16), preferred_element_type=F32)
    both = jnp.concatenate([jnp.where(chosen, rank, -1.0), gates], axis=0)
    em_ref[...] = both
    tm_ref[...] = both.T
    cnt_ref[...] = jnp.broadcast_to(jnp.sum(chosen_f, axis=1, keepdims=True), (N_EXPERTS, 128))


def _route(logits_t, bias):
    e, t = logits_t.shape
    nblk = t // TB
    return pl.pallas_call(
        _route_kernel,
        grid=(nblk,),
        in_specs=[pl.BlockSpec((e, TB), lambda i: (0, i)),
                  pl.BlockSpec((e, 1), lambda i: (0, 0))],
        out_specs=[pl.BlockSpec((2 * e, TB), lambda i: (0, i)),
                   pl.BlockSpec((TB, 2 * e), lambda i: (i, 0)),
                   pl.BlockSpec((None, e, 128), lambda i: (i, 0, 0))],
        out_shape=[jax.ShapeDtypeStruct((2 * e, t), F32),
                   jax.ShapeDtypeStruct((t, 2 * e), F32),
                   jax.ShapeDtypeStruct((nblk, e, 128), F32)],
        compiler_params=pltpu.CompilerParams(dimension_semantics=("parallel",)),
        name="route",
    )(logits_t, bias.reshape(e, 1))


def _dispatch_plan(cnt, n_tiles):
    nblk, e = cnt.shape
    pc = (cnt + ROW_CHUNK - 1) // ROW_CHUNK * ROW_CHUNK
    start = jnp.cumsum(pc, axis=1) - pc
    nchunk = jnp.sum(pc, axis=1) // ROW_CHUNK
    off = jnp.cumsum(pc, axis=0) - pc
    ecount = jnp.sum(pc, axis=0)
    epad = (ecount + TM_FFN - 1) // TM_FFN * TM_FFN
    gend = jnp.cumsum(epad)
    gbase = gend - epad
    nact = gend[-1] // TM_FFN
    tiles = jnp.minimum(jnp.arange(n_tiles, dtype=I32), nact - 1)
    tile_expert = jnp.minimum(
        jnp.sum((gend[None, :] <= tiles[:, None] * TM_FFN).astype(I32), axis=1), e - 1)
    cidx = jnp.arange(BLOCK_CHUNKS, dtype=I32)
    start16 = start // ROW_CHUNK
    shift = (gbase[None, :] + off) // ROW_CHUNK - start16
    dshift = shift - jnp.pad(shift, ((0, 0), (1, 0)))[:, :-1]
    in_or_after = (start16[:, None, :] <= cidx[None, :, None]).astype(I32)
    where = cidx[None, :] + jnp.sum(in_or_after * dshift[:, None, :], axis=2)
    used = cidx[None, :] < nchunk[:, None]
    spare = (n_tiles * TM_FFN // ROW_CHUNK
             + jnp.arange(nblk, dtype=I32)[:, None] * GROUP_CHUNKS + cidx[None, :] % GROUP_CHUNKS)
    gdst = jnp.where(used, where, spare)
    gsrc = jnp.where(used, where, 0)
    ngrp = (nchunk + GROUP_CHUNKS - 1) // GROUP_CHUNKS
    zbase = (gbase + ecount) // ROW_CHUNK
    zn = (epad - ecount) // ROW_CHUNK

    lo = jnp.tile(start.astype(F32), (1, 2))
    hi = jnp.tile((start + pc).astype(F32), (1, 2))
    bounds_row = jnp.stack([jnp.broadcast_to(lo[:, None, :], (nblk, 8, 2 * e)),
                            jnp.broadcast_to(hi[:, None, :], (nblk, 8, 2 * e))], axis=1)
    bounds_col = jnp.stack([jnp.broadcast_to(lo[:, :, None], (nblk, 2 * e, 128)),
                            jnp.broadcast_to(hi[:, :, None], (nblk, 2 * e, 128))], axis=1)
    as_i32 = lambda a: a.astype(I32)
    return dict(ngrp=as_i32(ngrp), gdst=as_i32(gdst.reshape(-1)), gsrc=as_i32(gsrc.reshape(-1)),
                zn=as_i32(zn), zbase=as_i32(zbase), tile_expert=as_i32(tile_expert),
                nact=as_i32(nact.reshape(1)), bounds_row=bounds_row, bounds_col=bounds_col)


def _dispatch_kernel(ngrp_ref, gdst_ref, zn_ref, zbase_ref, h_ref, em_ref, brow_ref,
                     xs_ref, buf_ref, zero_ref, sem, zsem):
    j = pl.program_id(0)
    last = pl.num_programs(0) - 1
    slot = j % 2

    def chunk_copy(s, c, g):
        return pltpu.make_async_copy(buf_ref.at[s, c], xs_ref.at[g], sem.at[s])

    def wait_block(jj, s):
        def body(i, carry):
            pltpu.make_async_copy(buf_ref.at[s, pl.ds(0, GROUP_CHUNKS)],
                                  xs_ref.at[pl.ds(0, GROUP_CHUNKS)], sem.at[s]).wait()
            return carry
        lax.fori_loop(0, ngrp_ref[jj], body, 0)

    def zero_copy(g, chunks):
        return pltpu.make_async_copy(zero_ref.at[pl.ds(0, chunks)], xs_ref.at[pl.ds(g, chunks)],
                                     zsem.at[0])

    def zero_fill(start):
        big = ZERO_ROWS // ROW_CHUNK

        def per_expert(e, carry):
            n = zn_ref[e]
            base = zbase_ref[e]

            def big_copy(i, c2):
                cp = zero_copy(base + i * big, big)
                cp.start() if start else cp.wait()
                return c2
            lax.fori_loop(0, n // big, big_copy, 0)

            def small_copy(i, c2):
                cp = zero_copy(base + n // big * big + i, 1)
                cp.start() if start else cp.wait()
                return c2
            lax.fori_loop(0, n % big, small_copy, 0)
            return carry
        lax.fori_loop(0, N_EXPERTS, per_expert, 0)

    @pl.when(j == 0)
    def _():
        zero_ref[...] = jnp.zeros_like(zero_ref)
        zero_fill(True)

    @pl.when(j >= 2)
    def _():
        wait_block(j - 2, slot)

    lo = brow_ref[0, 0:1, :]
    hi = brow_ref[1, 0:1, :]
    first_copy = lax.broadcasted_iota(I32, (KC, 2 * N_EXPERTS), 1) < N_EXPERTS
    row_iota = lax.broadcasted_iota(I32, (KC, 2 * N_EXPERTS), 0)
    ranks = em_ref[...].astype(BF16)
    h = h_ref[...]

    def trip(ti, carry):
        onehots = []
        for u in range(GROUP_ROWS // KC):
            rows_e = (row_iota + ti * GROUP_ROWS + u * KC).astype(F32)
            in_run = jnp.logical_and(jnp.logical_and(rows_e >= lo, rows_e < hi), first_copy)
            sel = jnp.dot(jnp.where(in_run, 1.0, 0.0).astype(BF16), ranks,
                          preferred_element_type=F32)
            local = jnp.sum(jnp.where(in_run, rows_e - lo + 3.0, 0.0), axis=1,
                            keepdims=True) - 3.0
            onehots.append(jnp.where(sel == local, 1.0, 0.0).astype(BF16))
        xs = jnp.dot(jnp.concatenate(onehots, axis=0), h, preferred_element_type=F32)
        first = pl.multiple_of(ti * GROUP_CHUNKS, GROUP_CHUNKS)
        buf_ref[slot, pl.ds(first, GROUP_CHUNKS)] = xs.astype(BF16).reshape(
            GROUP_CHUNKS, ROW_CHUNK, xs.shape[1])
        for k in range(GROUP_CHUNKS):
            chunk_copy(slot, first + k, gdst_ref[j * BLOCK_CHUNKS + first + k]).start()
        return carry
    lax.fori_loop(0, ngrp_ref[j], trip, 0)

    @pl.when(j == last)
    def _():
        wait_block(j, slot)

        @pl.when(j >= 1)
        def _():
            wait_block(j - 1, 1 - slot)

        zero_fill(False)


def _bounds_specs():
    return [pl.BlockSpec((None, 2, 8, 2 * N_EXPERTS), lambda j, *_: (j, 0, 0, 0)),
            pl.BlockSpec((None, 2, 2 * N_EXPERTS, 128), lambda j, *_: (j, 0, 0, 0))]


def _dispatch(plan, h2_flat, em, n_rows):
    t, d = h2_flat.shape
    grid_spec = pltpu.PrefetchScalarGridSpec(
        num_scalar_prefetch=4,
        grid=(t // TB,),
        in_specs=[pl.BlockSpec((TB, d), lambda j, *_: (j, 0)),
                  pl.BlockSpec((2 * N_EXPERTS, TB), lambda j, *_: (0, j))] + _bounds_specs()[:1],
        out_specs=pl.BlockSpec(memory_space=pl.ANY),
        scratch_shapes=[pltpu.VMEM((2, BLOCK_CHUNKS, ROW_CHUNK, d), BF16),
                        pltpu.VMEM((ZERO_ROWS // ROW_CHUNK, ROW_CHUNK, d), BF16),
                        pltpu.SemaphoreType.DMA((2,)),
                        pltpu.SemaphoreType.DMA((1,))],
    )
    xs = pl.pallas_call(
        _dispatch_kernel,
        grid_spec=grid_spec,
        out_shape=jax.ShapeDtypeStruct((n_rows // ROW_CHUNK, ROW_CHUNK, d), BF16),
        compiler_params=pltpu.CompilerParams(
            dimension_semantics=("arbitrary",), vmem_limit_bytes=VMEM_LIMIT),
        name="moe_dispatch",
    )(plan["ngrp"], plan["gdst"], plan["zn"], plan["zbase"], h2_flat, em, plan["bounds_row"])
    return xs.reshape(n_rows, d)


def _ffn_kernel(te_ref, nact_ref, x_ref, wg_ref, wu_ref, wd_ref, y_ref, wgu_s, wd_s):
    i = pl.program_id(0)

    @pl.when(i < nact_ref[0])
    def _():
        @pl.when(jnp.logical_or(i == 0, te_ref[i] != te_ref[jnp.maximum(i - 1, 0)]))
        def _():
            wgu_s[:, :D_EXPERT] = wg_ref[...].astype(BF16)
            wgu_s[:, D_EXPERT:] = wu_ref[...].astype(BF16)
            wd_s[...] = wd_ref[...].astype(BF16)

        gu = jnp.dot(x_ref[...], wgu_s[...], preferred_element_type=F32)
        act = _silu(gu[:, :D_EXPERT]) * gu[:, D_EXPERT:]
        y_ref[...] = jnp.dot(act.astype(BF16), wd_s[...], preferred_element_type=F32).astype(BF16)


def _expert_ffn(tile_expert, nact, xs, n_tiles, w_gate, w_up, w_down):
    n_rows, d = n_tiles * TM_FFN, xs.shape[1]
    f = D_EXPERT
    row_tile = lambda i, te, na: (jnp.minimum(i, na[0] - 1), 0)
    grid_spec = pltpu.PrefetchScalarGridSpec(
        num_scalar_prefetch=2,
        grid=(n_tiles,),
        in_specs=[pl.BlockSpec((TM_FFN, d), row_tile),
                  pl.BlockSpec((None, d, f), lambda i, te, na: (te[i], 0, 0)),
                  pl.BlockSpec((None, d, f), lambda i, te, na: (te[i], 0, 0)),
                  pl.BlockSpec((None, f, d), lambda i, te, na: (te[i], 0, 0))],
        out_specs=pl.BlockSpec((TM_FFN, d), row_tile),
        scratch_shapes=[pltpu.VMEM((d, 2 * f), BF16), pltpu.VMEM((f, d), BF16)],
    )
    return pl.pallas_call(
        _ffn_kernel,
        grid_spec=grid_spec,
        out_shape=jax.ShapeDtypeStruct((n_rows, d), BF16),
        compiler_params=pltpu.CompilerParams(
            dimension_semantics=("arbitrary",), vmem_limit_bytes=VMEM_LIMIT),
        name="moe_experts",
    )(tile_expert, nact, xs, w_gate, w_up, w_down)


def _combine_kernel(ngrp_ref, gsrc_ref, ys_ref, tm_ref, brow_ref, bcol_ref, h_ref, wsg_ref,
                    wsu_ref, wsd_ref, x1_ref, mod_ref, gpost_ref, o_ref, buf_ref, p_ref, sem):
    j = pl.program_id(0)
    last = pl.num_programs(0) - 1
    slot = j % 2

    def chunk_copy(s, c, g):
        return pltpu.make_async_copy(ys_ref.at[g], buf_ref.at[s, c], sem.at[s])

    def fetch_block(jj, s):
        def issue_group(gi, carry):
            for k in range(GROUP_CHUNKS):
                c = gi * GROUP_CHUNKS + k
                chunk_copy(s, c, gsrc_ref[jj * BLOCK_CHUNKS + c]).start()
            return carry
        lax.fori_loop(0, ngrp_ref[jj], issue_group, 0)

    @pl.when(j == 0)
    def _():
        buf_ref[...] = jnp.zeros_like(buf_ref)
        fetch_block(0, 0)

    @pl.when(j < last)
    def _():
        fetch_block(j + 1, 1 - slot)

    h = h_ref[...]
    act = _silu(jnp.dot(h, wsg_ref[...], preferred_element_type=F32)) * jnp.dot(
        h, wsu_ref[...], preferred_element_type=F32)
    y = jnp.dot(act.astype(BF16), wsd_ref[...], preferred_element_type=F32)

    tm = tm_ref[...]
    lane = lax.broadcasted_iota(I32, tm.shape, 1)
    rank_t = jnp.where(lane < N_EXPERTS, tm, 0.0).astype(BF16)
    gate_t = jnp.where(lane < N_EXPERTS, 0.0, tm).astype(BF16)
    lane8 = lax.broadcasted_iota(I32, (8, 2 * N_EXPERTS), 1)
    lo_row = jnp.where(lane8 < N_EXPERTS, brow_ref[0], 0.0).astype(BF16)
    lo_col = jnp.concatenate([bcol_ref[0]] * (KC // 128), axis=1)
    hi_col = jnp.concatenate([bcol_ref[1]] * (KC // 128), axis=1)
    for cb in range(BLOCK_ROWS // KC):
        rows = (lax.broadcasted_iota(I32, (2 * N_EXPERTS, KC), 1) + cb * KC).astype(F32)
        in_run = jnp.where(jnp.logical_and(rows >= lo_col, rows < hi_col), 1.0, 0.0).astype(BF16)
        sel_rank = jnp.dot(rank_t, in_run, preferred_element_type=F32)
        sel_gate = jnp.dot(gate_t, in_run, preferred_element_type=F32)
        sel_lo = jnp.dot(lo_row, in_run, preferred_element_type=F32)
        local = rows[0:1, :] - sel_lo[0:1, :]
        p_ref[:, cb * KC:(cb + 1) * KC] = jnp.where(sel_rank == local, sel_gate,
                                                    0.0).astype(BF16)

    def wait_group(gi, carry):
        pltpu.make_async_copy(ys_ref.at[pl.ds(0, GROUP_CHUNKS)],
                              buf_ref.at[slot, pl.ds(0, GROUP_CHUNKS)], sem.at[slot]).wait()
        return carry
    lax.fori_loop(0, ngrp_ref[j], wait_group, 0)

    rows = buf_ref[slot].reshape(BLOCK_ROWS, buf_ref.shape[3])
    y = y + jnp.dot(p_ref[...], rows, preferred_element_type=F32)

    ga2 = mod_ref[5:6, :]
    o_ref[...] = x1_ref[...] + ga2 * _rms(y, gpost_ref[...])


def _combine(plan, ys, tm, h2_flat, ws_gate, ws_up, ws_down, x1_flat, mod3, g_post, s):
    t, d = h2_flat.shape
    f = ws_gate.shape[1]
    blocks_per_seq = s // TB
    full = lambda shape: pl.BlockSpec(shape, lambda j, *_: (0,) * len(shape))
    grid_spec = pltpu.PrefetchScalarGridSpec(
        num_scalar_prefetch=2,
        grid=(t // TB,),
        in_specs=[pl.BlockSpec(memory_space=pl.ANY),
                  pl.BlockSpec((TB, 2 * N_EXPERTS), lambda j, *_: (j, 0))] + _bounds_specs() + [
                  pl.BlockSpec((TB, d), lambda j, *_: (j, 0)),
                  full((d, f)), full((d, f)), full((f, d)),
                  pl.BlockSpec((TB, d), lambda j, *_: (j, 0)),
                  pl.BlockSpec((None, 6, d), lambda j, *_: (j // blocks_per_seq, 0, 0)),
                  full((1, d))],
        out_specs=pl.BlockSpec((TB, d), lambda j, *_: (j, 0)),
        scratch_shapes=[pltpu.VMEM((2, BLOCK_CHUNKS, ROW_CHUNK, d), BF16),
                        pltpu.VMEM((TB, BLOCK_ROWS), BF16),
                        pltpu.SemaphoreType.DMA((2,))],
    )
    return pl.pallas_call(
        _combine_kernel,
        grid_spec=grid_spec,
        out_shape=jax.ShapeDtypeStruct((t, d), F32),
        compiler_params=pltpu.CompilerParams(
            dimension_semantics=("arbitrary",), vmem_limit_bytes=VMEM_LIMIT),
        name="moe_combine",
    )(plan["ngrp"], plan["gsrc"], ys.reshape(-1, ROW_CHUNK, d), tm, plan["bounds_row"],
      plan["bounds_col"], h2_flat, ws_gate, ws_up, ws_down, x1_flat, mod3, g_post)


def kernel(x, c, w_ada, b_ada, g_pre_mix, w_in, ln_sgu_g, ln_sgu_b, w_spatial, b_spatial,
           g_branch, w_out, g_post_mix, g_pre_ffn, w_router, router_bias, w_gate, w_up, w_down,
           ws_gate, ws_up, ws_down, g_post_ffn):
    bsz, s, d = x.shape
    t = bsz * s
    nblk = t // TB
    max_rows = t * TOP_K + nblk * N_EXPERTS * (ROW_CHUNK - 1) + N_EXPERTS * (TM_FFN - 1)
    n_tiles = -(-max_rows // TM_FFN)
    row = lambda a: a.reshape(1, -1)
    for l in range(w_ada.shape[0]):
        mod3 = _ada(c, w_ada[l], b_ada[l]).reshape(bsz, 6, d)
        og, q, k, v = _premix(x, mod3, row(g_pre_mix[l]), w_in[l].astype(BF16),
                              row(ln_sgu_g[l]), row(ln_sgu_b[l]), w_spatial[l],
                              b_spatial[l].T, row(g_branch[l, :D_GMLP]))
        osb = _attention(q, k, v)
        x1, h2, logits_t = _postmix(og, osb, x, mod3, row(g_branch[l, D_GMLP:]),
                                    w_out[l].astype(BF16), row(g_post_mix[l]),
                                    row(g_pre_ffn[l]), w_router[l].T)
        em, tm, cnt = _route(logits_t, router_bias[l])
        plan = _dispatch_plan(cnt[:, :, 0].astype(I32), n_tiles)
        h2_flat = h2.reshape(t, d)
        xs = _dispatch(plan, h2_flat, em, n_tiles * TM_FFN + nblk * GROUP_ROWS)
        ys = _expert_ffn(plan["tile_expert"], plan["nact"], xs, n_tiles,
                         w_gate[l], w_up[l], w_down[l])
        out = _combine(plan, ys, tm, h2_flat, ws_gate[l].astype(BF16), ws_up[l].astype(BF16),
                       ws_down[l].astype(BF16), x1.reshape(t, d), mod3, row(g_post_ffn[l]), s)
        x = out.reshape(bsz, s, d)
    return x
```

```python
import jax
import jax.numpy as jnp
from jax import lax
from jax.experimental import pallas as pl
from jax.experimental.pallas import tpu as pltpu

F32 = jnp.float32
BF16 = jnp.bfloat16
I32 = jnp.int32

D_MODEL = 1024
D_GMLP = 512
GMLP_GROUPS = 4
GMLP_BLOCK = 128
CHUNK = 64
D_SB = 512
SB_HEAD_DIM = 64
N_PAIRS = D_SB // 128
N_EXPERTS = 64
N_GROUPS = 8
GROUP_SIZE = N_EXPERTS // N_GROUPS
TOPK_GROUPS = 4
TOP_K = 8
D_EXPERT = 256
ROUTED_SCALE = 2.5
EPS = 1e-6
D_IN = 2 * D_GMLP + 3 * D_SB

TM_MIX = 512
TQ = 128
VMEM_LIMIT = 56 * 1024 * 1024
ATTN_DEAD_LOG = -110.0

TB = 256
ROW_CHUNK = 16
KC = 256
GROUP_CHUNKS = 32
GROUP_ROWS = GROUP_CHUNKS * ROW_CHUNK
BLOCK_ROWS = -(-(TB * TOP_K + N_EXPERTS * (ROW_CHUNK - 1)) // GROUP_ROWS) * GROUP_ROWS
BLOCK_CHUNKS = BLOCK_ROWS // ROW_CHUNK
TM_FFN = 512
FFN_AHEAD = 2
ZERO_ROWS = 128


def _rms(x, g):
    return x * lax.rsqrt(jnp.mean(x * x, axis=-1, keepdims=True) + EPS) * g


def _silu(x):
    return x * jax.nn.sigmoid(x)


def _ada_kernel(c_ref, w_ref, b_ref, o_ref):
    o_ref[...] = jnp.dot(_silu(c_ref[...]), w_ref[...], preferred_element_type=F32,
                         precision=lax.Precision.HIGHEST) + b_ref[...]


def _ada(c, w, b):
    bsz, d = c.shape
    n = w.shape[1]
    tn = 512
    return pl.pallas_call(
        _ada_kernel,
        grid=(n // tn,),
        in_specs=[pl.BlockSpec((bsz, d), lambda j: (0, 0)),
                  pl.BlockSpec((d, tn), lambda j: (0, j)),
                  pl.BlockSpec((1, tn), lambda j: (0, j))],
        out_specs=pl.BlockSpec((bsz, tn), lambda j: (0, j)),
        out_shape=jax.ShapeDtypeStruct((bsz, n), F32),
        name="ada_mod",
    )(c, w, b.reshape(1, n))


def _premix_kernel(x_ref, mod_ref, gpre_ref, win_ref, lng_ref, lnb_ref, wsp_ref, bsp_ref,
                   gbr_ref, og_ref, q_ref, k_ref, v_ref):
    x = x_ref[...]
    sh1 = mod_ref[0:1, :]
    sc1 = mod_ref[1:2, :]
    h = _rms(x, gpre_ref[...]) * (1.0 + sc1) + sh1
    proj = jnp.dot(h.astype(BF16), win_ref[...], preferred_element_type=F32)

    u = jax.nn.gelu(proj[:, :D_GMLP])
    v = jax.nn.gelu(proj[:, D_GMLP:2 * D_GMLP])
    mu = jnp.mean(v, axis=-1, keepdims=True)
    var = jnp.mean(jnp.square(v - mu), axis=-1, keepdims=True)
    v = ((v - mu) * lax.rsqrt(var + EPS) * lng_ref[...] + lnb_ref[...]).astype(BF16)

    i = lax.broadcasted_iota(I32, (GMLP_BLOCK, GMLP_BLOCK), 0)
    j = lax.broadcasted_iota(I32, (GMLP_BLOCK, GMLP_BLOCK), 1)
    causal = (j // CHUNK) <= (i // CHUNK)
    gd = D_GMLP // GMLP_GROUPS
    blocks = []
    for nb in range(x.shape[0] // GMLP_BLOCK):
        rows = slice(nb * GMLP_BLOCK, (nb + 1) * GMLP_BLOCK)
        cols = []
        for g in range(GMLP_GROUPS):
            w = jnp.where(causal, wsp_ref[g], 0.0).astype(BF16)
            mixed = jnp.dot(w, v[rows, g * gd:(g + 1) * gd], preferred_element_type=F32)
            cols.append(mixed + bsp_ref[:, g:g + 1])
        blocks.append(u[rows, :] * jnp.concatenate(cols, axis=1))
    og = jnp.concatenate(blocks, axis=0)
    og_ref[...] = _rms(og, gbr_ref[...]).astype(BF16)

    base = 2 * D_GMLP
    scale = SB_HEAD_DIM ** -0.5
    for p in range(N_PAIRS):
        q_ref[p] = (proj[:, base + 128 * p:base + 128 * (p + 1)] * scale).astype(BF16)
        k_ref[p] = proj[:, base + D_SB + 128 * p:base + D_SB + 128 * (p + 1)].astype(BF16)
        v_ref[p] = proj[:, base + 2 * D_SB + 128 * p:base + 2 * D_SB + 128 * (p + 1)].astype(BF16)


def _premix(x, mod3, g_pre, w_in, ln_g, ln_b, w_sp, b_sp_t, g_br):
    bsz, s, d = x.shape
    tm = TM_MIX
    full = lambda shape: pl.BlockSpec(shape, lambda b, i: (0,) * len(shape))
    qkv_spec = pl.BlockSpec((None, N_PAIRS, tm, 128), lambda b, i: (b, 0, i, 0))
    qkv_shape = jax.ShapeDtypeStruct((bsz, N_PAIRS, s, 128), BF16)
    return pl.pallas_call(
        _premix_kernel,
        grid=(bsz, s // tm),
        in_specs=[pl.BlockSpec((None, tm, d), lambda b, i: (b, i, 0)),
                  pl.BlockSpec((None, 6, d), lambda b, i: (b, 0, 0)),
                  full((1, d)), full((d, D_IN)), full((1, D_GMLP)), full((1, D_GMLP)),
                  full((GMLP_GROUPS, GMLP_BLOCK, GMLP_BLOCK)), full((GMLP_BLOCK, GMLP_GROUPS)),
                  full((1, D_GMLP))],
        out_specs=[pl.BlockSpec((None, tm, D_GMLP), lambda b, i: (b, i, 0)),
                   qkv_spec, qkv_spec, qkv_spec],
        out_shape=[jax.ShapeDtypeStruct((bsz, s, D_GMLP), BF16), qkv_shape, qkv_shape, qkv_shape],
        compiler_params=pltpu.CompilerParams(
            dimension_semantics=("parallel", "parallel"), vmem_limit_bytes=VMEM_LIMIT),
        name="premix",
    )(x, mod3, g_pre, w_in, ln_g, ln_b, w_sp, b_sp_t, g_br)


def _attn_kernel(q_ref, k_ref, v_ref, o_ref, *scratch):
    qs_refs = scratch[:N_PAIRS]
    acc_refs = scratch[N_PAIRS:2 * N_PAIRS]
    carry_refs = scratch[2 * N_PAIRS:]
    qi = pl.program_id(1)
    first_head = lax.broadcasted_iota(I32, (TQ, 128), 1) < SB_HEAD_DIM
    for p in range(N_PAIRS):
        q2 = q_ref[p]
        zero = jnp.zeros_like(q2)
        qs_refs[p][:TQ, :] = jnp.where(first_head, q2, zero)
        qs_refs[p][TQ:, :] = jnp.where(first_head, zero, q2)
        acc_refs[p][...] = jnp.zeros_like(acc_refs[p])
        carry_refs[p][...] = jnp.zeros_like(carry_refs[p])

    r = jnp.bitwise_and(lax.broadcasted_iota(I32, (2 * TQ, TQ), 0), TQ - 1)
    c = lax.broadcasted_iota(I32, (2 * TQ, TQ), 1)
    strict_causal = c < r
    kr = lax.broadcasted_iota(I32, (TQ, TQ), 0)
    kc = lax.broadcasted_iota(I32, (TQ, TQ), 1)
    suffix = jnp.concatenate([(kr > kc).astype(BF16), jnp.ones((TQ, TQ), BF16)], axis=1)
    suffix2 = jnp.concatenate([suffix, suffix], axis=0)

    def key_block(j, diagonal):
        start = pl.multiple_of(j * TQ, TQ)
        pairs = range(N_PAIRS)
        zs = [lax.dot_general(qs_refs[p][...], k_ref[p, pl.ds(start, TQ), :],
                              (((1,), (1,)), ((), ())), preferred_element_type=F32)
              for p in pairs]
        log_betas, splits = [], []
        for p in pairs:
            z = zs[p]
            log_beta = jnp.minimum(z, 0.0) - jnp.log(1.0 + jnp.exp(-jnp.abs(z)))
            log_1mb = log_beta - z
            if diagonal:
                log_1mb = jnp.where(strict_causal, log_1mb, 0.0)
            hi = log_1mb.astype(BF16)
            lo = (log_1mb - hi.astype(F32)).astype(BF16)
            log_betas.append(log_beta)
            splits.append(jnp.concatenate([hi, lo], axis=1))
        sums = [jnp.dot(splits[p], suffix2, preferred_element_type=F32) for p in pairs]
        weights = []
        live = None
        for p in pairs:
            s = sums[p]
            carry = carry_refs[p][...]
            a = jnp.exp(log_betas[p] + carry + s[:, :TQ])
            if diagonal:
                a = jnp.where(strict_causal, a, 0.0)
            weights.append(a.astype(BF16))
            carry = carry + s[:, TQ:]
            carry_refs[p][...] = carry
            live = carry if live is None else jnp.maximum(live, carry)
        for p in pairs:
            acc_refs[p][...] += jnp.dot(weights[p], v_ref[p, pl.ds(start, TQ), :],
                                        preferred_element_type=F32)
        return jnp.max(live)

    live = key_block(qi, True)

    def cond(state):
        j, live = state
        return jnp.logical_and(j >= 0, live > ATTN_DEAD_LOG)

    def body(state):
        j, _ = state
        return j - 1, key_block(j, False)

    lax.while_loop(cond, body, (qi - 1, live))
    for p in range(N_PAIRS):
        o_ref[:, 128 * p:128 * (p + 1)] = jnp.where(first_head, acc_refs[p][:TQ, :],
                                                    acc_refs[p][TQ:, :])


def _attention(q, k, v):
    bsz, npair, s, _ = q.shape
    kv_spec = pl.BlockSpec((None, npair, s, 128), lambda b, i: (b, 0, 0, 0))
    return pl.pallas_call(
        _attn_kernel,
        grid=(bsz, s // TQ),
        in_specs=[pl.BlockSpec((None, npair, TQ, 128), lambda b, i: (b, 0, i, 0)),
                  kv_spec, kv_spec],
        out_specs=pl.BlockSpec((None, TQ, npair * 128), lambda b, i: (b, i, 0)),
        out_shape=jax.ShapeDtypeStruct((bsz, s, npair * 128), F32),
        scratch_shapes=([pltpu.VMEM((2 * TQ, 128), BF16)] * npair
                        + [pltpu.VMEM((2 * TQ, 128), F32)] * npair
                        + [pltpu.VMEM((2 * TQ, TQ), F32)] * npair),
        compiler_params=pltpu.CompilerParams(
            dimension_semantics=("parallel", "parallel"), vmem_limit_bytes=VMEM_LIMIT),
        name="stickbreak_attn",
    )(q, k, v)


def _postmix_kernel(og_ref, osb_ref, x_ref, mod_ref, gbr_ref, wout_ref, gpost_ref, gpre_ref,
                    wrt_ref, x1_ref, h2_ref, logit_ref):
    ga1 = mod_ref[2:3, :]
    sh2 = mod_ref[3:4, :]
    sc2 = mod_ref[4:5, :]
    osb = _rms(osb_ref[...], gbr_ref[...]).astype(BF16)
    m = (jnp.dot(og_ref[...], wout_ref[:D_GMLP, :], preferred_element_type=F32)
         + jnp.dot(osb, wout_ref[D_GMLP:, :], preferred_element_type=F32))
    x1 = x_ref[...] + ga1 * _rms(m, gpost_ref[...])
    x1_ref[...] = x1
    h2 = _rms(x1, gpre_ref[...]) * (1.0 + sc2) + sh2
    h_hi = h2.astype(BF16)
    h2_ref[...] = h_hi
    h_lo = (h2 - h_hi.astype(F32)).astype(BF16)
    w = wrt_ref[...]
    w_hi = w.astype(BF16)
    w_lo = (w - w_hi.astype(F32)).astype(BF16)
    nt = (((1,), (1,)), ((), ()))
    by_hi = lax.dot_general(jnp.concatenate([w_hi, w_lo], axis=0), h_hi, nt,
                            preferred_element_type=F32)
    by_lo = lax.dot_general(w_hi, h_lo, nt, preferred_element_type=F32)
    logit_ref[...] = by_hi[:N_EXPERTS] + by_hi[N_EXPERTS:] + by_lo


def _postmix(og, osb, x, mod3, g_br, w_out, g_post, g_pre, w_router_t):
    bsz, s, d = x.shape
    tm = TM_MIX
    nt = s // tm
    full = lambda shape: pl.BlockSpec(shape, lambda b, i: (0,) * len(shape))
    return pl.pallas_call(
        _postmix_kernel,
        grid=(bsz, nt),
        in_specs=[pl.BlockSpec((None, tm, D_GMLP), lambda b, i: (b, i, 0)),
                  pl.BlockSpec((None, tm, D_SB), lambda b, i: (b, i, 0)),
                  pl.BlockSpec((None, tm, d), lambda b, i: (b, i, 0)),
                  pl.BlockSpec((None, 6, d), lambda b, i: (b, 0, 0)),
                  full((1, D_SB)), full((d, d)), full((1, d)), full((1, d)),
                  full((N_EXPERTS, d))],
        out_specs=[pl.BlockSpec((None, tm, d), lambda b, i: (b, i, 0)),
                   pl.BlockSpec((None, tm, d), lambda b, i: (b, i, 0)),
                   pl.BlockSpec((N_EXPERTS, tm), lambda b, i: (0, b * nt + i))],
        out_shape=[jax.ShapeDtypeStruct((bsz, s, d), F32),
                   jax.ShapeDtypeStruct((bsz, s, d), BF16),
                   jax.ShapeDtypeStruct((N_EXPERTS, bsz * s), F32)],
        compiler_params=pltpu.CompilerParams(
            dimension_semantics=("parallel", "parallel"), vmem_limit_bytes=VMEM_LIMIT),
        name="postmix",
    )(og, osb, x, mod3, g_br, w_out, g_post, g_pre, w_router_t)


def _first_index_of_max(x, idx, axis, size):
    m = jnp.max(x, axis=axis, keepdims=True)
    return jnp.min(jnp.where(x == m, idx, size), axis=axis, keepdims=True)


def _route_kernel(logit_ref, bias_ref, em_ref, tm_ref, cnt_ref):
    scores = jax.nn.sigmoid(logit_ref[...])
    biased = scores + bias_ref[...]
    neg = jnp.float32(-jnp.inf)

    grouped = biased.reshape(N_GROUPS, GROUP_SIZE, TB)
    within = lax.broadcasted_iota(I32, grouped.shape, 1)
    top1 = jnp.max(grouped, axis=1, keepdims=True)
    first = _first_index_of_max(grouped, within, 1, GROUP_SIZE)
    top2 = jnp.max(jnp.where(within == first, neg, grouped), axis=1, keepdims=True)
    group_score = (top1 + top2).reshape(N_GROUPS, TB)

    gidx = lax.broadcasted_iota(I32, group_score.shape, 0)
    group_on = jnp.zeros(group_score.shape, jnp.bool_)
    for _ in range(TOPK_GROUPS):
        pick = gidx == _first_index_of_max(group_score, gidx, 0, N_GROUPS)
        group_on = jnp.logical_or(group_on, pick)
        group_score = jnp.where(pick, neg, group_score)

    masked = jnp.where(group_on.reshape(N_GROUPS, 1, TB), grouped, neg).reshape(N_EXPERTS, TB)
    eidx = lax.broadcasted_iota(I32, masked.shape, 0)
    chosen = jnp.zeros(masked.shape, jnp.bool_)
    for _ in range(TOP_K):
        pick = eidx == _first_index_of_max(masked, eidx, 0, N_EXPERTS)
        chosen = jnp.logical_or(chosen, pick)
        masked = jnp.where(pick, neg, masked)

    w = jnp.where(chosen, scores, 0.0)
    gates = w / jnp.sum(w, axis=0, keepdims=True) * ROUTED_SCALE

    chosen_f = chosen.astype(F32)
    tr = lax.broadcasted_iota(I32, (TB, TB), 0)
    tc = lax.broadcasted_iota(I32, (TB, TB), 1)
    rank = jnp.dot(chosen_f.astype(BF16), (tr < tc).astype(BF16), preferred_element_type=F32)
    both = jnp.concatenate([jnp.where(chosen, rank, -1.0), gates], axis=0)
    em_ref[...] = both
    tm_ref[...] = both.T
    cnt_ref[...] = jnp.broadcast_to(jnp.sum(chosen_f, axis=1, keepdims=True), (N_EXPERTS, 128))


def _route(logits_t, bias):
    e, t = logits_t.shape
    nblk = t // TB
    return pl.pallas_call(
        _route_kernel,
        grid=(nblk,),
        in_specs=[pl.BlockSpec((e, TB), lambda i: (0, i)),
                  pl.BlockSpec((e, 1), lambda i: (0, 0))],
        out_specs=[pl.BlockSpec((2 * e, TB), lambda i: (0, i)),
                   pl.BlockSpec((TB, 2 * e), lambda i: (i, 0)),
                   pl.BlockSpec((None, e, 128), lambda i: (i, 0, 0))],
        out_shape=[jax.ShapeDtypeStruct((2 * e, t), F32),
                   jax.ShapeDtypeStruct((t, 2 * e), F32),
                   jax.ShapeDtypeStruct((nblk, e, 128), F32)],
        compiler_params=pltpu.CompilerParams(dimension_semantics=("parallel",)),
        name="route",
    )(logits_t, bias.reshape(e, 1))


def _dispatch_plan(cnt, n_tiles):
    nblk, e = cnt.shape
    pc = (cnt + ROW_CHUNK - 1) // ROW_CHUNK * ROW_CHUNK
    start = jnp.cumsum(pc, axis=1) - pc
    nchunk = jnp.sum(pc, axis=1) // ROW_CHUNK
    off = jnp.cumsum(pc, axis=0) - pc
    ecount = jnp.sum(pc, axis=0)
    epad = (ecount + TM_FFN - 1) // TM_FFN * TM_FFN
    gend = jnp.cumsum(epad)
    gbase = gend - epad
    nact = gend[-1] // TM_FFN
    tile_start = jnp.concatenate([gbase, gend[-1:]]) // TM_FFN
    cidx = jnp.arange(BLOCK_CHUNKS, dtype=I32)
    start16 = start // ROW_CHUNK
    shift = (gbase[None, :] + off) // ROW_CHUNK - start16
    dshift = shift - jnp.pad(shift, ((0, 0), (1, 0)))[:, :-1]
    in_or_after = (start16[:, None, :] <= cidx[None, :, None]).astype(I32)
    where = cidx[None, :] + jnp.sum(in_or_after * dshift[:, None, :], axis=2)
    used = cidx[None, :] < nchunk[:, None]
    spare = (n_tiles * TM_FFN // ROW_CHUNK
             + jnp.arange(nblk, dtype=I32)[:, None] * GROUP_CHUNKS + cidx[None, :] % GROUP_CHUNKS)
    gdst = jnp.where(used, where, spare)
    gsrc = jnp.where(used, where, 0)
    ngrp = (nchunk + GROUP_CHUNKS - 1) // GROUP_CHUNKS
    zbase = (gbase + ecount) // ROW_CHUNK
    zn = (epad - ecount) // ROW_CHUNK

    lo = jnp.tile(start.astype(F32), (1, 2))
    hi = jnp.tile((start + pc).astype(F32), (1, 2))
    bounds_row = jnp.stack([jnp.broadcast_to(lo[:, None, :], (nblk, 8, 2 * e)),
                            jnp.broadcast_to(hi[:, None, :], (nblk, 8, 2 * e))], axis=1)
    bounds_col = jnp.stack([jnp.broadcast_to(lo[:, :, None], (nblk, 2 * e, 128)),
                            jnp.broadcast_to(hi[:, :, None], (nblk, 2 * e, 128))], axis=1)
    as_i32 = lambda a: a.astype(I32)
    return dict(ngrp=as_i32(ngrp), gdst=as_i32(gdst.reshape(-1)), gsrc=as_i32(gsrc.reshape(-1)),
                zn=as_i32(zn), zbase=as_i32(zbase), tile_start=as_i32(tile_start),
                nact=as_i32(nact.reshape(1)), bounds_row=bounds_row, bounds_col=bounds_col)


def _dispatch_kernel(ngrp_ref, gdst_ref, zn_ref, zbase_ref, h_ref, em_ref, brow_ref,
                     xs_ref, buf_ref, zero_ref, sem, zsem):
    j = pl.program_id(0)
    last = pl.num_programs(0) - 1
    slot = j % 2

    def chunk_copy(s, c, g):
        return pltpu.make_async_copy(buf_ref.at[s, c], xs_ref.at[g], sem.at[s])

    def wait_block(jj, s):
        def body(i, carry):
            pltpu.make_async_copy(buf_ref.at[s, pl.ds(0, GROUP_CHUNKS)],
                                  xs_ref.at[pl.ds(0, GROUP_CHUNKS)], sem.at[s]).wait()
            return carry
        lax.fori_loop(0, ngrp_ref[jj], body, 0)

    def zero_copy(g, chunks):
        return pltpu.make_async_copy(zero_ref.at[pl.ds(0, chunks)], xs_ref.at[pl.ds(g, chunks)],
                                     zsem.at[0])

    def zero_fill(start):
        big = ZERO_ROWS // ROW_CHUNK

        def per_expert(e, carry):
            n = zn_ref[e]
            base = zbase_ref[e]

            def big_copy(i, c2):
                cp = zero_copy(base + i * big, big)
                cp.start() if start else cp.wait()
                return c2
            lax.fori_loop(0, n // big, big_copy, 0)

            def small_copy(i, c2):
                cp = zero_copy(base + n // big * big + i, 1)
                cp.start() if start else cp.wait()
                return c2
            lax.fori_loop(0, n % big, small_copy, 0)
            return carry
        lax.fori_loop(0, N_EXPERTS, per_expert, 0)

    @pl.when(j == 0)
    def _():
        zero_ref[...] = jnp.zeros_like(zero_ref)
        zero_fill(True)

    @pl.when(j >= 2)
    def _():
        wait_block(j - 2, slot)

    lo = brow_ref[0, 0:1, :]
    hi = brow_ref[1, 0:1, :]
    first_copy = lax.broadcasted_iota(I32, (KC, 2 * N_EXPERTS), 1) < N_EXPERTS
    row_iota = lax.broadcasted_iota(I32, (KC, 2 * N_EXPERTS), 0)
    ranks = em_ref[...].astype(BF16)
    h = h_ref[...]

    def trip(ti, carry):
        onehots = []
        for u in range(GROUP_ROWS // KC):
            rows_e = (row_iota + ti * GROUP_ROWS + u * KC).astype(F32)
            in_run = jnp.logical_and(jnp.logical_and(rows_e >= lo, rows_e < hi), first_copy)
            sel = jnp.dot(jnp.where(in_run, 1.0, 0.0).astype(BF16), ranks,
                          preferred_element_type=F32)
            local = jnp.sum(jnp.where(in_run, rows_e - lo + 3.0, 0.0), axis=1,
                            keepdims=True) - 3.0
            onehots.append(jnp.where(sel == local, 1.0, 0.0).astype(BF16))
        xs = jnp.dot(jnp.concatenate(onehots, axis=0), h, preferred_element_type=F32)
        first = pl.multiple_of(ti * GROUP_CHUNKS, GROUP_CHUNKS)
        buf_ref[slot, pl.ds(first, GROUP_CHUNKS)] = xs.astype(BF16).reshape(
            GROUP_CHUNKS, ROW_CHUNK, xs.shape[1])
        for k in range(GROUP_CHUNKS):
            chunk_copy(slot, first + k, gdst_ref[j * BLOCK_CHUNKS + first + k]).start()
        return carry
    lax.fori_loop(0, ngrp_ref[j], trip, 0)

    @pl.when(j == last)
    def _():
        wait_block(j, slot)

        @pl.when(j >= 1)
        def _():
            wait_block(j - 1, 1 - slot)

        zero_fill(False)


def _bounds_specs():
    return [pl.BlockSpec((None, 2, 8, 2 * N_EXPERTS), lambda j, *_: (j, 0, 0, 0)),
            pl.BlockSpec((None, 2, 2 * N_EXPERTS, 128), lambda j, *_: (j, 0, 0, 0))]


def _dispatch(plan, h2_flat, em, n_rows):
    t, d = h2_flat.shape
    grid_spec = pltpu.PrefetchScalarGridSpec(
        num_scalar_prefetch=4,
        grid=(t // TB,),
        in_specs=[pl.BlockSpec((TB, d), lambda j, *_: (j, 0)),
                  pl.BlockSpec((2 * N_EXPERTS, TB), lambda j, *_: (0, j))] + _bounds_specs()[:1],
        out_specs=pl.BlockSpec(memory_space=pl.ANY),
        scratch_shapes=[pltpu.VMEM((2, BLOCK_CHUNKS, ROW_CHUNK, d), BF16),
                        pltpu.VMEM((ZERO_ROWS // ROW_CHUNK, ROW_CHUNK, d), BF16),
                        pltpu.SemaphoreType.DMA((2,)),
                        pltpu.SemaphoreType.DMA((1,))],
    )
    xs = pl.pallas_call(
        _dispatch_kernel,
        grid_spec=grid_spec,
        out_shape=jax.ShapeDtypeStruct((n_rows // ROW_CHUNK, ROW_CHUNK, d), BF16),
        compiler_params=pltpu.CompilerParams(
            dimension_semantics=("arbitrary",), vmem_limit_bytes=VMEM_LIMIT),
        name="moe_dispatch",
    )(plan["ngrp"], plan["gdst"], plan["zn"], plan["zbase"], h2_flat, em, plan["bounds_row"])
    return xs.reshape(n_rows, d)


def _ffn_kernel(tstart_ref, nact_ref, wg_ref, wu_ref, wd_ref, xs_ref, ys_ref,
                xbuf, ybuf, wgu_s, wd_s, xsem, ysem):
    e = pl.program_id(0)
    nact = nact_ref[0]
    t0 = tstart_ref[e]
    t1 = tstart_ref[e + 1]

    def x_copy(t):
        s = t % (FFN_AHEAD + 1)
        return pltpu.make_async_copy(
            xs_ref.at[pl.ds(pl.multiple_of(t * TM_FFN, TM_FFN), TM_FFN), :], xbuf.at[s],
            xsem.at[s])

    def y_copy(t):
        s = t % 2
        return pltpu.make_async_copy(
            ybuf.at[s], ys_ref.at[pl.ds(pl.multiple_of(t * TM_FFN, TM_FFN), TM_FFN), :],
            ysem.at[s])

    @pl.when(e == 0)
    def _():
        for a in range(FFN_AHEAD):
            @pl.when(a < nact)
            def _():
                x_copy(a).start()

    @pl.when(t1 > t0)
    def _():
        wgu_s[:, :D_EXPERT] = wg_ref[...].astype(BF16)
        wgu_s[:, D_EXPERT:] = wu_ref[...].astype(BF16)
        wd_s[...] = wd_ref[...].astype(BF16)

        def tile(t, carry):
            x_copy(t).wait()

            @pl.when(t + FFN_AHEAD < nact)
            def _():
                x_copy(t + FFN_AHEAD).start()

            gu = jnp.dot(xbuf[t % (FFN_AHEAD + 1)], wgu_s[...], preferred_element_type=F32)
            act = _silu(gu[:, :D_EXPERT]) * gu[:, D_EXPERT:]
            y = jnp.dot(act.astype(BF16), wd_s[...], preferred_element_type=F32)

            @pl.when(t >= 2)
            def _():
                y_copy(t - 2).wait()

            ybuf[t % 2] = y.astype(BF16)
            y_copy(t).start()
            return carry
        lax.fori_loop(t0, t1, tile, 0)

    @pl.when(e == pl.num_programs(0) - 1)
    def _():
        @pl.when(nact >= 2)
        def _():
            y_copy(nact - 2).wait()
        y_copy(nact - 1).wait()


def _expert_ffn(tile_start, nact, xs, n_tiles, w_gate, w_up, w_down):
    n_rows, d = n_tiles * TM_FFN, xs.shape[1]
    f = D_EXPERT
    grid_spec = pltpu.PrefetchScalarGridSpec(
        num_scalar_prefetch=2,
        grid=(N_EXPERTS,),
        in_specs=[pl.BlockSpec((None, d, f), lambda e, ts, na: (e, 0, 0)),
                  pl.BlockSpec((None, d, f), lambda e, ts, na: (e, 0, 0)),
                  pl.BlockSpec((None, f, d), lambda e, ts, na: (e, 0, 0)),
                  pl.BlockSpec(memory_space=pl.ANY)],
        out_specs=pl.BlockSpec(memory_space=pl.ANY),
        scratch_shapes=[pltpu.VMEM((FFN_AHEAD + 1, TM_FFN, d), BF16),
                        pltpu.VMEM((2, TM_FFN, d), BF16),
                        pltpu.VMEM((d, 2 * f), BF16), pltpu.VMEM((f, d), BF16),
                        pltpu.SemaphoreType.DMA((FFN_AHEAD + 1,)),
                        pltpu.SemaphoreType.DMA((2,))],
    )
    return pl.pallas_call(
        _ffn_kernel,
        grid_spec=grid_spec,
        out_shape=jax.ShapeDtypeStruct((n_rows, d), BF16),
        compiler_params=pltpu.CompilerParams(
            dimension_semantics=("arbitrary",), vmem_limit_bytes=VMEM_LIMIT),
        name="moe_experts",
    )(tile_start, nact, w_gate, w_up, w_down, xs)


def _combine_kernel(ngrp_ref, gsrc_ref, ys_ref, tm_ref, brow_ref, bcol_ref, h_ref, wsg_ref,
                    wsu_ref, wsd_ref, x1_ref, mod_ref, gpost_ref, o_ref, buf_ref, p_ref, sem):
    j = pl.program_id(0)
    last = pl.num_programs(0) - 1
    slot = j % 2

    def chunk_copy(s, c, g):
        return pltpu.make_async_copy(ys_ref.at[g], buf_ref.at[s, c], sem.at[s])

    def fetch_block(jj, s):
        def issue_group(gi, carry):
            for k in range(GROUP_CHUNKS):
                c = gi * GROUP_CHUNKS + k
                chunk_copy(s, c, gsrc_ref[jj * BLOCK_CHUNKS + c]).start()
            return carry
        lax.fori_loop(0, ngrp_ref[jj], issue_group, 0)

    @pl.when(j == 0)
    def _():
        buf_ref[...] = jnp.zeros_like(buf_ref)
        fetch_block(0, 0)

    @pl.when(j < last)
    def _():
        fetch_block(j + 1, 1 - slot)

    h = h_ref[...]
    act = _silu(jnp.dot(h, wsg_ref[...], preferred_element_type=F32)) * jnp.dot(
        h, wsu_ref[...], preferred_element_type=F32)
    y = jnp.dot(act.astype(BF16), wsd_ref[...], preferred_element_type=F32)

    tm = tm_ref[...]
    lane = lax.broadcasted_iota(I32, tm.shape, 1)
    rank_t = jnp.where(lane < N_EXPERTS, tm, 0.0).astype(BF16)
    gate_t = jnp.where(lane < N_EXPERTS, 0.0, tm).astype(BF16)
    lane8 = lax.broadcasted_iota(I32, (8, 2 * N_EXPERTS), 1)
    lo_row = jnp.where(lane8 < N_EXPERTS, brow_ref[0], 0.0).astype(BF16)
    lo_col = jnp.concatenate([bcol_ref[0]] * (KC // 128), axis=1)
    hi_col = jnp.concatenate([bcol_ref[1]] * (KC // 128), axis=1)
    for cb in range(BLOCK_ROWS // KC):
        rows = (lax.broadcasted_iota(I32, (2 * N_EXPERTS, KC), 1) + cb * KC).astype(F32)
        in_run = jnp.where(jnp.logical_and(rows >= lo_col, rows < hi_col), 1.0, 0.0).astype(BF16)
        sel_rank = jnp.dot(rank_t, in_run, preferred_element_type=F32)
        sel_gate = jnp.dot(gate_t, in_run, preferred_element_type=F32)
        sel_lo = jnp.dot(lo_row, in_run, preferred_element_type=F32)
        local = rows[0:1, :] - sel_lo[0:1, :]
        p_ref[:, cb * KC:(cb + 1) * KC] = jnp.where(sel_rank == local, sel_gate,
                                                    0.0).astype(BF16)

    def wait_group(gi, carry):
        pltpu.make_async_copy(ys_ref.at[pl.ds(0, GROUP_CHUNKS)],
                              buf_ref.at[slot, pl.ds(0, GROUP_CHUNKS)], sem.at[slot]).wait()
        return carry
    lax.fori_loop(0, ngrp_ref[j], wait_group, 0)

    rows = buf_ref[slot].reshape(BLOCK_ROWS, buf_ref.shape[3])
    y = y + jnp.dot(p_ref[...], rows, preferred_element_type=F32)

    ga2 = mod_ref[5:6, :]
    o_ref[...] = x1_ref[...] + ga2 * _rms(y, gpost_ref[...])


def _combine(plan, ys, tm, h2_flat, ws_gate, ws_up, ws_down, x1_flat, mod3, g_post, s):
    t, d = h2_flat.shape
    f = ws_gate.shape[1]
    blocks_per_seq = s // TB
    full = lambda shape: pl.BlockSpec(shape, lambda j, *_: (0,) * len(shape))
    grid_spec = pltpu.PrefetchScalarGridSpec(
        num_scalar_prefetch=2,
        grid=(t // TB,),
        in_specs=[pl.BlockSpec(memory_space=pl.ANY),
                  pl.BlockSpec((TB, 2 * N_EXPERTS), lambda j, *_: (j, 0))] + _bounds_specs() + [
                  pl.BlockSpec((TB, d), lambda j, *_: (j, 0)),
                  full((d, f)), full((d, f)), full((f, d)),
                  pl.BlockSpec((TB, d), lambda j, *_: (j, 0)),
                  pl.BlockSpec((None, 6, d), lambda j, *_: (j // blocks_per_seq, 0, 0)),
                  full((1, d))],
        out_specs=pl.BlockSpec((TB, d), lambda j, *_: (j, 0)),
        scratch_shapes=[pltpu.VMEM((2, BLOCK_CHUNKS, ROW_CHUNK, d), BF16),
                        pltpu.VMEM((TB, BLOCK_ROWS), BF16),
                        pltpu.SemaphoreType.DMA((2,))],
    )
    return pl.pallas_call(
        _combine_kernel,
        grid_spec=grid_spec,
        out_shape=jax.ShapeDtypeStruct((t, d), F32),
        compiler_params=pltpu.CompilerParams(
            dimension_semantics=("arbitrary",), vmem_limit_bytes=VMEM_LIMIT),
        name="moe_combine",
    )(plan["ngrp"], plan["gsrc"], ys.reshape(-1, ROW_CHUNK, d), tm, plan["bounds_row"],
      plan["bounds_col"], h2_flat, ws_gate, ws_up, ws_down, x1_flat, mod3, g_post)


def kernel(x, c, w_ada, b_ada, g_pre_mix, w_in, ln_sgu_g, ln_sgu_b, w_spatial, b_spatial,
           g_branch, w_out, g_post_mix, g_pre_ffn, w_router, router_bias, w_gate, w_up, w_down,
           ws_gate, ws_up, ws_down, g_post_ffn):
    bsz, s, d = x.shape
    t = bsz * s
    nblk = t // TB
    max_rows = t * TOP_K + nblk * N_EXPERTS * (ROW_CHUNK - 1) + N_EXPERTS * (TM_FFN - 1)
    n_tiles = -(-max_rows // TM_FFN)
    row = lambda a: a.reshape(1, -1)
    for l in range(w_ada.shape[0]):
        mod3 = _ada(c, w_ada[l], b_ada[l]).reshape(bsz, 6, d)
        og, q, k, v = _premix(x, mod3, row(g_pre_mix[l]), w_in[l].astype(BF16),
                              row(ln_sgu_g[l]), row(ln_sgu_b[l]), w_spatial[l],
                              b_spatial[l].T, row(g_branch[l, :D_GMLP]))
        osb = _attention(q, k, v)
        x1, h2, logits_t = _postmix(og, osb, x, mod3, row(g_branch[l, D_GMLP:]),
                                    w_out[l].astype(BF16), row(g_post_mix[l]),
                                    row(g_pre_ffn[l]), w_router[l].T)
        em, tm, cnt = _route(logits_t, router_bias[l])
        plan = _dispatch_plan(cnt[:, :, 0].astype(I32), n_tiles)
        h2_flat = h2.reshape(t, d)
        xs = _dispatch(plan, h2_flat, em, n_tiles * TM_FFN + nblk * GROUP_ROWS)
        ys = _expert_ffn(plan["tile_start"], plan["nact"], xs, n_tiles,
                         w_gate[l], w_up[l], w_down[l])
        out = _combine(plan, ys, tm, h2_flat, ws_gate[l].astype(BF16), ws_up[l].astype(BF16),
                       ws_down[l].astype(BF16), x1.reshape(t, d), mod3, row(g_post_ffn[l]), s)
        x = out.reshape(bsz, s, d)
    return x
```

```python
import jax
import jax.numpy as jnp
from jax import lax
from jax.experimental import pallas as pl
from jax.experimental.pallas import tpu as pltpu

F32 = jnp.float32
BF16 = jnp.bfloat16
I32 = jnp.int32

D_MODEL = 1024
D_GMLP = 512
GMLP_GROUPS = 4
GMLP_BLOCK = 128
CHUNK = 64
D_SB = 512
SB_HEAD_DIM = 64
N_PAIRS = D_SB // 128
N_EXPERTS = 64
N_GROUPS = 8
GROUP_SIZE = N_EXPERTS // N_GROUPS
TOPK_GROUPS = 4
TOP_K = 8
D_EXPERT = 256
ROUTED_SCALE = 2.5
EPS = 1e-6
D_IN = 2 * D_GMLP + 3 * D_SB

TM_MIX = 512
TQ = 128
VMEM_LIMIT = 56 * 1024 * 1024
ATTN_DEAD_LOG = -110.0

TB = 256
ROW_CHUNK = 16
KC = 256
GROUP_CHUNKS = 32
GROUP_ROWS = GROUP_CHUNKS * ROW_CHUNK
BLOCK_ROWS = -(-(TB * TOP_K + N_EXPERTS * (ROW_CHUNK - 1)) // GROUP_ROWS) * GROUP_ROWS
BLOCK_CHUNKS = BLOCK_ROWS // ROW_CHUNK
TM_FFN = 512
FFN_AHEAD = 3
FFN_OUT = 3
ZERO_ROWS = 128


def _rms(x, g):
    return x * lax.rsqrt(jnp.mean(x * x, axis=-1, keepdims=True) + EPS) * g


def _silu(x):
    return x * jax.nn.sigmoid(x)


def _ada_kernel(c_ref, w_ref, b_ref, o_ref):
    o_ref[...] = jnp.dot(_silu(c_ref[...]), w_ref[...], preferred_element_type=F32,
                         precision=lax.Precision.HIGHEST) + b_ref[...]


def _ada(c, w, b):
    bsz, d = c.shape
    n = w.shape[1]
    tn = 512
    return pl.pallas_call(
        _ada_kernel,
        grid=(n // tn,),
        in_specs=[pl.BlockSpec((bsz, d), lambda j: (0, 0)),
                  pl.BlockSpec((d, tn), lambda j: (0, j)),
                  pl.BlockSpec((1, tn), lambda j: (0, j))],
        out_specs=pl.BlockSpec((bsz, tn), lambda j: (0, j)),
        out_shape=jax.ShapeDtypeStruct((bsz, n), F32),
        name="ada_mod",
    )(c, w, b.reshape(1, n))


def _premix_kernel(x_ref, mod_ref, gpre_ref, win_ref, lng_ref, lnb_ref, wsp_ref, bsp_ref,
                   gbr_ref, og_ref, q_ref, k_ref, v_ref):
    x = x_ref[...]
    sh1 = mod_ref[0:1, :]
    sc1 = mod_ref[1:2, :]
    h = _rms(x, gpre_ref[...]) * (1.0 + sc1) + sh1
    proj = jnp.dot(h.astype(BF16), win_ref[...], preferred_element_type=F32)

    u = jax.nn.gelu(proj[:, :D_GMLP])
    v = jax.nn.gelu(proj[:, D_GMLP:2 * D_GMLP])
    mu = jnp.mean(v, axis=-1, keepdims=True)
    var = jnp.mean(jnp.square(v - mu), axis=-1, keepdims=True)
    v = ((v - mu) * lax.rsqrt(var + EPS) * lng_ref[...] + lnb_ref[...]).astype(BF16)

    i = lax.broadcasted_iota(I32, (GMLP_BLOCK, GMLP_BLOCK), 0)
    j = lax.broadcasted_iota(I32, (GMLP_BLOCK, GMLP_BLOCK), 1)
    causal = (j // CHUNK) <= (i // CHUNK)
    gd = D_GMLP // GMLP_GROUPS
    blocks = []
    for nb in range(x.shape[0] // GMLP_BLOCK):
        rows = slice(nb * GMLP_BLOCK, (nb + 1) * GMLP_BLOCK)
        cols = []
        for g in range(GMLP_GROUPS):
            w = jnp.where(causal, wsp_ref[g], 0.0).astype(BF16)
            mixed = jnp.dot(w, v[rows, g * gd:(g + 1) * gd], preferred_element_type=F32)
            cols.append(mixed + bsp_ref[:, g:g + 1])
        blocks.append(u[rows, :] * jnp.concatenate(cols, axis=1))
    og = jnp.concatenate(blocks, axis=0)
    og_ref[...] = _rms(og, gbr_ref[...]).astype(BF16)

    base = 2 * D_GMLP
    scale = SB_HEAD_DIM ** -0.5
    for p in range(N_PAIRS):
        q_ref[p] = (proj[:, base + 128 * p:base + 128 * (p + 1)] * scale).astype(BF16)
        k_ref[p] = proj[:, base + D_SB + 128 * p:base + D_SB + 128 * (p + 1)].astype(BF16)
        v_ref[p] = proj[:, base + 2 * D_SB + 128 * p:base + 2 * D_SB + 128 * (p + 1)].astype(BF16)


def _premix(x, mod3, g_pre, w_in, ln_g, ln_b, w_sp, b_sp_t, g_br):
    bsz, s, d = x.shape
    tm = TM_MIX
    full = lambda shape: pl.BlockSpec(shape, lambda b, i: (0,) * len(shape))
    qkv_spec = pl.BlockSpec((None, N_PAIRS, tm, 128), lambda b, i: (b, 0, i, 0))
    qkv_shape = jax.ShapeDtypeStruct((bsz, N_PAIRS, s, 128), BF16)
    return pl.pallas_call(
        _premix_kernel,
        grid=(bsz, s // tm),
        in_specs=[pl.BlockSpec((None, tm, d), lambda b, i: (b, i, 0)),
                  pl.BlockSpec((None, 6, d), lambda b, i: (b, 0, 0)),
                  full((1, d)), full((d, D_IN)), full((1, D_GMLP)), full((1, D_GMLP)),
                  full((GMLP_GROUPS, GMLP_BLOCK, GMLP_BLOCK)), full((GMLP_BLOCK, GMLP_GROUPS)),
                  full((1, D_GMLP))],
        out_specs=[pl.BlockSpec((None, tm, D_GMLP), lambda b, i: (b, i, 0)),
                   qkv_spec, qkv_spec, qkv_spec],
        out_shape=[jax.ShapeDtypeStruct((bsz, s, D_GMLP), BF16), qkv_shape, qkv_shape, qkv_shape],
        compiler_params=pltpu.CompilerParams(
            dimension_semantics=("parallel", "parallel"), vmem_limit_bytes=VMEM_LIMIT),
        name="premix",
    )(x, mod3, g_pre, w_in, ln_g, ln_b, w_sp, b_sp_t, g_br)


def _attn_kernel(q_ref, k_ref, v_ref, o_ref, *scratch):
    qs_refs = scratch[:N_PAIRS]
    acc_refs = scratch[N_PAIRS:2 * N_PAIRS]
    carry_refs = scratch[2 * N_PAIRS:]
    qi = pl.program_id(1)
    first_head = lax.broadcasted_iota(I32, (TQ, 128), 1) < SB_HEAD_DIM
    for p in range(N_PAIRS):
        q2 = q_ref[p]
        zero = jnp.zeros_like(q2)
        qs_refs[p][:TQ, :] = jnp.where(first_head, q2, zero)
        qs_refs[p][TQ:, :] = jnp.where(first_head, zero, q2)
        acc_refs[p][...] = jnp.zeros_like(acc_refs[p])
        carry_refs[p][...] = jnp.zeros_like(carry_refs[p])

    r = jnp.bitwise_and(lax.broadcasted_iota(I32, (2 * TQ, TQ), 0), TQ - 1)
    c = lax.broadcasted_iota(I32, (2 * TQ, TQ), 1)
    strict_causal = c < r
    kr = lax.broadcasted_iota(I32, (TQ, TQ), 0)
    kc = lax.broadcasted_iota(I32, (TQ, TQ), 1)
    suffix = jnp.concatenate([(kr > kc).astype(BF16), jnp.ones((TQ, TQ), BF16)], axis=1)
    suffix2 = jnp.concatenate([suffix, suffix], axis=0)

    def key_block(j, diagonal):
        start = pl.multiple_of(j * TQ, TQ)
        pairs = range(N_PAIRS)
        zs = [lax.dot_general(qs_refs[p][...], k_ref[p, pl.ds(start, TQ), :],
                              (((1,), (1,)), ((), ())), preferred_element_type=F32)
              for p in pairs]
        log_betas, splits = [], []
        for p in pairs:
            z = zs[p]
            log_beta = jnp.minimum(z, 0.0) - jnp.log(1.0 + jnp.exp(-jnp.abs(z)))
            log_1mb = log_beta - z
            if diagonal:
                log_1mb = jnp.where(strict_causal, log_1mb, 0.0)
            hi = log_1mb.astype(BF16)
            lo = (log_1mb - hi.astype(F32)).astype(BF16)
            log_betas.append(log_beta)
            splits.append(jnp.concatenate([hi, lo], axis=1))
        sums = [jnp.dot(splits[p], suffix2, preferred_element_type=F32) for p in pairs]
        weights = []
        live = None
        for p in pairs:
            s = sums[p]
            carry = carry_refs[p][...]
            a = jnp.exp(log_betas[p] + carry + s[:, :TQ])
            if diagonal:
                a = jnp.where(strict_causal, a, 0.0)
            weights.append(a.astype(BF16))
            carry = carry + s[:, TQ:]
            carry_refs[p][...] = carry
            live = carry if live is None else jnp.maximum(live, carry)
        for p in pairs:
            acc_refs[p][...] += jnp.dot(weights[p], v_ref[p, pl.ds(start, TQ), :],
                                        preferred_element_type=F32)
        return jnp.max(live)

    live = key_block(qi, True)

    def cond(state):
        j, live = state
        return jnp.logical_and(j >= 0, live > ATTN_DEAD_LOG)

    def body(state):
        j, _ = state
        return j - 1, key_block(j, False)

    lax.while_loop(cond, body, (qi - 1, live))
    for p in range(N_PAIRS):
        o_ref[:, 128 * p:128 * (p + 1)] = jnp.where(first_head, acc_refs[p][:TQ, :],
                                                    acc_refs[p][TQ:, :])


def _attention(q, k, v):
    bsz, npair, s, _ = q.shape
    kv_spec = pl.BlockSpec((None, npair, s, 128), lambda b, i: (b, 0, 0, 0))
    return pl.pallas_call(
        _attn_kernel,
        grid=(bsz, s // TQ),
        in_specs=[pl.BlockSpec((None, npair, TQ, 128), lambda b, i: (b, 0, i, 0)),
                  kv_spec, kv_spec],
        out_specs=pl.BlockSpec((None, TQ, npair * 128), lambda b, i: (b, i, 0)),
        out_shape=jax.ShapeDtypeStruct((bsz, s, npair * 128), F32),
        scratch_shapes=([pltpu.VMEM((2 * TQ, 128), BF16)] * npair
                        + [pltpu.VMEM((2 * TQ, 128), F32)] * npair
                        + [pltpu.VMEM((2 * TQ, TQ), F32)] * npair),
        compiler_params=pltpu.CompilerParams(
            dimension_semantics=("parallel", "parallel"), vmem_limit_bytes=VMEM_LIMIT),
        name="stickbreak_attn",
    )(q, k, v)


def _postmix_kernel(og_ref, osb_ref, x_ref, mod_ref, gbr_ref, wout_ref, gpost_ref, gpre_ref,
                    wrt_ref, x1_ref, h2_ref, logit_ref):
    ga1 = mod_ref[2:3, :]
    sh2 = mod_ref[3:4, :]
    sc2 = mod_ref[4:5, :]
    osb = _rms(osb_ref[...], gbr_ref[...]).astype(BF16)
    m = (jnp.dot(og_ref[...], wout_ref[:D_GMLP, :], preferred_element_type=F32)
         + jnp.dot(osb, wout_ref[D_GMLP:, :], preferred_element_type=F32))
    x1 = x_ref[...] + ga1 * _rms(m, gpost_ref[...])
    x1_ref[...] = x1
    h2 = _rms(x1, gpre_ref[...]) * (1.0 + sc2) + sh2
    h_hi = h2.astype(BF16)
    h2_ref[...] = h_hi
    h_lo = (h2 - h_hi.astype(F32)).astype(BF16)
    w = wrt_ref[...]
    w_hi = w.astype(BF16)
    w_lo = (w - w_hi.astype(F32)).astype(BF16)
    nt = (((1,), (1,)), ((), ()))
    by_hi = lax.dot_general(jnp.concatenate([w_hi, w_lo], axis=0), h_hi, nt,
                            preferred_element_type=F32)
    by_lo = lax.dot_general(w_hi, h_lo, nt, preferred_element_type=F32)
    logit_ref[...] = by_hi[:N_EXPERTS] + by_hi[N_EXPERTS:] + by_lo


def _postmix(og, osb, x, mod3, g_br, w_out, g_post, g_pre, w_router_t):
    bsz, s, d = x.shape
    tm = TM_MIX
    nt = s // tm
    full = lambda shape: pl.BlockSpec(shape, lambda b, i: (0,) * len(shape))
    return pl.pallas_call(
        _postmix_kernel,
        grid=(bsz, nt),
        in_specs=[pl.BlockSpec((None, tm, D_GMLP), lambda b, i: (b, i, 0)),
                  pl.BlockSpec((None, tm, D_SB), lambda b, i: (b, i, 0)),
                  pl.BlockSpec((None, tm, d), lambda b, i: (b, i, 0)),
                  pl.BlockSpec((None, 6, d), lambda b, i: (b, 0, 0)),
                  full((1, D_SB)), full((d, d)), full((1, d)), full((1, d)),
                  full((N_EXPERTS, d))],
        out_specs=[pl.BlockSpec((None, tm, d), lambda b, i: (b, i, 0)),
                   pl.BlockSpec((None, tm, d), lambda b, i: (b, i, 0)),
                   pl.BlockSpec((N_EXPERTS, tm), lambda b, i: (0, b * nt + i))],
        out_shape=[jax.ShapeDtypeStruct((bsz, s, d), F32),
                   jax.ShapeDtypeStruct((bsz, s, d), BF16),
                   jax.ShapeDtypeStruct((N_EXPERTS, bsz * s), F32)],
        compiler_params=pltpu.CompilerParams(
            dimension_semantics=("parallel", "parallel"), vmem_limit_bytes=VMEM_LIMIT),
        name="postmix",
    )(og, osb, x, mod3, g_br, w_out, g_post, g_pre, w_router_t)


def _first_index_of_max(x, idx, axis, size):
    m = jnp.max(x, axis=axis, keepdims=True)
    return jnp.min(jnp.where(x == m, idx, size), axis=axis, keepdims=True)


def _route_kernel(logit_ref, bias_ref, em_ref, tm_ref, cnt_ref):
    scores = jax.nn.sigmoid(logit_ref[...])
    biased = scores + bias_ref[...]
    neg = jnp.float32(-jnp.inf)

    grouped = biased.reshape(N_GROUPS, GROUP_SIZE, TB)
    within = lax.broadcasted_iota(I32, grouped.shape, 1)
    top1 = jnp.max(grouped, axis=1, keepdims=True)
    first = _first_index_of_max(grouped, within, 1, GROUP_SIZE)
    top2 = jnp.max(jnp.where(within == first, neg, grouped), axis=1, keepdims=True)
    group_score = (top1 + top2).reshape(N_GROUPS, TB)

    gidx = lax.broadcasted_iota(I32, group_score.shape, 0)
    group_on = jnp.zeros(group_score.shape, jnp.bool_)
    for _ in range(TOPK_GROUPS):
        pick = gidx == _first_index_of_max(group_score, gidx, 0, N_GROUPS)
        group_on = jnp.logical_or(group_on, pick)
        group_score = jnp.where(pick, neg, group_score)

    masked = jnp.where(group_on.reshape(N_GROUPS, 1, TB), grouped, neg).reshape(N_EXPERTS, TB)
    eidx = lax.broadcasted_iota(I32, masked.shape, 0)
    chosen = jnp.zeros(masked.shape, jnp.bool_)
    for _ in range(TOP_K):
        pick = eidx == _first_index_of_max(masked, eidx, 0, N_EXPERTS)
        chosen = jnp.logical_or(chosen, pick)
        masked = jnp.where(pick, neg, masked)

    w = jnp.where(chosen, scores, 0.0)
    gates = w / jnp.sum(w, axis=0, keepdims=True) * ROUTED_SCALE

    chosen_f = chosen.astype(F32)
    tr = lax.broadcasted_iota(I32, (TB, TB), 0)
    tc = lax.broadcasted_iota(I32, (TB, TB), 1)
    rank = jnp.dot(chosen_f.astype(BF16), (tr < tc).astype(BF16), preferred_element_type=F32)
    both = jnp.concatenate([jnp.where(chosen, rank, -1.0), gates], axis=0)
    em_ref[...] = both
    tm_ref[...] = both.T
    cnt_ref[...] = jnp.broadcast_to(jnp.sum(chosen_f, axis=1, keepdims=True), (N_EXPERTS, 128))


def _route(logits_t, bias):
    e, t = logits_t.shape
    nblk = t // TB
    return pl.pallas_call(
        _route_kernel,
        grid=(nblk,),
        in_specs=[pl.BlockSpec((e, TB), lambda i: (0, i)),
                  pl.BlockSpec((e, 1), lambda i: (0, 0))],
        out_specs=[pl.BlockSpec((2 * e, TB), lambda i: (0, i)),
                   pl.BlockSpec((TB, 2 * e), lambda i: (i, 0)),
                   pl.BlockSpec((None, e, 128), lambda i: (i, 0, 0))],
        out_shape=[jax.ShapeDtypeStruct((2 * e, t), F32),
                   jax.ShapeDtypeStruct((t, 2 * e), F32),
                   jax.ShapeDtypeStruct((nblk, e, 128), F32)],
        compiler_params=pltpu.CompilerParams(dimension_semantics=("parallel",)),
        name="route",
    )(logits_t, bias.reshape(e, 1))


def _dispatch_plan(cnt, n_tiles):
    nblk, e = cnt.shape
    pc = (cnt + ROW_CHUNK - 1) // ROW_CHUNK * ROW_CHUNK
    start = jnp.cumsum(pc, axis=1) - pc
    nchunk = jnp.sum(pc, axis=1) // ROW_CHUNK
    off = jnp.cumsum(pc, axis=0) - pc
    ecount = jnp.sum(pc, axis=0)
    epad = (ecount + TM_FFN - 1) // TM_FFN * TM_FFN
    gend = jnp.cumsum(epad)
    gbase = gend - epad
    nact = gend[-1] // TM_FFN
    tile_start = jnp.concatenate([gbase, gend[-1:]]) // TM_FFN
    cidx = jnp.arange(BLOCK_CHUNKS, dtype=I32)
    start16 = start // ROW_CHUNK
    shift = (gbase[None, :] + off) // ROW_CHUNK - start16
    dshift = shift - jnp.pad(shift, ((0, 0), (1, 0)))[:, :-1]
    in_or_after = (start16[:, None, :] <= cidx[None, :, None]).astype(I32)
    where = cidx[None, :] + jnp.sum(in_or_after * dshift[:, None, :], axis=2)
    used = cidx[None, :] < nchunk[:, None]
    spare = (n_tiles * TM_FFN // ROW_CHUNK
             + jnp.arange(nblk, dtype=I32)[:, None] * GROUP_CHUNKS + cidx[None, :] % GROUP_CHUNKS)
    gdst = jnp.where(used, where, spare)
    gsrc = jnp.where(used, where, 0)
    ngrp = (nchunk + GROUP_CHUNKS - 1) // GROUP_CHUNKS
    zbase = (gbase + ecount) // ROW_CHUNK
    zn = (epad - ecount) // ROW_CHUNK

    lo = jnp.tile(start.astype(F32), (1, 2))
    hi = jnp.tile((start + pc).astype(F32), (1, 2))
    bounds_row = jnp.stack([jnp.broadcast_to(lo[:, None, :], (nblk, 8, 2 * e)),
                            jnp.broadcast_to(hi[:, None, :], (nblk, 8, 2 * e))], axis=1)
    bounds_col = jnp.stack([jnp.broadcast_to(lo[:, :, None], (nblk, 2 * e, 128)),
                            jnp.broadcast_to(hi[:, :, None], (nblk, 2 * e, 128))], axis=1)
    as_i32 = lambda a: a.astype(I32)
    return dict(ngrp=as_i32(ngrp), gdst=as_i32(gdst.reshape(-1)), gsrc=as_i32(gsrc.reshape(-1)),
                zn=as_i32(zn), zbase=as_i32(zbase), tile_start=as_i32(tile_start),
                nact=as_i32(nact.reshape(1)), bounds_row=bounds_row, bounds_col=bounds_col)


def _dispatch_kernel(ngrp_ref, gdst_ref, zn_ref, zbase_ref, h_ref, em_ref, brow_ref,
                     xs_ref, buf_ref, zero_ref, sem, zsem):
    j = pl.program_id(0)
    last = pl.num_programs(0) - 1
    slot = j % 2

    def chunk_copy(s, c, g):
        return pltpu.make_async_copy(buf_ref.at[s, c], xs_ref.at[g], sem.at[s])

    def wait_block(jj, s):
        def body(i, carry):
            pltpu.make_async_copy(buf_ref.at[s, pl.ds(0, GROUP_CHUNKS)],
                                  xs_ref.at[pl.ds(0, GROUP_CHUNKS)], sem.at[s]).wait()
            return carry
        lax.fori_loop(0, ngrp_ref[jj], body, 0)

    def zero_copy(g, chunks):
        return pltpu.make_async_copy(zero_ref.at[pl.ds(0, chunks)], xs_ref.at[pl.ds(g, chunks)],
                                     zsem.at[0])

    def zero_fill(start):
        big = ZERO_ROWS // ROW_CHUNK

        def per_expert(e, carry):
            n = zn_ref[e]
            base = zbase_ref[e]

            def big_copy(i, c2):
                cp = zero_copy(base + i * big, big)
                cp.start() if start else cp.wait()
                return c2
            lax.fori_loop(0, n // big, big_copy, 0)

            def small_copy(i, c2):
                cp = zero_copy(base + n // big * big + i, 1)
                cp.start() if start else cp.wait()
                return c2
            lax.fori_loop(0, n % big, small_copy, 0)
            return carry
        lax.fori_loop(0, N_EXPERTS, per_expert, 0)

    @pl.when(j == 0)
    def _():
        zero_ref[...] = jnp.zeros_like(zero_ref)
        zero_fill(True)

    @pl.when(j >= 2)
    def _():
        wait_block(j - 2, slot)

    lo = brow_ref[0, 0:1, :]
    hi = brow_ref[1, 0:1, :]
    first_copy = lax.broadcasted_iota(I32, (KC, 2 * N_EXPERTS), 1) < N_EXPERTS
    row_iota = lax.broadcasted_iota(I32, (KC, 2 * N_EXPERTS), 0)
    ranks = em_ref[...].astype(BF16)
    h = h_ref[...]

    def trip(ti, carry):
        onehots = []
        for u in range(GROUP_ROWS // KC):
            rows_e = (row_iota + ti * GROUP_ROWS + u * KC).astype(F32)
            in_run = jnp.logical_and(jnp.logical_and(rows_e >= lo, rows_e < hi), first_copy)
            sel = jnp.dot(jnp.where(in_run, 1.0, 0.0).astype(BF16), ranks,
                          preferred_element_type=F32)
            local = jnp.sum(jnp.where(in_run, rows_e - lo + 3.0, 0.0), axis=1,
                            keepdims=True) - 3.0
            onehots.append(jnp.where(sel == local, 1.0, 0.0).astype(BF16))
        xs = jnp.dot(jnp.concatenate(onehots, axis=0), h, preferred_element_type=F32)
        first = pl.multiple_of(ti * GROUP_CHUNKS, GROUP_CHUNKS)
        buf_ref[slot, pl.ds(first, GROUP_CHUNKS)] = xs.astype(BF16).reshape(
            GROUP_CHUNKS, ROW_CHUNK, xs.shape[1])
        for k in range(GROUP_CHUNKS):
            chunk_copy(slot, first + k, gdst_ref[j * BLOCK_CHUNKS + first + k]).start()
        return carry
    lax.fori_loop(0, ngrp_ref[j], trip, 0)

    @pl.when(j == last)
    def _():
        wait_block(j, slot)

        @pl.when(j >= 1)
        def _():
            wait_block(j - 1, 1 - slot)

        zero_fill(False)


def _bounds_specs():
    return [pl.BlockSpec((None, 2, 8, 2 * N_EXPERTS), lambda j, *_: (j, 0, 0, 0)),
            pl.BlockSpec((None, 2, 2 * N_EXPERTS, 128), lambda j, *_: (j, 0, 0, 0))]


def _dispatch(plan, h2_flat, em, n_rows):
    t, d = h2_flat.shape
    grid_spec = pltpu.PrefetchScalarGridSpec(
        num_scalar_prefetch=4,
        grid=(t // TB,),
        in_specs=[pl.BlockSpec((TB, d), lambda j, *_: (j, 0)),
                  pl.BlockSpec((2 * N_EXPERTS, TB), lambda j, *_: (0, j))] + _bounds_specs()[:1],
        out_specs=pl.BlockSpec(memory_space=pl.ANY),
        scratch_shapes=[pltpu.VMEM((2, BLOCK_CHUNKS, ROW_CHUNK, d), BF16),
                        pltpu.VMEM((ZERO_ROWS // ROW_CHUNK, ROW_CHUNK, d), BF16),
                        pltpu.SemaphoreType.DMA((2,)),
                        pltpu.SemaphoreType.DMA((1,))],
    )
    xs = pl.pallas_call(
        _dispatch_kernel,
        grid_spec=grid_spec,
        out_shape=jax.ShapeDtypeStruct((n_rows // ROW_CHUNK, ROW_CHUNK, d), BF16),
        compiler_params=pltpu.CompilerParams(
            dimension_semantics=("arbitrary",), vmem_limit_bytes=VMEM_LIMIT),
        name="moe_dispatch",
    )(plan["ngrp"], plan["gdst"], plan["zn"], plan["zbase"], h2_flat, em, plan["bounds_row"])
    return xs.reshape(n_rows, d)


def _ffn_kernel(tstart_ref, nact_ref, wg_ref, wu_ref, wd_ref, xs_ref, ys_ref,
                xbuf, ybuf, wgu_s, wd_s, xsem, ysem):
    e = pl.program_id(0)
    nact = nact_ref[0]
    t0 = tstart_ref[e]
    t1 = tstart_ref[e + 1]

    def x_copy(t):
        s = t % (FFN_AHEAD + 1)
        return pltpu.make_async_copy(
            xs_ref.at[pl.ds(pl.multiple_of(t * TM_FFN, TM_FFN), TM_FFN), :], xbuf.at[s],
            xsem.at[s])

    def y_copy(t):
        s = t % FFN_OUT
        return pltpu.make_async_copy(
            ybuf.at[s], ys_ref.at[pl.ds(pl.multiple_of(t * TM_FFN, TM_FFN), TM_FFN), :],
            ysem.at[s])

    @pl.when(e == 0)
    def _():
        for a in range(FFN_AHEAD):
            @pl.when(a < nact)
            def _():
                x_copy(a).start()

    @pl.when(t1 > t0)
    def _():
        wgu_s[:, :D_EXPERT] = wg_ref[...].astype(BF16)
        wgu_s[:, D_EXPERT:] = wu_ref[...].astype(BF16)
        wd_s[...] = wd_ref[...].astype(BF16)

        def tile(t, carry):
            x_copy(t).wait()

            @pl.when(t + FFN_AHEAD < nact)
            def _():
                x_copy(t + FFN_AHEAD).start()

            gu = jnp.dot(xbuf[t % (FFN_AHEAD + 1)], wgu_s[...], preferred_element_type=F32)
            act = _silu(gu[:, :D_EXPERT]) * gu[:, D_EXPERT:]
            y = jnp.dot(act.astype(BF16), wd_s[...], preferred_element_type=F32)

            @pl.when(t >= FFN_OUT)
            def _():
                y_copy(t - FFN_OUT).wait()

            ybuf[t % FFN_OUT] = y.astype(BF16)
            y_copy(t).start()
            return carry
        lax.fori_loop(t0, t1, tile, 0)

    @pl.when(e == pl.num_programs(0) - 1)
    def _():
        for back in range(FFN_OUT, 0, -1):
            @pl.when(nact >= back)
            def _():
                y_copy(nact - back).wait()


def _expert_ffn(tile_start, nact, xs, n_tiles, w_gate, w_up, w_down):
    n_rows, d = n_tiles * TM_FFN, xs.shape[1]
    f = D_EXPERT
    grid_spec = pltpu.PrefetchScalarGridSpec(
        num_scalar_prefetch=2,
        grid=(N_EXPERTS,),
        in_specs=[pl.BlockSpec((None, d, f), lambda e, ts, na: (e, 0, 0)),
                  pl.BlockSpec((None, d, f), lambda e, ts, na: (e, 0, 0)),
                  pl.BlockSpec((None, f, d), lambda e, ts, na: (e, 0, 0)),
                  pl.BlockSpec(memory_space=pl.ANY)],
        out_specs=pl.BlockSpec(memory_space=pl.ANY),
        scratch_shapes=[pltpu.VMEM((FFN_AHEAD + 1, TM_FFN, d), BF16),
                        pltpu.VMEM((FFN_OUT, TM_FFN, d), BF16),
                        pltpu.VMEM((d, 2 * f), BF16), pltpu.VMEM((f, d), BF16),
                        pltpu.SemaphoreType.DMA((FFN_AHEAD + 1,)),
                        pltpu.SemaphoreType.DMA((FFN_OUT,))],
    )
    return pl.pallas_call(
        _ffn_kernel,
        grid_spec=grid_spec,
        out_shape=jax.ShapeDtypeStruct((n_rows, d), BF16),
        compiler_params=pltpu.CompilerParams(
            dimension_semantics=("arbitrary",), vmem_limit_bytes=VMEM_LIMIT),
        name="moe_experts",
    )(tile_start, nact, w_gate, w_up, w_down, xs)


def _combine_kernel(gsrc_ref, ys_ref, tm_ref, brow_ref, bcol_ref, h_ref, wsg_ref,
                    wsu_ref, wsd_ref, x1_ref, mod_ref, gpost_ref, o_ref, buf_ref, p_ref, sem):
    j = pl.program_id(0)
    last = pl.num_programs(0) - 1
    slot = j % 2

    def chunk_copy(s, c, g):
        return pltpu.make_async_copy(ys_ref.at[g], buf_ref.at[s, c], sem.at[s])

    n_groups = BLOCK_CHUNKS // GROUP_CHUNKS

    def fetch_group(jj, s, g):
        for k in range(GROUP_CHUNKS):
            c = g * GROUP_CHUNKS + k
            chunk_copy(s, c, gsrc_ref[jj * BLOCK_CHUNKS + c]).start()

    def wait_block(s):
        for _ in range(n_groups):
            pltpu.make_async_copy(ys_ref.at[pl.ds(0, GROUP_CHUNKS)],
                                  buf_ref.at[s, pl.ds(0, GROUP_CHUNKS)], sem.at[s]).wait()

    @pl.when(j == 0)
    def _():
        for g in range(n_groups):
            fetch_group(0, 0, g)

    nxt = jnp.minimum(j + 1, last)

    h = h_ref[...]
    act = _silu(jnp.dot(h, wsg_ref[...], preferred_element_type=F32)) * jnp.dot(
        h, wsu_ref[...], preferred_element_type=F32)
    y = jnp.dot(act.astype(BF16), wsd_ref[...], preferred_element_type=F32)

    tm = tm_ref[...]
    lane = lax.broadcasted_iota(I32, tm.shape, 1)
    rank_t = jnp.where(lane < N_EXPERTS, tm, 0.0).astype(BF16)
    gate_t = jnp.where(lane < N_EXPERTS, 0.0, tm).astype(BF16)
    lane8 = lax.broadcasted_iota(I32, (8, 2 * N_EXPERTS), 1)
    lo_row = jnp.where(lane8 < N_EXPERTS, brow_ref[0], 0.0).astype(BF16)
    lo_col = jnp.concatenate([bcol_ref[0]] * (KC // 128), axis=1)
    hi_col = jnp.concatenate([bcol_ref[1]] * (KC // 128), axis=1)
    for cb in range(BLOCK_ROWS // KC):
        rows = (lax.broadcasted_iota(I32, (2 * N_EXPERTS, KC), 1) + cb * KC).astype(F32)
        in_run = jnp.where(jnp.logical_and(rows >= lo_col, rows < hi_col), 1.0, 0.0).astype(BF16)
        sel_rank = jnp.dot(rank_t, in_run, preferred_element_type=F32)
        sel_gate = jnp.dot(gate_t, in_run, preferred_element_type=F32)
        sel_lo = jnp.dot(lo_row, in_run, preferred_element_type=F32)
        local = rows[0:1, :] - sel_lo[0:1, :]
        p_ref[:, cb * KC:(cb + 1) * KC] = jnp.where(sel_rank == local, sel_gate,
                                                    0.0).astype(BF16)
        per_group = GROUP_ROWS // KC
        if cb % per_group == per_group - 1:
            fetch_group(nxt, 1 - slot, cb // per_group)

    wait_block(slot)
    rows = buf_ref[slot].reshape(BLOCK_ROWS, buf_ref.shape[3])
    y = y + jnp.dot(p_ref[...], rows, preferred_element_type=F32)

    @pl.when(j == last)
    def _():
        wait_block(1 - slot)

    ga2 = mod_ref[5:6, :]
    o_ref[...] = x1_ref[...] + ga2 * _rms(y, gpost_ref[...])


def _combine(plan, ys, tm, h2_flat, ws_gate, ws_up, ws_down, x1_flat, mod3, g_post, s):
    t, d = h2_flat.shape
    f = ws_gate.shape[1]
    blocks_per_seq = s // TB
    full = lambda shape: pl.BlockSpec(shape, lambda j, *_: (0,) * len(shape))
    grid_spec = pltpu.PrefetchScalarGridSpec(
        num_scalar_prefetch=1,
        grid=(t // TB,),
        in_specs=[pl.BlockSpec(memory_space=pl.ANY),
                  pl.BlockSpec((TB, 2 * N_EXPERTS), lambda j, *_: (j, 0))] + _bounds_specs() + [
                  pl.BlockSpec((TB, d), lambda j, *_: (j, 0)),
                  full((d, f)), full((d, f)), full((f, d)),
                  pl.BlockSpec((TB, d), lambda j, *_: (j, 0)),
                  pl.BlockSpec((None, 6, d), lambda j, *_: (j // blocks_per_seq, 0, 0)),
                  full((1, d))],
        out_specs=pl.BlockSpec((TB, d), lambda j, *_: (j, 0)),
        scratch_shapes=[pltpu.VMEM((2, BLOCK_CHUNKS, ROW_CHUNK, d), BF16),
                        pltpu.VMEM((TB, BLOCK_ROWS), BF16),
                        pltpu.SemaphoreType.DMA((2,))],
    )
    return pl.pallas_call(
        _combine_kernel,
        grid_spec=grid_spec,
        out_shape=jax.ShapeDtypeStruct((t, d), F32),
        compiler_params=pltpu.CompilerParams(
            dimension_semantics=("arbitrary",), vmem_limit_bytes=VMEM_LIMIT),
        name="moe_combine",
    )(plan["gsrc"], ys.reshape(-1, ROW_CHUNK, d), tm, plan["bounds_row"],
      plan["bounds_col"], h2_flat, ws_gate, ws_up, ws_down, x1_flat, mod3, g_post)


def kernel(x, c, w_ada, b_ada, g_pre_mix, w_in, ln_sgu_g, ln_sgu_b, w_spatial, b_spatial,
           g_branch, w_out, g_post_mix, g_pre_ffn, w_router, router_bias, w_gate, w_up, w_down,
           ws_gate, ws_up, ws_down, g_post_ffn):
    bsz, s, d = x.shape
    t = bsz * s
    nblk = t // TB
    max_rows = t * TOP_K + nblk * N_EXPERTS * (ROW_CHUNK - 1) + N_EXPERTS * (TM_FFN - 1)
    n_tiles = -(-max_rows // TM_FFN)
    row = lambda a: a.reshape(1, -1)
    for l in range(w_ada.shape[0]):
        mod3 = _ada(c, w_ada[l], b_ada[l]).reshape(bsz, 6, d)
        og, q, k, v = _premix(x, mod3, row(g_pre_mix[l]), w_in[l].astype(BF16),
                              row(ln_sgu_g[l]), row(ln_sgu_b[l]), w_spatial[l],
                              b_spatial[l].T, row(g_branch[l, :D_GMLP]))
        osb = _attention(q, k, v)
        x1, h2, logits_t = _postmix(og, osb, x, mod3, row(g_branch[l, D_GMLP:]),
                                    w_out[l].astype(BF16), row(g_post_mix[l]),
                                    row(g_pre_ffn[l]), w_router[l].T)
        em, tm, cnt = _route(logits_t, router_bias[l])
        plan = _dispatch_plan(cnt[:, :, 0].astype(I32), n_tiles)
        h2_flat = h2.reshape(t, d)
        xs = _dispatch(plan, h2_flat, em, n_tiles * TM_FFN + nblk * GROUP_ROWS)
        ys = _expert_ffn(plan["tile_start"], plan["nact"], xs, n_tiles,
                         w_gate[l], w_up[l], w_down[l])
        out = _combine(plan, ys, tm, h2_flat, ws_gate[l].astype(BF16), ws_up[l].astype(BF16),
                       ws_down[l].astype(BF16), x1.reshape(t, d), mod3, row(g_post_ffn[l]), s)
        x = out.reshape(bsz, s, d)
    return x
```

```python
import jax
import jax.numpy as jnp
from jax import lax
from jax.experimental import pallas as pl
from jax.experimental.pallas import tpu as pltpu

F32 = jnp.float32
BF16 = jnp.bfloat16
I32 = jnp.int32

LANES = 128
SUBLANES = 8
V7X_VMEM_BYTES = 64 * 1024 * 1024

D_MODEL = 1024
D_GMLP = 512
GMLP_GROUPS = 4
GMLP_BLOCK = 128
CHUNK = 64
D_SB = 512
SB_HEAD_DIM = 64
N_PAIRS = D_SB // LANES
N_EXPERTS = 64
N_GROUPS = 8
GROUP_SIZE = N_EXPERTS // N_GROUPS
TOPK_GROUPS = 4
TOP_K = 8
D_EXPERT = 256
ROUTED_SCALE = 2.5
EPS = 1e-6
D_IN = 2 * D_GMLP + 3 * D_SB

ADA_TN = 512
TM_MIX = 512
TQ = 128
VMEM_LIMIT = V7X_VMEM_BYTES * 7 // 8
ATTN_DEAD_LOG = -110.0

TB = 256
ROW_CHUNK = 16
KC = 256
GROUP_CHUNKS = 32
GROUP_ROWS = GROUP_CHUNKS * ROW_CHUNK
BLOCK_ROWS = -(-(TB * TOP_K + N_EXPERTS * (ROW_CHUNK - 1)) // GROUP_ROWS) * GROUP_ROWS
BLOCK_CHUNKS = BLOCK_ROWS // ROW_CHUNK
TM_FFN = 512
FFN_AHEAD = 3
FFN_OUT = 3
ZERO_ROWS = 128


def _rms(x, g):
    return x * lax.rsqrt(jnp.mean(x * x, axis=-1, keepdims=True) + EPS) * g


def _silu(x):
    return x * jax.nn.sigmoid(x)


def _ada_kernel(c_ref, w_ref, b_ref, o_ref):
    o_ref[...] = jnp.dot(_silu(c_ref[...]), w_ref[...], preferred_element_type=F32,
                         precision=lax.Precision.HIGHEST) + b_ref[...]


def _ada(c, w, b):
    bsz, d = c.shape
    n = w.shape[1]
    tn = ADA_TN
    return pl.pallas_call(
        _ada_kernel,
        grid=(n // tn,),
        in_specs=[pl.BlockSpec((bsz, d), lambda j: (0, 0)),
                  pl.BlockSpec((d, tn), lambda j: (0, j)),
                  pl.BlockSpec((1, tn), lambda j: (0, j))],
        out_specs=pl.BlockSpec((bsz, tn), lambda j: (0, j)),
        out_shape=jax.ShapeDtypeStruct((bsz, n), F32),
        name="ada_mod",
    )(c, w, b.reshape(1, n))


def _premix_kernel(x_ref, mod_ref, gpre_ref, win_ref, lng_ref, lnb_ref, wsp_ref, bsp_ref,
                   gbr_ref, og_ref, q_ref, k_ref, v_ref):
    x = x_ref[...]
    sh1 = mod_ref[0:1, :]
    sc1 = mod_ref[1:2, :]
    h = _rms(x, gpre_ref[...]) * (1.0 + sc1) + sh1
    proj = jnp.dot(h.astype(BF16), win_ref[...], preferred_element_type=F32)

    u = jax.nn.gelu(proj[:, :D_GMLP])
    v = jax.nn.gelu(proj[:, D_GMLP:2 * D_GMLP])
    mu = jnp.mean(v, axis=-1, keepdims=True)
    var = jnp.mean(jnp.square(v - mu), axis=-1, keepdims=True)
    v = ((v - mu) * lax.rsqrt(var + EPS) * lng_ref[...] + lnb_ref[...]).astype(BF16)

    i = lax.broadcasted_iota(I32, (GMLP_BLOCK, GMLP_BLOCK), 0)
    j = lax.broadcasted_iota(I32, (GMLP_BLOCK, GMLP_BLOCK), 1)
    causal = (j // CHUNK) <= (i // CHUNK)
    gd = D_GMLP // GMLP_GROUPS
    blocks = []
    for nb in range(x.shape[0] // GMLP_BLOCK):
        rows = slice(nb * GMLP_BLOCK, (nb + 1) * GMLP_BLOCK)
        cols = []
        for g in range(GMLP_GROUPS):
            w = jnp.where(causal, wsp_ref[g], 0.0).astype(BF16)
            mixed = jnp.dot(w, v[rows, g * gd:(g + 1) * gd], preferred_element_type=F32)
            cols.append(mixed + bsp_ref[:, g:g + 1])
        blocks.append(u[rows, :] * jnp.concatenate(cols, axis=1))
    og = jnp.concatenate(blocks, axis=0)
    og_ref[...] = _rms(og, gbr_ref[...]).astype(BF16)

    base = 2 * D_GMLP
    scale = SB_HEAD_DIM ** -0.5
    for p in range(N_PAIRS):
        q_ref[p] = (proj[:, base + LANES * p:base + LANES * (p + 1)] * scale).astype(BF16)
        k_ref[p] = proj[:, base + D_SB + LANES * p:base + D_SB + LANES * (p + 1)].astype(BF16)
        v_ref[p] = proj[:, base + 2 * D_SB + LANES * p:base + 2 * D_SB + LANES * (p + 1)].astype(BF16)


def _premix(x, mod3, g_pre, w_in, ln_g, ln_b, w_sp, b_sp_t, g_br):
    bsz, s, d = x.shape
    tm = TM_MIX
    full = lambda shape: pl.BlockSpec(shape, lambda b, i: (0,) * len(shape))
    qkv_spec = pl.BlockSpec((None, N_PAIRS, tm, LANES), lambda b, i: (b, 0, i, 0))
    qkv_shape = jax.ShapeDtypeStruct((bsz, N_PAIRS, s, LANES), BF16)
    return pl.pallas_call(
        _premix_kernel,
        grid=(bsz, s // tm),
        in_specs=[pl.BlockSpec((None, tm, d), lambda b, i: (b, i, 0)),
                  pl.BlockSpec((None, 6, d), lambda b, i: (b, 0, 0)),
                  full((1, d)), full((d, D_IN)), full((1, D_GMLP)), full((1, D_GMLP)),
                  full((GMLP_GROUPS, GMLP_BLOCK, GMLP_BLOCK)), full((GMLP_BLOCK, GMLP_GROUPS)),
                  full((1, D_GMLP))],
        out_specs=[pl.BlockSpec((None, tm, D_GMLP), lambda b, i: (b, i, 0)),
                   qkv_spec, qkv_spec, qkv_spec],
        out_shape=[jax.ShapeDtypeStruct((bsz, s, D_GMLP), BF16), qkv_shape, qkv_shape, qkv_shape],
        compiler_params=pltpu.CompilerParams(
            dimension_semantics=("parallel", "parallel"), vmem_limit_bytes=VMEM_LIMIT),
        name="premix",
    )(x, mod3, g_pre, w_in, ln_g, ln_b, w_sp, b_sp_t, g_br)


def _attn_kernel(q_ref, k_ref, v_ref, o_ref, *scratch):
    qs_refs = scratch[:N_PAIRS]
    acc_refs = scratch[N_PAIRS:2 * N_PAIRS]
    carry_refs = scratch[2 * N_PAIRS:]
    qi = pl.program_id(1)
    first_head = lax.broadcasted_iota(I32, (TQ, LANES), 1) < SB_HEAD_DIM
    for p in range(N_PAIRS):
        q2 = q_ref[p]
        zero = jnp.zeros_like(q2)
        qs_refs[p][:TQ, :] = jnp.where(first_head, q2, zero)
        qs_refs[p][TQ:, :] = jnp.where(first_head, zero, q2)
        acc_refs[p][...] = jnp.zeros_like(acc_refs[p])
        carry_refs[p][...] = jnp.zeros_like(carry_refs[p])

    r = jnp.bitwise_and(lax.broadcasted_iota(I32, (2 * TQ, TQ), 0), TQ - 1)
    c = lax.broadcasted_iota(I32, (2 * TQ, TQ), 1)
    strict_causal = c < r
    kr = lax.broadcasted_iota(I32, (TQ, TQ), 0)
    kc = lax.broadcasted_iota(I32, (TQ, TQ), 1)
    suffix = jnp.concatenate([(kr > kc).astype(BF16), jnp.ones((TQ, TQ), BF16)], axis=1)
    suffix2 = jnp.concatenate([suffix, suffix], axis=0)

    def key_block(j, diagonal):
        start = pl.multiple_of(j * TQ, TQ)
        pairs = range(N_PAIRS)
        zs = [lax.dot_general(qs_refs[p][...], k_ref[p, pl.ds(start, TQ), :],
                              (((1,), (1,)), ((), ())), preferred_element_type=F32)
              for p in pairs]
        log_betas, splits = [], []
        for p in pairs:
            z = zs[p]
            log_beta = jnp.minimum(z, 0.0) - jnp.log(1.0 + jnp.exp(-jnp.abs(z)))
            log_1mb = log_beta - z
            if diagonal:
                log_1mb = jnp.where(strict_causal, log_1mb, 0.0)
            hi = log_1mb.astype(BF16)
            lo = (log_1mb - hi.astype(F32)).astype(BF16)
            log_betas.append(log_beta)
            splits.append(jnp.concatenate([hi, lo], axis=1))
        sums = [jnp.dot(splits[p], suffix2, preferred_element_type=F32) for p in pairs]
        weights = []
        live = None
        for p in pairs:
            s = sums[p]
            carry = carry_refs[p][...]
            a = jnp.exp(log_betas[p] + carry + s[:, :TQ])
            if diagonal:
                a = jnp.where(strict_causal, a, 0.0)
            weights.append(a.astype(BF16))
            carry = carry + s[:, TQ:]
            carry_refs[p][...] = carry
            live = carry if live is None else jnp.maximum(live, carry)
        for p in pairs:
            acc_refs[p][...] += jnp.dot(weights[p], v_ref[p, pl.ds(start, TQ), :],
                                        preferred_element_type=F32)
        return jnp.max(live)

    live = key_block(qi, True)

    def cond(state):
        j, live = state
        return jnp.logical_and(j >= 0, live > ATTN_DEAD_LOG)

    def body(state):
        j, _ = state
        return j - 1, key_block(j, False)

    lax.while_loop(cond, body, (qi - 1, live))
    for p in range(N_PAIRS):
        o_ref[:, LANES * p:LANES * (p + 1)] = jnp.where(first_head, acc_refs[p][:TQ, :],
                                                    acc_refs[p][TQ:, :])


def _attention(q, k, v):
    bsz, npair, s, _ = q.shape
    kv_spec = pl.BlockSpec((None, npair, s, LANES), lambda b, i: (b, 0, 0, 0))
    return pl.pallas_call(
        _attn_kernel,
        grid=(bsz, s // TQ),
        in_specs=[pl.BlockSpec((None, npair, TQ, LANES), lambda b, i: (b, 0, i, 0)),
                  kv_spec, kv_spec],
        out_specs=pl.BlockSpec((None, TQ, npair * LANES), lambda b, i: (b, i, 0)),
        out_shape=jax.ShapeDtypeStruct((bsz, s, npair * LANES), F32),
        scratch_shapes=([pltpu.VMEM((2 * TQ, LANES), BF16)] * npair
                        + [pltpu.VMEM((2 * TQ, LANES), F32)] * npair
                        + [pltpu.VMEM((2 * TQ, TQ), F32)] * npair),
        compiler_params=pltpu.CompilerParams(
            dimension_semantics=("parallel", "parallel"), vmem_limit_bytes=VMEM_LIMIT),
        name="stickbreak_attn",
    )(q, k, v)


def _postmix_kernel(og_ref, osb_ref, x_ref, mod_ref, gbr_ref, wout_ref, gpost_ref, gpre_ref,
                    wrt_ref, x1_ref, h2_ref, logit_ref):
    ga1 = mod_ref[2:3, :]
    sh2 = mod_ref[3:4, :]
    sc2 = mod_ref[4:5, :]
    osb = _rms(osb_ref[...], gbr_ref[...]).astype(BF16)
    m = (jnp.dot(og_ref[...], wout_ref[:D_GMLP, :], preferred_element_type=F32)
         + jnp.dot(osb, wout_ref[D_GMLP:, :], preferred_element_type=F32))
    x1 = x_ref[...] + ga1 * _rms(m, gpost_ref[...])
    x1_ref[...] = x1
    h2 = _rms(x1, gpre_ref[...]) * (1.0 + sc2) + sh2
    h_hi = h2.astype(BF16)
    h2_ref[...] = h_hi
    h_lo = (h2 - h_hi.astype(F32)).astype(BF16)
    w = wrt_ref[...]
    w_hi = w.astype(BF16)
    w_lo = (w - w_hi.astype(F32)).astype(BF16)
    nt = (((1,), (1,)), ((), ()))
    by_hi = lax.dot_general(jnp.concatenate([w_hi, w_lo], axis=0), h_hi, nt,
                            preferred_element_type=F32)
    by_lo = lax.dot_general(w_hi, h_lo, nt, preferred_element_type=F32)
    logit_ref[...] = by_hi[:N_EXPERTS] + by_hi[N_EXPERTS:] + by_lo


def _postmix(og, osb, x, mod3, g_br, w_out, g_post, g_pre, w_router_t):
    bsz, s, d = x.shape
    tm = TM_MIX
    nt = s // tm
    full = lambda shape: pl.BlockSpec(shape, lambda b, i: (0,) * len(shape))
    return pl.pallas_call(
        _postmix_kernel,
        grid=(bsz, nt),
        in_specs=[pl.BlockSpec((None, tm, D_GMLP), lambda b, i: (b, i, 0)),
                  pl.BlockSpec((None, tm, D_SB), lambda b, i: (b, i, 0)),
                  pl.BlockSpec((None, tm, d), lambda b, i: (b, i, 0)),
                  pl.BlockSpec((None, 6, d), lambda b, i: (b, 0, 0)),
                  full((1, D_SB)), full((d, d)), full((1, d)), full((1, d)),
                  full((N_EXPERTS, d))],
        out_specs=[pl.BlockSpec((None, tm, d), lambda b, i: (b, i, 0)),
                   pl.BlockSpec((None, tm, d), lambda b, i: (b, i, 0)),
                   pl.BlockSpec((N_EXPERTS, tm), lambda b, i: (0, b * nt + i))],
        out_shape=[jax.ShapeDtypeStruct((bsz, s, d), F32),
                   jax.ShapeDtypeStruct((bsz, s, d), BF16),
                   jax.ShapeDtypeStruct((N_EXPERTS, bsz * s), F32)],
        compiler_params=pltpu.CompilerParams(
            dimension_semantics=("parallel", "parallel"), vmem_limit_bytes=VMEM_LIMIT),
        name="postmix",
    )(og, osb, x, mod3, g_br, w_out, g_post, g_pre, w_router_t)


def _first_index_of_max(x, idx, axis, size):
    m = jnp.max(x, axis=axis, keepdims=True)
    return jnp.min(jnp.where(x == m, idx, size), axis=axis, keepdims=True)


def _route_kernel(logit_ref, bias_ref, em_ref, tm_ref, cnt_ref):
    scores = jax.nn.sigmoid(logit_ref[...])
    biased = scores + bias_ref[...]
    neg = jnp.float32(-jnp.inf)

    grouped = biased.reshape(N_GROUPS, GROUP_SIZE, TB)
    within = lax.broadcasted_iota(I32, grouped.shape, 1)
    top1 = jnp.max(grouped, axis=1, keepdims=True)
    first = _first_index_of_max(grouped, within, 1, GROUP_SIZE)
    top2 = jnp.max(jnp.where(within == first, neg, grouped), axis=1, keepdims=True)
    group_score = (top1 + top2).reshape(N_GROUPS, TB)

    gidx = lax.broadcasted_iota(I32, group_score.shape, 0)
    group_on = jnp.zeros(group_score.shape, jnp.bool_)
    for _ in range(TOPK_GROUPS):
        pick = gidx == _first_index_of_max(group_score, gidx, 0, N_GROUPS)
        group_on = jnp.logical_or(group_on, pick)
        group_score = jnp.where(pick, neg, group_score)

    masked = jnp.where(group_on.reshape(N_GROUPS, 1, TB), grouped, neg).reshape(N_EXPERTS, TB)
    eidx = lax.broadcasted_iota(I32, masked.shape, 0)
    chosen = jnp.zeros(masked.shape, jnp.bool_)
    for _ in range(TOP_K):
        pick = eidx == _first_index_of_max(masked, eidx, 0, N_EXPERTS)
        chosen = jnp.logical_or(chosen, pick)
        masked = jnp.where(pick, neg, masked)

    w = jnp.where(chosen, scores, 0.0)
    gates = w / jnp.sum(w, axis=0, keepdims=True) * ROUTED_SCALE

    chosen_f = chosen.astype(F32)
    tr = lax.broadcasted_iota(I32, (TB, TB), 0)
    tc = lax.broadcasted_iota(I32, (TB, TB), 1)
    rank = jnp.dot(chosen_f.astype(BF16), (tr < tc).astype(BF16), preferred_element_type=F32)
    both = jnp.concatenate([jnp.where(chosen, rank, -1.0), gates], axis=0)
    em_ref[...] = both
    tm_ref[...] = both.T
    cnt_ref[...] = jnp.broadcast_to(jnp.sum(chosen_f, axis=1, keepdims=True), (N_EXPERTS, LANES))


def _route(logits_t, bias):
    e, t = logits_t.shape
    nblk = t // TB
    return pl.pallas_call(
        _route_kernel,
        grid=(nblk,),
        in_specs=[pl.BlockSpec((e, TB), lambda i: (0, i)),
                  pl.BlockSpec((e, 1), lambda i: (0, 0))],
        out_specs=[pl.BlockSpec((2 * e, TB), lambda i: (0, i)),
                   pl.BlockSpec((TB, 2 * e), lambda i: (i, 0)),
                   pl.BlockSpec((None, e, LANES), lambda i: (i, 0, 0))],
        out_shape=[jax.ShapeDtypeStruct((2 * e, t), F32),
                   jax.ShapeDtypeStruct((t, 2 * e), F32),
                   jax.ShapeDtypeStruct((nblk, e, LANES), F32)],
        compiler_params=pltpu.CompilerParams(dimension_semantics=("parallel",)),
        name="route",
    )(logits_t, bias.reshape(e, 1))


def _dispatch_plan(cnt, n_tiles):
    nblk, e = cnt.shape
    pc = (cnt + ROW_CHUNK - 1) // ROW_CHUNK * ROW_CHUNK
    start = jnp.cumsum(pc, axis=1) - pc
    nchunk = jnp.sum(pc, axis=1) // ROW_CHUNK
    off = jnp.cumsum(pc, axis=0) - pc
    ecount = jnp.sum(pc, axis=0)
    epad = (ecount + TM_FFN - 1) // TM_FFN * TM_FFN
    gend = jnp.cumsum(epad)
    gbase = gend - epad
    nact = gend[-1] // TM_FFN
    tile_start = jnp.concatenate([gbase, gend[-1:]]) // TM_FFN
    cidx = jnp.arange(BLOCK_CHUNKS, dtype=I32)
    start16 = start // ROW_CHUNK
    shift = (gbase[None, :] + off) // ROW_CHUNK - start16
    dshift = shift - jnp.pad(shift, ((0, 0), (1, 0)))[:, :-1]
    in_or_after = (start16[:, None, :] <= cidx[None, :, None]).astype(I32)
    where = cidx[None, :] + jnp.sum(in_or_after * dshift[:, None, :], axis=2)
    used = cidx[None, :] < nchunk[:, None]
    spare = (n_tiles * TM_FFN // ROW_CHUNK
             + jnp.arange(nblk, dtype=I32)[:, None] * GROUP_CHUNKS + cidx[None, :] % GROUP_CHUNKS)
    gdst = jnp.where(used, where, spare)
    gsrc = jnp.where(used, where, 0)
    ngrp = (nchunk + GROUP_CHUNKS - 1) // GROUP_CHUNKS
    zbase = (gbase + ecount) // ROW_CHUNK
    zn = (epad - ecount) // ROW_CHUNK

    lo = jnp.tile(start.astype(F32), (1, 2))
    hi = jnp.tile((start + pc).astype(F32), (1, 2))
    bounds_row = jnp.stack([jnp.broadcast_to(lo[:, None, :], (nblk, SUBLANES, 2 * e)),
                            jnp.broadcast_to(hi[:, None, :], (nblk, SUBLANES, 2 * e))], axis=1)
    bounds_col = jnp.stack([jnp.broadcast_to(lo[:, :, None], (nblk, 2 * e, LANES)),
                            jnp.broadcast_to(hi[:, :, None], (nblk, 2 * e, LANES))], axis=1)
    as_i32 = lambda a: a.astype(I32)
    return dict(ngrp=as_i32(ngrp), gdst=as_i32(gdst.reshape(-1)), gsrc=as_i32(gsrc.reshape(-1)),
                zn=as_i32(zn), zbase=as_i32(zbase), tile_start=as_i32(tile_start),
                nact=as_i32(nact.reshape(1)), bounds_row=bounds_row, bounds_col=bounds_col)


def _dispatch_kernel(ngrp_ref, gdst_ref, zn_ref, zbase_ref, h_ref, em_ref, brow_ref,
                     xs_ref, buf_ref, zero_ref, sem, zsem):
    j = pl.program_id(0)
    last = pl.num_programs(0) - 1
    slot = j % 2

    def chunk_copy(s, c, g):
        return pltpu.make_async_copy(buf_ref.at[s, c], xs_ref.at[g], sem.at[s])

    def wait_block(jj, s):
        def body(i, carry):
            pltpu.make_async_copy(buf_ref.at[s, pl.ds(0, GROUP_CHUNKS)],
                                  xs_ref.at[pl.ds(0, GROUP_CHUNKS)], sem.at[s]).wait()
            return carry
        lax.fori_loop(0, ngrp_ref[jj], body, 0)

    def zero_copy(g, chunks):
        return pltpu.make_async_copy(zero_ref.at[pl.ds(0, chunks)], xs_ref.at[pl.ds(g, chunks)],
                                     zsem.at[0])

    def zero_fill(start):
        big = ZERO_ROWS // ROW_CHUNK

        def per_expert(e, carry):
            n = zn_ref[e]
            base = zbase_ref[e]

            def big_copy(i, c2):
                cp = zero_copy(base + i * big, big)
                cp.start() if start else cp.wait()
                return c2
            lax.fori_loop(0, n // big, big_copy, 0)

            def small_copy(i, c2):
                cp = zero_copy(base + n // big * big + i, 1)
                cp.start() if start else cp.wait()
                return c2
            lax.fori_loop(0, n % big, small_copy, 0)
            return carry
        lax.fori_loop(0, N_EXPERTS, per_expert, 0)

    @pl.when(j == 0)
    def _():
        zero_ref[...] = jnp.zeros_like(zero_ref)
        zero_fill(True)

    @pl.when(j >= 2)
    def _():
        wait_block(j - 2, slot)

    lo = brow_ref[0, 0:1, :]
    hi = brow_ref[1, 0:1, :]
    first_copy = lax.broadcasted_iota(I32, (KC, 2 * N_EXPERTS), 1) < N_EXPERTS
    row_iota = lax.broadcasted_iota(I32, (KC, 2 * N_EXPERTS), 0)
    ranks = em_ref[...].astype(BF16)
    h = h_ref[...]

    def trip(ti, carry):
        onehots = []
        for u in range(GROUP_ROWS // KC):
            rows_e = (row_iota + ti * GROUP_ROWS + u * KC).astype(F32)
            in_run = jnp.logical_and(jnp.logical_and(rows_e >= lo, rows_e < hi), first_copy)
            sel = jnp.dot(jnp.where(in_run, 1.0, 0.0).astype(BF16), ranks,
                          preferred_element_type=F32)
            local = jnp.sum(jnp.where(in_run, rows_e - lo + 3.0, 0.0), axis=1,
                            keepdims=True) - 3.0
            onehots.append(jnp.where(sel == local, 1.0, 0.0).astype(BF16))
        xs = jnp.dot(jnp.concatenate(onehots, axis=0), h, preferred_element_type=F32)
        first = pl.multiple_of(ti * GROUP_CHUNKS, GROUP_CHUNKS)
        buf_ref[slot, pl.ds(first, GROUP_CHUNKS)] = xs.astype(BF16).reshape(
            GROUP_CHUNKS, ROW_CHUNK, xs.shape[1])
        for k in range(GROUP_CHUNKS):
            chunk_copy(slot, first + k, gdst_ref[j * BLOCK_CHUNKS + first + k]).start()
        return carry
    lax.fori_loop(0, ngrp_ref[j], trip, 0)

    @pl.when(j == last)
    def _():
        wait_block(j, slot)

        @pl.when(j >= 1)
        def _():
            wait_block(j - 1, 1 - slot)

        zero_fill(False)


def _bounds_specs():
    return [pl.BlockSpec((None, 2, SUBLANES, 2 * N_EXPERTS), lambda j, *_: (j, 0, 0, 0)),
            pl.BlockSpec((None, 2, 2 * N_EXPERTS, LANES), lambda j, *_: (j, 0, 0, 0))]


def _dispatch(plan, h2_flat, em, n_rows):
    t, d = h2_flat.shape
    grid_spec = pltpu.PrefetchScalarGridSpec(
        num_scalar_prefetch=4,
        grid=(t // TB,),
        in_specs=[pl.BlockSpec((TB, d), lambda j, *_: (j, 0)),
                  pl.BlockSpec((2 * N_EXPERTS, TB), lambda j, *_: (0, j))] + _bounds_specs()[:1],
        out_specs=pl.BlockSpec(memory_space=pl.ANY),
        scratch_shapes=[pltpu.VMEM((2, BLOCK_CHUNKS, ROW_CHUNK, d), BF16),
                        pltpu.VMEM((ZERO_ROWS // ROW_CHUNK, ROW_CHUNK, d), BF16),
                        pltpu.SemaphoreType.DMA((2,)),
                        pltpu.SemaphoreType.DMA((1,))],
    )
    xs = pl.pallas_call(
        _dispatch_kernel,
        grid_spec=grid_spec,
        out_shape=jax.ShapeDtypeStruct((n_rows // ROW_CHUNK, ROW_CHUNK, d), BF16),
        compiler_params=pltpu.CompilerParams(
            dimension_semantics=("arbitrary",), vmem_limit_bytes=VMEM_LIMIT),
        name="moe_dispatch",
    )(plan["ngrp"], plan["gdst"], plan["zn"], plan["zbase"], h2_flat, em, plan["bounds_row"])
    return xs.reshape(n_rows, d)


def _ffn_kernel(tstart_ref, nact_ref, wg_ref, wu_ref, wd_ref, xs_ref, ys_ref,
                xbuf, ybuf, wgu_s, wd_s, xsem, ysem):
    e = pl.program_id(0)
    nact = nact_ref[0]
    t0 = tstart_ref[e]
    t1 = tstart_ref[e + 1]

    def x_copy(t):
        s = t % (FFN_AHEAD + 1)
        return pltpu.make_async_copy(
            xs_ref.at[pl.ds(pl.multiple_of(t * TM_FFN, TM_FFN), TM_FFN), :], xbuf.at[s],
            xsem.at[s])

    def y_copy(t):
        s = t % FFN_OUT
        return pltpu.make_async_copy(
            ybuf.at[s], ys_ref.at[pl.ds(pl.multiple_of(t * TM_FFN, TM_FFN), TM_FFN), :],
            ysem.at[s])

    @pl.when(e == 0)
    def _():
        for a in range(FFN_AHEAD):
            @pl.when(a < nact)
            def _():
                x_copy(a).start()

    @pl.when(t1 > t0)
    def _():
        wgu_s[:, :D_EXPERT] = wg_ref[...].astype(BF16)
        wgu_s[:, D_EXPERT:] = wu_ref[...].astype(BF16)
        wd_s[...] = wd_ref[...].astype(BF16)

        def tile(t, carry):
            x_copy(t).wait()

            @pl.when(t + FFN_AHEAD < nact)
            def _():
                x_copy(t + FFN_AHEAD).start()

            gu = jnp.dot(xbuf[t % (FFN_AHEAD + 1)], wgu_s[...], preferred_element_type=F32)
            act = _silu(gu[:, :D_EXPERT]) * gu[:, D_EXPERT:]
            y = jnp.dot(act.astype(BF16), wd_s[...], preferred_element_type=F32)

            @pl.when(t >= FFN_OUT)
            def _():
                y_copy(t - FFN_OUT).wait()

            ybuf[t % FFN_OUT] = y.astype(BF16)
            y_copy(t).start()
            return carry
        lax.fori_loop(t0, t1, tile, 0)

    @pl.when(e == pl.num_programs(0) - 1)
    def _():
        for back in range(FFN_OUT, 0, -1):
            @pl.when(nact >= back)
            def _():
                y_copy(nact - back).wait()


def _expert_ffn(tile_start, nact, xs, n_tiles, w_gate, w_up, w_down):
    n_rows, d = n_tiles * TM_FFN, xs.shape[1]
    f = D_EXPERT
    grid_spec = pltpu.PrefetchScalarGridSpec(
        num_scalar_prefetch=2,
        grid=(N_EXPERTS,),
        in_specs=[pl.BlockSpec((None, d, f), lambda e, ts, na: (e, 0, 0)),
                  pl.BlockSpec((None, d, f), lambda e, ts, na: (e, 0, 0)),
                  pl.BlockSpec((None, f, d), lambda e, ts, na: (e, 0, 0)),
                  pl.BlockSpec(memory_space=pl.ANY)],
        out_specs=pl.BlockSpec(memory_space=pl.ANY),
        scratch_shapes=[pltpu.VMEM((FFN_AHEAD + 1, TM_FFN, d), BF16),
                        pltpu.VMEM((FFN_OUT, TM_FFN, d), BF16),
                        pltpu.VMEM((d, 2 * f), BF16), pltpu.VMEM((f, d), BF16),
                        pltpu.SemaphoreType.DMA((FFN_AHEAD + 1,)),
                        pltpu.SemaphoreType.DMA((FFN_OUT,))],
    )
    return pl.pallas_call(
        _ffn_kernel,
        grid_spec=grid_spec,
        out_shape=jax.ShapeDtypeStruct((n_rows, d), BF16),
        compiler_params=pltpu.CompilerParams(
            dimension_semantics=("arbitrary",), vmem_limit_bytes=VMEM_LIMIT),
        name="moe_experts",
    )(tile_start, nact, w_gate, w_up, w_down, xs)


def _combine_kernel(ngrp_ref, gsrc_ref, ys_ref, tm_ref, brow_ref, bcol_ref, h_ref, wsg_ref,
                    wsu_ref, wsd_ref, x1_ref, mod_ref, gpost_ref, o_ref, buf_ref, p_ref, sem):
    j = pl.program_id(0)
    last = pl.num_programs(0) - 1
    slot = j % 2

    def chunk_copy(s, c, g):
        return pltpu.make_async_copy(ys_ref.at[g], buf_ref.at[s, c], sem.at[s])

    def fetch_block(jj, s):
        def issue_group(gi, carry):
            for k in range(GROUP_CHUNKS):
                c = gi * GROUP_CHUNKS + k
                chunk_copy(s, c, gsrc_ref[jj * BLOCK_CHUNKS + c]).start()
            return carry
        lax.fori_loop(0, ngrp_ref[jj], issue_group, 0)

    @pl.when(j == 0)
    def _():
        buf_ref[...] = jnp.zeros_like(buf_ref)
        fetch_block(0, 0)

    @pl.when(j < last)
    def _():
        fetch_block(j + 1, 1 - slot)

    h = h_ref[...]
    act = _silu(jnp.dot(h, wsg_ref[...], preferred_element_type=F32)) * jnp.dot(
        h, wsu_ref[...], preferred_element_type=F32)
    y = jnp.dot(act.astype(BF16), wsd_ref[...], preferred_element_type=F32)

    tm = tm_ref[...]
    lane = lax.broadcasted_iota(I32, tm.shape, 1)
    rank_t = jnp.where(lane < N_EXPERTS, tm, 0.0).astype(BF16)
    gate_t = jnp.where(lane < N_EXPERTS, 0.0, tm).astype(BF16)
    lane8 = lax.broadcasted_iota(I32, (SUBLANES, 2 * N_EXPERTS), 1)
    lo_row = jnp.where(lane8 < N_EXPERTS, brow_ref[0], 0.0).astype(BF16)
    lo_col = jnp.concatenate([bcol_ref[0]] * (KC // LANES), axis=1)
    hi_col = jnp.concatenate([bcol_ref[1]] * (KC // LANES), axis=1)
    for cb in range(BLOCK_ROWS // KC):
        rows = (lax.broadcasted_iota(I32, (2 * N_EXPERTS, KC), 1) + cb * KC).astype(F32)
        in_run = jnp.where(jnp.logical_and(rows >= lo_col, rows < hi_col), 1.0, 0.0).astype(BF16)
        sel_rank = jnp.dot(rank_t, in_run, preferred_element_type=F32)
        sel_gate = jnp.dot(gate_t, in_run, preferred_element_type=F32)
        sel_lo = jnp.dot(lo_row, in_run, preferred_element_type=F32)
        local = rows[0:1, :] - sel_lo[0:1, :]
        p_ref[:, cb * KC:(cb + 1) * KC] = jnp.where(sel_rank == local, sel_gate,
                                                    0.0).astype(BF16)

    def wait_group(gi, carry):
        pltpu.make_async_copy(ys_ref.at[pl.ds(0, GROUP_CHUNKS)],
                              buf_ref.at[slot, pl.ds(0, GROUP_CHUNKS)], sem.at[slot]).wait()
        return carry
    lax.fori_loop(0, ngrp_ref[j], wait_group, 0)

    rows = buf_ref[slot].reshape(BLOCK_ROWS, buf_ref.shape[3])
    y = y + jnp.dot(p_ref[...], rows, preferred_element_type=F32)

    ga2 = mod_ref[5:6, :]
    o_ref[...] = x1_ref[...] + ga2 * _rms(y, gpost_ref[...])


def _combine(plan, ys, tm, h2_flat, ws_gate, ws_up, ws_down, x1_flat, mod3, g_post, s):
    t, d = h2_flat.shape
    f = ws_gate.shape[1]
    blocks_per_seq = s // TB
    full = lambda shape: pl.BlockSpec(shape, lambda j, *_: (0,) * len(shape))
    grid_spec = pltpu.PrefetchScalarGridSpec(
        num_scalar_prefetch=2,
        grid=(t // TB,),
        in_specs=[pl.BlockSpec(memory_space=pl.ANY),
                  pl.BlockSpec((TB, 2 * N_EXPERTS), lambda j, *_: (j, 0))] + _bounds_specs() + [
                  pl.BlockSpec((TB, d), lambda j, *_: (j, 0)),
                  full((d, f)), full((d, f)), full((f, d)),
                  pl.BlockSpec((TB, d), lambda j, *_: (j, 0)),
                  pl.BlockSpec((None, 6, d), lambda j, *_: (j // blocks_per_seq, 0, 0)),
                  full((1, d))],
        out_specs=pl.BlockSpec((TB, d), lambda j, *_: (j, 0)),
        scratch_shapes=[pltpu.VMEM((2, BLOCK_CHUNKS, ROW_CHUNK, d), BF16),
                        pltpu.VMEM((TB, BLOCK_ROWS), BF16),
                        pltpu.SemaphoreType.DMA((2,))],
    )
    return pl.pallas_call(
        _combine_kernel,
        grid_spec=grid_spec,
        out_shape=jax.ShapeDtypeStruct((t, d), F32),
        compiler_params=pltpu.CompilerParams(
            dimension_semantics=("arbitrary",), vmem_limit_bytes=VMEM_LIMIT),
        name="moe_combine",
    )(plan["ngrp"], plan["gsrc"], ys.reshape(-1, ROW_CHUNK, d), tm, plan["bounds_row"],
      plan["bounds_col"], h2_flat, ws_gate, ws_up, ws_down, x1_flat, mod3, g_post)


def kernel(x, c, w_ada, b_ada, g_pre_mix, w_in, ln_sgu_g, ln_sgu_b, w_spatial, b_spatial,
           g_branch, w_out, g_post_mix, g_pre_ffn, w_router, router_bias, w_gate, w_up, w_down,
           ws_gate, ws_up, ws_down, g_post_ffn):
    bsz, s, d = x.shape
    t = bsz * s
    nblk = t // TB
    max_rows = t * TOP_K + nblk * N_EXPERTS * (ROW_CHUNK - 1) + N_EXPERTS * (TM_FFN - 1)
    n_tiles = -(-max_rows // TM_FFN)
    row = lambda a: a.reshape(1, -1)
    for l in range(w_ada.shape[0]):
        mod3 = _ada(c, w_ada[l], b_ada[l]).reshape(bsz, 6, d)
        og, q, k, v = _premix(x, mod3, row(g_pre_mix[l]), w_in[l].astype(BF16),
                              row(ln_sgu_g[l]), row(ln_sgu_b[l]), w_spatial[l],
                              b_spatial[l].T, row(g_branch[l, :D_GMLP]))
        osb = _attention(q, k, v)
        x1, h2, logits_t = _postmix(og, osb, x, mod3, row(g_branch[l, D_GMLP:]),
                                    w_out[l].astype(BF16), row(g_post_mix[l]),
                                    row(g_pre_ffn[l]), w_router[l].T)
        em, tm, cnt = _route(logits_t, router_bias[l])
        plan = _dispatch_plan(cnt[:, :, 0].astype(I32), n_tiles)
        h2_flat = h2.reshape(t, d)
        xs = _dispatch(plan, h2_flat, em, n_tiles * TM_FFN + nblk * GROUP_ROWS)
        ys = _expert_ffn(plan["tile_start"], plan["nact"], xs, n_tiles,
                         w_gate[l], w_up[l], w_down[l])
        out = _combine(plan, ys, tm, h2_flat, ws_gate[l].astype(BF16), ws_up[l].astype(BF16),
                       ws_down[l].astype(BF16), x1.reshape(t, d), mod3, row(g_post_ffn[l]), s)
        x = out.reshape(bsz, s, d)
    return x
```

```python
import jax
import jax.numpy as jnp
from jax import lax
from jax.experimental import pallas as pl
from jax.experimental.pallas import tpu as pltpu

F32 = jnp.float32
BF16 = jnp.bfloat16
I32 = jnp.int32

LANES = 128
SUBLANES = 8
V7X_VMEM_BYTES = 64 * 1024 * 1024

D_MODEL = 1024
D_GMLP = 512
GMLP_GROUPS = 4
GMLP_BLOCK = 128
CHUNK = 64
D_SB = 512
SB_HEAD_DIM = 64
N_PAIRS = D_SB // LANES
N_EXPERTS = 64
N_GROUPS = 8
GROUP_SIZE = N_EXPERTS // N_GROUPS
TOPK_GROUPS = 4
TOP_K = 8
D_EXPERT = 256
ROUTED_SCALE = 2.5
EPS = 1e-6
D_IN = 2 * D_GMLP + 3 * D_SB

ADA_TN = 1536
TM_MIX = 512
TQ = 128
VMEM_LIMIT = V7X_VMEM_BYTES * 7 // 8
ATTN_DEAD_LOG = -110.0

TB = 256
ROW_CHUNK = 16
KC = 256
GROUP_CHUNKS = 32
GROUP_ROWS = GROUP_CHUNKS * ROW_CHUNK
BLOCK_ROWS = -(-(TB * TOP_K + N_EXPERTS * (ROW_CHUNK - 1)) // GROUP_ROWS) * GROUP_ROWS
BLOCK_CHUNKS = BLOCK_ROWS // ROW_CHUNK
TM_FFN = 1024
FFN_AHEAD = 3
FFN_OUT = 3
ZERO_ROWS = 128


def _rms(x, g):
    return x * lax.rsqrt(jnp.mean(x * x, axis=-1, keepdims=True) + EPS) * g


def _silu(x):
    return x * jax.nn.sigmoid(x)


def _ada_kernel(c_ref, w_ref, b_ref, o_ref):
    o_ref[...] = jnp.dot(_silu(c_ref[...]), w_ref[...], preferred_element_type=F32,
                         precision=lax.Precision.HIGHEST) + b_ref[...]


def _ada(c, w, b):
    bsz, d = c.shape
    n = w.shape[1]
    tn = ADA_TN
    return pl.pallas_call(
        _ada_kernel,
        grid=(n // tn,),
        in_specs=[pl.BlockSpec((bsz, d), lambda j: (0, 0)),
                  pl.BlockSpec((d, tn), lambda j: (0, j)),
                  pl.BlockSpec((1, tn), lambda j: (0, j))],
        out_specs=pl.BlockSpec((bsz, tn), lambda j: (0, j)),
        out_shape=jax.ShapeDtypeStruct((bsz, n), F32),
        name="ada_mod",
    )(c, w, b.reshape(1, n))


def _premix_kernel(x_ref, mod_ref, gpre_ref, win_ref, lng_ref, lnb_ref, wsp_ref, bsp_ref,
                   gbr_ref, og_ref, q_ref, k_ref, v_ref):
    x = x_ref[...]
    sh1 = mod_ref[0:1, :]
    sc1 = mod_ref[1:2, :]
    h = _rms(x, gpre_ref[...]) * (1.0 + sc1) + sh1
    proj = jnp.dot(h.astype(BF16), win_ref[...], preferred_element_type=F32)

    u = jax.nn.gelu(proj[:, :D_GMLP])
    v = jax.nn.gelu(proj[:, D_GMLP:2 * D_GMLP])
    mu = jnp.mean(v, axis=-1, keepdims=True)
    var = jnp.mean(jnp.square(v - mu), axis=-1, keepdims=True)
    v = ((v - mu) * lax.rsqrt(var + EPS) * lng_ref[...] + lnb_ref[...]).astype(BF16)

    i = lax.broadcasted_iota(I32, (GMLP_BLOCK, GMLP_BLOCK), 0)
    j = lax.broadcasted_iota(I32, (GMLP_BLOCK, GMLP_BLOCK), 1)
    causal = (j // CHUNK) <= (i // CHUNK)
    gd = D_GMLP // GMLP_GROUPS
    blocks = []
    for nb in range(x.shape[0] // GMLP_BLOCK):
        rows = slice(nb * GMLP_BLOCK, (nb + 1) * GMLP_BLOCK)
        cols = []
        for g in range(GMLP_GROUPS):
            w = jnp.where(causal, wsp_ref[g], 0.0).astype(BF16)
            mixed = jnp.dot(w, v[rows, g * gd:(g + 1) * gd], preferred_element_type=F32)
            cols.append(mixed + bsp_ref[:, g:g + 1])
        blocks.append(u[rows, :] * jnp.concatenate(cols, axis=1))
    og = jnp.concatenate(blocks, axis=0)
    og_ref[...] = _rms(og, gbr_ref[...]).astype(BF16)

    base = 2 * D_GMLP
    scale = SB_HEAD_DIM ** -0.5
    for p in range(N_PAIRS):
        q_ref[p] = (proj[:, base + LANES * p:base + LANES * (p + 1)] * scale).astype(BF16)
        k_ref[p] = proj[:, base + D_SB + LANES * p:base + D_SB + LANES * (p + 1)].astype(BF16)
        v_ref[p] = proj[:, base + 2 * D_SB + LANES * p:base + 2 * D_SB + LANES * (p + 1)].astype(BF16)


def _premix(x, mod3, g_pre, w_in, ln_g, ln_b, w_sp, b_sp_t, g_br):
    bsz, s, d = x.shape
    tm = TM_MIX
    full = lambda shape: pl.BlockSpec(shape, lambda b, i: (0,) * len(shape))
    qkv_spec = pl.BlockSpec((None, N_PAIRS, tm, LANES), lambda b, i: (b, 0, i, 0))
    qkv_shape = jax.ShapeDtypeStruct((bsz, N_PAIRS, s, LANES), BF16)
    return pl.pallas_call(
        _premix_kernel,
        grid=(bsz, s // tm),
        in_specs=[pl.BlockSpec((None, tm, d), lambda b, i: (b, i, 0)),
                  pl.BlockSpec((None, 6, d), lambda b, i: (b, 0, 0)),
                  full((1, d)), full((d, D_IN)), full((1, D_GMLP)), full((1, D_GMLP)),
                  full((GMLP_GROUPS, GMLP_BLOCK, GMLP_BLOCK)), full((GMLP_BLOCK, GMLP_GROUPS)),
                  full((1, D_GMLP))],
        out_specs=[pl.BlockSpec((None, tm, D_GMLP), lambda b, i: (b, i, 0)),
                   qkv_spec, qkv_spec, qkv_spec],
        out_shape=[jax.ShapeDtypeStruct((bsz, s, D_GMLP), BF16), qkv_shape, qkv_shape, qkv_shape],
        compiler_params=pltpu.CompilerParams(
            dimension_semantics=("parallel", "parallel"), vmem_limit_bytes=VMEM_LIMIT),
        name="premix",
    )(x, mod3, g_pre, w_in, ln_g, ln_b, w_sp, b_sp_t, g_br)


def _attn_kernel(q_ref, k_ref, v_ref, o_ref, *scratch):
    qs_refs = scratch[:N_PAIRS]
    acc_refs = scratch[N_PAIRS:2 * N_PAIRS]
    carry_refs = scratch[2 * N_PAIRS:]
    qi = pl.program_id(1)
    first_head = lax.broadcasted_iota(I32, (TQ, LANES), 1) < SB_HEAD_DIM
    for p in range(N_PAIRS):
        q2 = q_ref[p]
        zero = jnp.zeros_like(q2)
        qs_refs[p][:TQ, :] = jnp.where(first_head, q2, zero)
        qs_refs[p][TQ:, :] = jnp.where(first_head, zero, q2)
        acc_refs[p][...] = jnp.zeros_like(acc_refs[p])
        carry_refs[p][...] = jnp.zeros_like(carry_refs[p])

    r = jnp.bitwise_and(lax.broadcasted_iota(I32, (2 * TQ, TQ), 0), TQ - 1)
    c = lax.broadcasted_iota(I32, (2 * TQ, TQ), 1)
    strict_causal = c < r
    kr = lax.broadcasted_iota(I32, (TQ, TQ), 0)
    kc = lax.broadcasted_iota(I32, (TQ, TQ), 1)
    suffix = jnp.concatenate([(kr > kc).astype(BF16), jnp.ones((TQ, TQ), BF16)], axis=1)
    suffix2 = jnp.concatenate([suffix, suffix], axis=0)

    def key_block(j, diagonal):
        start = pl.multiple_of(j * TQ, TQ)
        pairs = range(N_PAIRS)
        zs = [lax.dot_general(qs_refs[p][...], k_ref[p, pl.ds(start, TQ), :],
                              (((1,), (1,)), ((), ())), preferred_element_type=F32)
              for p in pairs]
        log_betas, splits = [], []
        for p in pairs:
            z = zs[p]
            log_beta = jnp.minimum(z, 0.0) - jnp.log(1.0 + jnp.exp(-jnp.abs(z)))
            log_1mb = log_beta - z
            if diagonal:
                log_1mb = jnp.where(strict_causal, log_1mb, 0.0)
            hi = log_1mb.astype(BF16)
            lo = (log_1mb - hi.astype(F32)).astype(BF16)
            log_betas.append(log_beta)
            splits.append(jnp.concatenate([hi, lo], axis=1))
        sums = [jnp.dot(splits[p], suffix2, preferred_element_type=F32) for p in pairs]
        weights = []
        live = None
        for p in pairs:
            s = sums[p]
            carry = carry_refs[p][...]
            a = jnp.exp(log_betas[p] + carry + s[:, :TQ])
            if diagonal:
                a = jnp.where(strict_causal, a, 0.0)
            weights.append(a.astype(BF16))
            carry = carry + s[:, TQ:]
            carry_refs[p][...] = carry
            live = carry if live is None else jnp.maximum(live, carry)
        for p in pairs:
            acc_refs[p][...] += jnp.dot(weights[p], v_ref[p, pl.ds(start, TQ), :],
                                        preferred_element_type=F32)
        return jnp.max(live)

    live = key_block(qi, True)

    def cond(state):
        j, live = state
        return jnp.logical_and(j >= 0, live > ATTN_DEAD_LOG)

    def body(state):
        j, _ = state
        return j - 1, key_block(j, False)

    lax.while_loop(cond, body, (qi - 1, live))
    for p in range(N_PAIRS):
        o_ref[:, LANES * p:LANES * (p + 1)] = jnp.where(first_head, acc_refs[p][:TQ, :],
                                                    acc_refs[p][TQ:, :])


def _attention(q, k, v):
    bsz, npair, s, _ = q.shape
    kv_spec = pl.BlockSpec((None, npair, s, LANES), lambda b, i: (b, 0, 0, 0))
    return pl.pallas_call(
        _attn_kernel,
        grid=(bsz, s // TQ),
        in_specs=[pl.BlockSpec((None, npair, TQ, LANES), lambda b, i: (b, 0, i, 0)),
                  kv_spec, kv_spec],
        out_specs=pl.BlockSpec((None, TQ, npair * LANES), lambda b, i: (b, i, 0)),
        out_shape=jax.ShapeDtypeStruct((bsz, s, npair * LANES), F32),
        scratch_shapes=([pltpu.VMEM((2 * TQ, LANES), BF16)] * npair
                        + [pltpu.VMEM((2 * TQ, LANES), F32)] * npair
                        + [pltpu.VMEM((2 * TQ, TQ), F32)] * npair),
        compiler_params=pltpu.CompilerParams(
            dimension_semantics=("parallel", "parallel"), vmem_limit_bytes=VMEM_LIMIT),
        name="stickbreak_attn",
    )(q, k, v)


def _postmix_kernel(og_ref, osb_ref, x_ref, mod_ref, gbr_ref, wout_ref, gpost_ref, gpre_ref,
                    wrt_ref, x1_ref, h2_ref, logit_ref):
    ga1 = mod_ref[2:3, :]
    sh2 = mod_ref[3:4, :]
    sc2 = mod_ref[4:5, :]
    osb = _rms(osb_ref[...], gbr_ref[...]).astype(BF16)
    m = (jnp.dot(og_ref[...], wout_ref[:D_GMLP, :], preferred_element_type=F32)
         + jnp.dot(osb, wout_ref[D_GMLP:, :], preferred_element_type=F32))
    x1 = x_ref[...] + ga1 * _rms(m, gpost_ref[...])
    x1_ref[...] = x1
    h2 = _rms(x1, gpre_ref[...]) * (1.0 + sc2) + sh2
    h_hi = h2.astype(BF16)
    h2_ref[...] = h_hi
    h_lo = (h2 - h_hi.astype(F32)).astype(BF16)
    w = wrt_ref[...]
    w_hi = w.astype(BF16)
    w_lo = (w - w_hi.astype(F32)).astype(BF16)
    nt = (((1,), (1,)), ((), ()))
    by_hi = lax.dot_general(jnp.concatenate([w_hi, w_lo], axis=0), h_hi, nt,
                            preferred_element_type=F32)
    by_lo = lax.dot_general(w_hi, h_lo, nt, preferred_element_type=F32)
    logit_ref[...] = by_hi[:N_EXPERTS] + by_hi[N_EXPERTS:] + by_lo


def _postmix(og, osb, x, mod3, g_br, w_out, g_post, g_pre, w_router_t):
    bsz, s, d = x.shape
    tm = TM_MIX
    nt = s // tm
    full = lambda shape: pl.BlockSpec(shape, lambda b, i: (0,) * len(shape))
    return pl.pallas_call(
        _postmix_kernel,
        grid=(bsz, nt),
        in_specs=[pl.BlockSpec((None, tm, D_GMLP), lambda b, i: (b, i, 0)),
                  pl.BlockSpec((None, tm, D_SB), lambda b, i: (b, i, 0)),
                  pl.BlockSpec((None, tm, d), lambda b, i: (b, i, 0)),
                  pl.BlockSpec((None, 6, d), lambda b, i: (b, 0, 0)),
                  full((1, D_SB)), full((d, d)), full((1, d)), full((1, d)),
                  full((N_EXPERTS, d))],
        out_specs=[pl.BlockSpec((None, tm, d), lambda b, i: (b, i, 0)),
                   pl.BlockSpec((None, tm, d), lambda b, i: (b, i, 0)),
                   pl.BlockSpec((N_EXPERTS, tm), lambda b, i: (0, b * nt + i))],
        out_shape=[jax.ShapeDtypeStruct((bsz, s, d), F32),
                   jax.ShapeDtypeStruct((bsz, s, d), BF16),
                   jax.ShapeDtypeStruct((N_EXPERTS, bsz * s), F32)],
        compiler_params=pltpu.CompilerParams(
            dimension_semantics=("parallel", "parallel"), vmem_limit_bytes=VMEM_LIMIT),
        name="postmix",
    )(og, osb, x, mod3, g_br, w_out, g_post, g_pre, w_router_t)


def _first_index_of_max(x, idx, axis, size):
    m = jnp.max(x, axis=axis, keepdims=True)
    return jnp.min(jnp.where(x == m, idx, size), axis=axis, keepdims=True)


def _route_kernel(logit_ref, bias_ref, em_ref, tm_ref, cnt_ref):
    scores = jax.nn.sigmoid(logit_ref[...])
    biased = scores + bias_ref[...]
    neg = jnp.float32(-jnp.inf)

    grouped = biased.reshape(N_GROUPS, GROUP_SIZE, TB)
    within = lax.broadcasted_iota(I32, grouped.shape, 1)
    top1 = jnp.max(grouped, axis=1, keepdims=True)
    first = _first_index_of_max(grouped, within, 1, GROUP_SIZE)
    top2 = jnp.max(jnp.where(within == first, neg, grouped), axis=1, keepdims=True)
    group_score = (top1 + top2).reshape(N_GROUPS, TB)

    gidx = lax.broadcasted_iota(I32, group_score.shape, 0)
    group_on = jnp.zeros(group_score.shape, jnp.bool_)
    for _ in range(TOPK_GROUPS):
        pick = gidx == _first_index_of_max(group_score, gidx, 0, N_GROUPS)
        group_on = jnp.logical_or(group_on, pick)
        group_score = jnp.where(pick, neg, group_score)

    masked = jnp.where(group_on.reshape(N_GROUPS, 1, TB), grouped, neg).reshape(N_EXPERTS, TB)
    eidx = lax.broadcasted_iota(I32, masked.shape, 0)
    chosen = jnp.zeros(masked.shape, jnp.bool_)
    for _ in range(TOP_K):
        pick = eidx == _first_index_of_max(masked, eidx, 0, N_EXPERTS)
        chosen = jnp.logical_or(chosen, pick)
        masked = jnp.where(pick, neg, masked)

    w = jnp.where(chosen, scores, 0.0)
    gates = w / jnp.sum(w, axis=0, keepdims=True) * ROUTED_SCALE

    chosen_f = chosen.astype(F32)
    tr = lax.broadcasted_iota(I32, (TB, TB), 0)
    tc = lax.broadcasted_iota(I32, (TB, TB), 1)
    rank = jnp.dot(chosen_f.astype(BF16), (tr < tc).astype(BF16), preferred_element_type=F32)
    both = jnp.concatenate([jnp.where(chosen, rank, -1.0), gates], axis=0)
    em_ref[...] = both
    tm_ref[...] = both.T
    cnt_ref[...] = jnp.broadcast_to(jnp.sum(chosen_f, axis=1, keepdims=True), (N_EXPERTS, LANES))


def _route(logits_t, bias):
    e, t = logits_t.shape
    nblk = t // TB
    return pl.pallas_call(
        _route_kernel,
        grid=(nblk,),
        in_specs=[pl.BlockSpec((e, TB), lambda i: (0, i)),
                  pl.BlockSpec((e, 1), lambda i: (0, 0))],
        out_specs=[pl.BlockSpec((2 * e, TB), lambda i: (0, i)),
                   pl.BlockSpec((TB, 2 * e), lambda i: (i, 0)),
                   pl.BlockSpec((None, e, LANES), lambda i: (i, 0, 0))],
        out_shape=[jax.ShapeDtypeStruct((2 * e, t), F32),
                   jax.ShapeDtypeStruct((t, 2 * e), F32),
                   jax.ShapeDtypeStruct((nblk, e, LANES), F32)],
        compiler_params=pltpu.CompilerParams(dimension_semantics=("parallel",)),
        name="route",
    )(logits_t, bias.reshape(e, 1))


def _dispatch_plan(cnt, n_tiles):
    nblk, e = cnt.shape
    pc = (cnt + ROW_CHUNK - 1) // ROW_CHUNK * ROW_CHUNK
    start = jnp.cumsum(pc, axis=1) - pc
    nchunk = jnp.sum(pc, axis=1) // ROW_CHUNK
    off = jnp.cumsum(pc, axis=0) - pc
    ecount = jnp.sum(pc, axis=0)
    epad = (ecount + TM_FFN - 1) // TM_FFN * TM_FFN
    gend = jnp.cumsum(epad)
    gbase = gend - epad
    nact = gend[-1] // TM_FFN
    tile_start = jnp.concatenate([gbase, gend[-1:]]) // TM_FFN
    cidx = jnp.arange(BLOCK_CHUNKS, dtype=I32)
    start16 = start // ROW_CHUNK
    shift = (gbase[None, :] + off) // ROW_CHUNK - start16
    dshift = shift - jnp.pad(shift, ((0, 0), (1, 0)))[:, :-1]
    in_or_after = (start16[:, None, :] <= cidx[None, :, None]).astype(I32)
    where = cidx[None, :] + jnp.sum(in_or_after * dshift[:, None, :], axis=2)
    used = cidx[None, :] < nchunk[:, None]
    spare = (n_tiles * TM_FFN // ROW_CHUNK
             + jnp.arange(nblk, dtype=I32)[:, None] * GROUP_CHUNKS + cidx[None, :] % GROUP_CHUNKS)
    gdst = jnp.where(used, where, spare)
    gsrc = jnp.where(used, where, 0)
    ngrp = (nchunk + GROUP_CHUNKS - 1) // GROUP_CHUNKS
    zbase = (gbase + ecount) // ROW_CHUNK
    zn = (epad - ecount) // ROW_CHUNK

    lo = jnp.tile(start.astype(F32), (1, 2))
    hi = jnp.tile((start + pc).astype(F32), (1, 2))
    bounds_row = jnp.stack([jnp.broadcast_to(lo[:, None, :], (nblk, SUBLANES, 2 * e)),
                            jnp.broadcast_to(hi[:, None, :], (nblk, SUBLANES, 2 * e))], axis=1)
    bounds_col = jnp.stack([jnp.broadcast_to(lo[:, :, None], (nblk, 2 * e, LANES)),
                            jnp.broadcast_to(hi[:, :, None], (nblk, 2 * e, LANES))], axis=1)
    as_i32 = lambda a: a.astype(I32)
    return dict(ngrp=as_i32(ngrp), gdst=as_i32(gdst.reshape(-1)), gsrc=as_i32(gsrc.reshape(-1)),
                zn=as_i32(zn), zbase=as_i32(zbase), tile_start=as_i32(tile_start),
                nact=as_i32(nact.reshape(1)), bounds_row=bounds_row, bounds_col=bounds_col)


def _dispatch_kernel(ngrp_ref, gdst_ref, zn_ref, zbase_ref, h_ref, em_ref, brow_ref,
                     xs_ref, buf_ref, zero_ref, sem, zsem):
    j = pl.program_id(0)
    last = pl.num_programs(0) - 1
    slot = j % 2

    def chunk_copy(s, c, g):
        return pltpu.make_async_copy(buf_ref.at[s, c], xs_ref.at[g], sem.at[s])

    def wait_block(jj, s):
        def body(i, carry):
            pltpu.make_async_copy(buf_ref.at[s, pl.ds(0, GROUP_CHUNKS)],
                                  xs_ref.at[pl.ds(0, GROUP_CHUNKS)], sem.at[s]).wait()
            return carry
        lax.fori_loop(0, ngrp_ref[jj], body, 0)

    def zero_copy(g, chunks):
        return pltpu.make_async_copy(zero_ref.at[pl.ds(0, chunks)], xs_ref.at[pl.ds(g, chunks)],
                                     zsem.at[0])

    def zero_fill(start):
        big = ZERO_ROWS // ROW_CHUNK

        def per_expert(e, carry):
            n = zn_ref[e]
            base = zbase_ref[e]

            def big_copy(i, c2):
                cp = zero_copy(base + i * big, big)
                cp.start() if start else cp.wait()
                return c2
            lax.fori_loop(0, n // big, big_copy, 0)

            def small_copy(i, c2):
                cp = zero_copy(base + n // big * big + i, 1)
                cp.start() if start else cp.wait()
                return c2
            lax.fori_loop(0, n % big, small_copy, 0)
            return carry
        lax.fori_loop(0, N_EXPERTS, per_expert, 0)

    @pl.when(j == 0)
    def _():
        zero_ref[...] = jnp.zeros_like(zero_ref)
        zero_fill(True)

    @pl.when(j >= 2)
    def _():
        wait_block(j - 2, slot)

    lo = brow_ref[0, 0:1, :]
    hi = brow_ref[1, 0:1, :]
    first_copy = lax.broadcasted_iota(I32, (KC, 2 * N_EXPERTS), 1) < N_EXPERTS
    row_iota = lax.broadcasted_iota(I32, (KC, 2 * N_EXPERTS), 0)
    ranks = em_ref[...].astype(BF16)
    h = h_ref[...]

    def trip(ti, carry):
        onehots = []
        for u in range(GROUP_ROWS // KC):
            rows_e = (row_iota + ti * GROUP_ROWS + u * KC).astype(F32)
            in_run = jnp.logical_and(jnp.logical_and(rows_e >= lo, rows_e < hi), first_copy)
            sel = jnp.dot(jnp.where(in_run, 1.0, 0.0).astype(BF16), ranks,
                          preferred_element_type=F32)
            local = jnp.sum(jnp.where(in_run, rows_e - lo + 3.0, 0.0), axis=1,
                            keepdims=True) - 3.0
            onehots.append(jnp.where(sel == local, 1.0, 0.0).astype(BF16))
        xs = jnp.dot(jnp.concatenate(onehots, axis=0), h, preferred_element_type=F32)
        first = pl.multiple_of(ti * GROUP_CHUNKS, GROUP_CHUNKS)
        buf_ref[slot, pl.ds(first, GROUP_CHUNKS)] = xs.astype(BF16).reshape(
            GROUP_CHUNKS, ROW_CHUNK, xs.shape[1])
        for k in range(GROUP_CHUNKS):
            chunk_copy(slot, first + k, gdst_ref[j * BLOCK_CHUNKS + first + k]).start()
        return carry
    lax.fori_loop(0, ngrp_ref[j], trip, 0)

    @pl.when(j == last)
    def _():
        wait_block(j, slot)

        @pl.when(j >= 1)
        def _():
            wait_block(j - 1, 1 - slot)

        zero_fill(False)


def _bounds_specs():
    return [pl.BlockSpec((None, 2, SUBLANES, 2 * N_EXPERTS), lambda j, *_: (j, 0, 0, 0)),
            pl.BlockSpec((None, 2, 2 * N_EXPERTS, LANES), lambda j, *_: (j, 0, 0, 0))]


def _dispatch(plan, h2_flat, em, n_rows):
    t, d = h2_flat.shape
    grid_spec = pltpu.PrefetchScalarGridSpec(
        num_scalar_prefetch=4,
        grid=(t // TB,),
        in_specs=[pl.BlockSpec((TB, d), lambda j, *_: (j, 0)),
                  pl.BlockSpec((2 * N_EXPERTS, TB), lambda j, *_: (0, j))] + _bounds_specs()[:1],
        out_specs=pl.BlockSpec(memory_space=pl.ANY),
        scratch_shapes=[pltpu.VMEM((2, BLOCK_CHUNKS, ROW_CHUNK, d), BF16),
                        pltpu.VMEM((ZERO_ROWS // ROW_CHUNK, ROW_CHUNK, d), BF16),
                        pltpu.SemaphoreType.DMA((2,)),
                        pltpu.SemaphoreType.DMA((1,))],
    )
    xs = pl.pallas_call(
        _dispatch_kernel,
        grid_spec=grid_spec,
        out_shape=jax.ShapeDtypeStruct((n_rows // ROW_CHUNK, ROW_CHUNK, d), BF16),
        compiler_params=pltpu.CompilerParams(
            dimension_semantics=("arbitrary",), vmem_limit_bytes=VMEM_LIMIT),
        name="moe_dispatch",
    )(plan["ngrp"], plan["gdst"], plan["zn"], plan["zbase"], h2_flat, em, plan["bounds_row"])
    return xs.reshape(n_rows, d)


def _ffn_kernel(tstart_ref, nact_ref, wg_ref, wu_ref, wd_ref, xs_ref, ys_ref,
                xbuf, ybuf, wgu_s, wd_s, xsem, ysem):
    e = pl.program_id(0)
    nact = nact_ref[0]
    t0 = tstart_ref[e]
    t1 = tstart_ref[e + 1]

    def x_copy(t):
        s = t % (FFN_AHEAD + 1)
        return pltpu.make_async_copy(
            xs_ref.at[pl.ds(pl.multiple_of(t * TM_FFN, TM_FFN), TM_FFN), :], xbuf.at[s],
            xsem.at[s])

    def y_copy(t):
        s = t % FFN_OUT
        return pltpu.make_async_copy(
            ybuf.at[s], ys_ref.at[pl.ds(pl.multiple_of(t * TM_FFN, TM_FFN), TM_FFN), :],
            ysem.at[s])

    @pl.when(e == 0)
    def _():
        for a in range(FFN_AHEAD):
            @pl.when(a < nact)
            def _():
                x_copy(a).start()

    @pl.when(t1 > t0)
    def _():
        wgu_s[:, :D_EXPERT] = wg_ref[...].astype(BF16)
        wgu_s[:, D_EXPERT:] = wu_ref[...].astype(BF16)
        wd_s[...] = wd_ref[...].astype(BF16)

        def tile(t, carry):
            x_copy(t).wait()

            @pl.when(t + FFN_AHEAD < nact)
            def _():
                x_copy(t + FFN_AHEAD).start()

            gu = jnp.dot(xbuf[t % (FFN_AHEAD + 1)], wgu_s[...], preferred_element_type=F32)
            act = _silu(gu[:, :D_EXPERT]) * gu[:, D_EXPERT:]
            y = jnp.dot(act.astype(BF16), wd_s[...], preferred_element_type=F32)

            @pl.when(t >= FFN_OUT)
            def _():
                y_copy(t - FFN_OUT).wait()

            ybuf[t % FFN_OUT] = y.astype(BF16)
            y_copy(t).start()
            return carry
        lax.fori_loop(t0, t1, tile, 0)

    @pl.when(e == pl.num_programs(0) - 1)
    def _():
        for back in range(FFN_OUT, 0, -1):
            @pl.when(nact >= back)
            def _():
                y_copy(nact - back).wait()


def _expert_ffn(tile_start, nact, xs, n_tiles, w_gate, w_up, w_down):
    n_rows, d = n_tiles * TM_FFN, xs.shape[1]
    f = D_EXPERT
    grid_spec = pltpu.PrefetchScalarGridSpec(
        num_scalar_prefetch=2,
        grid=(N_EXPERTS,),
        in_specs=[pl.BlockSpec((None, d, f), lambda e, ts, na: (e, 0, 0)),
                  pl.BlockSpec((None, d, f), lambda e, ts, na: (e, 0, 0)),
                  pl.BlockSpec((None, f, d), lambda e, ts, na: (e, 0, 0)),
                  pl.BlockSpec(memory_space=pl.ANY)],
        out_specs=pl.BlockSpec(memory_space=pl.ANY),
        scratch_shapes=[pltpu.VMEM((FFN_AHEAD + 1, TM_FFN, d), BF16),
                        pltpu.VMEM((FFN_OUT, TM_FFN, d), BF16),
                        pltpu.VMEM((d, 2 * f), BF16), pltpu.VMEM((f, d), BF16),
                        pltpu.SemaphoreType.DMA((FFN_AHEAD + 1,)),
                        pltpu.SemaphoreType.DMA((FFN_OUT,))],
    )
    return pl.pallas_call(
        _ffn_kernel,
        grid_spec=grid_spec,
        out_shape=jax.ShapeDtypeStruct((n_rows, d), BF16),
        compiler_params=pltpu.CompilerParams(
            dimension_semantics=("arbitrary",), vmem_limit_bytes=VMEM_LIMIT),
        name="moe_experts",
    )(tile_start, nact, w_gate, w_up, w_down, xs)


def _combine_kernel(ngrp_ref, gsrc_ref, ys_ref, tm_ref, brow_ref, bcol_ref, h_ref, wsg_ref,
                    wsu_ref, wsd_ref, x1_ref, mod_ref, gpost_ref, o_ref, buf_ref, p_ref, sem):
    j = pl.program_id(0)
    last = pl.num_programs(0) - 1
    slot = j % 2

    def chunk_copy(s, c, g):
        return pltpu.make_async_copy(ys_ref.at[g], buf_ref.at[s, c], sem.at[s])

    def fetch_block(jj, s):
        def issue_group(gi, carry):
            for k in range(GROUP_CHUNKS):
                c = gi * GROUP_CHUNKS + k
                chunk_copy(s, c, gsrc_ref[jj * BLOCK_CHUNKS + c]).start()
            return carry
        lax.fori_loop(0, ngrp_ref[jj], issue_group, 0)

    @pl.when(j == 0)
    def _():
        buf_ref[...] = jnp.zeros_like(buf_ref)
        fetch_block(0, 0)

    @pl.when(j < last)
    def _():
        fetch_block(j + 1, 1 - slot)

    h = h_ref[...]
    act = _silu(jnp.dot(h, wsg_ref[...], preferred_element_type=F32)) * jnp.dot(
        h, wsu_ref[...], preferred_element_type=F32)
    y = jnp.dot(act.astype(BF16), wsd_ref[...], preferred_element_type=F32)

    tm = tm_ref[...]
    lane = lax.broadcasted_iota(I32, tm.shape, 1)
    rank_t = jnp.where(lane < N_EXPERTS, tm, 0.0).astype(BF16)
    gate_t = jnp.where(lane < N_EXPERTS, 0.0, tm).astype(BF16)
    lane8 = lax.broadcasted_iota(I32, (SUBLANES, 2 * N_EXPERTS), 1)
    lo_row = jnp.where(lane8 < N_EXPERTS, brow_ref[0], 0.0).astype(BF16)
    lo_col = jnp.concatenate([bcol_ref[0]] * (KC // LANES), axis=1)
    hi_col = jnp.concatenate([bcol_ref[1]] * (KC // LANES), axis=1)
    for cb in range(BLOCK_ROWS // KC):
        rows = (lax.broadcasted_iota(I32, (2 * N_EXPERTS, KC), 1) + cb * KC).astype(F32)
        in_run = jnp.where(jnp.logical_and(rows >= lo_col, rows < hi_col), 1.0, 0.0).astype(BF16)
        sel_rank = jnp.dot(rank_t, in_run, preferred_element_type=F32)
        sel_gate = jnp.dot(gate_t, in_run, preferred_element_type=F32)
        sel_lo = jnp.dot(lo_row, in_run, preferred_element_type=F32)
        local = rows[0:1, :] - sel_lo[0:1, :]
        p_ref[:, cb * KC:(cb + 1) * KC] = jnp.where(sel_rank == local, sel_gate,
                                                    0.0).astype(BF16)

    def wait_group(gi, carry):
        pltpu.make_async_copy(ys_ref.at[pl.ds(0, GROUP_CHUNKS)],
                              buf_ref.at[slot, pl.ds(0, GROUP_CHUNKS)], sem.at[slot]).wait()
        return carry
    lax.fori_loop(0, ngrp_ref[j], wait_group, 0)

    rows = buf_ref[slot].reshape(BLOCK_ROWS, buf_ref.shape[3])
    y = y + jnp.dot(p_ref[...], rows, preferred_element_type=F32)

    ga2 = mod_ref[5:6, :]
    o_ref[...] = x1_ref[...] + ga2 * _rms(y, gpost_ref[...])


def _combine(plan, ys, tm, h2_flat, ws_gate, ws_up, ws_down, x1_flat, mod3, g_post, s):
    t, d = h2_flat.shape
    f = ws_gate.shape[1]
    blocks_per_seq = s // TB
    full = lambda shape: pl.BlockSpec(shape, lambda j, *_: (0,) * len(shape))
    grid_spec = pltpu.PrefetchScalarGridSpec(
        num_scalar_prefetch=2,
        grid=(t // TB,),
        in_specs=[pl.BlockSpec(memory_space=pl.ANY),
                  pl.BlockSpec((TB, 2 * N_EXPERTS), lambda j, *_: (j, 0))] + _bounds_specs() + [
                  pl.BlockSpec((TB, d), lambda j, *_: (j, 0)),
                  full((d, f)), full((d, f)), full((f, d)),
                  pl.BlockSpec((TB, d), lambda j, *_: (j, 0)),
                  pl.BlockSpec((None, 6, d), lambda j, *_: (j // blocks_per_seq, 0, 0)),
                  full((1, d))],
        out_specs=pl.BlockSpec((TB, d), lambda j, *_: (j, 0)),
        scratch_shapes=[pltpu.VMEM((2, BLOCK_CHUNKS, ROW_CHUNK, d), BF16),
                        pltpu.VMEM((TB, BLOCK_ROWS), BF16),
                        pltpu.SemaphoreType.DMA((2,))],
    )
    return pl.pallas_call(
        _combine_kernel,
        grid_spec=grid_spec,
        out_shape=jax.ShapeDtypeStruct((t, d), F32),
        compiler_params=pltpu.CompilerParams(
            dimension_semantics=("arbitrary",), vmem_limit_bytes=VMEM_LIMIT),
        name="moe_combine",
    )(plan["ngrp"], plan["gsrc"], ys.reshape(-1, ROW_CHUNK, d), tm, plan["bounds_row"],
      plan["bounds_col"], h2_flat, ws_gate, ws_up, ws_down, x1_flat, mod3, g_post)


def kernel(x, c, w_ada, b_ada, g_pre_mix, w_in, ln_sgu_g, ln_sgu_b, w_spatial, b_spatial,
           g_branch, w_out, g_post_mix, g_pre_ffn, w_router, router_bias, w_gate, w_up, w_down,
           ws_gate, ws_up, ws_down, g_post_ffn):
    bsz, s, d = x.shape
    t = bsz * s
    nblk = t // TB
    max_rows = t * TOP_K + nblk * N_EXPERTS * (ROW_CHUNK - 1) + N_EXPERTS * (TM_FFN - 1)
    n_tiles = -(-max_rows // TM_FFN)
    row = lambda a: a.reshape(1, -1)
    for l in range(w_ada.shape[0]):
        mod3 = _ada(c, w_ada[l], b_ada[l]).reshape(bsz, 6, d)
        og, q, k, v = _premix(x, mod3, row(g_pre_mix[l]), w_in[l].astype(BF16),
                              row(ln_sgu_g[l]), row(ln_sgu_b[l]), w_spatial[l],
                              b_spatial[l].T, row(g_branch[l, :D_GMLP]))
        osb = _attention(q, k, v)
        x1, h2, logits_t = _postmix(og, osb, x, mod3, row(g_branch[l, D_GMLP:]),
                                    w_out[l].astype(BF16), row(g_post_mix[l]),
                                    row(g_pre_ffn[l]), w_router[l].T)
        em, tm, cnt = _route(logits_t, router_bias[l])
        plan = _dispatch_plan(cnt[:, :, 0].astype(I32), n_tiles)
        h2_flat = h2.reshape(t, d)
        xs = _dispatch(plan, h2_flat, em, n_tiles * TM_FFN + nblk * GROUP_ROWS)
        ys = _expert_ffn(plan["tile_start"], plan["nact"], xs, n_tiles,
                         w_gate[l], w_up[l], w_down[l])
        out = _combine(plan, ys, tm, h2_flat, ws_gate[l].astype(BF16), ws_up[l].astype(BF16),
                       ws_down[l].astype(BF16), x1.reshape(t, d), mod3, row(g_post_ffn[l]), s)
        x = out.reshape(bsz, s, d)
    return x
```

```python
import jax
import jax.numpy as jnp
from jax import lax
from jax.experimental import pallas as pl
from jax.experimental.pallas import tpu as pltpu

F32 = jnp.float32
BF16 = jnp.bfloat16
I32 = jnp.int32

LANES = 128
SUBLANES = 8
V7X_VMEM_BYTES = 64 * 1024 * 1024

D_MODEL = 1024
D_GMLP = 512
GMLP_GROUPS = 4
GMLP_BLOCK = 128
CHUNK = 64
D_SB = 512
SB_HEAD_DIM = 64
N_PAIRS = D_SB // LANES
N_EXPERTS = 64
N_GROUPS = 8
GROUP_SIZE = N_EXPERTS // N_GROUPS
TOPK_GROUPS = 4
TOP_K = 8
D_EXPERT = 256
ROUTED_SCALE = 2.5
EPS = 1e-6
D_IN = 2 * D_GMLP + 3 * D_SB

ADA_TN = 1536
TM_MIX = 512
TQ = 128
VMEM_LIMIT = V7X_VMEM_BYTES * 7 // 8
ATTN_DEAD_LOG = -110.0

TB = 256
ROW_CHUNK = 16
KC = 256
GROUP_CHUNKS = 32
GROUP_ROWS = GROUP_CHUNKS * ROW_CHUNK
BLOCK_ROWS = -(-(TB * TOP_K + N_EXPERTS * (ROW_CHUNK - 1)) // GROUP_ROWS) * GROUP_ROWS
BLOCK_CHUNKS = BLOCK_ROWS // ROW_CHUNK
COMMON_ROWS = -(-(TB * TOP_K + N_EXPERTS * ROW_CHUNK // 2) // GROUP_ROWS) * GROUP_ROWS
TM_FFN = 1024
FFN_AHEAD = 3
FFN_OUT = 3
ZERO_ROWS = 128


def _rms(x, g):
    return x * lax.rsqrt(jnp.mean(x * x, axis=-1, keepdims=True) + EPS) * g


def _silu(x):
    return x * jax.nn.sigmoid(x)


def _ada_kernel(c_ref, w_ref, b_ref, o_ref):
    o_ref[...] = jnp.dot(_silu(c_ref[...]), w_ref[...], preferred_element_type=F32,
                         precision=lax.Precision.HIGHEST) + b_ref[...]


def _ada(c, w, b):
    bsz, d = c.shape
    n = w.shape[1]
    tn = ADA_TN
    return pl.pallas_call(
        _ada_kernel,
        grid=(n // tn,),
        in_specs=[pl.BlockSpec((bsz, d), lambda j: (0, 0)),
                  pl.BlockSpec((d, tn), lambda j: (0, j)),
                  pl.BlockSpec((1, tn), lambda j: (0, j))],
        out_specs=pl.BlockSpec((bsz, tn), lambda j: (0, j)),
        out_shape=jax.ShapeDtypeStruct((bsz, n), F32),
        name="ada_mod",
    )(c, w, b.reshape(1, n))


def _premix_kernel(x_ref, mod_ref, gpre_ref, win_ref, lng_ref, lnb_ref, wsp_ref, bsp_ref,
                   gbr_ref, og_ref, q_ref, k_ref, v_ref):
    x = x_ref[...]
    sh1 = mod_ref[0:1, :]
    sc1 = mod_ref[1:2, :]
    h = _rms(x, gpre_ref[...]) * (1.0 + sc1) + sh1
    proj = jnp.dot(h.astype(BF16), win_ref[...], preferred_element_type=F32)

    u = jax.nn.gelu(proj[:, :D_GMLP])
    v = jax.nn.gelu(proj[:, D_GMLP:2 * D_GMLP])
    mu = jnp.mean(v, axis=-1, keepdims=True)
    var = jnp.mean(jnp.square(v - mu), axis=-1, keepdims=True)
    v = ((v - mu) * lax.rsqrt(var + EPS) * lng_ref[...] + lnb_ref[...]).astype(BF16)

    i = lax.broadcasted_iota(I32, (GMLP_BLOCK, GMLP_BLOCK), 0)
    j = lax.broadcasted_iota(I32, (GMLP_BLOCK, GMLP_BLOCK), 1)
    causal = (j // CHUNK) <= (i // CHUNK)
    gd = D_GMLP // GMLP_GROUPS
    blocks = []
    for nb in range(x.shape[0] // GMLP_BLOCK):
        rows = slice(nb * GMLP_BLOCK, (nb + 1) * GMLP_BLOCK)
        cols = []
        for g in range(GMLP_GROUPS):
            w = jnp.where(causal, wsp_ref[g], 0.0).astype(BF16)
            mixed = jnp.dot(w, v[rows, g * gd:(g + 1) * gd], preferred_element_type=F32)
            cols.append(mixed + bsp_ref[:, g:g + 1])
        blocks.append(u[rows, :] * jnp.concatenate(cols, axis=1))
    og = jnp.concatenate(blocks, axis=0)
    og_ref[...] = _rms(og, gbr_ref[...]).astype(BF16)

    base = 2 * D_GMLP
    scale = SB_HEAD_DIM ** -0.5
    for p in range(N_PAIRS):
        q_ref[p] = (proj[:, base + LANES * p:base + LANES * (p + 1)] * scale).astype(BF16)
        k_ref[p] = proj[:, base + D_SB + LANES * p:base + D_SB + LANES * (p + 1)].astype(BF16)
        v_ref[p] = proj[:, base + 2 * D_SB + LANES * p:base + 2 * D_SB + LANES * (p + 1)].astype(BF16)


def _premix(x, mod3, g_pre, w_in, ln_g, ln_b, w_sp, b_sp_t, g_br):
    bsz, s, d = x.shape
    tm = TM_MIX
    full = lambda shape: pl.BlockSpec(shape, lambda b, i: (0,) * len(shape))
    qkv_spec = pl.BlockSpec((None, N_PAIRS, tm, LANES), lambda b, i: (b, 0, i, 0))
    qkv_shape = jax.ShapeDtypeStruct((bsz, N_PAIRS, s, LANES), BF16)
    return pl.pallas_call(
        _premix_kernel,
        grid=(bsz, s // tm),
        in_specs=[pl.BlockSpec((None, tm, d), lambda b, i: (b, i, 0)),
                  pl.BlockSpec((None, 6, d), lambda b, i: (b, 0, 0)),
                  full((1, d)), full((d, D_IN)), full((1, D_GMLP)), full((1, D_GMLP)),
                  full((GMLP_GROUPS, GMLP_BLOCK, GMLP_BLOCK)), full((GMLP_BLOCK, GMLP_GROUPS)),
                  full((1, D_GMLP))],
        out_specs=[pl.BlockSpec((None, tm, D_GMLP), lambda b, i: (b, i, 0)),
                   qkv_spec, qkv_spec, qkv_spec],
        out_shape=[jax.ShapeDtypeStruct((bsz, s, D_GMLP), BF16), qkv_shape, qkv_shape, qkv_shape],
        compiler_params=pltpu.CompilerParams(
            dimension_semantics=("parallel", "parallel"), vmem_limit_bytes=VMEM_LIMIT),
        name="premix",
    )(x, mod3, g_pre, w_in, ln_g, ln_b, w_sp, b_sp_t, g_br)


def _attn_kernel(q_ref, k_ref, v_ref, o_ref, *scratch):
    qs_refs = scratch[:N_PAIRS]
    acc_refs = scratch[N_PAIRS:2 * N_PAIRS]
    carry_refs = scratch[2 * N_PAIRS:]
    qi = pl.program_id(1)
    first_head = lax.broadcasted_iota(I32, (TQ, LANES), 1) < SB_HEAD_DIM
    for p in range(N_PAIRS):
        q2 = q_ref[p]
        zero = jnp.zeros_like(q2)
        qs_refs[p][:TQ, :] = jnp.where(first_head, q2, zero)
        qs_refs[p][TQ:, :] = jnp.where(first_head, zero, q2)
        acc_refs[p][...] = jnp.zeros_like(acc_refs[p])
        carry_refs[p][...] = jnp.zeros_like(carry_refs[p])

    r = jnp.bitwise_and(lax.broadcasted_iota(I32, (2 * TQ, TQ), 0), TQ - 1)
    c = lax.broadcasted_iota(I32, (2 * TQ, TQ), 1)
    strict_causal = c < r
    kr = lax.broadcasted_iota(I32, (TQ, TQ), 0)
    kc = lax.broadcasted_iota(I32, (TQ, TQ), 1)
    suffix = jnp.concatenate([(kr > kc).astype(BF16), jnp.ones((TQ, TQ), BF16)], axis=1)
    suffix2 = jnp.concatenate([suffix, suffix], axis=0)

    def key_block(j, diagonal):
        start = pl.multiple_of(j * TQ, TQ)
        pairs = range(N_PAIRS)
        zs = [lax.dot_general(qs_refs[p][...], k_ref[p, pl.ds(start, TQ), :],
                              (((1,), (1,)), ((), ())), preferred_element_type=F32)
              for p in pairs]
        log_betas, splits = [], []
        for p in pairs:
            z = zs[p]
            log_beta = jnp.minimum(z, 0.0) - jnp.log(1.0 + jnp.exp(-jnp.abs(z)))
            log_1mb = log_beta - z
            if diagonal:
                log_1mb = jnp.where(strict_causal, log_1mb, 0.0)
            hi = log_1mb.astype(BF16)
            lo = (log_1mb - hi.astype(F32)).astype(BF16)
            log_betas.append(log_beta)
            splits.append(jnp.concatenate([hi, lo], axis=1))
        sums = [jnp.dot(splits[p], suffix2, preferred_element_type=F32) for p in pairs]
        weights = []
        live = None
        for p in pairs:
            s = sums[p]
            carry = carry_refs[p][...]
            a = jnp.exp(log_betas[p] + carry + s[:, :TQ])
            if diagonal:
                a = jnp.where(strict_causal, a, 0.0)
            weights.append(a.astype(BF16))
            carry = carry + s[:, TQ:]
            carry_refs[p][...] = carry
            live = carry if live is None else jnp.maximum(live, carry)
        for p in pairs:
            acc_refs[p][...] += jnp.dot(weights[p], v_ref[p, pl.ds(start, TQ), :],
                                        preferred_element_type=F32)
        return jnp.max(live)

    live = key_block(qi, True)

    def cond(state):
        j, live = state
        return jnp.logical_and(j >= 0, live > ATTN_DEAD_LOG)

    def body(state):
        j, _ = state
        return j - 1, key_block(j, False)

    lax.while_loop(cond, body, (qi - 1, live))
    for p in range(N_PAIRS):
        o_ref[:, LANES * p:LANES * (p + 1)] = jnp.where(first_head, acc_refs[p][:TQ, :],
                                                    acc_refs[p][TQ:, :])


def _attention(q, k, v):
    bsz, npair, s, _ = q.shape
    kv_spec = pl.BlockSpec((None, npair, s, LANES), lambda b, i: (b, 0, 0, 0))
    return pl.pallas_call(
        _attn_kernel,
        grid=(bsz, s // TQ),
        in_specs=[pl.BlockSpec((None, npair, TQ, LANES), lambda b, i: (b, 0, i, 0)),
                  kv_spec, kv_spec],
        out_specs=pl.BlockSpec((None, TQ, npair * LANES), lambda b, i: (b, i, 0)),
        out_shape=jax.ShapeDtypeStruct((bsz, s, npair * LANES), F32),
        scratch_shapes=([pltpu.VMEM((2 * TQ, LANES), BF16)] * npair
                        + [pltpu.VMEM((2 * TQ, LANES), F32)] * npair
                        + [pltpu.VMEM((2 * TQ, TQ), F32)] * npair),
        compiler_params=pltpu.CompilerParams(
            dimension_semantics=("parallel", "parallel"), vmem_limit_bytes=VMEM_LIMIT),
        name="stickbreak_attn",
    )(q, k, v)


def _postmix_kernel(og_ref, osb_ref, x_ref, mod_ref, gbr_ref, wout_ref, gpost_ref, gpre_ref,
                    wrt_ref, x1_ref, h2_ref, logit_ref):
    ga1 = mod_ref[2:3, :]
    sh2 = mod_ref[3:4, :]
    sc2 = mod_ref[4:5, :]
    osb = _rms(osb_ref[...], gbr_ref[...]).astype(BF16)
    m = (jnp.dot(og_ref[...], wout_ref[:D_GMLP, :], preferred_element_type=F32)
         + jnp.dot(osb, wout_ref[D_GMLP:, :], preferred_element_type=F32))
    x1 = x_ref[...] + ga1 * _rms(m, gpost_ref[...])
    x1_ref[...] = x1
    h2 = _rms(x1, gpre_ref[...]) * (1.0 + sc2) + sh2
    h_hi = h2.astype(BF16)
    h2_ref[...] = h_hi
    h_lo = (h2 - h_hi.astype(F32)).astype(BF16)
    w = wrt_ref[...]
    w_hi = w.astype(BF16)
    w_lo = (w - w_hi.astype(F32)).astype(BF16)
    nt = (((1,), (1,)), ((), ()))
    by_hi = lax.dot_general(jnp.concatenate([w_hi, w_lo], axis=0), h_hi, nt,
                            preferred_element_type=F32)
    by_lo = lax.dot_general(w_hi, h_lo, nt, preferred_element_type=F32)
    logit_ref[...] = by_hi[:N_EXPERTS] + by_hi[N_EXPERTS:] + by_lo


def _postmix(og, osb, x, mod3, g_br, w_out, g_post, g_pre, w_router_t):
    bsz, s, d = x.shape
    tm = TM_MIX
    nt = s // tm
    full = lambda shape: pl.BlockSpec(shape, lambda b, i: (0,) * len(shape))
    return pl.pallas_call(
        _postmix_kernel,
        grid=(bsz, nt),
        in_specs=[pl.BlockSpec((None, tm, D_GMLP), lambda b, i: (b, i, 0)),
                  pl.BlockSpec((None, tm, D_SB), lambda b, i: (b, i, 0)),
                  pl.BlockSpec((None, tm, d), lambda b, i: (b, i, 0)),
                  pl.BlockSpec((None, 6, d), lambda b, i: (b, 0, 0)),
                  full((1, D_SB)), full((d, d)), full((1, d)), full((1, d)),
                  full((N_EXPERTS, d))],
        out_specs=[pl.BlockSpec((None, tm, d), lambda b, i: (b, i, 0)),
                   pl.BlockSpec((None, tm, d), lambda b, i: (b, i, 0)),
                   pl.BlockSpec((N_EXPERTS, tm), lambda b, i: (0, b * nt + i))],
        out_shape=[jax.ShapeDtypeStruct((bsz, s, d), F32),
                   jax.ShapeDtypeStruct((bsz, s, d), BF16),
                   jax.ShapeDtypeStruct((N_EXPERTS, bsz * s), F32)],
        compiler_params=pltpu.CompilerParams(
            dimension_semantics=("parallel", "parallel"), vmem_limit_bytes=VMEM_LIMIT),
        name="postmix",
    )(og, osb, x, mod3, g_br, w_out, g_post, g_pre, w_router_t)


def _first_index_of_max(x, idx, axis, size):
    m = jnp.max(x, axis=axis, keepdims=True)
    return jnp.min(jnp.where(x == m, idx, size), axis=axis, keepdims=True)


def _route_kernel(logit_ref, bias_ref, em_ref, tm_ref, cnt_ref):
    scores = jax.nn.sigmoid(logit_ref[...])
    biased = scores + bias_ref[...]
    neg = jnp.float32(-jnp.inf)

    grouped = biased.reshape(N_GROUPS, GROUP_SIZE, TB)
    within = lax.broadcasted_iota(I32, grouped.shape, 1)
    top1 = jnp.max(grouped, axis=1, keepdims=True)
    first = _first_index_of_max(grouped, within, 1, GROUP_SIZE)
    top2 = jnp.max(jnp.where(within == first, neg, grouped), axis=1, keepdims=True)
    group_score = (top1 + top2).reshape(N_GROUPS, TB)

    gidx = lax.broadcasted_iota(I32, group_score.shape, 0)
    group_on = jnp.zeros(group_score.shape, jnp.bool_)
    for _ in range(TOPK_GROUPS):
        pick = gidx == _first_index_of_max(group_score, gidx, 0, N_GROUPS)
        group_on = jnp.logical_or(group_on, pick)
        group_score = jnp.where(pick, neg, group_score)

    masked = jnp.where(group_on.reshape(N_GROUPS, 1, TB), grouped, neg).reshape(N_EXPERTS, TB)
    eidx = lax.broadcasted_iota(I32, masked.shape, 0)
    chosen = jnp.zeros(masked.shape, jnp.bool_)
    for _ in range(TOP_K):
        pick = eidx == _first_index_of_max(masked, eidx, 0, N_EXPERTS)
        chosen = jnp.logical_or(chosen, pick)
        masked = jnp.where(pick, neg, masked)

    w = jnp.where(chosen, scores, 0.0)
    gates = w / jnp.sum(w, axis=0, keepdims=True) * ROUTED_SCALE

    chosen_f = chosen.astype(F32)
    tr = lax.broadcasted_iota(I32, (TB, TB), 0)
    tc = lax.broadcasted_iota(I32, (TB, TB), 1)
    rank = jnp.dot(chosen_f.astype(BF16), (tr < tc).astype(BF16), preferred_element_type=F32)
    both = jnp.concatenate([jnp.where(chosen, rank, -1.0), gates], axis=0)
    em_ref[...] = both
    tm_ref[...] = both.T
    cnt_ref[...] = jnp.broadcast_to(jnp.sum(chosen_f, axis=1, keepdims=True), (N_EXPERTS, LANES))


def _route(logits_t, bias):
    e, t = logits_t.shape
    nblk = t // TB
    return pl.pallas_call(
        _route_kernel,
        grid=(nblk,),
        in_specs=[pl.BlockSpec((e, TB), lambda i: (0, i)),
                  pl.BlockSpec((e, 1), lambda i: (0, 0))],
        out_specs=[pl.BlockSpec((2 * e, TB), lambda i: (0, i)),
                   pl.BlockSpec((TB, 2 * e), lambda i: (i, 0)),
                   pl.BlockSpec((None, e, LANES), lambda i: (i, 0, 0))],
        out_shape=[jax.ShapeDtypeStruct((2 * e, t), F32),
                   jax.ShapeDtypeStruct((t, 2 * e), F32),
                   jax.ShapeDtypeStruct((nblk, e, LANES), F32)],
        compiler_params=pltpu.CompilerParams(dimension_semantics=("parallel",)),
        name="route",
    )(logits_t, bias.reshape(e, 1))


def _dispatch_plan(cnt, n_tiles):
    nblk, e = cnt.shape
    pc = (cnt + ROW_CHUNK - 1) // ROW_CHUNK * ROW_CHUNK
    start = jnp.cumsum(pc, axis=1) - pc
    nchunk = jnp.sum(pc, axis=1) // ROW_CHUNK
    off = jnp.cumsum(pc, axis=0) - pc
    ecount = jnp.sum(pc, axis=0)
    epad = (ecount + TM_FFN - 1) // TM_FFN * TM_FFN
    gend = jnp.cumsum(epad)
    gbase = gend - epad
    nact = gend[-1] // TM_FFN
    tile_start = jnp.concatenate([gbase, gend[-1:]]) // TM_FFN
    cidx = jnp.arange(BLOCK_CHUNKS, dtype=I32)
    start16 = start // ROW_CHUNK
    shift = (gbase[None, :] + off) // ROW_CHUNK - start16
    dshift = shift - jnp.pad(shift, ((0, 0), (1, 0)))[:, :-1]
    in_or_after = (start16[:, None, :] <= cidx[None, :, None]).astype(I32)
    where = cidx[None, :] + jnp.sum(in_or_after * dshift[:, None, :], axis=2)
    used = cidx[None, :] < nchunk[:, None]
    spare = (n_tiles * TM_FFN // ROW_CHUNK
             + jnp.arange(nblk, dtype=I32)[:, None] * GROUP_CHUNKS + cidx[None, :] % GROUP_CHUNKS)
    gdst = jnp.where(used, where, spare)
    gsrc = jnp.where(used, where, 0)
    ngrp = (nchunk + GROUP_CHUNKS - 1) // GROUP_CHUNKS
    zbase = (gbase + ecount) // ROW_CHUNK
    zn = (epad - ecount) // ROW_CHUNK

    lo = jnp.tile(start.astype(F32), (1, 2))
    hi = jnp.tile((start + pc).astype(F32), (1, 2))
    bounds_row = jnp.stack([jnp.broadcast_to(lo[:, None, :], (nblk, SUBLANES, 2 * e)),
                            jnp.broadcast_to(hi[:, None, :], (nblk, SUBLANES, 2 * e))], axis=1)
    bounds_col = jnp.stack([jnp.broadcast_to(lo[:, :, None], (nblk, 2 * e, LANES)),
                            jnp.broadcast_to(hi[:, :, None], (nblk, 2 * e, LANES))], axis=1)
    as_i32 = lambda a: a.astype(I32)
    return dict(ngrp=as_i32(ngrp), gdst=as_i32(gdst.reshape(-1)), gsrc=as_i32(gsrc.reshape(-1)),
                zn=as_i32(zn), zbase=as_i32(zbase), tile_start=as_i32(tile_start),
                nact=as_i32(nact.reshape(1)), bounds_row=bounds_row, bounds_col=bounds_col)


def _dispatch_kernel(ngrp_ref, gdst_ref, zn_ref, zbase_ref, h_ref, em_ref, brow_ref,
                     xs_ref, buf_ref, zero_ref, sem, zsem):
    j = pl.program_id(0)
    last = pl.num_programs(0) - 1
    slot = j % 2

    def chunk_copy(s, c, g):
        return pltpu.make_async_copy(buf_ref.at[s, c], xs_ref.at[g], sem.at[s])

    def wait_block(jj, s):
        def body(i, carry):
            pltpu.make_async_copy(buf_ref.at[s, pl.ds(0, GROUP_CHUNKS)],
                                  xs_ref.at[pl.ds(0, GROUP_CHUNKS)], sem.at[s]).wait()
            return carry
        lax.fori_loop(0, ngrp_ref[jj], body, 0)

    def zero_copy(g, chunks):
        return pltpu.make_async_copy(zero_ref.at[pl.ds(0, chunks)], xs_ref.at[pl.ds(g, chunks)],
                                     zsem.at[0])

    def zero_fill(start):
        big = ZERO_ROWS // ROW_CHUNK

        def per_expert(e, carry):
            n = zn_ref[e]
            base = zbase_ref[e]

            def big_copy(i, c2):
                cp = zero_copy(base + i * big, big)
                cp.start() if start else cp.wait()
                return c2
            lax.fori_loop(0, n // big, big_copy, 0)

            def small_copy(i, c2):
                cp = zero_copy(base + n // big * big + i, 1)
                cp.start() if start else cp.wait()
                return c2
            lax.fori_loop(0, n % big, small_copy, 0)
            return carry
        lax.fori_loop(0, N_EXPERTS, per_expert, 0)

    @pl.when(j == 0)
    def _():
        zero_ref[...] = jnp.zeros_like(zero_ref)
        zero_fill(True)

    @pl.when(j >= 2)
    def _():
        wait_block(j - 2, slot)

    lo = brow_ref[0, 0:1, :]
    hi = brow_ref[1, 0:1, :]
    first_copy = lax.broadcasted_iota(I32, (KC, 2 * N_EXPERTS), 1) < N_EXPERTS
    row_iota = lax.broadcasted_iota(I32, (KC, 2 * N_EXPERTS), 0)
    ranks = em_ref[...].astype(BF16)
    h = h_ref[...]

    def trip(ti, carry):
        onehots = []
        for u in range(GROUP_ROWS // KC):
            rows_e = (row_iota + ti * GROUP_ROWS + u * KC).astype(F32)
            in_run = jnp.logical_and(jnp.logical_and(rows_e >= lo, rows_e < hi), first_copy)
            sel = jnp.dot(jnp.where(in_run, 1.0, 0.0).astype(BF16), ranks,
                          preferred_element_type=F32)
            local = jnp.sum(jnp.where(in_run, rows_e - lo + 3.0, 0.0), axis=1,
                            keepdims=True) - 3.0
            onehots.append(jnp.where(sel == local, 1.0, 0.0).astype(BF16))
        xs = jnp.dot(jnp.concatenate(onehots, axis=0), h, preferred_element_type=F32)
        first = pl.multiple_of(ti * GROUP_CHUNKS, GROUP_CHUNKS)
        buf_ref[slot, pl.ds(first, GROUP_CHUNKS)] = xs.astype(BF16).reshape(
            GROUP_CHUNKS, ROW_CHUNK, xs.shape[1])
        for k in range(GROUP_CHUNKS):
            chunk_copy(slot, first + k, gdst_ref[j * BLOCK_CHUNKS + first + k]).start()
        return carry
    lax.fori_loop(0, ngrp_ref[j], trip, 0)

    @pl.when(j == last)
    def _():
        wait_block(j, slot)

        @pl.when(j >= 1)
        def _():
            wait_block(j - 1, 1 - slot)

        zero_fill(False)


def _bounds_specs():
    return [pl.BlockSpec((None, 2, SUBLANES, 2 * N_EXPERTS), lambda j, *_: (j, 0, 0, 0)),
            pl.BlockSpec((None, 2, 2 * N_EXPERTS, LANES), lambda j, *_: (j, 0, 0, 0))]


def _dispatch(plan, h2_flat, em, n_rows):
    t, d = h2_flat.shape
    grid_spec = pltpu.PrefetchScalarGridSpec(
        num_scalar_prefetch=4,
        grid=(t // TB,),
        in_specs=[pl.BlockSpec((TB, d), lambda j, *_: (j, 0)),
                  pl.BlockSpec((2 * N_EXPERTS, TB), lambda j, *_: (0, j))] + _bounds_specs()[:1],
        out_specs=pl.BlockSpec(memory_space=pl.ANY),
        scratch_shapes=[pltpu.VMEM((2, BLOCK_CHUNKS, ROW_CHUNK, d), BF16),
                        pltpu.VMEM((ZERO_ROWS // ROW_CHUNK, ROW_CHUNK, d), BF16),
                        pltpu.SemaphoreType.DMA((2,)),
                        pltpu.SemaphoreType.DMA((1,))],
    )
    xs = pl.pallas_call(
        _dispatch_kernel,
        grid_spec=grid_spec,
        out_shape=jax.ShapeDtypeStruct((n_rows // ROW_CHUNK, ROW_CHUNK, d), BF16),
        compiler_params=pltpu.CompilerParams(
            dimension_semantics=("arbitrary",), vmem_limit_bytes=VMEM_LIMIT),
        name="moe_dispatch",
    )(plan["ngrp"], plan["gdst"], plan["zn"], plan["zbase"], h2_flat, em, plan["bounds_row"])
    return xs.reshape(n_rows, d)


def _ffn_kernel(tstart_ref, nact_ref, wg_ref, wu_ref, wd_ref, xs_ref, ys_ref,
                xbuf, ybuf, wgu_s, wd_s, xsem, ysem):
    e = pl.program_id(0)
    nact = nact_ref[0]
    t0 = tstart_ref[e]
    t1 = tstart_ref[e + 1]

    def x_copy(t):
        s = t % (FFN_AHEAD + 1)
        return pltpu.make_async_copy(
            xs_ref.at[pl.ds(pl.multiple_of(t * TM_FFN, TM_FFN), TM_FFN), :], xbuf.at[s],
            xsem.at[s])

    def y_copy(t):
        s = t % FFN_OUT
        return pltpu.make_async_copy(
            ybuf.at[s], ys_ref.at[pl.ds(pl.multiple_of(t * TM_FFN, TM_FFN), TM_FFN), :],
            ysem.at[s])

    @pl.when(e == 0)
    def _():
        for a in range(FFN_AHEAD):
            @pl.when(a < nact)
            def _():
                x_copy(a).start()

    @pl.when(t1 > t0)
    def _():
        wgu_s[:, :D_EXPERT] = wg_ref[...].astype(BF16)
        wgu_s[:, D_EXPERT:] = wu_ref[...].astype(BF16)
        wd_s[...] = wd_ref[...].astype(BF16)

        def tile(t, carry):
            x_copy(t).wait()

            @pl.when(t + FFN_AHEAD < nact)
            def _():
                x_copy(t + FFN_AHEAD).start()

            gu = jnp.dot(xbuf[t % (FFN_AHEAD + 1)], wgu_s[...], preferred_element_type=F32)
            act = _silu(gu[:, :D_EXPERT]) * gu[:, D_EXPERT:]
            y = jnp.dot(act.astype(BF16), wd_s[...], preferred_element_type=F32)

            @pl.when(t >= FFN_OUT)
            def _():
                y_copy(t - FFN_OUT).wait()

            ybuf[t % FFN_OUT] = y.astype(BF16)
            y_copy(t).start()
            return carry
        lax.fori_loop(t0, t1, tile, 0)

    @pl.when(e == pl.num_programs(0) - 1)
    def _():
        for back in range(FFN_OUT, 0, -1):
            @pl.when(nact >= back)
            def _():
                y_copy(nact - back).wait()


def _expert_ffn(tile_start, nact, xs, n_tiles, w_gate, w_up, w_down):
    n_rows, d = n_tiles * TM_FFN, xs.shape[1]
    f = D_EXPERT
    grid_spec = pltpu.PrefetchScalarGridSpec(
        num_scalar_prefetch=2,
        grid=(N_EXPERTS,),
        in_specs=[pl.BlockSpec((None, d, f), lambda e, ts, na: (e, 0, 0)),
                  pl.BlockSpec((None, d, f), lambda e, ts, na: (e, 0, 0)),
                  pl.BlockSpec((None, f, d), lambda e, ts, na: (e, 0, 0)),
                  pl.BlockSpec(memory_space=pl.ANY)],
        out_specs=pl.BlockSpec(memory_space=pl.ANY),
        scratch_shapes=[pltpu.VMEM((FFN_AHEAD + 1, TM_FFN, d), BF16),
                        pltpu.VMEM((FFN_OUT, TM_FFN, d), BF16),
                        pltpu.VMEM((d, 2 * f), BF16), pltpu.VMEM((f, d), BF16),
                        pltpu.SemaphoreType.DMA((FFN_AHEAD + 1,)),
                        pltpu.SemaphoreType.DMA((FFN_OUT,))],
    )
    return pl.pallas_call(
        _ffn_kernel,
        grid_spec=grid_spec,
        out_shape=jax.ShapeDtypeStruct((n_rows, d), BF16),
        compiler_params=pltpu.CompilerParams(
            dimension_semantics=("arbitrary",), vmem_limit_bytes=VMEM_LIMIT),
        name="moe_experts",
    )(tile_start, nact, w_gate, w_up, w_down, xs)


def _combine_kernel(ngrp_ref, gsrc_ref, ys_ref, tm_ref, brow_ref, bcol_ref, h_ref, wsg_ref,
                    wsu_ref, wsd_ref, x1_ref, mod_ref, gpost_ref, o_ref, buf_ref, p_ref, y_ref,
                    sem):
    j = pl.program_id(0)
    last = pl.num_programs(0) - 1
    slot = j % 2

    def chunk_copy(s, c, g):
        return pltpu.make_async_copy(ys_ref.at[g], buf_ref.at[s, c], sem.at[s])

    def fetch_block(jj, s):
        def issue_group(gi, carry):
            for k in range(GROUP_CHUNKS):
                c = gi * GROUP_CHUNKS + k
                chunk_copy(s, c, gsrc_ref[jj * BLOCK_CHUNKS + c]).start()
            return carry
        lax.fori_loop(0, ngrp_ref[jj], issue_group, 0)

    @pl.when(j == 0)
    def _():
        buf_ref[...] = jnp.zeros_like(buf_ref)
        fetch_block(0, 0)

    @pl.when(j < last)
    def _():
        fetch_block(j + 1, 1 - slot)

    h = h_ref[...]
    act = _silu(jnp.dot(h, wsg_ref[...], preferred_element_type=F32)) * jnp.dot(
        h, wsu_ref[...], preferred_element_type=F32)
    y = jnp.dot(act.astype(BF16), wsd_ref[...], preferred_element_type=F32)

    tm = tm_ref[...]
    lane = lax.broadcasted_iota(I32, tm.shape, 1)
    rank_t = jnp.where(lane < N_EXPERTS, tm, 0.0).astype(BF16)
    gate_t = jnp.where(lane < N_EXPERTS, 0.0, tm).astype(BF16)
    lane8 = lax.broadcasted_iota(I32, (SUBLANES, 2 * N_EXPERTS), 1)
    lo_row = jnp.where(lane8 < N_EXPERTS, brow_ref[0], 0.0).astype(BF16)
    lo_col = jnp.concatenate([bcol_ref[0]] * (KC // LANES), axis=1)
    hi_col = jnp.concatenate([bcol_ref[1]] * (KC // LANES), axis=1)
    def build_gate_columns(first_cb, last_cb):
        for cb in range(first_cb, last_cb):
            rows = (lax.broadcasted_iota(I32, (2 * N_EXPERTS, KC), 1) + cb * KC).astype(F32)
            in_run = jnp.where(jnp.logical_and(rows >= lo_col, rows < hi_col),
                               1.0, 0.0).astype(BF16)
            sel_rank = jnp.dot(rank_t, in_run, preferred_element_type=F32)
            sel_gate = jnp.dot(gate_t, in_run, preferred_element_type=F32)
            sel_lo = jnp.dot(lo_row, in_run, preferred_element_type=F32)
            local = rows[0:1, :] - sel_lo[0:1, :]
            p_ref[:, cb * KC:(cb + 1) * KC] = jnp.where(sel_rank == local, sel_gate,
                                                        0.0).astype(BF16)

    def weighted_rows(first_row, last_row):
        slabs = buf_ref[slot, first_row // ROW_CHUNK:last_row // ROW_CHUNK]
        return jnp.dot(p_ref[:, first_row:last_row],
                       slabs.reshape(last_row - first_row, buf_ref.shape[3]),
                       preferred_element_type=F32)

    build_gate_columns(0, COMMON_ROWS // KC)

    def wait_group(gi, carry):
        pltpu.make_async_copy(ys_ref.at[pl.ds(0, GROUP_CHUNKS)],
                              buf_ref.at[slot, pl.ds(0, GROUP_CHUNKS)], sem.at[slot]).wait()
        return carry
    lax.fori_loop(0, ngrp_ref[j], wait_group, 0)

    y_ref[...] = y + weighted_rows(0, COMMON_ROWS)

    @pl.when(ngrp_ref[j] * GROUP_ROWS > COMMON_ROWS)
    def _():
        build_gate_columns(COMMON_ROWS // KC, BLOCK_ROWS // KC)
        y_ref[...] += weighted_rows(COMMON_ROWS, BLOCK_ROWS)

    y = y_ref[...]

    ga2 = mod_ref[5:6, :]
    o_ref[...] = x1_ref[...] + ga2 * _rms(y, gpost_ref[...])


def _combine(plan, ys, tm, h2_flat, ws_gate, ws_up, ws_down, x1_flat, mod3, g_post, s):
    t, d = h2_flat.shape
    f = ws_gate.shape[1]
    blocks_per_seq = s // TB
    full = lambda shape: pl.BlockSpec(shape, lambda j, *_: (0,) * len(shape))
    grid_spec = pltpu.PrefetchScalarGridSpec(
        num_scalar_prefetch=2,
        grid=(t // TB,),
        in_specs=[pl.BlockSpec(memory_space=pl.ANY),
                  pl.BlockSpec((TB, 2 * N_EXPERTS), lambda j, *_: (j, 0))] + _bounds_specs() + [
                  pl.BlockSpec((TB, d), lambda j, *_: (j, 0)),
                  full((d, f)), full((d, f)), full((f, d)),
                  pl.BlockSpec((TB, d), lambda j, *_: (j, 0)),
                  pl.BlockSpec((None, 6, d), lambda j, *_: (j // blocks_per_seq, 0, 0)),
                  full((1, d))],
        out_specs=pl.BlockSpec((TB, d), lambda j, *_: (j, 0)),
        scratch_shapes=[pltpu.VMEM((2, BLOCK_CHUNKS, ROW_CHUNK, d), BF16),
                        pltpu.VMEM((TB, BLOCK_ROWS), BF16),
                        pltpu.VMEM((TB, d), F32),
                        pltpu.SemaphoreType.DMA((2,))],
    )
    return pl.pallas_call(
        _combine_kernel,
        grid_spec=grid_spec,
        out_shape=jax.ShapeDtypeStruct((t, d), F32),
        compiler_params=pltpu.CompilerParams(
            dimension_semantics=("arbitrary",), vmem_limit_bytes=VMEM_LIMIT),
        name="moe_combine",
    )(plan["ngrp"], plan["gsrc"], ys.reshape(-1, ROW_CHUNK, d), tm, plan["bounds_row"],
      plan["bounds_col"], h2_flat, ws_gate, ws_up, ws_down, x1_flat, mod3, g_post)


def kernel(x, c, w_ada, b_ada, g_pre_mix, w_in, ln_sgu_g, ln_sgu_b, w_spatial, b_spatial,
           g_branch, w_out, g_post_mix, g_pre_ffn, w_router, router_bias, w_gate, w_up, w_down,
           ws_gate, ws_up, ws_down, g_post_ffn):
    bsz, s, d = x.shape
    t = bsz * s
    nblk = t // TB
    max_rows = t * TOP_K + nblk * N_EXPERTS * (ROW_CHUNK - 1) + N_EXPERTS * (TM_FFN - 1)
    n_tiles = -(-max_rows // TM_FFN)
    row = lambda a: a.reshape(1, -1)
    for l in range(w_ada.shape[0]):
        mod3 = _ada(c, w_ada[l], b_ada[l]).reshape(bsz, 6, d)
        og, q, k, v = _premix(x, mod3, row(g_pre_mix[l]), w_in[l].astype(BF16),
                              row(ln_sgu_g[l]), row(ln_sgu_b[l]), w_spatial[l],
                              b_spatial[l].T, row(g_branch[l, :D_GMLP]))
        osb = _attention(q, k, v)
        x1, h2, logits_t = _postmix(og, osb, x, mod3, row(g_branch[l, D_GMLP:]),
                                    w_out[l].astype(BF16), row(g_post_mix[l]),
                                    row(g_pre_ffn[l]), w_router[l].T)
        em, tm, cnt = _route(logits_t, router_bias[l])
        plan = _dispatch_plan(cnt[:, :, 0].astype(I32), n_tiles)
        h2_flat = h2.reshape(t, d)
        xs = _dispatch(plan, h2_flat, em, n_tiles * TM_FFN + nblk * GROUP_ROWS)
        ys = _expert_ffn(plan["tile_start"], plan["nact"], xs, n_tiles,
                         w_gate[l], w_up[l], w_down[l])
        out = _combine(plan, ys, tm, h2_flat, ws_gate[l].astype(BF16), ws_up[l].astype(BF16),
                       ws_down[l].astype(BF16), x1.reshape(t, d), mod3, row(g_post_ffn[l]), s)
        x = out.reshape(bsz, s, d)
    return x
```

```python
import jax
import jax.numpy as jnp
from jax import lax
from jax.experimental import pallas as pl
from jax.experimental.pallas import tpu as pltpu

F32 = jnp.float32
BF16 = jnp.bfloat16
I32 = jnp.int32

LANES = 128
SUBLANES = 8
V7X_VMEM_BYTES = 64 * 1024 * 1024

D_MODEL = 1024
D_GMLP = 512
GMLP_GROUPS = 4
GMLP_BLOCK = 128
CHUNK = 64
D_SB = 512
SB_HEAD_DIM = 64
N_PAIRS = D_SB // LANES
N_EXPERTS = 64
N_GROUPS = 8
GROUP_SIZE = N_EXPERTS // N_GROUPS
TOPK_GROUPS = 4
TOP_K = 8
D_EXPERT = 256
ROUTED_SCALE = 2.5
EPS = 1e-6
D_IN = 2 * D_GMLP + 3 * D_SB

ADA_TN = 1536
TM_MIX = 512
TQ = 128
VMEM_LIMIT = V7X_VMEM_BYTES * 7 // 8
ATTN_DEAD_LOG = -110.0

TB = 256
ROUTE_BLOCKS = 4
ROW_CHUNK = 16
KC = 256
GROUP_CHUNKS = 32
GROUP_ROWS = GROUP_CHUNKS * ROW_CHUNK
BLOCK_ROWS = -(-(TB * TOP_K + N_EXPERTS * (ROW_CHUNK - 1)) // GROUP_ROWS) * GROUP_ROWS
BLOCK_CHUNKS = BLOCK_ROWS // ROW_CHUNK
COMMON_ROWS = -(-(TB * TOP_K + N_EXPERTS * ROW_CHUNK // 2) // GROUP_ROWS) * GROUP_ROWS
TM_FFN = 1024
FFN_AHEAD = 3
FFN_OUT = 3
ZERO_ROWS = 128


def _rms(x, g):
    return x * lax.rsqrt(jnp.mean(x * x, axis=-1, keepdims=True) + EPS) * g


def _silu(x):
    return x * jax.nn.sigmoid(x)


def _ada_kernel(c_ref, w_ref, b_ref, o_ref):
    o_ref[...] = jnp.dot(_silu(c_ref[...]), w_ref[...], preferred_element_type=F32,
                         precision=lax.Precision.HIGHEST) + b_ref[...]


def _ada(c, w, b):
    bsz, d = c.shape
    n = w.shape[1]
    tn = ADA_TN
    return pl.pallas_call(
        _ada_kernel,
        grid=(n // tn,),
        in_specs=[pl.BlockSpec((bsz, d), lambda j: (0, 0)),
                  pl.BlockSpec((d, tn), lambda j: (0, j)),
                  pl.BlockSpec((1, tn), lambda j: (0, j))],
        out_specs=pl.BlockSpec((bsz, tn), lambda j: (0, j)),
        out_shape=jax.ShapeDtypeStruct((bsz, n), F32),
        name="ada_mod",
    )(c, w, b.reshape(1, n))


def _premix_kernel(x_ref, mod_ref, gpre_ref, win_ref, lng_ref, lnb_ref, wsp_ref, bsp_ref,
                   gbr_ref, og_ref, q_ref, k_ref, v_ref):
    x = x_ref[...]
    sh1 = mod_ref[0:1, :]
    sc1 = mod_ref[1:2, :]
    h = _rms(x, gpre_ref[...]) * (1.0 + sc1) + sh1
    proj = jnp.dot(h.astype(BF16), win_ref[...], preferred_element_type=F32)

    u = jax.nn.gelu(proj[:, :D_GMLP])
    v = jax.nn.gelu(proj[:, D_GMLP:2 * D_GMLP])
    mu = jnp.mean(v, axis=-1, keepdims=True)
    var = jnp.mean(jnp.square(v - mu), axis=-1, keepdims=True)
    v = ((v - mu) * lax.rsqrt(var + EPS) * lng_ref[...] + lnb_ref[...]).astype(BF16)

    i = lax.broadcasted_iota(I32, (GMLP_BLOCK, GMLP_BLOCK), 0)
    j = lax.broadcasted_iota(I32, (GMLP_BLOCK, GMLP_BLOCK), 1)
    causal = (j // CHUNK) <= (i // CHUNK)
    gd = D_GMLP // GMLP_GROUPS
    blocks = []
    for nb in range(x.shape[0] // GMLP_BLOCK):
        rows = slice(nb * GMLP_BLOCK, (nb + 1) * GMLP_BLOCK)
        cols = []
        for g in range(GMLP_GROUPS):
            w = jnp.where(causal, wsp_ref[g], 0.0).astype(BF16)
            mixed = jnp.dot(w, v[rows, g * gd:(g + 1) * gd], preferred_element_type=F32)
            cols.append(mixed + bsp_ref[:, g:g + 1])
        blocks.append(u[rows, :] * jnp.concatenate(cols, axis=1))
    og = jnp.concatenate(blocks, axis=0)
    og_ref[...] = _rms(og, gbr_ref[...]).astype(BF16)

    base = 2 * D_GMLP
    scale = SB_HEAD_DIM ** -0.5
    for p in range(N_PAIRS):
        q_ref[p] = (proj[:, base + LANES * p:base + LANES * (p + 1)] * scale).astype(BF16)
        k_ref[p] = proj[:, base + D_SB + LANES * p:base + D_SB + LANES * (p + 1)].astype(BF16)
        v_ref[p] = proj[:, base + 2 * D_SB + LANES * p:base + 2 * D_SB + LANES * (p + 1)].astype(BF16)


def _premix(x, mod3, g_pre, w_in, ln_g, ln_b, w_sp, b_sp_t, g_br):
    bsz, s, d = x.shape
    tm = TM_MIX
    full = lambda shape: pl.BlockSpec(shape, lambda b, i: (0,) * len(shape))
    qkv_spec = pl.BlockSpec((None, N_PAIRS, tm, LANES), lambda b, i: (b, 0, i, 0))
    qkv_shape = jax.ShapeDtypeStruct((bsz, N_PAIRS, s, LANES), BF16)
    return pl.pallas_call(
        _premix_kernel,
        grid=(bsz, s // tm),
        in_specs=[pl.BlockSpec((None, tm, d), lambda b, i: (b, i, 0)),
                  pl.BlockSpec((None, 6, d), lambda b, i: (b, 0, 0)),
                  full((1, d)), full((d, D_IN)), full((1, D_GMLP)), full((1, D_GMLP)),
                  full((GMLP_GROUPS, GMLP_BLOCK, GMLP_BLOCK)), full((GMLP_BLOCK, GMLP_GROUPS)),
                  full((1, D_GMLP))],
        out_specs=[pl.BlockSpec((None, tm, D_GMLP), lambda b, i: (b, i, 0)),
                   qkv_spec, qkv_spec, qkv_spec],
        out_shape=[jax.ShapeDtypeStruct((bsz, s, D_GMLP), BF16), qkv_shape, qkv_shape, qkv_shape],
        compiler_params=pltpu.CompilerParams(
            dimension_semantics=("parallel", "parallel"), vmem_limit_bytes=VMEM_LIMIT),
        name="premix",
    )(x, mod3, g_pre, w_in, ln_g, ln_b, w_sp, b_sp_t, g_br)


def _attn_kernel(q_ref, k_ref, v_ref, o_ref, *scratch):
    qs_refs = scratch[:N_PAIRS]
    acc_refs = scratch[N_PAIRS:2 * N_PAIRS]
    carry_refs = scratch[2 * N_PAIRS:]
    qi = pl.program_id(1)
    first_head = lax.broadcasted_iota(I32, (TQ, LANES), 1) < SB_HEAD_DIM
    for p in range(N_PAIRS):
        q2 = q_ref[p]
        zero = jnp.zeros_like(q2)
        qs_refs[p][:TQ, :] = jnp.where(first_head, q2, zero)
        qs_refs[p][TQ:, :] = jnp.where(first_head, zero, q2)
        acc_refs[p][...] = jnp.zeros_like(acc_refs[p])
        carry_refs[p][...] = jnp.zeros_like(carry_refs[p])

    r = jnp.bitwise_and(lax.broadcasted_iota(I32, (2 * TQ, TQ), 0), TQ - 1)
    c = lax.broadcasted_iota(I32, (2 * TQ, TQ), 1)
    strict_causal = c < r
    kr = lax.broadcasted_iota(I32, (TQ, TQ), 0)
    kc = lax.broadcasted_iota(I32, (TQ, TQ), 1)
    suffix = jnp.concatenate([(kr > kc).astype(BF16), jnp.ones((TQ, TQ), BF16)], axis=1)
    suffix2 = jnp.concatenate([suffix, suffix], axis=0)

    def key_block(j, diagonal):
        start = pl.multiple_of(j * TQ, TQ)
        pairs = range(N_PAIRS)
        zs = [lax.dot_general(qs_refs[p][...], k_ref[p, pl.ds(start, TQ), :],
                              (((1,), (1,)), ((), ())), preferred_element_type=F32)
              for p in pairs]
        log_betas, splits = [], []
        for p in pairs:
            z = zs[p]
            log_beta = jnp.minimum(z, 0.0) - jnp.log(1.0 + jnp.exp(-jnp.abs(z)))
            log_1mb = log_beta - z
            if diagonal:
                log_1mb = jnp.where(strict_causal, log_1mb, 0.0)
            hi = log_1mb.astype(BF16)
            lo = (log_1mb - hi.astype(F32)).astype(BF16)
            log_betas.append(log_beta)
            splits.append(jnp.concatenate([hi, lo], axis=1))
        sums = [jnp.dot(splits[p], suffix2, preferred_element_type=F32) for p in pairs]
        weights = []
        live = None
        for p in pairs:
            s = sums[p]
            carry = carry_refs[p][...]
            a = jnp.exp(log_betas[p] + carry + s[:, :TQ])
            if diagonal:
                a = jnp.where(strict_causal, a, 0.0)
            weights.append(a.astype(BF16))
            carry = carry + s[:, TQ:]
            carry_refs[p][...] = carry
            live = carry if live is None else jnp.maximum(live, carry)
        for p in pairs:
            acc_refs[p][...] += jnp.dot(weights[p], v_ref[p, pl.ds(start, TQ), :],
                                        preferred_element_type=F32)
        return jnp.max(live)

    live = key_block(qi, True)

    def cond(state):
        j, live = state
        return jnp.logical_and(j >= 0, live > ATTN_DEAD_LOG)

    def body(state):
        j, _ = state
        return j - 1, key_block(j, False)

    lax.while_loop(cond, body, (qi - 1, live))
    for p in range(N_PAIRS):
        o_ref[:, LANES * p:LANES * (p + 1)] = jnp.where(first_head, acc_refs[p][:TQ, :],
                                                    acc_refs[p][TQ:, :])


def _attention(q, k, v):
    bsz, npair, s, _ = q.shape
    kv_spec = pl.BlockSpec((None, npair, s, LANES), lambda b, i: (b, 0, 0, 0))
    return pl.pallas_call(
        _attn_kernel,
        grid=(bsz, s // TQ),
        in_specs=[pl.BlockSpec((None, npair, TQ, LANES), lambda b, i: (b, 0, i, 0)),
                  kv_spec, kv_spec],
        out_specs=pl.BlockSpec((None, TQ, npair * LANES), lambda b, i: (b, i, 0)),
        out_shape=jax.ShapeDtypeStruct((bsz, s, npair * LANES), F32),
        scratch_shapes=([pltpu.VMEM((2 * TQ, LANES), BF16)] * npair
                        + [pltpu.VMEM((2 * TQ, LANES), F32)] * npair
                        + [pltpu.VMEM((2 * TQ, TQ), F32)] * npair),
        compiler_params=pltpu.CompilerParams(
            dimension_semantics=("parallel", "parallel"), vmem_limit_bytes=VMEM_LIMIT),
        name="stickbreak_attn",
    )(q, k, v)


def _postmix_kernel(og_ref, osb_ref, x_ref, mod_ref, gbr_ref, wout_ref, gpost_ref, gpre_ref,
                    wrt_ref, x1_ref, h2_ref, logit_ref):
    ga1 = mod_ref[2:3, :]
    sh2 = mod_ref[3:4, :]
    sc2 = mod_ref[4:5, :]
    osb = _rms(osb_ref[...], gbr_ref[...]).astype(BF16)
    m = (jnp.dot(og_ref[...], wout_ref[:D_GMLP, :], preferred_element_type=F32)
         + jnp.dot(osb, wout_ref[D_GMLP:, :], preferred_element_type=F32))
    x1 = x_ref[...] + ga1 * _rms(m, gpost_ref[...])
    x1_ref[...] = x1
    h2 = _rms(x1, gpre_ref[...]) * (1.0 + sc2) + sh2
    h_hi = h2.astype(BF16)
    h2_ref[...] = h_hi
    h_lo = (h2 - h_hi.astype(F32)).astype(BF16)
    w = wrt_ref[...]
    w_hi = w.astype(BF16)
    w_lo = (w - w_hi.astype(F32)).astype(BF16)
    nt = (((1,), (1,)), ((), ()))
    by_hi = lax.dot_general(jnp.concatenate([w_hi, w_lo], axis=0), h_hi, nt,
                            preferred_element_type=F32)
    by_lo = lax.dot_general(w_hi, h_lo, nt, preferred_element_type=F32)
    logit_ref[...] = by_hi[:N_EXPERTS] + by_hi[N_EXPERTS:] + by_lo


def _postmix(og, osb, x, mod3, g_br, w_out, g_post, g_pre, w_router_t):
    bsz, s, d = x.shape
    tm = TM_MIX
    nt = s // tm
    full = lambda shape: pl.BlockSpec(shape, lambda b, i: (0,) * len(shape))
    return pl.pallas_call(
        _postmix_kernel,
        grid=(bsz, nt),
        in_specs=[pl.BlockSpec((None, tm, D_GMLP), lambda b, i: (b, i, 0)),
                  pl.BlockSpec((None, tm, D_SB), lambda b, i: (b, i, 0)),
                  pl.BlockSpec((None, tm, d), lambda b, i: (b, i, 0)),
                  pl.BlockSpec((None, 6, d), lambda b, i: (b, 0, 0)),
                  full((1, D_SB)), full((d, d)), full((1, d)), full((1, d)),
                  full((N_EXPERTS, d))],
        out_specs=[pl.BlockSpec((None, tm, d), lambda b, i: (b, i, 0)),
                   pl.BlockSpec((None, tm, d), lambda b, i: (b, i, 0)),
                   pl.BlockSpec((N_EXPERTS, tm), lambda b, i: (0, b * nt + i))],
        out_shape=[jax.ShapeDtypeStruct((bsz, s, d), F32),
                   jax.ShapeDtypeStruct((bsz, s, d), BF16),
                   jax.ShapeDtypeStruct((N_EXPERTS, bsz * s), F32)],
        compiler_params=pltpu.CompilerParams(
            dimension_semantics=("parallel", "parallel"), vmem_limit_bytes=VMEM_LIMIT),
        name="postmix",
    )(og, osb, x, mod3, g_br, w_out, g_post, g_pre, w_router_t)


def _first_index_of_max(x, idx, axis, size):
    m = jnp.max(x, axis=axis, keepdims=True)
    return jnp.min(jnp.where(x == m, idx, size), axis=axis, keepdims=True)


def _route_kernel(logit_ref, bias_ref, em_ref, tm_ref, cnt_ref):
    for u in range(ROUTE_BLOCKS):
        cols = slice(u * TB, (u + 1) * TB)
        both, cnt = _route_block(logit_ref[:, cols], bias_ref[...])
        em_ref[:, cols] = both
        tm_ref[cols, :] = both.T
        cnt_ref[u] = jnp.broadcast_to(cnt, (N_EXPERTS, LANES))


def _route_block(logits, bias):
    scores = jax.nn.sigmoid(logits)
    biased = scores + bias
    neg = jnp.float32(-jnp.inf)

    grouped = biased.reshape(N_GROUPS, GROUP_SIZE, TB)
    within = lax.broadcasted_iota(I32, grouped.shape, 1)
    top1 = jnp.max(grouped, axis=1, keepdims=True)
    first = _first_index_of_max(grouped, within, 1, GROUP_SIZE)
    top2 = jnp.max(jnp.where(within == first, neg, grouped), axis=1, keepdims=True)
    group_score = (top1 + top2).reshape(N_GROUPS, TB)

    gidx = lax.broadcasted_iota(I32, group_score.shape, 0)
    group_on = jnp.zeros(group_score.shape, jnp.bool_)
    for _ in range(TOPK_GROUPS):
        pick = gidx == _first_index_of_max(group_score, gidx, 0, N_GROUPS)
        group_on = jnp.logical_or(group_on, pick)
        group_score = jnp.where(pick, neg, group_score)

    masked = jnp.where(group_on.reshape(N_GROUPS, 1, TB), grouped, neg).reshape(N_EXPERTS, TB)
    eidx = lax.broadcasted_iota(I32, masked.shape, 0)
    chosen = jnp.zeros(masked.shape, jnp.bool_)
    for _ in range(TOP_K):
        pick = eidx == _first_index_of_max(masked, eidx, 0, N_EXPERTS)
        chosen = jnp.logical_or(chosen, pick)
        masked = jnp.where(pick, neg, masked)

    w = jnp.where(chosen, scores, 0.0)
    gates = w / jnp.sum(w, axis=0, keepdims=True) * ROUTED_SCALE

    chosen_f = chosen.astype(F32)
    tr = lax.broadcasted_iota(I32, (TB, TB), 0)
    tc = lax.broadcasted_iota(I32, (TB, TB), 1)
    rank = jnp.dot(chosen_f.astype(BF16), (tr < tc).astype(BF16), preferred_element_type=F32)
    both = jnp.concatenate([jnp.where(chosen, rank, -1.0), gates], axis=0)
    return both, jnp.sum(chosen_f, axis=1, keepdims=True)


def _route(logits_t, bias):
    e, t = logits_t.shape
    nblk = t // TB
    return pl.pallas_call(
        _route_kernel,
        grid=(nblk // ROUTE_BLOCKS,),
        in_specs=[pl.BlockSpec((e, ROUTE_BLOCKS * TB), lambda i: (0, i)),
                  pl.BlockSpec((e, 1), lambda i: (0, 0))],
        out_specs=[pl.BlockSpec((2 * e, ROUTE_BLOCKS * TB), lambda i: (0, i)),
                   pl.BlockSpec((ROUTE_BLOCKS * TB, 2 * e), lambda i: (i, 0)),
                   pl.BlockSpec((ROUTE_BLOCKS, e, LANES), lambda i: (i, 0, 0))],
        out_shape=[jax.ShapeDtypeStruct((2 * e, t), F32),
                   jax.ShapeDtypeStruct((t, 2 * e), F32),
                   jax.ShapeDtypeStruct((nblk, e, LANES), F32)],
        compiler_params=pltpu.CompilerParams(dimension_semantics=("parallel",)),
        name="route",
    )(logits_t, bias.reshape(e, 1))


def _dispatch_plan(cnt, n_tiles):
    nblk, e = cnt.shape
    pc = (cnt + ROW_CHUNK - 1) // ROW_CHUNK * ROW_CHUNK
    start = jnp.cumsum(pc, axis=1) - pc
    nchunk = jnp.sum(pc, axis=1) // ROW_CHUNK
    off = jnp.cumsum(pc, axis=0) - pc
    ecount = jnp.sum(pc, axis=0)
    epad = (ecount + TM_FFN - 1) // TM_FFN * TM_FFN
    gend = jnp.cumsum(epad)
    gbase = gend - epad
    nact = gend[-1] // TM_FFN
    tile_start = jnp.concatenate([gbase, gend[-1:]]) // TM_FFN
    cidx = jnp.arange(BLOCK_CHUNKS, dtype=I32)
    start16 = start // ROW_CHUNK
    shift = (gbase[None, :] + off) // ROW_CHUNK - start16
    dshift = shift - jnp.pad(shift, ((0, 0), (1, 0)))[:, :-1]
    in_or_after = (start16[:, None, :] <= cidx[None, :, None]).astype(I32)
    where = cidx[None, :] + jnp.sum(in_or_after * dshift[:, None, :], axis=2)
    used = cidx[None, :] < nchunk[:, None]
    spare = (n_tiles * TM_FFN // ROW_CHUNK
             + jnp.arange(nblk, dtype=I32)[:, None] * GROUP_CHUNKS + cidx[None, :] % GROUP_CHUNKS)
    gdst = jnp.where(used, where, spare)
    gsrc = jnp.where(used, where, 0)
    ngrp = (nchunk + GROUP_CHUNKS - 1) // GROUP_CHUNKS
    zbase = (gbase + ecount) // ROW_CHUNK
    zn = (epad - ecount) // ROW_CHUNK

    lo = jnp.tile(start.astype(F32), (1, 2))
    hi = jnp.tile((start + pc).astype(F32), (1, 2))
    bounds_row = jnp.stack([jnp.broadcast_to(lo[:, None, :], (nblk, SUBLANES, 2 * e)),
                            jnp.broadcast_to(hi[:, None, :], (nblk, SUBLANES, 2 * e))], axis=1)
    bounds_col = jnp.stack([jnp.broadcast_to(lo[:, :, None], (nblk, 2 * e, LANES)),
                            jnp.broadcast_to(hi[:, :, None], (nblk, 2 * e, LANES))], axis=1)
    as_i32 = lambda a: a.astype(I32)
    return dict(ngrp=as_i32(ngrp), gdst=as_i32(gdst.reshape(-1)), gsrc=as_i32(gsrc.reshape(-1)),
                zn=as_i32(zn), zbase=as_i32(zbase), tile_start=as_i32(tile_start),
                nact=as_i32(nact.reshape(1)), bounds_row=bounds_row, bounds_col=bounds_col)


def _dispatch_kernel(ngrp_ref, gdst_ref, zn_ref, zbase_ref, h_ref, em_ref, brow_ref,
                     xs_ref, buf_ref, zero_ref, sem, zsem):
    j = pl.program_id(0)
    last = pl.num_programs(0) - 1
    slot = j % 2

    def chunk_copy(s, c, g):
        return pltpu.make_async_copy(buf_ref.at[s, c], xs_ref.at[g], sem.at[s])

    def wait_block(jj, s):
        def body(i, carry):
            pltpu.make_async_copy(buf_ref.at[s, pl.ds(0, GROUP_CHUNKS)],
                                  xs_ref.at[pl.ds(0, GROUP_CHUNKS)], sem.at[s]).wait()
            return carry
        lax.fori_loop(0, ngrp_ref[jj], body, 0)

    def zero_copy(g, chunks):
        return pltpu.make_async_copy(zero_ref.at[pl.ds(0, chunks)], xs_ref.at[pl.ds(g, chunks)],
                                     zsem.at[0])

    def zero_fill(start):
        big = ZERO_ROWS // ROW_CHUNK

        def per_expert(e, carry):
            n = zn_ref[e]
            base = zbase_ref[e]

            def big_copy(i, c2):
                cp = zero_copy(base + i * big, big)
                cp.start() if start else cp.wait()
                return c2
            lax.fori_loop(0, n // big, big_copy, 0)

            def small_copy(i, c2):
                cp = zero_copy(base + n // big * big + i, 1)
                cp.start() if start else cp.wait()
                return c2
            lax.fori_loop(0, n % big, small_copy, 0)
            return carry
        lax.fori_loop(0, N_EXPERTS, per_expert, 0)

    @pl.when(j == 0)
    def _():
        zero_ref[...] = jnp.zeros_like(zero_ref)
        zero_fill(True)

    @pl.when(j >= 2)
    def _():
        wait_block(j - 2, slot)

    lo = brow_ref[0, 0:1, :]
    hi = brow_ref[1, 0:1, :]
    first_copy = lax.broadcasted_iota(I32, (KC, 2 * N_EXPERTS), 1) < N_EXPERTS
    row_iota = lax.broadcasted_iota(I32, (KC, 2 * N_EXPERTS), 0)
    ranks = em_ref[...].astype(BF16)
    h = h_ref[...]

    def trip(ti, carry):
        onehots = []
        for u in range(GROUP_ROWS // KC):
            rows_e = (row_iota + ti * GROUP_ROWS + u * KC).astype(F32)
            in_run = jnp.logical_and(jnp.logical_and(rows_e >= lo, rows_e < hi), first_copy)
            sel = jnp.dot(jnp.where(in_run, 1.0, 0.0).astype(BF16), ranks,
                          preferred_element_type=F32)
            local = jnp.sum(jnp.where(in_run, rows_e - lo + 3.0, 0.0), axis=1,
                            keepdims=True) - 3.0
            onehots.append(jnp.where(sel == local, 1.0, 0.0).astype(BF16))
        xs = jnp.dot(jnp.concatenate(onehots, axis=0), h, preferred_element_type=F32)
        first = pl.multiple_of(ti * GROUP_CHUNKS, GROUP_CHUNKS)
        buf_ref[slot, pl.ds(first, GROUP_CHUNKS)] = xs.astype(BF16).reshape(
            GROUP_CHUNKS, ROW_CHUNK, xs.shape[1])
        for k in range(GROUP_CHUNKS):
            chunk_copy(slot, first + k, gdst_ref[j * BLOCK_CHUNKS + first + k]).start()
        return carry
    lax.fori_loop(0, ngrp_ref[j], trip, 0)

    @pl.when(j == last)
    def _():
        wait_block(j, slot)

        @pl.when(j >= 1)
        def _():
            wait_block(j - 1, 1 - slot)

        zero_fill(False)


def _bounds_specs():
    return [pl.BlockSpec((None, 2, SUBLANES, 2 * N_EXPERTS), lambda j, *_: (j, 0, 0, 0)),
            pl.BlockSpec((None, 2, 2 * N_EXPERTS, LANES), lambda j, *_: (j, 0, 0, 0))]


def _dispatch(plan, h2_flat, em, n_rows):
    t, d = h2_flat.shape
    grid_spec = pltpu.PrefetchScalarGridSpec(
        num_scalar_prefetch=4,
        grid=(t // TB,),
        in_specs=[pl.BlockSpec((TB, d), lambda j, *_: (j, 0)),
                  pl.BlockSpec((2 * N_EXPERTS, TB), lambda j, *_: (0, j))] + _bounds_specs()[:1],
        out_specs=pl.BlockSpec(memory_space=pl.ANY),
        scratch_shapes=[pltpu.VMEM((2, BLOCK_CHUNKS, ROW_CHUNK, d), BF16),
                        pltpu.VMEM((ZERO_ROWS // ROW_CHUNK, ROW_CHUNK, d), BF16),
                        pltpu.SemaphoreType.DMA((2,)),
                        pltpu.SemaphoreType.DMA((1,))],
    )
    xs = pl.pallas_call(
        _dispatch_kernel,
        grid_spec=grid_spec,
        out_shape=jax.ShapeDtypeStruct((n_rows // ROW_CHUNK, ROW_CHUNK, d), BF16),
        compiler_params=pltpu.CompilerParams(
            dimension_semantics=("arbitrary",), vmem_limit_bytes=VMEM_LIMIT),
        name="moe_dispatch",
    )(plan["ngrp"], plan["gdst"], plan["zn"], plan["zbase"], h2_flat, em, plan["bounds_row"])
    return xs.reshape(n_rows, d)


def _ffn_kernel(tstart_ref, nact_ref, wg_ref, wu_ref, wd_ref, xs_ref, ys_ref,
                xbuf, ybuf, wgu_s, wd_s, xsem, ysem):
    e = pl.program_id(0)
    nact = nact_ref[0]
    t0 = tstart_ref[e]
    t1 = tstart_ref[e + 1]

    def x_copy(t):
        s = t % (FFN_AHEAD + 1)
        return pltpu.make_async_copy(
            xs_ref.at[pl.ds(pl.multiple_of(t * TM_FFN, TM_FFN), TM_FFN), :], xbuf.at[s],
            xsem.at[s])

    def y_copy(t):
        s = t % FFN_OUT
        return pltpu.make_async_copy(
            ybuf.at[s], ys_ref.at[pl.ds(pl.multiple_of(t * TM_FFN, TM_FFN), TM_FFN), :],
            ysem.at[s])

    @pl.when(e == 0)
    def _():
        for a in range(FFN_AHEAD):
            @pl.when(a < nact)
            def _():
                x_copy(a).start()

    @pl.when(t1 > t0)
    def _():
        wgu_s[:, :D_EXPERT] = wg_ref[...].astype(BF16)
        wgu_s[:, D_EXPERT:] = wu_ref[...].astype(BF16)
        wd_s[...] = wd_ref[...].astype(BF16)

        def tile(t, carry):
            x_copy(t).wait()

            @pl.when(t + FFN_AHEAD < nact)
            def _():
                x_copy(t + FFN_AHEAD).start()

            gu = jnp.dot(xbuf[t % (FFN_AHEAD + 1)], wgu_s[...], preferred_element_type=F32)
            act = _silu(gu[:, :D_EXPERT]) * gu[:, D_EXPERT:]
            y = jnp.dot(act.astype(BF16), wd_s[...], preferred_element_type=F32)

            @pl.when(t >= FFN_OUT)
            def _():
                y_copy(t - FFN_OUT).wait()

            ybuf[t % FFN_OUT] = y.astype(BF16)
            y_copy(t).start()
            return carry
        lax.fori_loop(t0, t1, tile, 0)

    @pl.when(e == pl.num_programs(0) - 1)
    def _():
        for back in range(FFN_OUT, 0, -1):
            @pl.when(nact >= back)
            def _():
                y_copy(nact - back).wait()


def _expert_ffn(tile_start, nact, xs, n_tiles, w_gate, w_up, w_down):
    n_rows, d = n_tiles * TM_FFN, xs.shape[1]
    f = D_EXPERT
    grid_spec = pltpu.PrefetchScalarGridSpec(
        num_scalar_prefetch=2,
        grid=(N_EXPERTS,),
        in_specs=[pl.BlockSpec((None, d, f), lambda e, ts, na: (e, 0, 0)),
                  pl.BlockSpec((None, d, f), lambda e, ts, na: (e, 0, 0)),
                  pl.BlockSpec((None, f, d), lambda e, ts, na: (e, 0, 0)),
                  pl.BlockSpec(memory_space=pl.ANY)],
        out_specs=pl.BlockSpec(memory_space=pl.ANY),
        scratch_shapes=[pltpu.VMEM((FFN_AHEAD + 1, TM_FFN, d), BF16),
                        pltpu.VMEM((FFN_OUT, TM_FFN, d), BF16),
                        pltpu.VMEM((d, 2 * f), BF16), pltpu.VMEM((f, d), BF16),
                        pltpu.SemaphoreType.DMA((FFN_AHEAD + 1,)),
                        pltpu.SemaphoreType.DMA((FFN_OUT,))],
    )
    return pl.pallas_call(
        _ffn_kernel,
        grid_spec=grid_spec,
        out_shape=jax.ShapeDtypeStruct((n_rows, d), BF16),
        compiler_params=pltpu.CompilerParams(
            dimension_semantics=("arbitrary",), vmem_limit_bytes=VMEM_LIMIT),
        name="moe_experts",
    )(tile_start, nact, w_gate, w_up, w_down, xs)


def _combine_kernel(ngrp_ref, gsrc_ref, ys_ref, tm_ref, brow_ref, bcol_ref, h_ref, wsg_ref,
                    wsu_ref, wsd_ref, x1_ref, mod_ref, gpost_ref, o_ref, buf_ref, p_ref, y_ref,
                    sem):
    j = pl.program_id(0)
    last = pl.num_programs(0) - 1
    slot = j % 2

    def chunk_copy(s, c, g):
        return pltpu.make_async_copy(ys_ref.at[g], buf_ref.at[s, c], sem.at[s])

    def fetch_block(jj, s):
        def issue_group(gi, carry):
            for k in range(GROUP_CHUNKS):
                c = gi * GROUP_CHUNKS + k
                chunk_copy(s, c, gsrc_ref[jj * BLOCK_CHUNKS + c]).start()
            return carry
        lax.fori_loop(0, ngrp_ref[jj], issue_group, 0)

    @pl.when(j == 0)
    def _():
        buf_ref[...] = jnp.zeros_like(buf_ref)
        fetch_block(0, 0)

    @pl.when(j < last)
    def _():
        fetch_block(j + 1, 1 - slot)

    h = h_ref[...]
    act = _silu(jnp.dot(h, wsg_ref[...], preferred_element_type=F32)) * jnp.dot(
        h, wsu_ref[...], preferred_element_type=F32)
    y = jnp.dot(act.astype(BF16), wsd_ref[...], preferred_element_type=F32)

    tm = tm_ref[...]
    lane = lax.broadcasted_iota(I32, tm.shape, 1)
    rank_t = jnp.where(lane < N_EXPERTS, tm, 0.0).astype(BF16)
    gate_t = jnp.where(lane < N_EXPERTS, 0.0, tm).astype(BF16)
    lane8 = lax.broadcasted_iota(I32, (SUBLANES, 2 * N_EXPERTS), 1)
    lo_row = jnp.where(lane8 < N_EXPERTS, brow_ref[0], 0.0).astype(BF16)
    lo_col = jnp.concatenate([bcol_ref[0]] * (KC // LANES), axis=1)
    hi_col = jnp.concatenate([bcol_ref[1]] * (KC // LANES), axis=1)
    def build_gate_columns(first_cb, last_cb):
        for cb in range(first_cb, last_cb):
            rows = (lax.broadcasted_iota(I32, (2 * N_EXPERTS, KC), 1) + cb * KC).astype(F32)
            in_run = jnp.where(jnp.logical_and(rows >= lo_col, rows < hi_col),
                               1.0, 0.0).astype(BF16)
            sel_rank = jnp.dot(rank_t, in_run, preferred_element_type=F32)
            sel_gate = jnp.dot(gate_t, in_run, preferred_element_type=F32)
            sel_lo = jnp.dot(lo_row, in_run, preferred_element_type=F32)
            local = rows[0:1, :] - sel_lo[0:1, :]
            p_ref[:, cb * KC:(cb + 1) * KC] = jnp.where(sel_rank == local, sel_gate,
                                                        0.0).astype(BF16)

    def weighted_rows(first_row, last_row):
        slabs = buf_ref[slot, first_row // ROW_CHUNK:last_row // ROW_CHUNK]
        return jnp.dot(p_ref[:, first_row:last_row],
                       slabs.reshape(last_row - first_row, buf_ref.shape[3]),
                       preferred_element_type=F32)

    build_gate_columns(0, COMMON_ROWS // KC)

    def wait_group(gi, carry):
        pltpu.make_async_copy(ys_ref.at[pl.ds(0, GROUP_CHUNKS)],
                              buf_ref.at[slot, pl.ds(0, GROUP_CHUNKS)], sem.at[slot]).wait()
        return carry
    lax.fori_loop(0, ngrp_ref[j], wait_group, 0)

    y_ref[...] = y + weighted_rows(0, COMMON_ROWS)

    @pl.when(ngrp_ref[j] * GROUP_ROWS > COMMON_ROWS)
    def _():
        build_gate_columns(COMMON_ROWS // KC, BLOCK_ROWS // KC)
        y_ref[...] += weighted_rows(COMMON_ROWS, BLOCK_ROWS)

    y = y_ref[...]

    ga2 = mod_ref[5:6, :]
    o_ref[...] = x1_ref[...] + ga2 * _rms(y, gpost_ref[...])


def _combine(plan, ys, tm, h2_flat, ws_gate, ws_up, ws_down, x1_flat, mod3, g_post, s):
    t, d = h2_flat.shape
    f = ws_gate.shape[1]
    blocks_per_seq = s // TB
    full = lambda shape: pl.BlockSpec(shape, lambda j, *_: (0,) * len(shape))
    grid_spec = pltpu.PrefetchScalarGridSpec(
        num_scalar_prefetch=2,
        grid=(t // TB,),
        in_specs=[pl.BlockSpec(memory_space=pl.ANY),
                  pl.BlockSpec((TB, 2 * N_EXPERTS), lambda j, *_: (j, 0))] + _bounds_specs() + [
                  pl.BlockSpec((TB, d), lambda j, *_: (j, 0)),
                  full((d, f)), full((d, f)), full((f, d)),
                  pl.BlockSpec((TB, d), lambda j, *_: (j, 0)),
                  pl.BlockSpec((None, 6, d), lambda j, *_: (j // blocks_per_seq, 0, 0)),
                  full((1, d))],
        out_specs=pl.BlockSpec((TB, d), lambda j, *_: (j, 0)),
        scratch_shapes=[pltpu.VMEM((2, BLOCK_CHUNKS, ROW_CHUNK, d), BF16),
                        pltpu.VMEM((TB, BLOCK_ROWS), BF16),
                        pltpu.VMEM((TB, d), F32),
                        pltpu.SemaphoreType.DMA((2,))],
    )
    return pl.pallas_call(
        _combine_kernel,
        grid_spec=grid_spec,
        out_shape=jax.ShapeDtypeStruct((t, d), F32),
        compiler_params=pltpu.CompilerParams(
            dimension_semantics=("arbitrary",), vmem_limit_bytes=VMEM_LIMIT),
        name="moe_combine",
    )(plan["ngrp"], plan["gsrc"], ys.reshape(-1, ROW_CHUNK, d), tm, plan["bounds_row"],
      plan["bounds_col"], h2_flat, ws_gate, ws_up, ws_down, x1_flat, mod3, g_post)


def kernel(x, c, w_ada, b_ada, g_pre_mix, w_in, ln_sgu_g, ln_sgu_b, w_spatial, b_spatial,
           g_branch, w_out, g_post_mix, g_pre_ffn, w_router, router_bias, w_gate, w_up, w_down,
           ws_gate, ws_up, ws_down, g_post_ffn):
    bsz, s, d = x.shape
    t = bsz * s
    nblk = t // TB
    max_rows = t * TOP_K + nblk * N_EXPERTS * (ROW_CHUNK - 1) + N_EXPERTS * (TM_FFN - 1)
    n_tiles = -(-max_rows // TM_FFN)
    row = lambda a: a.reshape(1, -1)
    for l in range(w_ada.shape[0]):
        mod3 = _ada(c, w_ada[l], b_ada[l]).reshape(bsz, 6, d)
        og, q, k, v = _premix(x, mod3, row(g_pre_mix[l]), w_in[l].astype(BF16),
                              row(ln_sgu_g[l]), row(ln_sgu_b[l]), w_spatial[l],
                              b_spatial[l].T, row(g_branch[l, :D_GMLP]))
        osb = _attention(q, k, v)
        x1, h2, logits_t = _postmix(og, osb, x, mod3, row(g_branch[l, D_GMLP:]),
                                    w_out[l].astype(BF16), row(g_post_mix[l]),
                                    row(g_pre_ffn[l]), w_router[l].T)
        em, tm, cnt = _route(logits_t, router_bias[l])
        plan = _dispatch_plan(cnt[:, :, 0].astype(I32), n_tiles)
        h2_flat = h2.reshape(t, d)
        xs = _dispatch(plan, h2_flat, em, n_tiles * TM_FFN + nblk * GROUP_ROWS)
        ys = _expert_ffn(plan["tile_start"], plan["nact"], xs, n_tiles,
                         w_gate[l], w_up[l], w_down[l])
        out = _combine(plan, ys, tm, h2_flat, ws_gate[l].astype(BF16), ws_up[l].astype(BF16),
                       ws_down[l].astype(BF16), x1.reshape(t, d), mod3, row(g_post_ffn[l]), s)
        x = out.reshape(bsz, s, d)
    return x
```

```python
import jax
import jax.numpy as jnp
from jax import lax
from jax.experimental import pallas as pl
from jax.experimental.pallas import tpu as pltpu

F32 = jnp.float32
BF16 = jnp.bfloat16
I32 = jnp.int32

LANES = 128
SUBLANES = 8
V7X_VMEM_BYTES = 64 * 1024 * 1024

D_MODEL = 1024
D_GMLP = 512
GMLP_GROUPS = 4
GMLP_BLOCK = 128
CHUNK = 64
D_SB = 512
SB_HEAD_DIM = 64
N_PAIRS = D_SB // LANES
N_EXPERTS = 64
N_GROUPS = 8
GROUP_SIZE = N_EXPERTS // N_GROUPS
TOPK_GROUPS = 4
TOP_K = 8
D_EXPERT = 256
ROUTED_SCALE = 2.5
EPS = 1e-6
D_IN = 2 * D_GMLP + 3 * D_SB

ADA_TN = 1536
TM_MIX = 512
TQ = 128
VMEM_LIMIT = V7X_VMEM_BYTES * 7 // 8
ATTN_DEAD_LOG = -110.0

TB = 256
ROUTE_BLOCKS = 4
ROW_CHUNK = 16
KC = 256
GROUP_CHUNKS = 32
GROUP_ROWS = GROUP_CHUNKS * ROW_CHUNK
BLOCK_ROWS = -(-(TB * TOP_K + N_EXPERTS * (ROW_CHUNK - 1)) // GROUP_ROWS) * GROUP_ROWS
BLOCK_CHUNKS = BLOCK_ROWS // ROW_CHUNK
COMMON_ROWS = -(-(TB * TOP_K + N_EXPERTS * ROW_CHUNK // 2) // GROUP_ROWS) * GROUP_ROWS
TM_FFN = 1024
FFN_AHEAD = 3
FFN_OUT = 3
ZERO_ROWS = 128


def _rms(x, g):
    return x * lax.rsqrt(jnp.mean(x * x, axis=-1, keepdims=True) + EPS) * g


def _silu(x):
    return x * jax.nn.sigmoid(x)


def _ada_kernel(c_ref, w_ref, b_ref, o_ref):
    o_ref[...] = jnp.dot(_silu(c_ref[...]), w_ref[...], preferred_element_type=F32,
                         precision=lax.Precision.HIGHEST) + b_ref[...]


def _ada(c, w, b):
    bsz, d = c.shape
    n = w.shape[1]
    tn = ADA_TN
    return pl.pallas_call(
        _ada_kernel,
        grid=(n // tn,),
        in_specs=[pl.BlockSpec((bsz, d), lambda j: (0, 0)),
                  pl.BlockSpec((d, tn), lambda j: (0, j)),
                  pl.BlockSpec((1, tn), lambda j: (0, j))],
        out_specs=pl.BlockSpec((bsz, tn), lambda j: (0, j)),
        out_shape=jax.ShapeDtypeStruct((bsz, n), F32),
        name="ada_mod",
    )(c, w, b.reshape(1, n))


def _premix_kernel(x_ref, mod_ref, gpre_ref, win_ref, lng_ref, lnb_ref, wsp_ref, bsp_ref,
                   gbr_ref, og_ref, q_ref, k_ref, v_ref, win_s):
    @pl.when(jnp.logical_and(pl.program_id(0) == 0, pl.program_id(1) == 0))
    def _():
        win_s[...] = win_ref[...].astype(BF16)

    x = x_ref[...]
    sh1 = mod_ref[0:1, :]
    sc1 = mod_ref[1:2, :]
    h = _rms(x, gpre_ref[...]) * (1.0 + sc1) + sh1
    proj = jnp.dot(h.astype(BF16), win_s[...], preferred_element_type=F32)

    u = jax.nn.gelu(proj[:, :D_GMLP])
    v = jax.nn.gelu(proj[:, D_GMLP:2 * D_GMLP])
    mu = jnp.mean(v, axis=-1, keepdims=True)
    var = jnp.mean(jnp.square(v - mu), axis=-1, keepdims=True)
    v = ((v - mu) * lax.rsqrt(var + EPS) * lng_ref[...] + lnb_ref[...]).astype(BF16)

    i = lax.broadcasted_iota(I32, (GMLP_BLOCK, GMLP_BLOCK), 0)
    j = lax.broadcasted_iota(I32, (GMLP_BLOCK, GMLP_BLOCK), 1)
    causal = (j // CHUNK) <= (i // CHUNK)
    gd = D_GMLP // GMLP_GROUPS
    blocks = []
    for nb in range(x.shape[0] // GMLP_BLOCK):
        rows = slice(nb * GMLP_BLOCK, (nb + 1) * GMLP_BLOCK)
        cols = []
        for g in range(GMLP_GROUPS):
            w = jnp.where(causal, wsp_ref[g], 0.0).astype(BF16)
            mixed = jnp.dot(w, v[rows, g * gd:(g + 1) * gd], preferred_element_type=F32)
            cols.append(mixed + bsp_ref[:, g:g + 1])
        blocks.append(u[rows, :] * jnp.concatenate(cols, axis=1))
    og = jnp.concatenate(blocks, axis=0)
    og_ref[...] = _rms(og, gbr_ref[...]).astype(BF16)

    base = 2 * D_GMLP
    scale = SB_HEAD_DIM ** -0.5
    for p in range(N_PAIRS):
        q_ref[p] = (proj[:, base + LANES * p:base + LANES * (p + 1)] * scale).astype(BF16)
        k_ref[p] = proj[:, base + D_SB + LANES * p:base + D_SB + LANES * (p + 1)].astype(BF16)
        v_ref[p] = proj[:, base + 2 * D_SB + LANES * p:base + 2 * D_SB + LANES * (p + 1)].astype(BF16)


def _premix(x, mod3, g_pre, w_in, ln_g, ln_b, w_sp, b_sp_t, g_br):
    bsz, s, d = x.shape
    tm = TM_MIX
    full = lambda shape: pl.BlockSpec(shape, lambda b, i: (0,) * len(shape))
    qkv_spec = pl.BlockSpec((None, N_PAIRS, tm, LANES), lambda b, i: (b, 0, i, 0))
    qkv_shape = jax.ShapeDtypeStruct((bsz, N_PAIRS, s, LANES), BF16)
    return pl.pallas_call(
        _premix_kernel,
        grid=(bsz, s // tm),
        in_specs=[pl.BlockSpec((None, tm, d), lambda b, i: (b, i, 0)),
                  pl.BlockSpec((None, 6, d), lambda b, i: (b, 0, 0)),
                  full((1, d)), full((d, D_IN)), full((1, D_GMLP)), full((1, D_GMLP)),
                  full((GMLP_GROUPS, GMLP_BLOCK, GMLP_BLOCK)), full((GMLP_BLOCK, GMLP_GROUPS)),
                  full((1, D_GMLP))],
        out_specs=[pl.BlockSpec((None, tm, D_GMLP), lambda b, i: (b, i, 0)),
                   qkv_spec, qkv_spec, qkv_spec],
        out_shape=[jax.ShapeDtypeStruct((bsz, s, D_GMLP), BF16), qkv_shape, qkv_shape, qkv_shape],
        scratch_shapes=[pltpu.VMEM((d, D_IN), BF16)],
        compiler_params=pltpu.CompilerParams(
            dimension_semantics=("arbitrary", "arbitrary"), vmem_limit_bytes=VMEM_LIMIT),
        name="premix",
    )(x, mod3, g_pre, w_in, ln_g, ln_b, w_sp, b_sp_t, g_br)


def _attn_kernel(q_ref, k_ref, v_ref, o_ref, *scratch):
    qs_refs = scratch[:N_PAIRS]
    acc_refs = scratch[N_PAIRS:2 * N_PAIRS]
    carry_refs = scratch[2 * N_PAIRS:]
    qi = pl.program_id(1)
    first_head = lax.broadcasted_iota(I32, (TQ, LANES), 1) < SB_HEAD_DIM
    for p in range(N_PAIRS):
        q2 = q_ref[p]
        zero = jnp.zeros_like(q2)
        qs_refs[p][:TQ, :] = jnp.where(first_head, q2, zero)
        qs_refs[p][TQ:, :] = jnp.where(first_head, zero, q2)
        acc_refs[p][...] = jnp.zeros_like(acc_refs[p])
        carry_refs[p][...] = jnp.zeros_like(carry_refs[p])

    r = jnp.bitwise_and(lax.broadcasted_iota(I32, (2 * TQ, TQ), 0), TQ - 1)
    c = lax.broadcasted_iota(I32, (2 * TQ, TQ), 1)
    strict_causal = c < r
    kr = lax.broadcasted_iota(I32, (TQ, TQ), 0)
    kc = lax.broadcasted_iota(I32, (TQ, TQ), 1)
    suffix = jnp.concatenate([(kr > kc).astype(BF16), jnp.ones((TQ, TQ), BF16)], axis=1)
    suffix2 = jnp.concatenate([suffix, suffix], axis=0)

    def key_block(j, diagonal):
        start = pl.multiple_of(j * TQ, TQ)
        pairs = range(N_PAIRS)
        zs = [lax.dot_general(qs_refs[p][...], k_ref[p, pl.ds(start, TQ), :],
                              (((1,), (1,)), ((), ())), preferred_element_type=F32)
              for p in pairs]
        log_betas, splits = [], []
        for p in pairs:
            z = zs[p]
            log_beta = jnp.minimum(z, 0.0) - jnp.log(1.0 + jnp.exp(-jnp.abs(z)))
            log_1mb = log_beta - z
            if diagonal:
                log_1mb = jnp.where(strict_causal, log_1mb, 0.0)
            hi = log_1mb.astype(BF16)
            lo = (log_1mb - hi.astype(F32)).astype(BF16)
            log_betas.append(log_beta)
            splits.append(jnp.concatenate([hi, lo], axis=1))
        sums = [jnp.dot(splits[p], suffix2, preferred_element_type=F32) for p in pairs]
        weights = []
        live = None
        for p in pairs:
            s = sums[p]
            carry = carry_refs[p][...]
            a = jnp.exp(log_betas[p] + carry + s[:, :TQ])
            if diagonal:
                a = jnp.where(strict_causal, a, 0.0)
            weights.append(a.astype(BF16))
            carry = carry + s[:, TQ:]
            carry_refs[p][...] = carry
            live = carry if live is None else jnp.maximum(live, carry)
        for p in pairs:
            acc_refs[p][...] += jnp.dot(weights[p], v_ref[p, pl.ds(start, TQ), :],
                                        preferred_element_type=F32)
        return jnp.max(live)

    live = key_block(qi, True)

    def cond(state):
        j, live = state
        return jnp.logical_and(j >= 0, live > ATTN_DEAD_LOG)

    def body(state):
        j, _ = state
        return j - 1, key_block(j, False)

    lax.while_loop(cond, body, (qi - 1, live))
    for p in range(N_PAIRS):
        o_ref[:, LANES * p:LANES * (p + 1)] = jnp.where(first_head, acc_refs[p][:TQ, :],
                                                    acc_refs[p][TQ:, :])


def _attention(q, k, v):
    bsz, npair, s, _ = q.shape
    kv_spec = pl.BlockSpec((None, npair, s, LANES), lambda b, i: (b, 0, 0, 0))
    return pl.pallas_call(
        _attn_kernel,
        grid=(bsz, s // TQ),
        in_specs=[pl.BlockSpec((None, npair, TQ, LANES), lambda b, i: (b, 0, i, 0)),
                  kv_spec, kv_spec],
        out_specs=pl.BlockSpec((None, TQ, npair * LANES), lambda b, i: (b, i, 0)),
        out_shape=jax.ShapeDtypeStruct((bsz, s, npair * LANES), F32),
        scratch_shapes=([pltpu.VMEM((2 * TQ, LANES), BF16)] * npair
                        + [pltpu.VMEM((2 * TQ, LANES), F32)] * npair
                        + [pltpu.VMEM((2 * TQ, TQ), F32)] * npair),
        compiler_params=pltpu.CompilerParams(
            dimension_semantics=("parallel", "parallel"), vmem_limit_bytes=VMEM_LIMIT),
        name="stickbreak_attn",
    )(q, k, v)


def _postmix_kernel(og_ref, osb_ref, x_ref, mod_ref, gbr_ref, wout_ref, gpost_ref, gpre_ref,
                    wrt_ref, x1_ref, h2_ref, logit_ref, wout_s):
    @pl.when(jnp.logical_and(pl.program_id(0) == 0, pl.program_id(1) == 0))
    def _():
        wout_s[...] = wout_ref[...].astype(BF16)

    ga1 = mod_ref[2:3, :]
    sh2 = mod_ref[3:4, :]
    sc2 = mod_ref[4:5, :]
    osb = _rms(osb_ref[...], gbr_ref[...]).astype(BF16)
    m = (jnp.dot(og_ref[...], wout_s[:D_GMLP, :], preferred_element_type=F32)
         + jnp.dot(osb, wout_s[D_GMLP:, :], preferred_element_type=F32))
    x1 = x_ref[...] + ga1 * _rms(m, gpost_ref[...])
    x1_ref[...] = x1
    h2 = _rms(x1, gpre_ref[...]) * (1.0 + sc2) + sh2
    h_hi = h2.astype(BF16)
    h2_ref[...] = h_hi
    h_lo = (h2 - h_hi.astype(F32)).astype(BF16)
    w = wrt_ref[...]
    w_hi = w.astype(BF16)
    w_lo = (w - w_hi.astype(F32)).astype(BF16)
    nt = (((1,), (1,)), ((), ()))
    by_hi = lax.dot_general(jnp.concatenate([w_hi, w_lo], axis=0), h_hi, nt,
                            preferred_element_type=F32)
    by_lo = lax.dot_general(w_hi, h_lo, nt, preferred_element_type=F32)
    logit_ref[...] = by_hi[:N_EXPERTS] + by_hi[N_EXPERTS:] + by_lo


def _postmix(og, osb, x, mod3, g_br, w_out, g_post, g_pre, w_router_t):
    bsz, s, d = x.shape
    tm = TM_MIX
    nt = s // tm
    full = lambda shape: pl.BlockSpec(shape, lambda b, i: (0,) * len(shape))
    return pl.pallas_call(
        _postmix_kernel,
        grid=(bsz, nt),
        in_specs=[pl.BlockSpec((None, tm, D_GMLP), lambda b, i: (b, i, 0)),
                  pl.BlockSpec((None, tm, D_SB), lambda b, i: (b, i, 0)),
                  pl.BlockSpec((None, tm, d), lambda b, i: (b, i, 0)),
                  pl.BlockSpec((None, 6, d), lambda b, i: (b, 0, 0)),
                  full((1, D_SB)), full((d, d)), full((1, d)), full((1, d)),
                  full((N_EXPERTS, d))],
        out_specs=[pl.BlockSpec((None, tm, d), lambda b, i: (b, i, 0)),
                   pl.BlockSpec((None, tm, d), lambda b, i: (b, i, 0)),
                   pl.BlockSpec((N_EXPERTS, tm), lambda b, i: (0, b * nt + i))],
        out_shape=[jax.ShapeDtypeStruct((bsz, s, d), F32),
                   jax.ShapeDtypeStruct((bsz, s, d), BF16),
                   jax.ShapeDtypeStruct((N_EXPERTS, bsz * s), F32)],
        scratch_shapes=[pltpu.VMEM((d, d), BF16)],
        compiler_params=pltpu.CompilerParams(
            dimension_semantics=("arbitrary", "arbitrary"), vmem_limit_bytes=VMEM_LIMIT),
        name="postmix",
    )(og, osb, x, mod3, g_br, w_out, g_post, g_pre, w_router_t)


def _first_index_of_max(x, idx, axis, size):
    m = jnp.max(x, axis=axis, keepdims=True)
    return jnp.min(jnp.where(x == m, idx, size), axis=axis, keepdims=True)


def _route_kernel(logit_ref, bias_ref, em_ref, tm_ref, cnt_ref):
    for u in range(ROUTE_BLOCKS):
        cols = slice(u * TB, (u + 1) * TB)
        both, cnt = _route_block(logit_ref[:, cols], bias_ref[...])
        em_ref[:, cols] = both
        tm_ref[cols, :] = both.T
        cnt_ref[u] = jnp.broadcast_to(cnt, (N_EXPERTS, LANES))


def _route_block(logits, bias):
    scores = jax.nn.sigmoid(logits)
    biased = scores + bias
    neg = jnp.float32(-jnp.inf)

    grouped = biased.reshape(N_GROUPS, GROUP_SIZE, TB)
    within = lax.broadcasted_iota(I32, grouped.shape, 1)
    top1 = jnp.max(grouped, axis=1, keepdims=True)
    first = _first_index_of_max(grouped, within, 1, GROUP_SIZE)
    top2 = jnp.max(jnp.where(within == first, neg, grouped), axis=1, keepdims=True)
    group_score = (top1 + top2).reshape(N_GROUPS, TB)

    gidx = lax.broadcasted_iota(I32, group_score.shape, 0)
    group_on = jnp.zeros(group_score.shape, jnp.bool_)
    for _ in range(TOPK_GROUPS):
        pick = gidx == _first_index_of_max(group_score, gidx, 0, N_GROUPS)
        group_on = jnp.logical_or(group_on, pick)
        group_score = jnp.where(pick, neg, group_score)

    masked = jnp.where(group_on.reshape(N_GROUPS, 1, TB), grouped, neg).reshape(N_EXPERTS, TB)
    eidx = lax.broadcasted_iota(I32, masked.shape, 0)
    chosen = jnp.zeros(masked.shape, jnp.bool_)
    for _ in range(TOP_K):
        pick = eidx == _first_index_of_max(masked, eidx, 0, N_EXPERTS)
        chosen = jnp.logical_or(chosen, pick)
        masked = jnp.where(pick, neg, masked)

    w = jnp.where(chosen, scores, 0.0)
    gates = w / jnp.sum(w, axis=0, keepdims=True) * ROUTED_SCALE

    chosen_f = chosen.astype(F32)
    tr = lax.broadcasted_iota(I32, (TB, TB), 0)
    tc = lax.broadcasted_iota(I32, (TB, TB), 1)
    rank = jnp.dot(chosen_f.astype(BF16), (tr < tc).astype(BF16), preferred_element_type=F32)
    both = jnp.concatenate([jnp.where(chosen, rank, -1.0), gates], axis=0)
    return both, jnp.sum(chosen_f, axis=1, keepdims=True)


def _route(logits_t, bias):
    e, t = logits_t.shape
    nblk = t // TB
    return pl.pallas_call(
        _route_kernel,
        grid=(nblk // ROUTE_BLOCKS,),
        in_specs=[pl.BlockSpec((e, ROUTE_BLOCKS * TB), lambda i: (0, i)),
                  pl.BlockSpec((e, 1), lambda i: (0, 0))],
        out_specs=[pl.BlockSpec((2 * e, ROUTE_BLOCKS * TB), lambda i: (0, i)),
                   pl.BlockSpec((ROUTE_BLOCKS * TB, 2 * e), lambda i: (i, 0)),
                   pl.BlockSpec((ROUTE_BLOCKS, e, LANES), lambda i: (i, 0, 0))],
        out_shape=[jax.ShapeDtypeStruct((2 * e, t), F32),
                   jax.ShapeDtypeStruct((t, 2 * e), F32),
                   jax.ShapeDtypeStruct((nblk, e, LANES), F32)],
        compiler_params=pltpu.CompilerParams(dimension_semantics=("parallel",)),
        name="route",
    )(logits_t, bias.reshape(e, 1))


def _dispatch_plan(cnt, n_tiles):
    nblk, e = cnt.shape
    pc = (cnt + ROW_CHUNK - 1) // ROW_CHUNK * ROW_CHUNK
    start = jnp.cumsum(pc, axis=1) - pc
    nchunk = jnp.sum(pc, axis=1) // ROW_CHUNK
    off = jnp.cumsum(pc, axis=0) - pc
    ecount = jnp.sum(pc, axis=0)
    epad = (ecount + TM_FFN - 1) // TM_FFN * TM_FFN
    gend = jnp.cumsum(epad)
    gbase = gend - epad
    nact = gend[-1] // TM_FFN
    tile_start = jnp.concatenate([gbase, gend[-1:]]) // TM_FFN
    cidx = jnp.arange(BLOCK_CHUNKS, dtype=I32)
    start16 = start // ROW_CHUNK
    shift = (gbase[None, :] + off) // ROW_CHUNK - start16
    dshift = shift - jnp.pad(shift, ((0, 0), (1, 0)))[:, :-1]
    in_or_after = (start16[:, None, :] <= cidx[None, :, None]).astype(I32)
    where = cidx[None, :] + jnp.sum(in_or_after * dshift[:, None, :], axis=2)
    used = cidx[None, :] < nchunk[:, None]
    spare = (n_tiles * TM_FFN // ROW_CHUNK
             + jnp.arange(nblk, dtype=I32)[:, None] * GROUP_CHUNKS + cidx[None, :] % GROUP_CHUNKS)
    gdst = jnp.where(used, where, spare)
    gsrc = jnp.where(used, where, 0)
    ngrp = (nchunk + GROUP_CHUNKS - 1) // GROUP_CHUNKS
    zbase = (gbase + ecount) // ROW_CHUNK
    zn = (epad - ecount) // ROW_CHUNK

    lo = jnp.tile(start.astype(F32), (1, 2))
    hi = jnp.tile((start + pc).astype(F32), (1, 2))
    bounds_row = jnp.stack([jnp.broadcast_to(lo[:, None, :], (nblk, SUBLANES, 2 * e)),
                            jnp.broadcast_to(hi[:, None, :], (nblk, SUBLANES, 2 * e))], axis=1)
    bounds_col = jnp.stack([jnp.broadcast_to(lo[:, :, None], (nblk, 2 * e, LANES)),
                            jnp.broadcast_to(hi[:, :, None], (nblk, 2 * e, LANES))], axis=1)
    as_i32 = lambda a: a.astype(I32)
    return dict(ngrp=as_i32(ngrp), gdst=as_i32(gdst.reshape(-1)), gsrc=as_i32(gsrc.reshape(-1)),
                zn=as_i32(zn), zbase=as_i32(zbase), tile_start=as_i32(tile_start),
                nact=as_i32(nact.reshape(1)), bounds_row=bounds_row, bounds_col=bounds_col)


def _dispatch_kernel(ngrp_ref, gdst_ref, zn_ref, zbase_ref, h_ref, em_ref, brow_ref,
                     xs_ref, buf_ref, zero_ref, sem, zsem):
    j = pl.program_id(0)
    last = pl.num_programs(0) - 1
    slot = j % 2

    def chunk_copy(s, c, g):
        return pltpu.make_async_copy(buf_ref.at[s, c], xs_ref.at[g], sem.at[s])

    def wait_block(jj, s):
        def body(i, carry):
            pltpu.make_async_copy(buf_ref.at[s, pl.ds(0, GROUP_CHUNKS)],
                                  xs_ref.at[pl.ds(0, GROUP_CHUNKS)], sem.at[s]).wait()
            return carry
        lax.fori_loop(0, ngrp_ref[jj], body, 0)

    def zero_copy(g, chunks):
        return pltpu.make_async_copy(zero_ref.at[pl.ds(0, chunks)], xs_ref.at[pl.ds(g, chunks)],
                                     zsem.at[0])

    def zero_fill(start):
        big = ZERO_ROWS // ROW_CHUNK

        def per_expert(e, carry):
            n = zn_ref[e]
            base = zbase_ref[e]

            def big_copy(i, c2):
                cp = zero_copy(base + i * big, big)
                cp.start() if start else cp.wait()
                return c2
            lax.fori_loop(0, n // big, big_copy, 0)

            def small_copy(i, c2):
                cp = zero_copy(base + n // big * big + i, 1)
                cp.start() if start else cp.wait()
                return c2
            lax.fori_loop(0, n % big, small_copy, 0)
            return carry
        lax.fori_loop(0, N_EXPERTS, per_expert, 0)

    @pl.when(j == 0)
    def _():
        zero_ref[...] = jnp.zeros_like(zero_ref)
        zero_fill(True)

    @pl.when(j >= 2)
    def _():
        wait_block(j - 2, slot)

    lo = brow_ref[0, 0:1, :]
    hi = brow_ref[1, 0:1, :]
    first_copy = lax.broadcasted_iota(I32, (KC, 2 * N_EXPERTS), 1) < N_EXPERTS
    row_iota = lax.broadcasted_iota(I32, (KC, 2 * N_EXPERTS), 0)
    ranks = em_ref[...].astype(BF16)
    h = h_ref[...]

    def trip(ti, carry):
        onehots = []
        for u in range(GROUP_ROWS // KC):
            rows_e = (row_iota + ti * GROUP_ROWS + u * KC).astype(F32)
            in_run = jnp.logical_and(jnp.logical_and(rows_e >= lo, rows_e < hi), first_copy)
            sel = jnp.dot(jnp.where(in_run, 1.0, 0.0).astype(BF16), ranks,
                          preferred_element_type=F32)
            local = jnp.sum(jnp.where(in_run, rows_e - lo + 3.0, 0.0), axis=1,
                            keepdims=True) - 3.0
            onehots.append(jnp.where(sel == local, 1.0, 0.0).astype(BF16))
        xs = jnp.dot(jnp.concatenate(onehots, axis=0), h, preferred_element_type=F32)
        first = pl.multiple_of(ti * GROUP_CHUNKS, GROUP_CHUNKS)
        buf_ref[slot, pl.ds(first, GROUP_CHUNKS)] = xs.astype(BF16).reshape(
            GROUP_CHUNKS, ROW_CHUNK, xs.shape[1])
        for k in range(GROUP_CHUNKS):
            chunk_copy(slot, first + k, gdst_ref[j * BLOCK_CHUNKS + first + k]).start()
        return carry
    lax.fori_loop(0, ngrp_ref[j], trip, 0)

    @pl.when(j == last)
    def _():
        wait_block(j, slot)

        @pl.when(j >= 1)
        def _():
            wait_block(j - 1, 1 - slot)

        zero_fill(False)


def _bounds_specs():
    return [pl.BlockSpec((None, 2, SUBLANES, 2 * N_EXPERTS), lambda j, *_: (j, 0, 0, 0)),
            pl.BlockSpec((None, 2, 2 * N_EXPERTS, LANES), lambda j, *_: (j, 0, 0, 0))]


def _dispatch(plan, h2_flat, em, n_rows):
    t, d = h2_flat.shape
    grid_spec = pltpu.PrefetchScalarGridSpec(
        num_scalar_prefetch=4,
        grid=(t // TB,),
        in_specs=[pl.BlockSpec((TB, d), lambda j, *_: (j, 0)),
                  pl.BlockSpec((2 * N_EXPERTS, TB), lambda j, *_: (0, j))] + _bounds_specs()[:1],
        out_specs=pl.BlockSpec(memory_space=pl.ANY),
        scratch_shapes=[pltpu.VMEM((2, BLOCK_CHUNKS, ROW_CHUNK, d), BF16),
                        pltpu.VMEM((ZERO_ROWS // ROW_CHUNK, ROW_CHUNK, d), BF16),
                        pltpu.SemaphoreType.DMA((2,)),
                        pltpu.SemaphoreType.DMA((1,))],
    )
    xs = pl.pallas_call(
        _dispatch_kernel,
        grid_spec=grid_spec,
        out_shape=jax.ShapeDtypeStruct((n_rows // ROW_CHUNK, ROW_CHUNK, d), BF16),
        compiler_params=pltpu.CompilerParams(
            dimension_semantics=("arbitrary",), vmem_limit_bytes=VMEM_LIMIT),
        name="moe_dispatch",
    )(plan["ngrp"], plan["gdst"], plan["zn"], plan["zbase"], h2_flat, em, plan["bounds_row"])
    return xs.reshape(n_rows, d)


def _ffn_kernel(tstart_ref, nact_ref, wg_ref, wu_ref, wd_ref, xs_ref, ys_ref,
                xbuf, ybuf, wgu_s, wd_s, xsem, ysem):
    e = pl.program_id(0)
    nact = nact_ref[0]
    t0 = tstart_ref[e]
    t1 = tstart_ref[e + 1]

    def x_copy(t):
        s = t % (FFN_AHEAD + 1)
        return pltpu.make_async_copy(
            xs_ref.at[pl.ds(pl.multiple_of(t * TM_FFN, TM_FFN), TM_FFN), :], xbuf.at[s],
            xsem.at[s])

    def y_copy(t):
        s = t % FFN_OUT
        return pltpu.make_async_copy(
            ybuf.at[s], ys_ref.at[pl.ds(pl.multiple_of(t * TM_FFN, TM_FFN), TM_FFN), :],
            ysem.at[s])

    @pl.when(e == 0)
    def _():
        for a in range(FFN_AHEAD):
            @pl.when(a < nact)
            def _():
                x_copy(a).start()

    @pl.when(t1 > t0)
    def _():
        wgu_s[:, :D_EXPERT] = wg_ref[...].astype(BF16)
        wgu_s[:, D_EXPERT:] = wu_ref[...].astype(BF16)
        wd_s[...] = wd_ref[...].astype(BF16)

        def tile(t, carry):
            x_copy(t).wait()

            @pl.when(t + FFN_AHEAD < nact)
            def _():
                x_copy(t + FFN_AHEAD).start()

            gu = jnp.dot(xbuf[t % (FFN_AHEAD + 1)], wgu_s[...], preferred_element_type=F32)
            act = _silu(gu[:, :D_EXPERT]) * gu[:, D_EXPERT:]
            y = jnp.dot(act.astype(BF16), wd_s[...], preferred_element_type=F32)

            @pl.when(t >= FFN_OUT)
            def _():
                y_copy(t - FFN_OUT).wait()

            ybuf[t % FFN_OUT] = y.astype(BF16)
            y_copy(t).start()
            return carry
        lax.fori_loop(t0, t1, tile, 0)

    @pl.when(e == pl.num_programs(0) - 1)
    def _():
        for back in range(FFN_OUT, 0, -1):
            @pl.when(nact >= back)
            def _():
                y_copy(nact - back).wait()


def _expert_ffn(tile_start, nact, xs, n_tiles, w_gate, w_up, w_down):
    n_rows, d = n_tiles * TM_FFN, xs.shape[1]
    f = D_EXPERT
    grid_spec = pltpu.PrefetchScalarGridSpec(
        num_scalar_prefetch=2,
        grid=(N_EXPERTS,),
        in_specs=[pl.BlockSpec((None, d, f), lambda e, ts, na: (e, 0, 0)),
                  pl.BlockSpec((None, d, f), lambda e, ts, na: (e, 0, 0)),
                  pl.BlockSpec((None, f, d), lambda e, ts, na: (e, 0, 0)),
                  pl.BlockSpec(memory_space=pl.ANY)],
        out_specs=pl.BlockSpec(memory_space=pl.ANY),
        scratch_shapes=[pltpu.VMEM((FFN_AHEAD + 1, TM_FFN, d), BF16),
                        pltpu.VMEM((FFN_OUT, TM_FFN, d), BF16),
                        pltpu.VMEM((d, 2 * f), BF16), pltpu.VMEM((f, d), BF16),
                        pltpu.SemaphoreType.DMA((FFN_AHEAD + 1,)),
                        pltpu.SemaphoreType.DMA((FFN_OUT,))],
    )
    return pl.pallas_call(
        _ffn_kernel,
        grid_spec=grid_spec,
        out_shape=jax.ShapeDtypeStruct((n_rows, d), BF16),
        compiler_params=pltpu.CompilerParams(
            dimension_semantics=("arbitrary",), vmem_limit_bytes=VMEM_LIMIT),
        name="moe_experts",
    )(tile_start, nact, w_gate, w_up, w_down, xs)


def _combine_kernel(ngrp_ref, gsrc_ref, ys_ref, tm_ref, brow_ref, bcol_ref, h_ref, wsg_ref,
                    wsu_ref, wsd_ref, x1_ref, mod_ref, gpost_ref, o_ref, buf_ref, p_ref, y_ref,
                    wsgu_s, wsd_s, sem):
    j = pl.program_id(0)
    last = pl.num_programs(0) - 1
    slot = j % 2

    def chunk_copy(s, c, g):
        return pltpu.make_async_copy(ys_ref.at[g], buf_ref.at[s, c], sem.at[s])

    def fetch_block(jj, s):
        def issue_group(gi, carry):
            for k in range(GROUP_CHUNKS):
                c = gi * GROUP_CHUNKS + k
                chunk_copy(s, c, gsrc_ref[jj * BLOCK_CHUNKS + c]).start()
            return carry
        lax.fori_loop(0, ngrp_ref[jj], issue_group, 0)

    @pl.when(j == 0)
    def _():
        buf_ref[...] = jnp.zeros_like(buf_ref)
        fetch_block(0, 0)

    @pl.when(j < last)
    def _():
        fetch_block(j + 1, 1 - slot)

    @pl.when(j == 0)
    def _():
        wsgu_s[:, :D_EXPERT] = wsg_ref[...].astype(BF16)
        wsgu_s[:, D_EXPERT:] = wsu_ref[...].astype(BF16)
        wsd_s[...] = wsd_ref[...].astype(BF16)

    gu = jnp.dot(h_ref[...], wsgu_s[...], preferred_element_type=F32)
    act = _silu(gu[:, :D_EXPERT]) * gu[:, D_EXPERT:]
    y = jnp.dot(act.astype(BF16), wsd_s[...], preferred_element_type=F32)

    tm = tm_ref[...]
    lane = lax.broadcasted_iota(I32, tm.shape, 1)
    rank_t = jnp.where(lane < N_EXPERTS, tm, 0.0).astype(BF16)
    gate_t = jnp.where(lane < N_EXPERTS, 0.0, tm).astype(BF16)
    lane8 = lax.broadcasted_iota(I32, (SUBLANES, 2 * N_EXPERTS), 1)
    lo_row = jnp.where(lane8 < N_EXPERTS, brow_ref[0], 0.0).astype(BF16)
    lo_col = jnp.concatenate([bcol_ref[0]] * (KC // LANES), axis=1)
    hi_col = jnp.concatenate([bcol_ref[1]] * (KC // LANES), axis=1)
    def build_gate_columns(first_cb, last_cb):
        for cb in range(first_cb, last_cb):
            rows = (lax.broadcasted_iota(I32, (2 * N_EXPERTS, KC), 1) + cb * KC).astype(F32)
            in_run = jnp.where(jnp.logical_and(rows >= lo_col, rows < hi_col),
                               1.0, 0.0).astype(BF16)
            sel_rank = jnp.dot(rank_t, in_run, preferred_element_type=F32)
            sel_gate = jnp.dot(gate_t, in_run, preferred_element_type=F32)
            sel_lo = jnp.dot(lo_row, in_run, preferred_element_type=F32)
            local = rows[0:1, :] - sel_lo[0:1, :]
            p_ref[:, cb * KC:(cb + 1) * KC] = jnp.where(sel_rank == local, sel_gate,
                                                        0.0).astype(BF16)

    def weighted_rows(first_row, last_row):
        slabs = buf_ref[slot, first_row // ROW_CHUNK:last_row // ROW_CHUNK]
        return jnp.dot(p_ref[:, first_row:last_row],
                       slabs.reshape(last_row - first_row, buf_ref.shape[3]),
                       preferred_element_type=F32)

    build_gate_columns(0, COMMON_ROWS // KC)

    def wait_group(gi, carry):
        pltpu.make_async_copy(ys_ref.at[pl.ds(0, GROUP_CHUNKS)],
                              buf_ref.at[slot, pl.ds(0, GROUP_CHUNKS)], sem.at[slot]).wait()
        return carry
    lax.fori_loop(0, ngrp_ref[j], wait_group, 0)

    y_ref[...] = y + weighted_rows(0, COMMON_ROWS)

    @pl.when(ngrp_ref[j] * GROUP_ROWS > COMMON_ROWS)
    def _():
        build_gate_columns(COMMON_ROWS // KC, BLOCK_ROWS // KC)
        y_ref[...] += weighted_rows(COMMON_ROWS, BLOCK_ROWS)

    y = y_ref[...]

    ga2 = mod_ref[5:6, :]
    o_ref[...] = x1_ref[...] + ga2 * _rms(y, gpost_ref[...])


def _combine(plan, ys, tm, h2_flat, ws_gate, ws_up, ws_down, x1_flat, mod3, g_post, s):
    t, d = h2_flat.shape
    f = ws_gate.shape[1]
    blocks_per_seq = s // TB
    full = lambda shape: pl.BlockSpec(shape, lambda j, *_: (0,) * len(shape))
    grid_spec = pltpu.PrefetchScalarGridSpec(
        num_scalar_prefetch=2,
        grid=(t // TB,),
        in_specs=[pl.BlockSpec(memory_space=pl.ANY),
                  pl.BlockSpec((TB, 2 * N_EXPERTS), lambda j, *_: (j, 0))] + _bounds_specs() + [
                  pl.BlockSpec((TB, d), lambda j, *_: (j, 0)),
                  full((d, f)), full((d, f)), full((f, d)),
                  pl.BlockSpec((TB, d), lambda j, *_: (j, 0)),
                  pl.BlockSpec((None, 6, d), lambda j, *_: (j // blocks_per_seq, 0, 0)),
                  full((1, d))],
        out_specs=pl.BlockSpec((TB, d), lambda j, *_: (j, 0)),
        scratch_shapes=[pltpu.VMEM((2, BLOCK_CHUNKS, ROW_CHUNK, d), BF16),
                        pltpu.VMEM((TB, BLOCK_ROWS), BF16),
                        pltpu.VMEM((TB, d), F32),
                        pltpu.VMEM((d, 2 * f), BF16), pltpu.VMEM((f, d), BF16),
                        pltpu.SemaphoreType.DMA((2,))],
    )
    return pl.pallas_call(
        _combine_kernel,
        grid_spec=grid_spec,
        out_shape=jax.ShapeDtypeStruct((t, d), F32),
        compiler_params=pltpu.CompilerParams(
            dimension_semantics=("arbitrary",), vmem_limit_bytes=VMEM_LIMIT),
        name="moe_combine",
    )(plan["ngrp"], plan["gsrc"], ys.reshape(-1, ROW_CHUNK, d), tm, plan["bounds_row"],
      plan["bounds_col"], h2_flat, ws_gate, ws_up, ws_down, x1_flat, mod3, g_post)


def kernel(x, c, w_ada, b_ada, g_pre_mix, w_in, ln_sgu_g, ln_sgu_b, w_spatial, b_spatial,
           g_branch, w_out, g_post_mix, g_pre_ffn, w_router, router_bias, w_gate, w_up, w_down,
           ws_gate, ws_up, ws_down, g_post_ffn):
    bsz, s, d = x.shape
    t = bsz * s
    nblk = t // TB
    max_rows = t * TOP_K + nblk * N_EXPERTS * (ROW_CHUNK - 1) + N_EXPERTS * (TM_FFN - 1)
    n_tiles = -(-max_rows // TM_FFN)
    row = lambda a: a.reshape(1, -1)
    for l in range(w_ada.shape[0]):
        mod3 = _ada(c, w_ada[l], b_ada[l]).reshape(bsz, 6, d)
        og, q, k, v = _premix(x, mod3, row(g_pre_mix[l]), w_in[l],
                              row(ln_sgu_g[l]), row(ln_sgu_b[l]), w_spatial[l],
                              b_spatial[l].T, row(g_branch[l, :D_GMLP]))
        osb = _attention(q, k, v)
        x1, h2, logits_t = _postmix(og, osb, x, mod3, row(g_branch[l, D_GMLP:]),
                                    w_out[l], row(g_post_mix[l]),
                                    row(g_pre_ffn[l]), w_router[l].T)
        em, tm, cnt = _route(logits_t, router_bias[l])
        plan = _dispatch_plan(cnt[:, :, 0].astype(I32), n_tiles)
        h2_flat = h2.reshape(t, d)
        xs = _dispatch(plan, h2_flat, em, n_tiles * TM_FFN + nblk * GROUP_ROWS)
        ys = _expert_ffn(plan["tile_start"], plan["nact"], xs, n_tiles,
                         w_gate[l], w_up[l], w_down[l])
        out = _combine(plan, ys, tm, h2_flat, ws_gate[l], ws_up[l], ws_down[l],
                       x1.reshape(t, d), mod3, row(g_post_ffn[l]), s)
        x = out.reshape(bsz, s, d)
    return x
```

```python
import jax
import jax.numpy as jnp
from jax import lax
from jax.experimental import pallas as pl
from jax.experimental.pallas import tpu as pltpu

F32 = jnp.float32
BF16 = jnp.bfloat16
I32 = jnp.int32

LANES = 128
SUBLANES = 8
V7X_VMEM_BYTES = 64 * 1024 * 1024

D_MODEL = 1024
D_GMLP = 512
GMLP_GROUPS = 4
GMLP_BLOCK = 128
CHUNK = 64
D_SB = 512
SB_HEAD_DIM = 64
N_PAIRS = D_SB // LANES
N_EXPERTS = 64
N_GROUPS = 8
GROUP_SIZE = N_EXPERTS // N_GROUPS
TOPK_GROUPS = 4
TOP_K = 8
D_EXPERT = 256
ROUTED_SCALE = 2.5
EPS = 1e-6
D_IN = 2 * D_GMLP + 3 * D_SB

ADA_TN = 1536
TM_MIX = 512
TQ = 128
VMEM_LIMIT = V7X_VMEM_BYTES * 7 // 8
ATTN_DEAD_LOG = -110.0

TB = 256
ROUTE_BLOCKS = 8
ROW_CHUNK = 16
KC = 256
GROUP_CHUNKS = 32
GROUP_ROWS = GROUP_CHUNKS * ROW_CHUNK
BLOCK_ROWS = -(-(TB * TOP_K + N_EXPERTS * (ROW_CHUNK - 1)) // GROUP_ROWS) * GROUP_ROWS
BLOCK_CHUNKS = BLOCK_ROWS // ROW_CHUNK
COMMON_ROWS = -(-(TB * TOP_K + N_EXPERTS * ROW_CHUNK // 2) // GROUP_ROWS) * GROUP_ROWS
TM_FFN = 1024
FFN_AHEAD = 3
FFN_OUT = 3
ZERO_ROWS = 512


def _rms(x, g):
    return x * lax.rsqrt(jnp.mean(x * x, axis=-1, keepdims=True) + EPS) * g


def _silu(x):
    return x * jax.nn.sigmoid(x)


def _ada_kernel(c_ref, w_ref, b_ref, o_ref):
    o_ref[...] = jnp.dot(_silu(c_ref[...]), w_ref[...], preferred_element_type=F32,
                         precision=lax.Precision.HIGHEST) + b_ref[...]


def _ada(c, w, b):
    bsz, d = c.shape
    n = w.shape[1]
    tn = ADA_TN
    return pl.pallas_call(
        _ada_kernel,
        grid=(n // tn,),
        in_specs=[pl.BlockSpec((bsz, d), lambda j: (0, 0)),
                  pl.BlockSpec((d, tn), lambda j: (0, j)),
                  pl.BlockSpec((1, tn), lambda j: (0, j))],
        out_specs=pl.BlockSpec((bsz, tn), lambda j: (0, j)),
        out_shape=jax.ShapeDtypeStruct((bsz, n), F32),
        name="ada_mod",
    )(c, w, b.reshape(1, n))


def _premix_kernel(x_ref, mod_ref, gpre_ref, win_ref, lng_ref, lnb_ref, wsp_ref, bsp_ref,
                   gbr_ref, og_ref, q_ref, k_ref, v_ref, win_s):
    @pl.when(jnp.logical_and(pl.program_id(0) == 0, pl.program_id(1) == 0))
    def _():
        win_s[...] = win_ref[...].astype(BF16)

    x = x_ref[...]
    sh1 = mod_ref[0:1, :]
    sc1 = mod_ref[1:2, :]
    h = _rms(x, gpre_ref[...]) * (1.0 + sc1) + sh1
    proj = jnp.dot(h.astype(BF16), win_s[...], preferred_element_type=F32)

    u = jax.nn.gelu(proj[:, :D_GMLP])
    v = jax.nn.gelu(proj[:, D_GMLP:2 * D_GMLP])
    mu = jnp.mean(v, axis=-1, keepdims=True)
    var = jnp.mean(jnp.square(v - mu), axis=-1, keepdims=True)
    v = ((v - mu) * lax.rsqrt(var + EPS) * lng_ref[...] + lnb_ref[...]).astype(BF16)

    i = lax.broadcasted_iota(I32, (GMLP_BLOCK, GMLP_BLOCK), 0)
    j = lax.broadcasted_iota(I32, (GMLP_BLOCK, GMLP_BLOCK), 1)
    causal = (j // CHUNK) <= (i // CHUNK)
    gd = D_GMLP // GMLP_GROUPS
    blocks = []
    for nb in range(x.shape[0] // GMLP_BLOCK):
        rows = slice(nb * GMLP_BLOCK, (nb + 1) * GMLP_BLOCK)
        cols = []
        for g in range(GMLP_GROUPS):
            w = jnp.where(causal, wsp_ref[g], 0.0).astype(BF16)
            mixed = jnp.dot(w, v[rows, g * gd:(g + 1) * gd], preferred_element_type=F32)
            cols.append(mixed + bsp_ref[:, g:g + 1])
        blocks.append(u[rows, :] * jnp.concatenate(cols, axis=1))
    og = jnp.concatenate(blocks, axis=0)
    og_ref[...] = _rms(og, gbr_ref[...]).astype(BF16)

    base = 2 * D_GMLP
    scale = SB_HEAD_DIM ** -0.5
    for p in range(N_PAIRS):
        q_ref[p] = (proj[:, base + LANES * p:base + LANES * (p + 1)] * scale).astype(BF16)
        k_ref[p] = proj[:, base + D_SB + LANES * p:base + D_SB + LANES * (p + 1)].astype(BF16)
        v_ref[p] = proj[:, base + 2 * D_SB + LANES * p:base + 2 * D_SB + LANES * (p + 1)].astype(BF16)


def _premix(x, mod3, g_pre, w_in, ln_g, ln_b, w_sp, b_sp_t, g_br):
    bsz, s, d = x.shape
    tm = TM_MIX
    full = lambda shape: pl.BlockSpec(shape, lambda b, i: (0,) * len(shape))
    qkv_spec = pl.BlockSpec((None, N_PAIRS, tm, LANES), lambda b, i: (b, 0, i, 0))
    qkv_shape = jax.ShapeDtypeStruct((bsz, N_PAIRS, s, LANES), BF16)
    return pl.pallas_call(
        _premix_kernel,
        grid=(bsz, s // tm),
        in_specs=[pl.BlockSpec((None, tm, d), lambda b, i: (b, i, 0)),
                  pl.BlockSpec((None, 6, d), lambda b, i: (b, 0, 0)),
                  full((1, d)), full((d, D_IN)), full((1, D_GMLP)), full((1, D_GMLP)),
                  full((GMLP_GROUPS, GMLP_BLOCK, GMLP_BLOCK)), full((GMLP_BLOCK, GMLP_GROUPS)),
                  full((1, D_GMLP))],
        out_specs=[pl.BlockSpec((None, tm, D_GMLP), lambda b, i: (b, i, 0)),
                   qkv_spec, qkv_spec, qkv_spec],
        out_shape=[jax.ShapeDtypeStruct((bsz, s, D_GMLP), BF16), qkv_shape, qkv_shape, qkv_shape],
        scratch_shapes=[pltpu.VMEM((d, D_IN), BF16)],
        compiler_params=pltpu.CompilerParams(
            dimension_semantics=("arbitrary", "arbitrary"), vmem_limit_bytes=VMEM_LIMIT),
        name="premix",
    )(x, mod3, g_pre, w_in, ln_g, ln_b, w_sp, b_sp_t, g_br)


def _attn_kernel(q_ref, k_ref, v_ref, o_ref, *scratch):
    qs_refs = scratch[:N_PAIRS]
    acc_refs = scratch[N_PAIRS:2 * N_PAIRS]
    carry_refs = scratch[2 * N_PAIRS:]
    qi = pl.program_id(1)
    first_head = lax.broadcasted_iota(I32, (TQ, LANES), 1) < SB_HEAD_DIM
    for p in range(N_PAIRS):
        q2 = q_ref[p]
        zero = jnp.zeros_like(q2)
        qs_refs[p][:TQ, :] = jnp.where(first_head, q2, zero)
        qs_refs[p][TQ:, :] = jnp.where(first_head, zero, q2)
        acc_refs[p][...] = jnp.zeros_like(acc_refs[p])
        carry_refs[p][...] = jnp.zeros_like(carry_refs[p])

    r = jnp.bitwise_and(lax.broadcasted_iota(I32, (2 * TQ, TQ), 0), TQ - 1)
    c = lax.broadcasted_iota(I32, (2 * TQ, TQ), 1)
    strict_causal = c < r
    kr = lax.broadcasted_iota(I32, (TQ, TQ), 0)
    kc = lax.broadcasted_iota(I32, (TQ, TQ), 1)
    suffix = jnp.concatenate([(kr > kc).astype(BF16), jnp.ones((TQ, TQ), BF16)], axis=1)
    suffix2 = jnp.concatenate([suffix, suffix], axis=0)

    def key_block(j, diagonal):
        start = pl.multiple_of(j * TQ, TQ)
        pairs = range(N_PAIRS)
        zs = [lax.dot_general(qs_refs[p][...], k_ref[p, pl.ds(start, TQ), :],
                              (((1,), (1,)), ((), ())), preferred_element_type=F32)
              for p in pairs]
        log_betas, splits = [], []
        for p in pairs:
            z = zs[p]
            log_beta = jnp.minimum(z, 0.0) - jnp.log(1.0 + jnp.exp(-jnp.abs(z)))
            log_1mb = log_beta - z
            if diagonal:
                log_1mb = jnp.where(strict_causal, log_1mb, 0.0)
            hi = log_1mb.astype(BF16)
            lo = (log_1mb - hi.astype(F32)).astype(BF16)
            log_betas.append(log_beta)
            splits.append(jnp.concatenate([hi, lo], axis=1))
        sums = [jnp.dot(splits[p], suffix2, preferred_element_type=F32) for p in pairs]
        weights = []
        live = None
        for p in pairs:
            s = sums[p]
            carry = carry_refs[p][...]
            a = jnp.exp(log_betas[p] + carry + s[:, :TQ])
            if diagonal:
                a = jnp.where(strict_causal, a, 0.0)
            weights.append(a.astype(BF16))
            carry = carry + s[:, TQ:]
            carry_refs[p][...] = carry
            live = carry if live is None else jnp.maximum(live, carry)
        for p in pairs:
            acc_refs[p][...] += jnp.dot(weights[p], v_ref[p, pl.ds(start, TQ), :],
                                        preferred_element_type=F32)
        return jnp.max(live)

    live = key_block(qi, True)

    def cond(state):
        j, live = state
        return jnp.logical_and(j >= 0, live > ATTN_DEAD_LOG)

    def body(state):
        j, _ = state
        return j - 1, key_block(j, False)

    lax.while_loop(cond, body, (qi - 1, live))
    for p in range(N_PAIRS):
        o_ref[:, LANES * p:LANES * (p + 1)] = jnp.where(first_head, acc_refs[p][:TQ, :],
                                                    acc_refs[p][TQ:, :])


def _attention(q, k, v):
    bsz, npair, s, _ = q.shape
    kv_spec = pl.BlockSpec((None, npair, s, LANES), lambda b, i: (b, 0, 0, 0))
    return pl.pallas_call(
        _attn_kernel,
        grid=(bsz, s // TQ),
        in_specs=[pl.BlockSpec((None, npair, TQ, LANES), lambda b, i: (b, 0, i, 0)),
                  kv_spec, kv_spec],
        out_specs=pl.BlockSpec((None, TQ, npair * LANES), lambda b, i: (b, i, 0)),
        out_shape=jax.ShapeDtypeStruct((bsz, s, npair * LANES), F32),
        scratch_shapes=([pltpu.VMEM((2 * TQ, LANES), BF16)] * npair
                        + [pltpu.VMEM((2 * TQ, LANES), F32)] * npair
                        + [pltpu.VMEM((2 * TQ, TQ), F32)] * npair),
        compiler_params=pltpu.CompilerParams(
            dimension_semantics=("parallel", "parallel"), vmem_limit_bytes=VMEM_LIMIT),
        name="stickbreak_attn",
    )(q, k, v)


def _postmix_kernel(og_ref, osb_ref, x_ref, mod_ref, gbr_ref, wout_ref, gpost_ref, gpre_ref,
                    wrt_ref, x1_ref, h2_ref, logit_ref, wout_s):
    @pl.when(jnp.logical_and(pl.program_id(0) == 0, pl.program_id(1) == 0))
    def _():
        wout_s[...] = wout_ref[...].astype(BF16)

    ga1 = mod_ref[2:3, :]
    sh2 = mod_ref[3:4, :]
    sc2 = mod_ref[4:5, :]
    osb = _rms(osb_ref[...], gbr_ref[...]).astype(BF16)
    m = (jnp.dot(og_ref[...], wout_s[:D_GMLP, :], preferred_element_type=F32)
         + jnp.dot(osb, wout_s[D_GMLP:, :], preferred_element_type=F32))
    x1 = x_ref[...] + ga1 * _rms(m, gpost_ref[...])
    x1_ref[...] = x1
    h2 = _rms(x1, gpre_ref[...]) * (1.0 + sc2) + sh2
    h_hi = h2.astype(BF16)
    h2_ref[...] = h_hi
    h_lo = (h2 - h_hi.astype(F32)).astype(BF16)
    w = wrt_ref[...]
    w_hi = w.astype(BF16)
    w_lo = (w - w_hi.astype(F32)).astype(BF16)
    nt = (((1,), (1,)), ((), ()))
    by_hi = lax.dot_general(jnp.concatenate([w_hi, w_lo], axis=0), h_hi, nt,
                            preferred_element_type=F32)
    by_lo = lax.dot_general(w_hi, h_lo, nt, preferred_element_type=F32)
    logit_ref[...] = by_hi[:N_EXPERTS] + by_hi[N_EXPERTS:] + by_lo


def _postmix(og, osb, x, mod3, g_br, w_out, g_post, g_pre, w_router_t):
    bsz, s, d = x.shape
    tm = TM_MIX
    nt = s // tm
    full = lambda shape: pl.BlockSpec(shape, lambda b, i: (0,) * len(shape))
    return pl.pallas_call(
        _postmix_kernel,
        grid=(bsz, nt),
        in_specs=[pl.BlockSpec((None, tm, D_GMLP), lambda b, i: (b, i, 0)),
                  pl.BlockSpec((None, tm, D_SB), lambda b, i: (b, i, 0)),
                  pl.BlockSpec((None, tm, d), lambda b, i: (b, i, 0)),
                  pl.BlockSpec((None, 6, d), lambda b, i: (b, 0, 0)),
                  full((1, D_SB)), full((d, d)), full((1, d)), full((1, d)),
                  full((N_EXPERTS, d))],
        out_specs=[pl.BlockSpec((None, tm, d), lambda b, i: (b, i, 0)),
                   pl.BlockSpec((None, tm, d), lambda b, i: (b, i, 0)),
                   pl.BlockSpec((N_EXPERTS, tm), lambda b, i: (0, b * nt + i))],
        out_shape=[jax.ShapeDtypeStruct((bsz, s, d), F32),
                   jax.ShapeDtypeStruct((bsz, s, d), BF16),
                   jax.ShapeDtypeStruct((N_EXPERTS, bsz * s), F32)],
        scratch_shapes=[pltpu.VMEM((d, d), BF16)],
        compiler_params=pltpu.CompilerParams(
            dimension_semantics=("arbitrary", "arbitrary"), vmem_limit_bytes=VMEM_LIMIT),
        name="postmix",
    )(og, osb, x, mod3, g_br, w_out, g_post, g_pre, w_router_t)


def _first_index_of_max(x, idx, axis, size):
    m = jnp.max(x, axis=axis, keepdims=True)
    return jnp.min(jnp.where(x == m, idx, size), axis=axis, keepdims=True)


def _route_kernel(logit_ref, bias_ref, em_ref, tm_ref, cnt_ref):
    for u in range(ROUTE_BLOCKS):
        cols = slice(u * TB, (u + 1) * TB)
        both, cnt = _route_block(logit_ref[:, cols], bias_ref[...])
        em_ref[:, cols] = both
        tm_ref[cols, :] = both.T
        cnt_ref[u] = jnp.broadcast_to(cnt, (N_EXPERTS, LANES))


def _route_block(logits, bias):
    scores = jax.nn.sigmoid(logits)
    biased = scores + bias
    neg = jnp.float32(-jnp.inf)

    grouped = biased.reshape(N_GROUPS, GROUP_SIZE, TB)
    within = lax.broadcasted_iota(I32, grouped.shape, 1)
    top1 = jnp.max(grouped, axis=1, keepdims=True)
    first = _first_index_of_max(grouped, within, 1, GROUP_SIZE)
    top2 = jnp.max(jnp.where(within == first, neg, grouped), axis=1, keepdims=True)
    group_score = (top1 + top2).reshape(N_GROUPS, TB)

    gidx = lax.broadcasted_iota(I32, group_score.shape, 0)
    group_on = jnp.zeros(group_score.shape, jnp.bool_)
    for _ in range(TOPK_GROUPS):
        pick = gidx == _first_index_of_max(group_score, gidx, 0, N_GROUPS)
        group_on = jnp.logical_or(group_on, pick)
        group_score = jnp.where(pick, neg, group_score)

    masked = jnp.where(group_on.reshape(N_GROUPS, 1, TB), grouped, neg).reshape(N_EXPERTS, TB)
    eidx = lax.broadcasted_iota(I32, masked.shape, 0)
    chosen = jnp.zeros(masked.shape, jnp.bool_)
    for _ in range(TOP_K):
        pick = eidx == _first_index_of_max(masked, eidx, 0, N_EXPERTS)
        chosen = jnp.logical_or(chosen, pick)
        masked = jnp.where(pick, neg, masked)

    w = jnp.where(chosen, scores, 0.0)
    gates = w / jnp.sum(w, axis=0, keepdims=True) * ROUTED_SCALE

    chosen_f = chosen.astype(F32)
    tr = lax.broadcasted_iota(I32, (TB, TB), 0)
    tc = lax.broadcasted_iota(I32, (TB, TB), 1)
    rank = jnp.dot(chosen_f.astype(BF16), (tr < tc).astype(BF16), preferred_element_type=F32)
    both = jnp.concatenate([jnp.where(chosen, rank, -1.0), gates], axis=0)
    return both, jnp.sum(chosen_f, axis=1, keepdims=True)


def _route(logits_t, bias):
    e, t = logits_t.shape
    nblk = t // TB
    return pl.pallas_call(
        _route_kernel,
        grid=(nblk // ROUTE_BLOCKS,),
        in_specs=[pl.BlockSpec((e, ROUTE_BLOCKS * TB), lambda i: (0, i)),
                  pl.BlockSpec((e, 1), lambda i: (0, 0))],
        out_specs=[pl.BlockSpec((2 * e, ROUTE_BLOCKS * TB), lambda i: (0, i)),
                   pl.BlockSpec((ROUTE_BLOCKS * TB, 2 * e), lambda i: (i, 0)),
                   pl.BlockSpec((ROUTE_BLOCKS, e, LANES), lambda i: (i, 0, 0))],
        out_shape=[jax.ShapeDtypeStruct((2 * e, t), F32),
                   jax.ShapeDtypeStruct((t, 2 * e), F32),
                   jax.ShapeDtypeStruct((nblk, e, LANES), F32)],
        compiler_params=pltpu.CompilerParams(dimension_semantics=("parallel",)),
        name="route",
    )(logits_t, bias.reshape(e, 1))


def _dispatch_plan(cnt, n_tiles):
    nblk, e = cnt.shape
    pc = (cnt + ROW_CHUNK - 1) // ROW_CHUNK * ROW_CHUNK
    start = jnp.cumsum(pc, axis=1) - pc
    nchunk = jnp.sum(pc, axis=1) // ROW_CHUNK
    off = jnp.cumsum(pc, axis=0) - pc
    ecount = jnp.sum(pc, axis=0)
    epad = (ecount + TM_FFN - 1) // TM_FFN * TM_FFN
    gend = jnp.cumsum(epad)
    gbase = gend - epad
    nact = gend[-1] // TM_FFN
    tile_start = jnp.concatenate([gbase, gend[-1:]]) // TM_FFN
    cidx = jnp.arange(BLOCK_CHUNKS, dtype=I32)
    start16 = start // ROW_CHUNK
    shift = (gbase[None, :] + off) // ROW_CHUNK - start16
    dshift = shift - jnp.pad(shift, ((0, 0), (1, 0)))[:, :-1]
    in_or_after = (start16[:, None, :] <= cidx[None, :, None]).astype(I32)
    where = cidx[None, :] + jnp.sum(in_or_after * dshift[:, None, :], axis=2)
    used = cidx[None, :] < nchunk[:, None]
    spare = (n_tiles * TM_FFN // ROW_CHUNK
             + jnp.arange(nblk, dtype=I32)[:, None] * GROUP_CHUNKS + cidx[None, :] % GROUP_CHUNKS)
    gdst = jnp.where(used, where, spare)
    gsrc = jnp.where(used, where, 0)
    ngrp = (nchunk + GROUP_CHUNKS - 1) // GROUP_CHUNKS
    zbase = (gbase + ecount) // ROW_CHUNK
    zn = (epad - ecount) // ROW_CHUNK

    lo = jnp.tile(start.astype(F32), (1, 2))
    hi = jnp.tile((start + pc).astype(F32), (1, 2))
    bounds_row = jnp.stack([jnp.broadcast_to(lo[:, None, :], (nblk, SUBLANES, 2 * e)),
                            jnp.broadcast_to(hi[:, None, :], (nblk, SUBLANES, 2 * e))], axis=1)
    bounds_col = jnp.stack([jnp.broadcast_to(lo[:, :, None], (nblk, 2 * e, LANES)),
                            jnp.broadcast_to(hi[:, :, None], (nblk, 2 * e, LANES))], axis=1)
    as_i32 = lambda a: a.astype(I32)
    return dict(ngrp=as_i32(ngrp), gdst=as_i32(gdst.reshape(-1)), gsrc=as_i32(gsrc.reshape(-1)),
                zn=as_i32(zn), zbase=as_i32(zbase), tile_start=as_i32(tile_start),
                nact=as_i32(nact.reshape(1)), bounds_row=bounds_row, bounds_col=bounds_col)


def _dispatch_kernel(ngrp_ref, gdst_ref, zn_ref, zbase_ref, h_ref, em_ref, brow_ref,
                     xs_ref, buf_ref, zero_ref, sem, zsem):
    j = pl.program_id(0)
    last = pl.num_programs(0) - 1
    slot = j % 2

    def chunk_copy(s, c, g):
        return pltpu.make_async_copy(buf_ref.at[s, c], xs_ref.at[g], sem.at[s])

    def wait_block(jj, s):
        def body(i, carry):
            pltpu.make_async_copy(buf_ref.at[s, pl.ds(0, GROUP_CHUNKS)],
                                  xs_ref.at[pl.ds(0, GROUP_CHUNKS)], sem.at[s]).wait()
            return carry
        lax.fori_loop(0, ngrp_ref[jj], body, 0)

    def zero_copy(g, chunks):
        return pltpu.make_async_copy(zero_ref.at[pl.ds(0, chunks)], xs_ref.at[pl.ds(g, chunks)],
                                     zsem.at[0])

    def zero_fill(start):
        big = ZERO_ROWS // ROW_CHUNK

        def per_expert(e, carry):
            n = zn_ref[e]
            base = zbase_ref[e]

            def big_copy(i, c2):
                cp = zero_copy(base + i * big, big)
                cp.start() if start else cp.wait()
                return c2
            lax.fori_loop(0, n // big, big_copy, 0)

            def small_copy(i, c2):
                cp = zero_copy(base + n // big * big + i, 1)
                cp.start() if start else cp.wait()
                return c2
            lax.fori_loop(0, n % big, small_copy, 0)
            return carry
        lax.fori_loop(0, N_EXPERTS, per_expert, 0)

    @pl.when(j == 0)
    def _():
        zero_ref[...] = jnp.zeros_like(zero_ref)
        zero_fill(True)

    @pl.when(j >= 2)
    def _():
        wait_block(j - 2, slot)

    lo = brow_ref[0, 0:1, :]
    hi = brow_ref[1, 0:1, :]
    first_copy = lax.broadcasted_iota(I32, (KC, 2 * N_EXPERTS), 1) < N_EXPERTS
    row_iota = lax.broadcasted_iota(I32, (KC, 2 * N_EXPERTS), 0)
    ranks = em_ref[...].astype(BF16)
    h = h_ref[...]

    def trip(ti, carry):
        onehots = []
        for u in range(GROUP_ROWS // KC):
            rows_e = (row_iota + ti * GROUP_ROWS + u * KC).astype(F32)
            in_run = jnp.logical_and(jnp.logical_and(rows_e >= lo, rows_e < hi), first_copy)
            sel = jnp.dot(jnp.where(in_run, 1.0, 0.0).astype(BF16), ranks,
                          preferred_element_type=F32)
            local = jnp.sum(jnp.where(in_run, rows_e - lo + 3.0, 0.0), axis=1,
                            keepdims=True) - 3.0
            onehots.append(jnp.where(sel == local, 1.0, 0.0).astype(BF16))
        xs = jnp.dot(jnp.concatenate(onehots, axis=0), h, preferred_element_type=F32)
        first = pl.multiple_of(ti * GROUP_CHUNKS, GROUP_CHUNKS)
        buf_ref[slot, pl.ds(first, GROUP_CHUNKS)] = xs.astype(BF16).reshape(
            GROUP_CHUNKS, ROW_CHUNK, xs.shape[1])
        for k in range(GROUP_CHUNKS):
            chunk_copy(slot, first + k, gdst_ref[j * BLOCK_CHUNKS + first + k]).start()
        return carry
    lax.fori_loop(0, ngrp_ref[j], trip, 0)

    @pl.when(j == last)
    def _():
        wait_block(j, slot)

        @pl.when(j >= 1)
        def _():
            wait_block(j - 1, 1 - slot)

        zero_fill(False)


def _bounds_specs():
    return [pl.BlockSpec((None, 2, SUBLANES, 2 * N_EXPERTS), lambda j, *_: (j, 0, 0, 0)),
            pl.BlockSpec((None, 2, 2 * N_EXPERTS, LANES), lambda j, *_: (j, 0, 0, 0))]


def _dispatch(plan, h2_flat, em, n_rows):
    t, d = h2_flat.shape
    grid_spec = pltpu.PrefetchScalarGridSpec(
        num_scalar_prefetch=4,
        grid=(t // TB,),
        in_specs=[pl.BlockSpec((TB, d), lambda j, *_: (j, 0)),
                  pl.BlockSpec((2 * N_EXPERTS, TB), lambda j, *_: (0, j))] + _bounds_specs()[:1],
        out_specs=pl.BlockSpec(memory_space=pl.ANY),
        scratch_shapes=[pltpu.VMEM((2, BLOCK_CHUNKS, ROW_CHUNK, d), BF16),
                        pltpu.VMEM((ZERO_ROWS // ROW_CHUNK, ROW_CHUNK, d), BF16),
                        pltpu.SemaphoreType.DMA((2,)),
                        pltpu.SemaphoreType.DMA((1,))],
    )
    xs = pl.pallas_call(
        _dispatch_kernel,
        grid_spec=grid_spec,
        out_shape=jax.ShapeDtypeStruct((n_rows // ROW_CHUNK, ROW_CHUNK, d), BF16),
        compiler_params=pltpu.CompilerParams(
            dimension_semantics=("arbitrary",), vmem_limit_bytes=VMEM_LIMIT),
        name="moe_dispatch",
    )(plan["ngrp"], plan["gdst"], plan["zn"], plan["zbase"], h2_flat, em, plan["bounds_row"])
    return xs.reshape(n_rows, d)


def _ffn_kernel(tstart_ref, nact_ref, wg_ref, wu_ref, wd_ref, xs_ref, ys_ref,
                xbuf, ybuf, wgu_s, wd_s, xsem, ysem):
    e = pl.program_id(0)
    nact = nact_ref[0]
    t0 = tstart_ref[e]
    t1 = tstart_ref[e + 1]

    def x_copy(t):
        s = t % (FFN_AHEAD + 1)
        return pltpu.make_async_copy(
            xs_ref.at[pl.ds(pl.multiple_of(t * TM_FFN, TM_FFN), TM_FFN), :], xbuf.at[s],
            xsem.at[s])

    def y_copy(t):
        s = t % FFN_OUT
        return pltpu.make_async_copy(
            ybuf.at[s], ys_ref.at[pl.ds(pl.multiple_of(t * TM_FFN, TM_FFN), TM_FFN), :],
            ysem.at[s])

    @pl.when(e == 0)
    def _():
        for a in range(FFN_AHEAD):
            @pl.when(a < nact)
            def _():
                x_copy(a).start()

    @pl.when(t1 > t0)
    def _():
        wgu_s[:, :D_EXPERT] = wg_ref[...].astype(BF16)
        wgu_s[:, D_EXPERT:] = wu_ref[...].astype(BF16)
        wd_s[...] = wd_ref[...].astype(BF16)

        def tile(t, carry):
            x_copy(t).wait()

            @pl.when(t + FFN_AHEAD < nact)
            def _():
                x_copy(t + FFN_AHEAD).start()

            gu = jnp.dot(xbuf[t % (FFN_AHEAD + 1)], wgu_s[...], preferred_element_type=F32)
            act = _silu(gu[:, :D_EXPERT]) * gu[:, D_EXPERT:]
            y = jnp.dot(act.astype(BF16), wd_s[...], preferred_element_type=F32)

            @pl.when(t >= FFN_OUT)
            def _():
                y_copy(t - FFN_OUT).wait()

            ybuf[t % FFN_OUT] = y.astype(BF16)
            y_copy(t).start()
            return carry
        lax.fori_loop(t0, t1, tile, 0)

    @pl.when(e == pl.num_programs(0) - 1)
    def _():
        for back in range(FFN_OUT, 0, -1):
            @pl.when(nact >= back)
            def _():
                y_copy(nact - back).wait()


def _expert_ffn(tile_start, nact, xs, n_tiles, w_gate, w_up, w_down):
    n_rows, d = n_tiles * TM_FFN, xs.shape[1]
    f = D_EXPERT
    grid_spec = pltpu.PrefetchScalarGridSpec(
        num_scalar_prefetch=2,
        grid=(N_EXPERTS,),
        in_specs=[pl.BlockSpec((None, d, f), lambda e, ts, na: (e, 0, 0)),
                  pl.BlockSpec((None, d, f), lambda e, ts, na: (e, 0, 0)),
                  pl.BlockSpec((None, f, d), lambda e, ts, na: (e, 0, 0)),
                  pl.BlockSpec(memory_space=pl.ANY)],
        out_specs=pl.BlockSpec(memory_space=pl.ANY),
        scratch_shapes=[pltpu.VMEM((FFN_AHEAD + 1, TM_FFN, d), BF16),
                        pltpu.VMEM((FFN_OUT, TM_FFN, d), BF16),
                        pltpu.VMEM((d, 2 * f), BF16), pltpu.VMEM((f, d), BF16),
                        pltpu.SemaphoreType.DMA((FFN_AHEAD + 1,)),
                        pltpu.SemaphoreType.DMA((FFN_OUT,))],
    )
    return pl.pallas_call(
        _ffn_kernel,
        grid_spec=grid_spec,
        out_shape=jax.ShapeDtypeStruct((n_rows, d), BF16),
        compiler_params=pltpu.CompilerParams(
            dimension_semantics=("arbitrary",), vmem_limit_bytes=VMEM_LIMIT),
        name="moe_experts",
    )(tile_start, nact, w_gate, w_up, w_down, xs)


def _combine_kernel(ngrp_ref, gsrc_ref, ys_ref, tm_ref, brow_ref, bcol_ref, h_ref, wsg_ref,
                    wsu_ref, wsd_ref, x1_ref, mod_ref, gpost_ref, o_ref, buf_ref, p_ref, y_ref,
                    wsgu_s, wsd_s, sem):
    j = pl.program_id(0)
    last = pl.num_programs(0) - 1
    slot = j % 2

    def chunk_copy(s, c, g):
        return pltpu.make_async_copy(ys_ref.at[g], buf_ref.at[s, c], sem.at[s])

    def fetch_block(jj, s):
        def issue_group(gi, carry):
            for k in range(GROUP_CHUNKS):
                c = gi * GROUP_CHUNKS + k
                chunk_copy(s, c, gsrc_ref[jj * BLOCK_CHUNKS + c]).start()
            return carry
        lax.fori_loop(0, ngrp_ref[jj], issue_group, 0)

    @pl.when(j == 0)
    def _():
        buf_ref[...] = jnp.zeros_like(buf_ref)
        fetch_block(0, 0)

    @pl.when(j < last)
    def _():
        fetch_block(j + 1, 1 - slot)

    @pl.when(j == 0)
    def _():
        wsgu_s[:, :D_EXPERT] = wsg_ref[...].astype(BF16)
        wsgu_s[:, D_EXPERT:] = wsu_ref[...].astype(BF16)
        wsd_s[...] = wsd_ref[...].astype(BF16)

    gu = jnp.dot(h_ref[...], wsgu_s[...], preferred_element_type=F32)
    act = _silu(gu[:, :D_EXPERT]) * gu[:, D_EXPERT:]
    y = jnp.dot(act.astype(BF16), wsd_s[...], preferred_element_type=F32)

    tm = tm_ref[...]
    lane = lax.broadcasted_iota(I32, tm.shape, 1)
    rank_t = jnp.where(lane < N_EXPERTS, tm, 0.0).astype(BF16)
    gate_t = jnp.where(lane < N_EXPERTS, 0.0, tm).astype(BF16)
    lane8 = lax.broadcasted_iota(I32, (SUBLANES, 2 * N_EXPERTS), 1)
    lo_row = jnp.where(lane8 < N_EXPERTS, brow_ref[0], 0.0).astype(BF16)
    lo_col = jnp.concatenate([bcol_ref[0]] * (KC // LANES), axis=1)
    hi_col = jnp.concatenate([bcol_ref[1]] * (KC // LANES), axis=1)
    def build_gate_columns(first_cb, last_cb):
        for cb in range(first_cb, last_cb):
            rows = (lax.broadcasted_iota(I32, (2 * N_EXPERTS, KC), 1) + cb * KC).astype(F32)
            in_run = jnp.where(jnp.logical_and(rows >= lo_col, rows < hi_col),
                               1.0, 0.0).astype(BF16)
            sel_rank = jnp.dot(rank_t, in_run, preferred_element_type=F32)
            sel_gate = jnp.dot(gate_t, in_run, preferred_element_type=F32)
            sel_lo = jnp.dot(lo_row, in_run, preferred_element_type=F32)
            local = rows[0:1, :] - sel_lo[0:1, :]
            p_ref[:, cb * KC:(cb + 1) * KC] = jnp.where(sel_rank == local, sel_gate,
                                                        0.0).astype(BF16)

    def weighted_rows(first_row, last_row):
        slabs = buf_ref[slot, first_row // ROW_CHUNK:last_row // ROW_CHUNK]
        return jnp.dot(p_ref[:, first_row:last_row],
                       slabs.reshape(last_row - first_row, buf_ref.shape[3]),
                       preferred_element_type=F32)

    build_gate_columns(0, COMMON_ROWS // KC)

    def wait_group(gi, carry):
        pltpu.make_async_copy(ys_ref.at[pl.ds(0, GROUP_CHUNKS)],
                              buf_ref.at[slot, pl.ds(0, GROUP_CHUNKS)], sem.at[slot]).wait()
        return carry
    lax.fori_loop(0, ngrp_ref[j], wait_group, 0)

    y_ref[...] = y + weighted_rows(0, COMMON_ROWS)

    @pl.when(ngrp_ref[j] * GROUP_ROWS > COMMON_ROWS)
    def _():
        build_gate_columns(COMMON_ROWS // KC, BLOCK_ROWS // KC)
        y_ref[...] += weighted_rows(COMMON_ROWS, BLOCK_ROWS)

    y = y_ref[...]

    ga2 = mod_ref[5:6, :]
    o_ref[...] = x1_ref[...] + ga2 * _rms(y, gpost_ref[...])


def _combine(plan, ys, tm, h2_flat, ws_gate, ws_up, ws_down, x1_flat, mod3, g_post, s):
    t, d = h2_flat.shape
    f = ws_gate.shape[1]
    blocks_per_seq = s // TB
    full = lambda shape: pl.BlockSpec(shape, lambda j, *_: (0,) * len(shape))
    grid_spec = pltpu.PrefetchScalarGridSpec(
        num_scalar_prefetch=2,
        grid=(t // TB,),
        in_specs=[pl.BlockSpec(memory_space=pl.ANY),
                  pl.BlockSpec((TB, 2 * N_EXPERTS), lambda j, *_: (j, 0))] + _bounds_specs() + [
                  pl.BlockSpec((TB, d), lambda j, *_: (j, 0)),
                  full((d, f)), full((d, f)), full((f, d)),
                  pl.BlockSpec((TB, d), lambda j, *_: (j, 0)),
                  pl.BlockSpec((None, 6, d), lambda j, *_: (j // blocks_per_seq, 0, 0)),
                  full((1, d))],
        out_specs=pl.BlockSpec((TB, d), lambda j, *_: (j, 0)),
        scratch_shapes=[pltpu.VMEM((2, BLOCK_CHUNKS, ROW_CHUNK, d), BF16),
                        pltpu.VMEM((TB, BLOCK_ROWS), BF16),
                        pltpu.VMEM((TB, d), F32),
                        pltpu.VMEM((d, 2 * f), BF16), pltpu.VMEM((f, d), BF16),
                        pltpu.SemaphoreType.DMA((2,))],
    )
    return pl.pallas_call(
        _combine_kernel,
        grid_spec=grid_spec,
        out_shape=jax.ShapeDtypeStruct((t, d), F32),
        compiler_params=pltpu.CompilerParams(
            dimension_semantics=("arbitrary",), vmem_limit_bytes=VMEM_LIMIT),
        name="moe_combine",
    )(plan["ngrp"], plan["gsrc"], ys.reshape(-1, ROW_CHUNK, d), tm, plan["bounds_row"],
      plan["bounds_col"], h2_flat, ws_gate, ws_up, ws_down, x1_flat, mod3, g_post)


def kernel(x, c, w_ada, b_ada, g_pre_mix, w_in, ln_sgu_g, ln_sgu_b, w_spatial, b_spatial,
           g_branch, w_out, g_post_mix, g_pre_ffn, w_router, router_bias, w_gate, w_up, w_down,
           ws_gate, ws_up, ws_down, g_post_ffn):
    bsz, s, d = x.shape
    t = bsz * s
    nblk = t // TB
    max_rows = t * TOP_K + nblk * N_EXPERTS * (ROW_CHUNK - 1) + N_EXPERTS * (TM_FFN - 1)
    n_tiles = -(-max_rows // TM_FFN)
    row = lambda a: a.reshape(1, -1)
    for l in range(w_ada.shape[0]):
        mod3 = _ada(c, w_ada[l], b_ada[l]).reshape(bsz, 6, d)
        og, q, k, v = _premix(x, mod3, row(g_pre_mix[l]), w_in[l],
                              row(ln_sgu_g[l]), row(ln_sgu_b[l]), w_spatial[l],
                              b_spatial[l].T, row(g_branch[l, :D_GMLP]))
        osb = _attention(q, k, v)
        x1, h2, logits_t = _postmix(og, osb, x, mod3, row(g_branch[l, D_GMLP:]),
                                    w_out[l], row(g_post_mix[l]),
                                    row(g_pre_ffn[l]), w_router[l].T)
        em, tm, cnt = _route(logits_t, router_bias[l])
        plan = _dispatch_plan(cnt[:, :, 0].astype(I32), n_tiles)
        h2_flat = h2.reshape(t, d)
        xs = _dispatch(plan, h2_flat, em, n_tiles * TM_FFN + nblk * GROUP_ROWS)
        ys = _expert_ffn(plan["tile_start"], plan["nact"], xs, n_tiles,
                         w_gate[l], w_up[l], w_down[l])
        out = _combine(plan, ys, tm, h2_flat, ws_gate[l], ws_up[l], ws_down[l],
                       x1.reshape(t, d), mod3, row(g_post_ffn[l]), s)
        x = out.reshape(bsz, s, d)
    return x
```

```python
import jax
import jax.numpy as jnp
from jax import lax
from jax.experimental import pallas as pl
from jax.experimental.pallas import tpu as pltpu

F32 = jnp.float32
BF16 = jnp.bfloat16
I32 = jnp.int32

LANES = 128
SUBLANES = 8
V7X_VMEM_BYTES = 64 * 1024 * 1024

D_MODEL = 1024
D_GMLP = 512
GMLP_GROUPS = 4
GMLP_BLOCK = 128
CHUNK = 64
D_SB = 512
SB_HEAD_DIM = 64
N_PAIRS = D_SB // LANES
N_EXPERTS = 64
N_GROUPS = 8
GROUP_SIZE = N_EXPERTS // N_GROUPS
TOPK_GROUPS = 4
TOP_K = 8
D_EXPERT = 256
ROUTED_SCALE = 2.5
EPS = 1e-6
D_IN = 2 * D_GMLP + 3 * D_SB

ADA_TN = 1536
TM_MIX = 512
TQ = 128
VMEM_LIMIT = V7X_VMEM_BYTES * 7 // 8
ATTN_DEAD_LOG = -110.0

TB = 256
ROUTE_BLOCKS = 4
ROW_CHUNK = 16
KC = 256
GROUP_CHUNKS = 32
GROUP_ROWS = GROUP_CHUNKS * ROW_CHUNK
BLOCK_ROWS = -(-(TB * TOP_K + N_EXPERTS * (ROW_CHUNK - 1)) // GROUP_ROWS) * GROUP_ROWS
BLOCK_CHUNKS = BLOCK_ROWS // ROW_CHUNK
COMMON_ROWS = -(-(TB * TOP_K + N_EXPERTS * ROW_CHUNK // 2) // GROUP_ROWS) * GROUP_ROWS
TM_FFN = 1024
FFN_AHEAD = 4
FFN_OUT = 4
ZERO_ROWS = 128


def _rms(x, g):
    return x * lax.rsqrt(jnp.mean(x * x, axis=-1, keepdims=True) + EPS) * g


def _silu(x):
    return x * jax.nn.sigmoid(x)


def _ada_kernel(c_ref, w_ref, b_ref, o_ref):
    o_ref[...] = jnp.dot(_silu(c_ref[...]), w_ref[...], preferred_element_type=F32,
                         precision=lax.Precision.HIGHEST) + b_ref[...]


def _ada(c, w, b):
    bsz, d = c.shape
    n = w.shape[1]
    tn = ADA_TN
    return pl.pallas_call(
        _ada_kernel,
        grid=(n // tn,),
        in_specs=[pl.BlockSpec((bsz, d), lambda j: (0, 0)),
                  pl.BlockSpec((d, tn), lambda j: (0, j)),
                  pl.BlockSpec((1, tn), lambda j: (0, j))],
        out_specs=pl.BlockSpec((bsz, tn), lambda j: (0, j)),
        out_shape=jax.ShapeDtypeStruct((bsz, n), F32),
        name="ada_mod",
    )(c, w, b.reshape(1, n))


def _premix_kernel(x_ref, mod_ref, gpre_ref, win_ref, lng_ref, lnb_ref, wsp_ref, bsp_ref,
                   gbr_ref, og_ref, q_ref, k_ref, v_ref, win_s):
    @pl.when(jnp.logical_and(pl.program_id(0) == 0, pl.program_id(1) == 0))
    def _():
        win_s[...] = win_ref[...].astype(BF16)

    x = x_ref[...]
    sh1 = mod_ref[0:1, :]
    sc1 = mod_ref[1:2, :]
    h = _rms(x, gpre_ref[...]) * (1.0 + sc1) + sh1
    proj = jnp.dot(h.astype(BF16), win_s[...], preferred_element_type=F32)

    u = jax.nn.gelu(proj[:, :D_GMLP])
    v = jax.nn.gelu(proj[:, D_GMLP:2 * D_GMLP])
    mu = jnp.mean(v, axis=-1, keepdims=True)
    var = jnp.mean(jnp.square(v - mu), axis=-1, keepdims=True)
    v = ((v - mu) * lax.rsqrt(var + EPS) * lng_ref[...] + lnb_ref[...]).astype(BF16)

    i = lax.broadcasted_iota(I32, (GMLP_BLOCK, GMLP_BLOCK), 0)
    j = lax.broadcasted_iota(I32, (GMLP_BLOCK, GMLP_BLOCK), 1)
    causal = (j // CHUNK) <= (i // CHUNK)
    gd = D_GMLP // GMLP_GROUPS
    blocks = []
    for nb in range(x.shape[0] // GMLP_BLOCK):
        rows = slice(nb * GMLP_BLOCK, (nb + 1) * GMLP_BLOCK)
        cols = []
        for g in range(GMLP_GROUPS):
            w = jnp.where(causal, wsp_ref[g], 0.0).astype(BF16)
            mixed = jnp.dot(w, v[rows, g * gd:(g + 1) * gd], preferred_element_type=F32)
            cols.append(mixed + bsp_ref[:, g:g + 1])
        blocks.append(u[rows, :] * jnp.concatenate(cols, axis=1))
    og = jnp.concatenate(blocks, axis=0)
    og_ref[...] = _rms(og, gbr_ref[...]).astype(BF16)

    base = 2 * D_GMLP
    scale = SB_HEAD_DIM ** -0.5
    for p in range(N_PAIRS):
        q_ref[p] = (proj[:, base + LANES * p:base + LANES * (p + 1)] * scale).astype(BF16)
        k_ref[p] = proj[:, base + D_SB + LANES * p:base + D_SB + LANES * (p + 1)].astype(BF16)
        v_ref[p] = proj[:, base + 2 * D_SB + LANES * p:base + 2 * D_SB + LANES * (p + 1)].astype(BF16)


def _premix(x, mod3, g_pre, w_in, ln_g, ln_b, w_sp, b_sp_t, g_br):
    bsz, s, d = x.shape
    tm = TM_MIX
    full = lambda shape: pl.BlockSpec(shape, lambda b, i: (0,) * len(shape))
    qkv_spec = pl.BlockSpec((None, N_PAIRS, tm, LANES), lambda b, i: (b, 0, i, 0))
    qkv_shape = jax.ShapeDtypeStruct((bsz, N_PAIRS, s, LANES), BF16)
    return pl.pallas_call(
        _premix_kernel,
        grid=(bsz, s // tm),
        in_specs=[pl.BlockSpec((None, tm, d), lambda b, i: (b, i, 0)),
                  pl.BlockSpec((None, 6, d), lambda b, i: (b, 0, 0)),
                  full((1, d)), full((d, D_IN)), full((1, D_GMLP)), full((1, D_GMLP)),
                  full((GMLP_GROUPS, GMLP_BLOCK, GMLP_BLOCK)), full((GMLP_BLOCK, GMLP_GROUPS)),
                  full((1, D_GMLP))],
        out_specs=[pl.BlockSpec((None, tm, D_GMLP), lambda b, i: (b, i, 0)),
                   qkv_spec, qkv_spec, qkv_spec],
        out_shape=[jax.ShapeDtypeStruct((bsz, s, D_GMLP), BF16), qkv_shape, qkv_shape, qkv_shape],
        scratch_shapes=[pltpu.VMEM((d, D_IN), BF16)],
        compiler_params=pltpu.CompilerParams(
            dimension_semantics=("arbitrary", "arbitrary"), vmem_limit_bytes=VMEM_LIMIT),
        name="premix",
    )(x, mod3, g_pre, w_in, ln_g, ln_b, w_sp, b_sp_t, g_br)


def _attn_kernel(q_ref, k_ref, v_ref, o_ref, *scratch):
    qs_refs = scratch[:N_PAIRS]
    acc_refs = scratch[N_PAIRS:2 * N_PAIRS]
    carry_refs = scratch[2 * N_PAIRS:]
    qi = pl.program_id(1)
    first_head = lax.broadcasted_iota(I32, (TQ, LANES), 1) < SB_HEAD_DIM
    for p in range(N_PAIRS):
        q2 = q_ref[p]
        zero = jnp.zeros_like(q2)
        qs_refs[p][:TQ, :] = jnp.where(first_head, q2, zero)
        qs_refs[p][TQ:, :] = jnp.where(first_head, zero, q2)
        acc_refs[p][...] = jnp.zeros_like(acc_refs[p])
        carry_refs[p][...] = jnp.zeros_like(carry_refs[p])

    r = jnp.bitwise_and(lax.broadcasted_iota(I32, (2 * TQ, TQ), 0), TQ - 1)
    c = lax.broadcasted_iota(I32, (2 * TQ, TQ), 1)
    strict_causal = c < r
    kr = lax.broadcasted_iota(I32, (TQ, TQ), 0)
    kc = lax.broadcasted_iota(I32, (TQ, TQ), 1)
    suffix = jnp.concatenate([(kr > kc).astype(BF16), jnp.ones((TQ, TQ), BF16)], axis=1)
    suffix2 = jnp.concatenate([suffix, suffix], axis=0)

    def key_block(j, diagonal):
        start = pl.multiple_of(j * TQ, TQ)
        pairs = range(N_PAIRS)
        zs = [lax.dot_general(qs_refs[p][...], k_ref[p, pl.ds(start, TQ), :],
                              (((1,), (1,)), ((), ())), preferred_element_type=F32)
              for p in pairs]
        log_betas, splits = [], []
        for p in pairs:
            z = zs[p]
            log_beta = jnp.minimum(z, 0.0) - jnp.log(1.0 + jnp.exp(-jnp.abs(z)))
            log_1mb = log_beta - z
            if diagonal:
                log_1mb = jnp.where(strict_causal, log_1mb, 0.0)
            hi = log_1mb.astype(BF16)
            lo = (log_1mb - hi.astype(F32)).astype(BF16)
            log_betas.append(log_beta)
            splits.append(jnp.concatenate([hi, lo], axis=1))
        sums = [jnp.dot(splits[p], suffix2, preferred_element_type=F32) for p in pairs]
        weights = []
        live = None
        for p in pairs:
            s = sums[p]
            carry = carry_refs[p][...]
            a = jnp.exp(log_betas[p] + carry + s[:, :TQ])
            if diagonal:
                a = jnp.where(strict_causal, a, 0.0)
            weights.append(a.astype(BF16))
            carry = carry + s[:, TQ:]
            carry_refs[p][...] = carry
            live = carry if live is None else jnp.maximum(live, carry)
        for p in pairs:
            acc_refs[p][...] += jnp.dot(weights[p], v_ref[p, pl.ds(start, TQ), :],
                                        preferred_element_type=F32)
        return jnp.max(live)

    live = key_block(qi, True)

    def cond(state):
        j, live = state
        return jnp.logical_and(j >= 0, live > ATTN_DEAD_LOG)

    def body(state):
        j, _ = state
        return j - 1, key_block(j, False)

    lax.while_loop(cond, body, (qi - 1, live))
    for p in range(N_PAIRS):
        o_ref[:, LANES * p:LANES * (p + 1)] = jnp.where(first_head, acc_refs[p][:TQ, :],
                                                    acc_refs[p][TQ:, :])


def _attention(q, k, v):
    bsz, npair, s, _ = q.shape
    kv_spec = pl.BlockSpec((None, npair, s, LANES), lambda b, i: (b, 0, 0, 0))
    return pl.pallas_call(
        _attn_kernel,
        grid=(bsz, s // TQ),
        in_specs=[pl.BlockSpec((None, npair, TQ, LANES), lambda b, i: (b, 0, i, 0)),
                  kv_spec, kv_spec],
        out_specs=pl.BlockSpec((None, TQ, npair * LANES), lambda b, i: (b, i, 0)),
        out_shape=jax.ShapeDtypeStruct((bsz, s, npair * LANES), F32),
        scratch_shapes=([pltpu.VMEM((2 * TQ, LANES), BF16)] * npair
                        + [pltpu.VMEM((2 * TQ, LANES), F32)] * npair
                        + [pltpu.VMEM((2 * TQ, TQ), F32)] * npair),
        compiler_params=pltpu.CompilerParams(
            dimension_semantics=("parallel", "parallel"), vmem_limit_bytes=VMEM_LIMIT),
        name="stickbreak_attn",
    )(q, k, v)


def _postmix_kernel(og_ref, osb_ref, x_ref, mod_ref, gbr_ref, wout_ref, gpost_ref, gpre_ref,
                    wrt_ref, x1_ref, h2_ref, logit_ref, wout_s):
    @pl.when(jnp.logical_and(pl.program_id(0) == 0, pl.program_id(1) == 0))
    def _():
        wout_s[...] = wout_ref[...].astype(BF16)

    ga1 = mod_ref[2:3, :]
    sh2 = mod_ref[3:4, :]
    sc2 = mod_ref[4:5, :]
    osb = _rms(osb_ref[...], gbr_ref[...]).astype(BF16)
    m = (jnp.dot(og_ref[...], wout_s[:D_GMLP, :], preferred_element_type=F32)
         + jnp.dot(osb, wout_s[D_GMLP:, :], preferred_element_type=F32))
    x1 = x_ref[...] + ga1 * _rms(m, gpost_ref[...])
    x1_ref[...] = x1
    h2 = _rms(x1, gpre_ref[...]) * (1.0 + sc2) + sh2
    h_hi = h2.astype(BF16)
    h2_ref[...] = h_hi
    h_lo = (h2 - h_hi.astype(F32)).astype(BF16)
    w = wrt_ref[...]
    w_hi = w.astype(BF16)
    w_lo = (w - w_hi.astype(F32)).astype(BF16)
    nt = (((1,), (1,)), ((), ()))
    by_hi = lax.dot_general(jnp.concatenate([w_hi, w_lo], axis=0), h_hi, nt,
                            preferred_element_type=F32)
    by_lo = lax.dot_general(w_hi, h_lo, nt, preferred_element_type=F32)
    logit_ref[...] = by_hi[:N_EXPERTS] + by_hi[N_EXPERTS:] + by_lo


def _postmix(og, osb, x, mod3, g_br, w_out, g_post, g_pre, w_router_t):
    bsz, s, d = x.shape
    tm = TM_MIX
    nt = s // tm
    full = lambda shape: pl.BlockSpec(shape, lambda b, i: (0,) * len(shape))
    return pl.pallas_call(
        _postmix_kernel,
        grid=(bsz, nt),
        in_specs=[pl.BlockSpec((None, tm, D_GMLP), lambda b, i: (b, i, 0)),
                  pl.BlockSpec((None, tm, D_SB), lambda b, i: (b, i, 0)),
                  pl.BlockSpec((None, tm, d), lambda b, i: (b, i, 0)),
                  pl.BlockSpec((None, 6, d), lambda b, i: (b, 0, 0)),
                  full((1, D_SB)), full((d, d)), full((1, d)), full((1, d)),
                  full((N_EXPERTS, d))],
        out_specs=[pl.BlockSpec((None, tm, d), lambda b, i: (b, i, 0)),
                   pl.BlockSpec((None, tm, d), lambda b, i: (b, i, 0)),
                   pl.BlockSpec((N_EXPERTS, tm), lambda b, i: (0, b * nt + i))],
        out_shape=[jax.ShapeDtypeStruct((bsz, s, d), F32),
                   jax.ShapeDtypeStruct((bsz, s, d), BF16),
                   jax.ShapeDtypeStruct((N_EXPERTS, bsz * s), F32)],
        scratch_shapes=[pltpu.VMEM((d, d), BF16)],
        compiler_params=pltpu.CompilerParams(
            dimension_semantics=("arbitrary", "arbitrary"), vmem_limit_bytes=VMEM_LIMIT),
        name="postmix",
    )(og, osb, x, mod3, g_br, w_out, g_post, g_pre, w_router_t)


def _first_index_of_max(x, idx, axis, size):
    m = jnp.max(x, axis=axis, keepdims=True)
    return jnp.min(jnp.where(x == m, idx, size), axis=axis, keepdims=True)


def _route_kernel(logit_ref, bias_ref, em_ref, tm_ref, cnt_ref):
    for u in range(ROUTE_BLOCKS):
        cols = slice(u * TB, (u + 1) * TB)
        both, cnt = _route_block(logit_ref[:, cols], bias_ref[...])
        em_ref[:, cols] = both
        tm_ref[cols, :] = both.T
        cnt_ref[u] = jnp.broadcast_to(cnt, (N_EXPERTS, LANES))


def _route_block(logits, bias):
    scores = jax.nn.sigmoid(logits)
    biased = scores + bias
    neg = jnp.float32(-jnp.inf)

    grouped = biased.reshape(N_GROUPS, GROUP_SIZE, TB)
    within = lax.broadcasted_iota(I32, grouped.shape, 1)
    top1 = jnp.max(grouped, axis=1, keepdims=True)
    first = _first_index_of_max(grouped, within, 1, GROUP_SIZE)
    top2 = jnp.max(jnp.where(within == first, neg, grouped), axis=1, keepdims=True)
    group_score = (top1 + top2).reshape(N_GROUPS, TB)

    gidx = lax.broadcasted_iota(I32, group_score.shape, 0)
    group_on = jnp.zeros(group_score.shape, jnp.bool_)
    for _ in range(TOPK_GROUPS):
        pick = gidx == _first_index_of_max(group_score, gidx, 0, N_GROUPS)
        group_on = jnp.logical_or(group_on, pick)
        group_score = jnp.where(pick, neg, group_score)

    masked = jnp.where(group_on.reshape(N_GROUPS, 1, TB), grouped, neg).reshape(N_EXPERTS, TB)
    eidx = lax.broadcasted_iota(I32, masked.shape, 0)
    chosen = jnp.zeros(masked.shape, jnp.bool_)
    for _ in range(TOP_K):
        pick = eidx == _first_index_of_max(masked, eidx, 0, N_EXPERTS)
        chosen = jnp.logical_or(chosen, pick)
        masked = jnp.where(pick, neg, masked)

    w = jnp.where(chosen, scores, 0.0)
    gates = w / jnp.sum(w, axis=0, keepdims=True) * ROUTED_SCALE

    chosen_f = chosen.astype(F32)
    tr = lax.broadcasted_iota(I32, (TB, TB), 0)
    tc = lax.broadcasted_iota(I32, (TB, TB), 1)
    rank = jnp.dot(chosen_f.astype(BF16), (tr < tc).astype(BF16), preferred_element_type=F32)
    both = jnp.concatenate([jnp.where(chosen, rank, -1.0), gates], axis=0)
    return both, jnp.sum(chosen_f, axis=1, keepdims=True)


def _route(logits_t, bias):
    e, t = logits_t.shape
    nblk = t // TB
    return pl.pallas_call(
        _route_kernel,
        grid=(nblk // ROUTE_BLOCKS,),
        in_specs=[pl.BlockSpec((e, ROUTE_BLOCKS * TB), lambda i: (0, i)),
                  pl.BlockSpec((e, 1), lambda i: (0, 0))],
        out_specs=[pl.BlockSpec((2 * e, ROUTE_BLOCKS * TB), lambda i: (0, i)),
                   pl.BlockSpec((ROUTE_BLOCKS * TB, 2 * e), lambda i: (i, 0)),
                   pl.BlockSpec((ROUTE_BLOCKS, e, LANES), lambda i: (i, 0, 0))],
        out_shape=[jax.ShapeDtypeStruct((2 * e, t), F32),
                   jax.ShapeDtypeStruct((t, 2 * e), F32),
                   jax.ShapeDtypeStruct((nblk, e, LANES), F32)],
        compiler_params=pltpu.CompilerParams(dimension_semantics=("parallel",)),
        name="route",
    )(logits_t, bias.reshape(e, 1))


def _dispatch_plan(cnt, n_tiles):
    nblk, e = cnt.shape
    pc = (cnt + ROW_CHUNK - 1) // ROW_CHUNK * ROW_CHUNK
    start = jnp.cumsum(pc, axis=1) - pc
    nchunk = jnp.sum(pc, axis=1) // ROW_CHUNK
    off = jnp.cumsum(pc, axis=0) - pc
    ecount = jnp.sum(pc, axis=0)
    epad = (ecount + TM_FFN - 1) // TM_FFN * TM_FFN
    gend = jnp.cumsum(epad)
    gbase = gend - epad
    nact = gend[-1] // TM_FFN
    tile_start = jnp.concatenate([gbase, gend[-1:]]) // TM_FFN
    cidx = jnp.arange(BLOCK_CHUNKS, dtype=I32)
    start16 = start // ROW_CHUNK
    shift = (gbase[None, :] + off) // ROW_CHUNK - start16
    dshift = shift - jnp.pad(shift, ((0, 0), (1, 0)))[:, :-1]
    in_or_after = (start16[:, None, :] <= cidx[None, :, None]).astype(I32)
    where = cidx[None, :] + jnp.sum(in_or_after * dshift[:, None, :], axis=2)
    used = cidx[None, :] < nchunk[:, None]
    spare = (n_tiles * TM_FFN // ROW_CHUNK
             + jnp.arange(nblk, dtype=I32)[:, None] * GROUP_CHUNKS + cidx[None, :] % GROUP_CHUNKS)
    gdst = jnp.where(used, where, spare)
    gsrc = jnp.where(used, where, 0)
    ngrp = (nchunk + GROUP_CHUNKS - 1) // GROUP_CHUNKS
    zbase = (gbase + ecount) // ROW_CHUNK
    zn = (epad - ecount) // ROW_CHUNK

    lo = jnp.tile(start.astype(F32), (1, 2))
    hi = jnp.tile((start + pc).astype(F32), (1, 2))
    bounds_row = jnp.stack([jnp.broadcast_to(lo[:, None, :], (nblk, SUBLANES, 2 * e)),
                            jnp.broadcast_to(hi[:, None, :], (nblk, SUBLANES, 2 * e))], axis=1)
    bounds_col = jnp.stack([jnp.broadcast_to(lo[:, :, None], (nblk, 2 * e, LANES)),
                            jnp.broadcast_to(hi[:, :, None], (nblk, 2 * e, LANES))], axis=1)
    as_i32 = lambda a: a.astype(I32)
    return dict(ngrp=as_i32(ngrp), gdst=as_i32(gdst.reshape(-1)), gsrc=as_i32(gsrc.reshape(-1)),
                zn=as_i32(zn), zbase=as_i32(zbase), tile_start=as_i32(tile_start),
                nact=as_i32(nact.reshape(1)), bounds_row=bounds_row, bounds_col=bounds_col)


def _dispatch_kernel(ngrp_ref, gdst_ref, zn_ref, zbase_ref, h_ref, em_ref, brow_ref,
                     xs_ref, buf_ref, zero_ref, sem, zsem):
    j = pl.program_id(0)
    last = pl.num_programs(0) - 1
    slot = j % 2

    def chunk_copy(s, c, g):
        return pltpu.make_async_copy(buf_ref.at[s, c], xs_ref.at[g], sem.at[s])

    def wait_block(jj, s):
        def body(i, carry):
            pltpu.make_async_copy(buf_ref.at[s, pl.ds(0, GROUP_CHUNKS)],
                                  xs_ref.at[pl.ds(0, GROUP_CHUNKS)], sem.at[s]).wait()
            return carry
        lax.fori_loop(0, ngrp_ref[jj], body, 0)

    def zero_copy(g, chunks):
        return pltpu.make_async_copy(zero_ref.at[pl.ds(0, chunks)], xs_ref.at[pl.ds(g, chunks)],
                                     zsem.at[0])

    def zero_fill(start):
        big = ZERO_ROWS // ROW_CHUNK

        def per_expert(e, carry):
            n = zn_ref[e]
            base = zbase_ref[e]

            def big_copy(i, c2):
                cp = zero_copy(base + i * big, big)
                cp.start() if start else cp.wait()
                return c2
            lax.fori_loop(0, n // big, big_copy, 0)

            def small_copy(i, c2):
                cp = zero_copy(base + n // big * big + i, 1)
                cp.start() if start else cp.wait()
                return c2
            lax.fori_loop(0, n % big, small_copy, 0)
            return carry
        lax.fori_loop(0, N_EXPERTS, per_expert, 0)

    @pl.when(j == 0)
    def _():
        zero_ref[...] = jnp.zeros_like(zero_ref)
        zero_fill(True)

    @pl.when(j >= 2)
    def _():
        wait_block(j - 2, slot)

    lo = brow_ref[0, 0:1, :]
    hi = brow_ref[1, 0:1, :]
    first_copy = lax.broadcasted_iota(I32, (KC, 2 * N_EXPERTS), 1) < N_EXPERTS
    row_iota = lax.broadcasted_iota(I32, (KC, 2 * N_EXPERTS), 0)
    ranks = em_ref[...].astype(BF16)
    h = h_ref[...]

    def trip(ti, carry):
        onehots = []
        for u in range(GROUP_ROWS // KC):
            rows_e = (row_iota + ti * GROUP_ROWS + u * KC).astype(F32)
            in_run = jnp.logical_and(jnp.logical_and(rows_e >= lo, rows_e < hi), first_copy)
            sel = jnp.dot(jnp.where(in_run, 1.0, 0.0).astype(BF16), ranks,
                          preferred_element_type=F32)
            local = jnp.sum(jnp.where(in_run, rows_e - lo + 3.0, 0.0), axis=1,
                            keepdims=True) - 3.0
            onehots.append(jnp.where(sel == local, 1.0, 0.0).astype(BF16))
        xs = jnp.dot(jnp.concatenate(onehots, axis=0), h, preferred_element_type=F32)
        first = pl.multiple_of(ti * GROUP_CHUNKS, GROUP_CHUNKS)
        buf_ref[slot, pl.ds(first, GROUP_CHUNKS)] = xs.astype(BF16).reshape(
            GROUP_CHUNKS, ROW_CHUNK, xs.shape[1])
        for k in range(GROUP_CHUNKS):
            chunk_copy(slot, first + k, gdst_ref[j * BLOCK_CHUNKS + first + k]).start()
        return carry
    lax.fori_loop(0, ngrp_ref[j], trip, 0)

    @pl.when(j == last)
    def _():
        wait_block(j, slot)

        @pl.when(j >= 1)
        def _():
            wait_block(j - 1, 1 - slot)

        zero_fill(False)


def _bounds_specs():
    return [pl.BlockSpec((None, 2, SUBLANES, 2 * N_EXPERTS), lambda j, *_: (j, 0, 0, 0)),
            pl.BlockSpec((None, 2, 2 * N_EXPERTS, LANES), lambda j, *_: (j, 0, 0, 0))]


def _dispatch(plan, h2_flat, em, n_rows):
    t, d = h2_flat.shape
    grid_spec = pltpu.PrefetchScalarGridSpec(
        num_scalar_prefetch=4,
        grid=(t // TB,),
        in_specs=[pl.BlockSpec((TB, d), lambda j, *_: (j, 0)),
                  pl.BlockSpec((2 * N_EXPERTS, TB), lambda j, *_: (0, j))] + _bounds_specs()[:1],
        out_specs=pl.BlockSpec(memory_space=pl.ANY),
        scratch_shapes=[pltpu.VMEM((2, BLOCK_CHUNKS, ROW_CHUNK, d), BF16),
                        pltpu.VMEM((ZERO_ROWS // ROW_CHUNK, ROW_CHUNK, d), BF16),
                        pltpu.SemaphoreType.DMA((2,)),
                        pltpu.SemaphoreType.DMA((1,))],
    )
    xs = pl.pallas_call(
        _dispatch_kernel,
        grid_spec=grid_spec,
        out_shape=jax.ShapeDtypeStruct((n_rows // ROW_CHUNK, ROW_CHUNK, d), BF16),
        compiler_params=pltpu.CompilerParams(
            dimension_semantics=("arbitrary",), vmem_limit_bytes=VMEM_LIMIT),
        name="moe_dispatch",
    )(plan["ngrp"], plan["gdst"], plan["zn"], plan["zbase"], h2_flat, em, plan["bounds_row"])
    return xs.reshape(n_rows, d)


def _ffn_kernel(tstart_ref, nact_ref, wg_ref, wu_ref, wd_ref, xs_ref, ys_ref,
                xbuf, ybuf, wgu_s, wd_s, xsem, ysem):
    e = pl.program_id(0)
    nact = nact_ref[0]
    t0 = tstart_ref[e]
    t1 = tstart_ref[e + 1]

    def x_copy(t):
        s = t % (FFN_AHEAD + 1)
        return pltpu.make_async_copy(
            xs_ref.at[pl.ds(pl.multiple_of(t * TM_FFN, TM_FFN), TM_FFN), :], xbuf.at[s],
            xsem.at[s])

    def y_copy(t):
        s = t % FFN_OUT
        return pltpu.make_async_copy(
            ybuf.at[s], ys_ref.at[pl.ds(pl.multiple_of(t * TM_FFN, TM_FFN), TM_FFN), :],
            ysem.at[s])

    @pl.when(e == 0)
    def _():
        for a in range(FFN_AHEAD):
            @pl.when(a < nact)
            def _():
                x_copy(a).start()

    @pl.when(t1 > t0)
    def _():
        wgu_s[:, :D_EXPERT] = wg_ref[...].astype(BF16)
        wgu_s[:, D_EXPERT:] = wu_ref[...].astype(BF16)
        wd_s[...] = wd_ref[...].astype(BF16)

        def tile(t, carry):
            x_copy(t).wait()

            @pl.when(t + FFN_AHEAD < nact)
            def _():
                x_copy(t + FFN_AHEAD).start()

            gu = jnp.dot(xbuf[t % (FFN_AHEAD + 1)], wgu_s[...], preferred_element_type=F32)
            act = _silu(gu[:, :D_EXPERT]) * gu[:, D_EXPERT:]
            y = jnp.dot(act.astype(BF16), wd_s[...], preferred_element_type=F32)

            @pl.when(t >= FFN_OUT)
            def _():
                y_copy(t - FFN_OUT).wait()

            ybuf[t % FFN_OUT] = y.astype(BF16)
            y_copy(t).start()
            return carry
        lax.fori_loop(t0, t1, tile, 0)

    @pl.when(e == pl.num_programs(0) - 1)
    def _():
        for back in range(FFN_OUT, 0, -1):
            @pl.when(nact >= back)
            def _():
                y_copy(nact - back).wait()


def _expert_ffn(tile_start, nact, xs, n_tiles, w_gate, w_up, w_down):
    n_rows, d = n_tiles * TM_FFN, xs.shape[1]
    f = D_EXPERT
    grid_spec = pltpu.PrefetchScalarGridSpec(
        num_scalar_prefetch=2,
        grid=(N_EXPERTS,),
        in_specs=[pl.BlockSpec((None, d, f), lambda e, ts, na: (e, 0, 0)),
                  pl.BlockSpec((None, d, f), lambda e, ts, na: (e, 0, 0)),
                  pl.BlockSpec((None, f, d), lambda e, ts, na: (e, 0, 0)),
                  pl.BlockSpec(memory_space=pl.ANY)],
        out_specs=pl.BlockSpec(memory_space=pl.ANY),
        scratch_shapes=[pltpu.VMEM((FFN_AHEAD + 1, TM_FFN, d), BF16),
                        pltpu.VMEM((FFN_OUT, TM_FFN, d), BF16),
                        pltpu.VMEM((d, 2 * f), BF16), pltpu.VMEM((f, d), BF16),
                        pltpu.SemaphoreType.DMA((FFN_AHEAD + 1,)),
                        pltpu.SemaphoreType.DMA((FFN_OUT,))],
    )
    return pl.pallas_call(
        _ffn_kernel,
        grid_spec=grid_spec,
        out_shape=jax.ShapeDtypeStruct((n_rows, d), BF16),
        compiler_params=pltpu.CompilerParams(
            dimension_semantics=("arbitrary",), vmem_limit_bytes=VMEM_LIMIT),
        name="moe_experts",
    )(tile_start, nact, w_gate, w_up, w_down, xs)


def _combine_kernel(ngrp_ref, gsrc_ref, ys_ref, tm_ref, brow_ref, bcol_ref, h_ref, wsg_ref,
                    wsu_ref, wsd_ref, x1_ref, mod_ref, gpost_ref, o_ref, buf_ref, p_ref, y_ref,
                    wsgu_s, wsd_s, sem):
    j = pl.program_id(0)
    last = pl.num_programs(0) - 1
    slot = j % 2

    def chunk_copy(s, c, g):
        return pltpu.make_async_copy(ys_ref.at[g], buf_ref.at[s, c], sem.at[s])

    def fetch_block(jj, s):
        def issue_group(gi, carry):
            for k in range(GROUP_CHUNKS):
                c = gi * GROUP_CHUNKS + k
                chunk_copy(s, c, gsrc_ref[jj * BLOCK_CHUNKS + c]).start()
            return carry
        lax.fori_loop(0, ngrp_ref[jj], issue_group, 0)

    @pl.when(j == 0)
    def _():
        buf_ref[...] = jnp.zeros_like(buf_ref)
        fetch_block(0, 0)

    @pl.when(j < last)
    def _():
        fetch_block(j + 1, 1 - slot)

    @pl.when(j == 0)
    def _():
        wsgu_s[:, :D_EXPERT] = wsg_ref[...].astype(BF16)
        wsgu_s[:, D_EXPERT:] = wsu_ref[...].astype(BF16)
        wsd_s[...] = wsd_ref[...].astype(BF16)

    gu = jnp.dot(h_ref[...], wsgu_s[...], preferred_element_type=F32)
    act = _silu(gu[:, :D_EXPERT]) * gu[:, D_EXPERT:]
    y = jnp.dot(act.astype(BF16), wsd_s[...], preferred_element_type=F32)

    tm = tm_ref[...]
    lane = lax.broadcasted_iota(I32, tm.shape, 1)
    rank_t = jnp.where(lane < N_EXPERTS, tm, 0.0).astype(BF16)
    gate_t = jnp.where(lane < N_EXPERTS, 0.0, tm).astype(BF16)
    lane8 = lax.broadcasted_iota(I32, (SUBLANES, 2 * N_EXPERTS), 1)
    lo_row = jnp.where(lane8 < N_EXPERTS, brow_ref[0], 0.0).astype(BF16)
    lo_col = jnp.concatenate([bcol_ref[0]] * (KC // LANES), axis=1)
    hi_col = jnp.concatenate([bcol_ref[1]] * (KC // LANES), axis=1)
    def build_gate_columns(first_cb, last_cb):
        for cb in range(first_cb, last_cb):
            rows = (lax.broadcasted_iota(I32, (2 * N_EXPERTS, KC), 1) + cb * KC).astype(F32)
            in_run = jnp.where(jnp.logical_and(rows >= lo_col, rows < hi_col),
                               1.0, 0.0).astype(BF16)
            sel_rank = jnp.dot(rank_t, in_run, preferred_element_type=F32)
            sel_gate = jnp.dot(gate_t, in_run, preferred_element_type=F32)
            sel_lo = jnp.dot(lo_row, in_run, preferred_element_type=F32)
            local = rows[0:1, :] - sel_lo[0:1, :]
            p_ref[:, cb * KC:(cb + 1) * KC] = jnp.where(sel_rank == local, sel_gate,
                                                        0.0).astype(BF16)

    def weighted_rows(first_row, last_row):
        slabs = buf_ref[slot, first_row // ROW_CHUNK:last_row // ROW_CHUNK]
        return jnp.dot(p_ref[:, first_row:last_row],
                       slabs.reshape(last_row - first_row, buf_ref.shape[3]),
                       preferred_element_type=F32)

    build_gate_columns(0, COMMON_ROWS // KC)

    def wait_group(gi, carry):
        pltpu.make_async_copy(ys_ref.at[pl.ds(0, GROUP_CHUNKS)],
                              buf_ref.at[slot, pl.ds(0, GROUP_CHUNKS)], sem.at[slot]).wait()
        return carry
    lax.fori_loop(0, ngrp_ref[j], wait_group, 0)

    y_ref[...] = y + weighted_rows(0, COMMON_ROWS)

    @pl.when(ngrp_ref[j] * GROUP_ROWS > COMMON_ROWS)
    def _():
        build_gate_columns(COMMON_ROWS // KC, BLOCK_ROWS // KC)
        y_ref[...] += weighted_rows(COMMON_ROWS, BLOCK_ROWS)

    y = y_ref[...]

    ga2 = mod_ref[5:6, :]
    o_ref[...] = x1_ref[...] + ga2 * _rms(y, gpost_ref[...])


def _combine(plan, ys, tm, h2_flat, ws_gate, ws_up, ws_down, x1_flat, mod3, g_post, s):
    t, d = h2_flat.shape
    f = ws_gate.shape[1]
    blocks_per_seq = s // TB
    full = lambda shape: pl.BlockSpec(shape, lambda j, *_: (0,) * len(shape))
    grid_spec = pltpu.PrefetchScalarGridSpec(
        num_scalar_prefetch=2,
        grid=(t // TB,),
        in_specs=[pl.BlockSpec(memory_space=pl.ANY),
                  pl.BlockSpec((TB, 2 * N_EXPERTS), lambda j, *_: (j, 0))] + _bounds_specs() + [
                  pl.BlockSpec((TB, d), lambda j, *_: (j, 0)),
                  full((d, f)), full((d, f)), full((f, d)),
                  pl.BlockSpec((TB, d), lambda j, *_: (j, 0)),
                  pl.BlockSpec((None, 6, d), lambda j, *_: (j // blocks_per_seq, 0, 0)),
                  full((1, d))],
        out_specs=pl.BlockSpec((TB, d), lambda j, *_: (j, 0)),
        scratch_shapes=[pltpu.VMEM((2, BLOCK_CHUNKS, ROW_CHUNK, d), BF16),
                        pltpu.VMEM((TB, BLOCK_ROWS), BF16),
                        pltpu.VMEM((TB, d), F32),
                        pltpu.VMEM((d, 2 * f), BF16), pltpu.VMEM((f, d), BF16),
                        pltpu.SemaphoreType.DMA((2,))],
    )
    return pl.pallas_call(
        _combine_kernel,
        grid_spec=grid_spec,
        out_shape=jax.ShapeDtypeStruct((t, d), F32),
        compiler_params=pltpu.CompilerParams(
            dimension_semantics=("arbitrary",), vmem_limit_bytes=VMEM_LIMIT),
        name="moe_combine",
    )(plan["ngrp"], plan["gsrc"], ys.reshape(-1, ROW_CHUNK, d), tm, plan["bounds_row"],
      plan["bounds_col"], h2_flat, ws_gate, ws_up, ws_down, x1_flat, mod3, g_post)


def kernel(x, c, w_ada, b_ada, g_pre_mix, w_in, ln_sgu_g, ln_sgu_b, w_spatial, b_spatial,
           g_branch, w_out, g_post_mix, g_pre_ffn, w_router, router_bias, w_gate, w_up, w_down,
           ws_gate, ws_up, ws_down, g_post_ffn):
    bsz, s, d = x.shape
    t = bsz * s
    nblk = t // TB
    max_rows = t * TOP_K + nblk * N_EXPERTS * (ROW_CHUNK - 1) + N_EXPERTS * (TM_FFN - 1)
    n_tiles = -(-max_rows // TM_FFN)
    row = lambda a: a.reshape(1, -1)
    for l in range(w_ada.shape[0]):
        mod3 = _ada(c, w_ada[l], b_ada[l]).reshape(bsz, 6, d)
        og, q, k, v = _premix(x, mod3, row(g_pre_mix[l]), w_in[l],
                              row(ln_sgu_g[l]), row(ln_sgu_b[l]), w_spatial[l],
                              b_spatial[l].T, row(g_branch[l, :D_GMLP]))
        osb = _attention(q, k, v)
        x1, h2, logits_t = _postmix(og, osb, x, mod3, row(g_branch[l, D_GMLP:]),
                                    w_out[l], row(g_post_mix[l]),
                                    row(g_pre_ffn[l]), w_router[l].T)
        em, tm, cnt = _route(logits_t, router_bias[l])
        plan = _dispatch_plan(cnt[:, :, 0].astype(I32), n_tiles)
        h2_flat = h2.reshape(t, d)
        xs = _dispatch(plan, h2_flat, em, n_tiles * TM_FFN + nblk * GROUP_ROWS)
        ys = _expert_ffn(plan["tile_start"], plan["nact"], xs, n_tiles,
                         w_gate[l], w_up[l], w_down[l])
        out = _combine(plan, ys, tm, h2_flat, ws_gate[l], ws_up[l], ws_down[l],
                       x1.reshape(t, d), mod3, row(g_post_ffn[l]), s)
        x = out.reshape(bsz, s, d)
    return x
```
